```python
import math
import jax, jax.numpy as jnp
from jax import lax
import numpy as np

D_MODEL = 1024
BATCH = 16
SEQ = 4096
DEPTH = 1

CHUNK = 64
D_MIX = D_MODEL
CONV_CH = D_MIX // 2
CONV_WIDTH = 31
ATT_HEADS = 8
HEAD_DIM = (D_MIX - CONV_CH) // ATT_HEADS
ATT_W = ATT_HEADS * HEAD_DIM
IDX_HEADS = 4
IDX_DIM = 64
IDX_SCALE = (IDX_HEADS * IDX_DIM) ** -0.5
TOPK_MAX = 256
QBLOCK = 32
REL_BUCKETS = 32
REL_MAX_DIST = 128
PEER_HEADS = 8
N_KEYS = 128
N_EXPERTS = N_KEYS * N_KEYS
PEER_TOPK = 16
PEER_DKEY = 256
PEER_TOKBLOCK = 128
EPS = 1e-6
IN_WIDTHS = (CONV_CH, CONV_CH, ATT_W, ATT_W, ATT_W, IDX_HEADS * IDX_DIM, IDX_DIM, IDX_HEADS)
IN_COLS = 2 * CONV_CH + 3 * ATT_W + IDX_HEADS * IDX_DIM + IDX_DIM + IDX_HEADS

kernel_name = "hybrid_conv_dsa_peer_block"


def rmsnorm(x, g):
    xf = x.astype(jnp.float32)
    y = xf * lax.rsqrt(jnp.mean(xf * xf, axis=-1, keepdims=True) + EPS)
    return (y * g.astype(jnp.float32)).astype(x.dtype)


def layernorm(x, g, b):
    xf = x.astype(jnp.float32)
    mu = jnp.mean(xf, axis=-1, keepdims=True)
    xc = xf - mu
    y = xc * lax.rsqrt(jnp.mean(xc * xc, axis=-1, keepdims=True) + EPS)
    return (y * g.astype(jnp.float32) + b.astype(jnp.float32)).astype(x.dtype)


def split_cols(a):
    outs = []
    off = 0
    for w in IN_WIDTHS:
        outs.append(a[..., off:off + w])
        off += w
    return outs


def t5_bucket(rel):
    half = REL_BUCKETS // 2
    max_exact = half // 2
    ret = jnp.where(rel > 0, half, 0)
    n = jnp.abs(rel)
    nf = jnp.maximum(n, 1).astype(jnp.float32)
    large = max_exact + (jnp.log(nf / max_exact) / math.log(REL_MAX_DIST / max_exact)
                         * (half - max_exact)).astype(jnp.int32)
    large = jnp.minimum(large, half - 1)
    return ret + jnp.where(n < max_exact, n, large)


def conformer_conv(val, gate, w_dw, b_dw, ln_g, ln_b):
    u = val * jax.nn.sigmoid(gate)
    y = lax.conv_general_dilated(u, w_dw, window_strides=(1,),
                                 padding=[(CONV_WIDTH - 1, 0)],
                                 dimension_numbers=('NWC', 'WIO', 'NWC'),
                                 feature_group_count=CONV_CH) + b_dw
    return jax.nn.silu(layernorm(y, ln_g, ln_b))


def dsa_attention(q, k, v, qi, ki, wi, rel_bias):
    B_, S_ = q.shape[0], q.shape[1]
    n_sel = min(TOPK_MAX, S_ // 4)
    nblk = S_ // QBLOCK
    key_pos = jnp.arange(S_, dtype=jnp.int32)
    t_blocks = key_pos.reshape(nblk, QBLOCK)
    scale = HEAD_DIM ** -0.5

    def to_blocks(a):
        return a.reshape((B_, nblk, QBLOCK) + a.shape[2:]).swapaxes(0, 1)

    def one_block(args):
        qb, qib, wib, tb = args
        idx_logits = jnp.einsum('bqhd,bsd->bqhs', qib, ki)
        index = jnp.einsum('bqh,bqhs->bqs', wib, jax.nn.relu(idx_logits)) * IDX_SCALE
        limit = (tb // CHUNK + 1) * CHUNK
        admissible = key_pos[None, :] < limit[:, None]
        index = jnp.where(admissible[None], index.astype(jnp.float32), -jnp.inf)
        _, sel = lax.top_k(index, n_sel)
        valid = sel < limit[None, :, None]
        k_sel = jax.vmap(lambda kb, ib: kb[ib])(k, sel)
        v_sel = jax.vmap(lambda vb, ib: vb[ib])(v, sel)
        bias = rel_bias[t5_bucket(sel - tb[None, :, None])]
        logits = (jnp.einsum('bqhd,bqkhd->bqhk', qb, k_sel).astype(jnp.float32) * scale
                  + jnp.swapaxes(bias, -1, -2).astype(jnp.float32))
        logits = jnp.where(valid[:, :, None, :], logits, -jnp.inf)
        p = jax.nn.softmax(logits, axis=-1).astype(v.dtype)
        return jnp.einsum('bqhk,bqkhd->bqhd', p, v_sel)

    out = lax.map(one_block, (to_blocks(q), to_blocks(qi), to_blocks(wi), t_blocks))
    return out.swapaxes(0, 1).reshape(B_, S_, ATT_W)


def peer(h, w_pq, sub_k1, sub_k2, u_tab, v_tab):
    B_, S_, D_ = h.shape
    hb = h.reshape(-1, PEER_TOKBLOCK, D_)

    def one(xb):
        qq = (xb @ w_pq).reshape(PEER_TOKBLOCK, PEER_HEADS, 2, PEER_DKEY // 2)
        s1 = jnp.einsum('thd,hnd->thn', qq[:, :, 0], sub_k1)
        s2 = jnp.einsum('thd,hnd->thn', qq[:, :, 1], sub_k2)
        v1, i1 = lax.top_k(s1, PEER_TOPK)
        v2, i2 = lax.top_k(s2, PEER_TOPK)
        cand = (v1[..., :, None] + v2[..., None, :]).reshape(PEER_TOKBLOCK, PEER_HEADS, -1)
        cand_idx = (i1[..., :, None] * N_KEYS + i2[..., None, :]).reshape(PEER_TOKBLOCK, PEER_HEADS, -1)
        best, pos = lax.top_k(cand, PEER_TOPK)
        experts = jnp.take_along_axis(cand_idx, pos, axis=-1)
        g = jax.nn.softmax(best.astype(jnp.float32), axis=-1).astype(xb.dtype)
        act = jax.nn.gelu(jnp.einsum('td,thkd->thk', xb, u_tab[experts]))
        return jnp.einsum('thk,thkd->td', g * act, v_tab[experts])

    return lax.map(one, hb).reshape(B_, S_, D_)


def setup_inputs(seed: int = 0) -> dict:
    key = jax.random.key(seed)
    ks = jax.random.split(key, 24)

    def nrm(k, shape, scale):
        return jax.random.normal(k, shape, jnp.float32) * scale

    D = D_MODEL
    return {
        "x": nrm(ks[0], (BATCH, SEQ, D), 1.0),
        "c": nrm(ks[1], (BATCH, D), 1.0),
        "w_ada": nrm(ks[2], (DEPTH, D, 6 * D), 0.5 * D ** -0.5),
        "b_ada": nrm(ks[3], (DEPTH, 6 * D), 0.02),
        "g_norm1": 1.0 + nrm(ks[4], (DEPTH, D), 0.02),
        "g_norm2": 1.0 + nrm(ks[5], (DEPTH, D), 0.02),
        "w_in": nrm(ks[6], (DEPTH, D, IN_COLS), D ** -0.5),
        "q_norm_g": 1.0 + nrm(ks[7], (DEPTH, HEAD_DIM), 0.02),
        "k_norm_g": 1.0 + nrm(ks[8], (DEPTH, HEAD_DIM), 0.02),
        "conv_w": nrm(ks[9], (DEPTH, CONV_WIDTH, 1, CONV_CH), CONV_WIDTH ** -0.5),
        "conv_b": nrm(ks[10], (DEPTH, CONV_CH), 0.02),
        "conv_ln_g": 1.0 + nrm(ks[11], (DEPTH, CONV_CH), 0.02),
        "conv_ln_b": nrm(ks[12], (DEPTH, CONV_CH), 0.02),
        "rel_bias": nrm(ks[13], (REL_BUCKETS, ATT_HEADS), 0.1),
        "g_out_conv": 1.0 + nrm(ks[14], (DEPTH, CONV_CH), 0.02),
        "g_out_attn": 1.0 + nrm(ks[15], (DEPTH, ATT_W), 0.02),
        "w_out": nrm(ks[16], (DEPTH, D_MIX, D), D_MIX ** -0.5),
        "w_peer_q": nrm(ks[17], (DEPTH, D, PEER_HEADS * PEER_DKEY), D ** -0.5),
        "peer_k1": nrm(ks[18], (DEPTH, PEER_HEADS, N_KEYS, PEER_DKEY // 2), (PEER_DKEY // 2) ** -0.5),
        "peer_k2": nrm(ks[19], (DEPTH, PEER_HEADS, N_KEYS, PEER_DKEY // 2), (PEER_DKEY // 2) ** -0.5),
        "peer_u": nrm(ks[20], (DEPTH, N_EXPERTS, D), D ** -0.5),
        "peer_v": nrm(ks[21], (DEPTH, N_EXPERTS, D), 0.5),
    }


def reference(x, c, w_ada, b_ada, g_norm1, g_norm2, w_in, q_norm_g, k_norm_g, conv_w, conv_b,
              conv_ln_g, conv_ln_b, rel_bias, g_out_conv, g_out_attn, w_out, w_peer_q,
              peer_k1, peer_k2, peer_u, peer_v):
    B_, S_, D_ = x.shape
    for l in range(DEPTH):
        mod = jax.nn.silu(c) @ w_ada[l] + b_ada[l]
        sh1, sc1, gt1, sh2, sc2, gt2 = jnp.split(mod[:, None, :], 6, axis=-1)

        h = rmsnorm(x, g_norm1[l]) * (1.0 + sc1) + sh1
        cv, cg, q, k, v, qi, ki, wi = split_cols(h @ w_in[l])
        q = rmsnorm(q.reshape(B_, S_, ATT_HEADS, HEAD_DIM), q_norm_g[l])
        k = rmsnorm(k.reshape(B_, S_, ATT_HEADS, HEAD_DIM), k_norm_g[l])
        v = v.reshape(B_, S_, ATT_HEADS, HEAD_DIM)
        qi = qi.reshape(B_, S_, IDX_HEADS, IDX_DIM)
        conv_out = conformer_conv(cv, cg, conv_w[l], conv_b[l], conv_ln_g[l], conv_ln_b[l])
        attn_out = dsa_attention(q, k, v, qi, ki, wi, rel_bias)
        mixed = jnp.concatenate([rmsnorm(conv_out, g_out_conv[l]),
                                 rmsnorm(attn_out, g_out_attn[l])], axis=-1)
        x = x + gt1 * (mixed @ w_out[l])

        h2 = rmsnorm(x, g_norm2[l]) * (1.0 + sc2) + sh2
        x = x + gt2 * peer(h2, w_peer_q[l], peer_k1[l], peer_k2[l], peer_u[l], peer_v[l])
    return x
```

```python
import functools
import math

import numpy as np
import jax
import jax.numpy as jnp
from jax import lax
from jax.experimental import pallas as pl
from jax.experimental.pallas import tpu as pltpu

F32 = jnp.float32
BF16 = jnp.bfloat16
I32 = jnp.int32
HIGHEST = lax.Precision.HIGHEST

D_MODEL = 1024
CHUNK = 64
CONV_CH = 512
CONV_WIDTH = 31
ATT_HEADS = 8
HEAD_DIM = 64
ATT_W = ATT_HEADS * HEAD_DIM
IDX_HEADS = 4
IDX_DIM = 64
IDX_SCALE = (IDX_HEADS * IDX_DIM) ** -0.5
TOPK_MAX = 256
REL_BUCKETS = 32
REL_MAX_DIST = 128
PEER_HEADS = 8
N_KEYS = 128
N_EXPERTS = N_KEYS * N_KEYS
PEER_TOPK = 16
PEER_SLOTS = PEER_HEADS * PEER_TOPK
EPS = 1e-6

LANES = 128
SUBLANES = 8
VMEM_LIMIT = 56 * 1024 * 1024

TM_IN = 512
CONV_ROWS = 64
HALO = 32
TQ = 256
KT = 256
TM_MID = 256
TB_PEER = 128
HALF = D_MODEL // 2
ROWS_PER_EXPERT = HALF // LANES

NT_DIMS = (((1,), (1,)), ((), ()))

_NEG_INF_BITS = int(np.array(-np.inf, np.float32).view(np.int32))
KEY_NEG_INF = _NEG_INF_BITS ^ 0x7FFFFFFF
INT_MIN = -(2 ** 31)


def _nt(a, b, precision=None):
    return lax.dot_general(a, b, NT_DIMS, precision=precision, preferred_element_type=F32)


def _ada_kernel(c_ref, w_ref, b_ref, o_ref):
    a = jax.nn.silu(c_ref[...])
    o_ref[...] = jnp.dot(a, w_ref[...], precision=HIGHEST, preferred_element_type=F32) + b_ref[...]


def _ada(c, w_ada, b_ada):
    bsz, d = c.shape
    return pl.pallas_call(
        _ada_kernel,
        grid=(6,),
        in_specs=[
            pl.BlockSpec((bsz, d), lambda j: (0, 0)),
            pl.BlockSpec((d, d), lambda j: (0, j)),
            pl.BlockSpec((1, d), lambda j: (0, j)),
        ],
        out_specs=pl.BlockSpec((bsz, d), lambda j: (0, j)),
        out_shape=jax.ShapeDtypeStruct((bsz, 6 * d), F32),
        name="ada",
    )(c, w_ada, b_ada.reshape(1, 6 * d))


def _inproj_kernel(x_ref, mod_ref, g1_ref, wa_ref, wqkv_ref, widx_ref, wwi_ref, gq_ref, gk_ref, e2_ref,
                   cw_ref, cb_ref, lng_ref, lnb_ref, goc_ref,
                   conv_ref, q_ref, k_ref, vt_ref, qi_ref, ki_ref, wit_ref, ubuf):
    j = pl.program_id(1)
    tm = x_ref.shape[1]
    x = x_ref[0]
    sh1 = mod_ref[0, 0:1, :]
    sc1 = mod_ref[0, 1:2, :]
    r = lax.rsqrt(jnp.mean(x * x, axis=-1, keepdims=True) + EPS)
    h = (x * r) * g1_ref[...] * (1.0 + sc1) + sh1
    hb = h.astype(BF16)

    pa = jnp.dot(hb, wa_ref[...], preferred_element_type=F32)
    u = pa[:, :CONV_CH] * jax.nn.sigmoid(pa[:, CONV_CH:])

    @pl.when(j == 0)
    def _():
        ubuf[0:HALO, :] = jnp.zeros((HALO, CONV_CH), F32)

    ubuf[HALO:HALO + tm, :] = u
    first = HALO - (CONV_WIDTH - 1)
    for rb in range(tm // CONV_ROWS):
        base = rb * CONV_ROWS
        acc = jnp.zeros((CONV_ROWS, CONV_CH), F32) + cb_ref[...]
        for t in range(CONV_WIDTH):
            acc = acc + cw_ref[t:t + 1, :] * ubuf[base + first + t:base + first + t + CONV_ROWS, :]
        mu = jnp.mean(acc, axis=-1, keepdims=True)
        xc = acc - mu
        y = xc * lax.rsqrt(jnp.mean(xc * xc, axis=-1, keepdims=True) + EPS)
        y = jax.nn.silu(y * lng_ref[...] + lnb_ref[...])
        y = y * lax.rsqrt(jnp.mean(y * y, axis=-1, keepdims=True) + EPS) * goc_ref[...]
        conv_ref[0, base:base + CONV_ROWS, :] = y.astype(BF16)
    ubuf[0:HALO, :] = ubuf[tm:tm + HALO, :]

    pq = jnp.dot(hb, wqkv_ref[...], preferred_element_type=F32)
    q = pq[:, :ATT_W]
    k = pq[:, ATT_W:2 * ATT_W]
    v = pq[:, 2 * ATT_W:]
    e2 = e2_ref[...]
    qs = jnp.dot(q * q, e2, precision=HIGHEST, preferred_element_type=F32) * (1.0 / HEAD_DIM)
    ks = jnp.dot(k * k, e2, precision=HIGHEST, preferred_element_type=F32) * (1.0 / HEAD_DIM)
    qn = q * lax.rsqrt(qs + EPS) * gq_ref[...] * (HEAD_DIM ** -0.5)
    kn = k * lax.rsqrt(ks + EPS) * gk_ref[...]
    lane = lax.broadcasted_iota(I32, (tm, LANES), 1)
    low = lane < HEAD_DIM
    for p in range(ATT_HEADS // 2):
        slab = qn[:, p * LANES:(p + 1) * LANES]
        q_ref[0, 2 * p] = jnp.where(low, slab, 0.0).astype(BF16)
        q_ref[0, 2 * p + 1] = jnp.where(low, 0.0, slab).astype(BF16)
        k_ref[0, p] = kn[:, p * LANES:(p + 1) * LANES].astype(BF16)
    vt_ref[0] = v.T.astype(BF16)

    pc = jnp.dot(hb, widx_ref[...], preferred_element_type=F32)
    for p in range(IDX_HEADS // 2):
        slab = pc[:, p * LANES:(p + 1) * LANES]
        qi_ref[0, 2 * p] = jnp.where(low, slab, 0.0).astype(BF16)
        qi_ref[0, 2 * p + 1] = jnp.where(low, 0.0, slab).astype(BF16)
    ki_ref[0] = pc[:, 2 * LANES:3 * LANES].astype(BF16)
    wit_ref[0] = _nt(wwi_ref[...], hb) * IDX_SCALE


def _inproj(x, mod3, g1, wa, wqkv, widx, wwi, gq, gk, e2, cw, cb, lng, lnb, goc):
    bsz, s, d = x.shape
    tm = TM_IN
    nt = s // tm
    full = lambda shape: pl.BlockSpec(shape, lambda b, j: (0,) * len(shape))
    out_shape = (
        jax.ShapeDtypeStruct((bsz, s, CONV_CH), BF16),
        jax.ShapeDtypeStruct((bsz, ATT_HEADS, s, LANES), BF16),
        jax.ShapeDtypeStruct((bsz, ATT_HEADS // 2, s, LANES), BF16),
        jax.ShapeDtypeStruct((bsz, ATT_W, s), BF16),
        jax.ShapeDtypeStruct((bsz, IDX_HEADS, s, LANES), BF16),
        jax.ShapeDtypeStruct((bsz, s, LANES), BF16),
        jax.ShapeDtypeStruct((bsz, SUBLANES, s), F32),
    )
    out_specs = (
        pl.BlockSpec((1, tm, CONV_CH), lambda b, j: (b, j, 0)),
        pl.BlockSpec((1, ATT_HEADS, tm, LANES), lambda b, j: (b, 0, j, 0)),
        pl.BlockSpec((1, ATT_HEADS // 2, tm, LANES), lambda b, j: (b, 0, j, 0)),
        pl.BlockSpec((1, ATT_W, tm), lambda b, j: (b, 0, j)),
        pl.BlockSpec((1, IDX_HEADS, tm, LANES), lambda b, j: (b, 0, j, 0)),
        pl.BlockSpec((1, tm, LANES), lambda b, j: (b, j, 0)),
        pl.BlockSpec((1, SUBLANES, tm), lambda b, j: (b, 0, j)),
    )
    return pl.pallas_call(
        _inproj_kernel,
        grid=(bsz, nt),
        in_specs=[
            pl.BlockSpec((1, tm, d), lambda b, j: (b, j, 0)),
            pl.BlockSpec((1, 6, d), lambda b, j: (b, 0, 0)),
            full(g1.shape), full(wa.shape), full(wqkv.shape), full(widx.shape), full(wwi.shape),
            full(gq.shape), full(gk.shape), full(e2.shape),
            full(cw.shape), full(cb.shape), full(lng.shape), full(lnb.shape), full(goc.shape),
        ],
        out_specs=out_specs,
        out_shape=out_shape,
        scratch_shapes=[pltpu.VMEM((tm + HALO, CONV_CH), F32)],
        compiler_params=pltpu.CompilerParams(
            dimension_semantics=("arbitrary", "arbitrary"), vmem_limit_bytes=VMEM_LIMIT),
        name="inproj",
    )(x, mod3, g1, wa, wqkv, widx, wwi, gq, gk, e2, cw, cb, lng, lnb, goc)


def _t5_bucket_np(rel):
    half = REL_BUCKETS // 2
    max_exact = half // 2
    ret = np.where(rel > 0, half, 0)
    n = np.abs(rel)
    nf = np.maximum(n, 1).astype(np.float64)
    large = max_exact + (np.log(nf / max_exact) / math.log(REL_MAX_DIST / max_exact)
                         * (half - max_exact)).astype(np.int32)
    large = np.minimum(large, half - 1)
    return (ret + np.where(n < max_exact, n, large)).astype(np.int32)


NEAR = REL_MAX_DIST + TQ
FAR_BUCKET = REL_BUCKETS // 2 - 1


def _near_buckets():
    r = np.arange(NEAR)[:, None]
    t = np.arange(TQ)[None, :]
    return _t5_bucket_np(r - REL_MAX_DIST - t)


def _attn_kernel(q_ref, k_ref, vt_ref, qi_ref, ki_ref, wit_ref, bkt_ref, rb_ref, goa_ref, o_ref,
                 keys_s, maskb_s, lg_s, relb_s, ot_s, *, nsel, jbits):
    b = pl.program_id(0)
    i = pl.program_id(1)
    t0 = i * TQ
    n_tiles = i + 1
    neg_inf = F32(-jnp.inf)

    @pl.when((b == 0) & (i == 0))
    def _():
        bk = bkt_ref[...]
        for h in range(ATT_HEADS):
            far = rb_ref[FAR_BUCKET, h]
            acc = jnp.zeros((NEAR, TQ), F32)
            for bb in range(REL_BUCKETS):
                acc = jnp.where(bk == bb, rb_ref[bb, h] - far, acc)
            relb_s[h, 0:2 * KT - NEAR, :] = jnp.zeros((2 * KT - NEAR, TQ), F32)
            relb_s[h, 2 * KT - NEAR:2 * KT, :] = acc

    qpos = t0 + lax.broadcasted_iota(I32, (1, TQ), 1)
    limit = (qpos // CHUNK + 1) * CHUNK
    row_iota = lax.broadcasted_iota(I32, (KT, TQ), 0)

    def tile_start(jt):
        return pl.multiple_of(jt * KT, KT)

    qia = qi_ref[0].reshape(IDX_HEADS * TQ, LANES)
    wi = wit_ref[0]

    def p1(jt, c):
        ks = tile_start(jt)
        lgt = _nt(ki_ref[0, pl.ds(ks, KT), :], qia)
        acc = jnp.zeros((KT, TQ), F32)
        for h in range(IDX_HEADS):
            acc = acc + jnp.maximum(lgt[:, h * TQ:(h + 1) * TQ], 0.0) * wi[h:h + 1, :]
        sc = jnp.where(row_iota + ks < limit, acc, neg_inf)
        bits = pltpu.bitcast(sc, I32)
        keys_s[pl.ds(ks, KT), :] = bits ^ (lax.shift_right_arithmetic(bits, 31) & 0x7FFFFFFF)
        return c

    lax.fori_loop(0, n_tiles, p1, 0)

    def count(pred):
        def body(jt, acc):
            ks = tile_start(jt)
            m = pred(keys_s[pl.ds(ks, KT), :], ks).astype(F32)
            return acc + jnp.sum(m.reshape(KT // SUBLANES, SUBLANES, TQ), axis=0)
        acc = lax.fori_loop(0, n_tiles, body, jnp.zeros((SUBLANES, TQ), F32))
        return jnp.sum(acc, axis=0, keepdims=True)

    def select_thr():
        c0 = count(lambda kt, ks: kt >= 0)
        t = jnp.where(c0 >= nsel, 0, INT_MIN).astype(I32)

        def bitstep(it, t):
            cand = t + lax.shift_left(I32(1), I32(30) - it)
            c = count(lambda kt, ks: kt >= cand)
            return jnp.where(c >= nsel, cand, t)

        return lax.fori_loop(0, 31, bitstep, t)

    thr = lax.cond(i > 0, select_thr, lambda: jnp.full((1, TQ), KEY_NEG_INF + 1, I32))

    c_gt = count(lambda kt, ks: kt > thr)
    c_ge = count(lambda kt, ks: kt >= thr)
    need = nsel - c_gt

    def tie_cut():
        def step(it, cut):
            cand = cut + lax.shift_left(I32(1), I32(jbits - 1) - it)
            f = count(lambda kt, ks: jnp.where(kt == thr, row_iota + ks, cand) < cand)
            return jnp.where(f <= need, cand, cut)
        return lax.fori_loop(0, jbits, step, jnp.zeros((1, TQ), I32))

    cut = lax.cond(jnp.max(c_ge) > nsel, tie_cut, lambda: jnp.full((1, TQ), 2 ** jbits - 1, I32))

    def p3(jt, c):
        ks = tile_start(jt)
        kt = keys_s[pl.ds(ks, KT), :]
        tied = jnp.where(kt == thr, jnp.where(row_iota + ks < cut, 0.0, neg_inf), neg_inf)
        maskb_s[pl.ds(ks, KT), :] = jnp.where(kt > thr, 0.0, tied)
        return c

    lax.fori_loop(0, n_tiles, p3, 0)

    n_far = jnp.maximum(n_tiles - 2, 0)
    for h in range(ATT_HEADS):
        qz = q_ref[0, h]

        def logits(jt):
            ks = tile_start(jt)
            s = _nt(k_ref[0, h // 2, pl.ds(ks, KT), :], qz)
            return ks, s + maskb_s[pl.ds(ks, KT), :]

        def tile_max(l):
            return jnp.max(l.reshape(KT // SUBLANES, SUBLANES, TQ), axis=0)

        def pa_far(jt, m):
            ks, l = logits(jt)
            lg_s[pl.ds(ks, KT), :] = l
            return jnp.maximum(m, tile_max(l))

        def pa_near(jt, m):
            ks, l = logits(jt)
            off = pl.multiple_of((jt - (n_tiles - 2)) * KT, KT)
            l = l + relb_s[h, pl.ds(off, KT), :]
            lg_s[pl.ds(ks, KT), :] = l
            return jnp.maximum(m, tile_max(l))

        m8 = lax.fori_loop(0, n_far, pa_far, jnp.full((SUBLANES, TQ), neg_inf, F32))
        m8 = lax.fori_loop(n_far, n_tiles, pa_near, m8)
        m = jnp.max(m8, axis=0, keepdims=True)

        def pb(jt, carry):
            s8, o = carry
            ks = tile_start(jt)
            p = jnp.exp(lg_s[pl.ds(ks, KT), :] - m)
            s8 = s8 + jnp.sum(p.reshape(KT // SUBLANES, SUBLANES, TQ), axis=0)
            vt = vt_ref[0, h * HEAD_DIM:(h + 1) * HEAD_DIM, pl.ds(ks, KT)]
            o = o + jnp.dot(vt, p.astype(BF16), preferred_element_type=F32)
            return s8, o

        s8, o = lax.fori_loop(0, n_tiles, pb, (jnp.zeros((SUBLANES, TQ), F32), jnp.zeros((HEAD_DIM, TQ), F32)))
        ot_s[h * HEAD_DIM:(h + 1) * HEAD_DIM, :] = o / jnp.sum(s8, axis=0, keepdims=True)

    ot = ot_s[...]
    ms = jnp.mean(ot * ot, axis=0, keepdims=True)
    y = (ot * lax.rsqrt(ms + EPS)).T * goa_ref[...]
    o_ref[0] = y.astype(BF16)


def _attn(q, k, vt, qi, ki, wit, rel_bias, goa):
    bsz, _, s, _ = q.shape
    nsel = min(TOPK_MAX, s // 4)
    assert nsel == TQ and s % TQ == 0, "attention kernel assumes TOPK_MAX-sized query blocks"
    jbits = int(s).bit_length()
    bkt = jnp.asarray(_near_buckets())
    kern = functools.partial(_attn_kernel, nsel=nsel, jbits=jbits)
    return pl.pallas_call(
        kern,
        grid=(bsz, s // TQ),
        in_specs=[
            pl.BlockSpec((1, ATT_HEADS, TQ, LANES), lambda b, i: (b, 0, i, 0)),
            pl.BlockSpec((1, ATT_HEADS // 2, s, LANES), lambda b, i: (b, 0, 0, 0)),
            pl.BlockSpec((1, ATT_W, s), lambda b, i: (b, 0, 0)),
            pl.BlockSpec((1, IDX_HEADS, TQ, LANES), lambda b, i: (b, 0, i, 0)),
            pl.BlockSpec((1, s, LANES), lambda b, i: (b, 0, 0)),
            pl.BlockSpec((1, SUBLANES, TQ), lambda b, i: (b, 0, i)),
            pl.BlockSpec((NEAR, TQ), lambda b, i: (0, 0)),
            pl.BlockSpec(memory_space=pltpu.SMEM),
            pl.BlockSpec((1, ATT_W), lambda b, i: (0, 0)),
        ],
        out_specs=pl.BlockSpec((1, TQ, ATT_W), lambda b, i: (b, i, 0)),
        out_shape=jax.ShapeDtypeStruct((bsz, s, ATT_W), BF16),
        scratch_shapes=[
            pltpu.VMEM((s, TQ), I32),
            pltpu.VMEM((s, TQ), F32),
            pltpu.VMEM((s, TQ), F32),
            pltpu.VMEM((ATT_HEADS, 2 * KT, TQ), F32),
            pltpu.VMEM((ATT_W, TQ), F32),
        ],
        compiler_params=pltpu.CompilerParams(
            dimension_semantics=("arbitrary", "arbitrary"), vmem_limit_bytes=VMEM_LIMIT),
        name="attn",
    )(q, k, vt, qi, ki, wit, bkt, rel_bias, goa)


def _topk_rows(s, payload, k):
    nrows = s.shape[0]
    rows = lax.broadcasted_iota(I32, s.shape, 0).astype(F32)
    vals, pays = [], []
    for _ in range(k):
        m = jnp.max(s, axis=0, keepdims=True)
        ix = jnp.min(jnp.where(s == m, rows, float(nrows)), axis=0, keepdims=True)
        hit = rows == ix
        if payload is None:
            pays.append(ix)
        else:
            pays.append(jnp.max(jnp.where(hit, payload, -1.0), axis=0, keepdims=True))
        vals.append(m)
        s = jnp.where(hit, -jnp.inf, s)
    return jnp.concatenate(vals, axis=0), jnp.concatenate(pays, axis=0)


def _mid_kernel(x_ref, cn_ref, an_ref, mod_ref, wo1_ref, wo2_ref, g2_ref, wpq_ref, k1_ref, k2_ref,
                x1_ref, h2_ref, idx_ref, gate_ref, qq_s, idt_s, gt_s):
    tm = x_ref.shape[1]
    x = x_ref[0]
    gt1 = mod_ref[0, 2:3, :]
    sh2 = mod_ref[0, 3:4, :]
    sc2 = mod_ref[0, 4:5, :]
    proj = (jnp.dot(cn_ref[0], wo1_ref[...], preferred_element_type=F32)
            + jnp.dot(an_ref[0], wo2_ref[...], preferred_element_type=F32))
    x1 = x + gt1 * proj
    x1_ref[0] = x1
    r = lax.rsqrt(jnp.mean(x1 * x1, axis=-1, keepdims=True) + EPS)
    h2 = (x1 * r) * g2_ref[...] * (1.0 + sc2) + sh2
    h2_ref[0] = h2
    qq_s[...] = jnp.dot(h2.astype(BF16), wpq_ref[...], preferred_element_type=F32).astype(BF16)

    def route(it, c):
        hh = it // (tm // LANES)
        lt = it % (tm // LANES)
        rows = pl.ds(pl.multiple_of(lt * LANES, LANES), LANES)
        q1 = qq_s[rows, pl.ds(pl.multiple_of(hh * 2 * N_KEYS, LANES), N_KEYS)]
        q2 = qq_s[rows, pl.ds(pl.multiple_of(hh * 2 * N_KEYS + N_KEYS, LANES), N_KEYS)]
        v1, i1 = _topk_rows(_nt(k1_ref[hh], q1), None, PEER_TOPK)
        v2, i2 = _topk_rows(_nt(k2_ref[hh], q2), None, PEER_TOPK)
        cand = jnp.concatenate([v1[a:a + 1, :] + v2 for a in range(PEER_TOPK)], axis=0)
        cidx = jnp.concatenate([i1[a:a + 1, :] * float(N_KEYS) + i2 for a in range(PEER_TOPK)], axis=0)
        best, experts = _topk_rows(cand, cidx, PEER_TOPK)
        e = jnp.exp(best - best[0:1, :])
        g = e / jnp.sum(e, axis=0, keepdims=True)
        slots = pl.ds(pl.multiple_of(hh * PEER_TOPK, PEER_TOPK), PEER_TOPK)
        cols = pl.ds(pl.multiple_of(lt * LANES, LANES), LANES)
        idt_s[slots, cols] = experts
        gt_s[slots, cols] = g
        return c

    lax.fori_loop(0, PEER_HEADS * (tm // LANES), route, 0)
    idx_ref[0] = idt_s[...].T.astype(I32)
    gate_ref[0] = gt_s[...].T


def _mid(x, cn, an, mod3, wo1, wo2, g2, wpq, k1, k2):
    bsz, s, d = x.shape
    tm = TM_MID
    full = lambda shape: pl.BlockSpec(shape, lambda b, j: (0,) * len(shape))
    tok = lambda w: pl.BlockSpec((1, tm, w), lambda b, j: (b, j, 0))
    return pl.pallas_call(
        _mid_kernel,
        grid=(bsz, s // tm),
        in_specs=[tok(d), tok(CONV_CH), tok(ATT_W), pl.BlockSpec((1, 6, d), lambda b, j: (b, 0, 0)),
                  full(wo1.shape), full(wo2.shape), full(g2.shape), full(wpq.shape), full(k1.shape), full(k2.shape)],
        out_specs=(tok(d), tok(d), tok(PEER_SLOTS), tok(PEER_SLOTS)),
        out_shape=(
            jax.ShapeDtypeStruct((bsz, s, d), F32),
            jax.ShapeDtypeStruct((bsz, s, d), F32),
            jax.ShapeDtypeStruct((bsz, s, PEER_SLOTS), I32),
            jax.ShapeDtypeStruct((bsz, s, PEER_SLOTS), F32),
        ),
        scratch_shapes=[
            pltpu.VMEM((tm, PEER_HEADS * 2 * N_KEYS), BF16),
            pltpu.VMEM((PEER_SLOTS, tm), F32),
            pltpu.VMEM((PEER_SLOTS, tm), F32),
        ],
        compiler_params=pltpu.CompilerParams(
            dimension_semantics=("arbitrary", "arbitrary"), vmem_limit_bytes=VMEM_LIMIT),
        name="mid",
    )(x, cn, an, mod3, wo1, wo2, g2, wpq, k1, k2)


def _pack_table(t):
    tb = t.astype(BF16)
    lo = lax.bitcast_convert_type(tb[:, :HALF], jnp.uint16).astype(jnp.uint32)
    hi = lax.bitcast_convert_type(tb[:, HALF:], jnp.uint16).astype(jnp.uint32)
    return (lo | (hi << 16)).reshape(t.shape[0] * ROWS_PER_EXPERT, LANES)


def _unpack(w):
    lo = pltpu.bitcast(lax.shift_left(w, jnp.uint32(16)), F32)
    hi = pltpu.bitcast(w & jnp.uint32(0xFFFF0000), F32)
    return lo, hi


def _gather_row(tab_ref, e):
    return tab_ref[pl.ds(pl.multiple_of(e * ROWS_PER_EXPERT, ROWS_PER_EXPERT), ROWS_PER_EXPERT), :]


def _peer_u_kernel(idx_ref, tab_ref, x_ref, gate_ref, w_ref, stage):
    tb = w_ref.shape[0]
    ones = jnp.ones((SUBLANES, LANES), F32)

    def token(t, c):
        xv = x_ref[pl.ds(pl.multiple_of(t * SUBLANES, SUBLANES), SUBLANES), :]
        xlo = xv[0:ROWS_PER_EXPERT]
        xhi = xv[ROWS_PER_EXPERT:]
        for j in range(PEER_SLOTS):
            lo, hi = _unpack(_gather_row(tab_ref, idx_ref[t, j]))
            stage[j * ROWS_PER_EXPERT:(j + 1) * ROWS_PER_EXPERT, :] = lo * xlo + hi * xhi
        cs = stage[pl.ds(0, PEER_SLOTS, stride=ROWS_PER_EXPERT), :]
        for r in range(1, ROWS_PER_EXPERT):
            cs = cs + stage[pl.ds(r, PEER_SLOTS, stride=ROWS_PER_EXPERT), :]
        act = _nt(ones, cs, precision=HIGHEST)[0:1, :]
        w_ref[pl.ds(t, 1), :] = gate_ref[pl.ds(t, 1), :] * jax.nn.gelu(act)
        return c

    lax.fori_loop(0, tb, token, 0)


def _peer_u(idx, tab, h2_rows, gate):
    n = idx.shape[0]
    tb = TB_PEER
    return pl.pallas_call(
        _peer_u_kernel,
        grid=(n // tb,),
        in_specs=[
            pl.BlockSpec((tb, PEER_SLOTS), lambda i: (i, 0), memory_space=pltpu.SMEM),
            pl.BlockSpec(tab.shape, lambda i: (0, 0), pipeline_mode=pl.Buffered(1)),
            pl.BlockSpec((tb * SUBLANES, LANES), lambda i: (i, 0)),
            pl.BlockSpec((tb, PEER_SLOTS), lambda i: (i, 0)),
        ],
        out_specs=pl.BlockSpec((tb, PEER_SLOTS), lambda i: (i, 0)),
        out_shape=jax.ShapeDtypeStruct((n, PEER_SLOTS), F32),
        scratch_shapes=[pltpu.VMEM((PEER_SLOTS * ROWS_PER_EXPERT, LANES), F32)],
        compiler_params=pltpu.CompilerParams(
            dimension_semantics=("arbitrary",), vmem_limit_bytes=VMEM_LIMIT),
        name="peer_u",
    )(idx, tab, h2_rows, gate)


def _peer_v_kernel(idx_ref, w_ref, tab_ref, x1_ref, gt2_ref, o_ref):
    tb = idx_ref.shape[0]
    gt2 = gt2_ref[0]

    def token(t, c):
        acc_lo = jnp.zeros((ROWS_PER_EXPERT, LANES), F32)
        acc_hi = jnp.zeros((ROWS_PER_EXPERT, LANES), F32)
        for j in range(PEER_SLOTS):
            lo, hi = _unpack(_gather_row(tab_ref, idx_ref[t, j]))
            wj = w_ref[t, j]
            acc_lo = acc_lo + wj * lo
            acc_hi = acc_hi + wj * hi
        rows = pl.ds(pl.multiple_of(t * SUBLANES, SUBLANES), SUBLANES)
        o_ref[rows, :] = x1_ref[rows, :] + gt2 * jnp.concatenate([acc_lo, acc_hi], axis=0)
        return c

    lax.fori_loop(0, tb, token, 0)


def _peer_v(idx, w, tab, x1_rows, gt2_rows, blocks_per_batch):
    n = idx.shape[0]
    tb = TB_PEER
    return pl.pallas_call(
        _peer_v_kernel,
        grid=(n // tb,),
        in_specs=[
            pl.BlockSpec((tb, PEER_SLOTS), lambda i: (i, 0), memory_space=pltpu.SMEM),
            pl.BlockSpec((tb, PEER_SLOTS), lambda i: (i, 0), memory_space=pltpu.SMEM),
            pl.BlockSpec(tab.shape, lambda i: (0, 0), pipeline_mode=pl.Buffered(1)),
            pl.BlockSpec((tb * SUBLANES, LANES), lambda i: (i, 0)),
            pl.BlockSpec((1, SUBLANES, LANES), lambda i: (i // blocks_per_batch, 0, 0)),
        ],
        out_specs=pl.BlockSpec((tb * SUBLANES, LANES), lambda i: (i, 0)),
        out_shape=jax.ShapeDtypeStruct((n * SUBLANES, LANES), F32),
        compiler_params=pltpu.CompilerParams(
            dimension_semantics=("arbitrary",), vmem_limit_bytes=VMEM_LIMIT),
        name="peer_v",
    )(idx, w, tab, x1_rows, gt2_rows)


def _layer(x, mod, g_norm1, g_norm2, w_in, q_norm_g, k_norm_g, conv_w, conv_b, conv_ln_g, conv_ln_b,
           rel_bias, g_out_conv, g_out_attn, w_out, w_peer_q, peer_k1, peer_k2, peer_u, peer_v):
    bsz, s, d = x.shape
    n = bsz * s
    mod3 = mod.reshape(bsz, 6, d)
    row = lambda a: a.reshape(1, -1)

    c0 = 2 * CONV_CH
    c1 = c0 + 3 * ATT_W
    c2 = c1 + IDX_HEADS * IDX_DIM
    c3 = c2 + IDX_DIM
    wa = w_in[:, :c0].astype(BF16)
    wqkv = w_in[:, c0:c1].astype(BF16)
    widx = jnp.concatenate([w_in[:, c1:c2], w_in[:, c2:c3], w_in[:, c2:c3]], axis=1).astype(BF16)
    wwi = jnp.zeros((SUBLANES, d), F32).at[:IDX_HEADS].set(w_in[:, c3:c3 + IDX_HEADS].T).astype(BF16)
    head = np.arange(ATT_W) // HEAD_DIM
    e2 = jnp.asarray((head[:, None] == head[None, :]).astype(np.float32))

    conv_n, q, k, vt, qi, ki, wit = _inproj(
        x, mod3, row(g_norm1), wa, wqkv, widx, wwi,
        row(jnp.tile(q_norm_g, ATT_HEADS)), row(jnp.tile(k_norm_g, ATT_HEADS)), e2,
        conv_w.reshape(CONV_WIDTH, CONV_CH), row(conv_b), row(conv_ln_g), row(conv_ln_b), row(g_out_conv))
    attn_n = _attn(q, k, vt, qi, ki, wit, rel_bias, row(g_out_attn))

    x1, h2, idx, gate = _mid(
        x, conv_n, attn_n, mod3, w_out[:CONV_CH].astype(BF16), w_out[CONV_CH:].astype(BF16), row(g_norm2),
        w_peer_q.astype(BF16), peer_k1.astype(BF16), peer_k2.astype(BF16))

    idx = idx.reshape(n, PEER_SLOTS)
    w = _peer_u(idx, _pack_table(peer_u), h2.reshape(n * SUBLANES, LANES), gate.reshape(n, PEER_SLOTS))
    gt2_rows = mod3[:, 5, :].reshape(bsz, SUBLANES, LANES)
    out = _peer_v(idx, w, _pack_table(peer_v), x1.reshape(n * SUBLANES, LANES), gt2_rows, s // TB_PEER)
    return out.reshape(bsz, s, d)


def kernel(x, c, w_ada, b_ada, g_norm1, g_norm2, w_in, q_norm_g, k_norm_g, conv_w, conv_b, conv_ln_g,
           conv_ln_b, rel_bias, g_out_conv, g_out_attn, w_out, w_peer_q, peer_k1, peer_k2, peer_u, peer_v):
    depth = w_ada.shape[0]
    for l in range(depth):
        mod = _ada(c, w_ada[l], b_ada[l])
        x = _layer(x, mod, g_norm1[l], g_norm2[l], w_in[l], q_norm_g[l], k_norm_g[l], conv_w[l], conv_b[l],
                   conv_ln_g[l], conv_ln_b[l], rel_bias, g_out_conv[l], g_out_attn[l], w_out[l],
                   w_peer_q[l], peer_k1[l], peer_k2[l], peer_u[l], peer_v[l])
    return x
```

```python
import functools
import math

import numpy as np
import jax
import jax.numpy as jnp
from jax import lax
from jax.experimental import pallas as pl
from jax.experimental.pallas import tpu as pltpu

F32 = jnp.float32
BF16 = jnp.bfloat16
I32 = jnp.int32
HIGHEST = lax.Precision.HIGHEST

D_MODEL = 1024
CHUNK = 64
CONV_CH = 512
CONV_WIDTH = 31
ATT_HEADS = 8
HEAD_DIM = 64
ATT_W = ATT_HEADS * HEAD_DIM
IDX_HEADS = 4
IDX_DIM = 64
IDX_SCALE = (IDX_HEADS * IDX_DIM) ** -0.5
TOPK_MAX = 256
REL_BUCKETS = 32
REL_MAX_DIST = 128
PEER_HEADS = 8
N_KEYS = 128
N_EXPERTS = N_KEYS * N_KEYS
PEER_TOPK = 16
PEER_SLOTS = PEER_HEADS * PEER_TOPK
EPS = 1e-6

LANES = 128
SUBLANES = 8
VMEM_LIMIT = 56 * 1024 * 1024

TM_IN = 512
CONV_ROWS = 64
HALO = 32
TQ = 256
KT = 256
TM_MID = 256
TB_PEER = 128
HALF = D_MODEL // 2
ROWS_PER_EXPERT = HALF // LANES

NT_DIMS = (((1,), (1,)), ((), ()))

_NEG_INF_BITS = int(np.array(-np.inf, np.float32).view(np.int32))
KEY_NEG_INF = _NEG_INF_BITS ^ 0x7FFFFFFF
INT_MIN = -(2 ** 31)


def _nt(a, b, precision=None):
    return lax.dot_general(a, b, NT_DIMS, precision=precision, preferred_element_type=F32)


def _ada_kernel(c_ref, w_ref, b_ref, o_ref):
    a = jax.nn.silu(c_ref[...])
    o_ref[...] = jnp.dot(a, w_ref[...], precision=HIGHEST, preferred_element_type=F32) + b_ref[...]


def _ada(c, w_ada, b_ada):
    bsz, d = c.shape
    return pl.pallas_call(
        _ada_kernel,
        grid=(6,),
        in_specs=[
            pl.BlockSpec((bsz, d), lambda j: (0, 0)),
            pl.BlockSpec((d, d), lambda j: (0, j)),
            pl.BlockSpec((1, d), lambda j: (0, j)),
        ],
        out_specs=pl.BlockSpec((bsz, d), lambda j: (0, j)),
        out_shape=jax.ShapeDtypeStruct((bsz, 6 * d), F32),
        name="ada",
    )(c, w_ada, b_ada.reshape(1, 6 * d))


def _inproj_kernel(x_ref, mod_ref, g1_ref, wa_ref, wqkv_ref, widx_ref, wwi_ref, gq_ref, gk_ref, e2_ref,
                   cw_ref, cb_ref, lng_ref, lnb_ref, goc_ref,
                   conv_ref, q_ref, k_ref, vt_ref, qi_ref, ki_ref, wit_ref, ubuf):
    j = pl.program_id(1)
    tm = x_ref.shape[1]
    x = x_ref[0]
    sh1 = mod_ref[0, 0:1, :]
    sc1 = mod_ref[0, 1:2, :]
    r = lax.rsqrt(jnp.mean(x * x, axis=-1, keepdims=True) + EPS)
    h = (x * r) * g1_ref[...] * (1.0 + sc1) + sh1
    hb = h.astype(BF16)

    pa = jnp.dot(hb, wa_ref[...], preferred_element_type=F32)
    u = pa[:, :CONV_CH] * jax.nn.sigmoid(pa[:, CONV_CH:])

    @pl.when(j == 0)
    def _():
        ubuf[0:HALO, :] = jnp.zeros((HALO, CONV_CH), F32)

    ubuf[HALO:HALO + tm, :] = u
    first = HALO - (CONV_WIDTH - 1)
    for rb in range(tm // CONV_ROWS):
        base = rb * CONV_ROWS
        acc = jnp.zeros((CONV_ROWS, CONV_CH), F32) + cb_ref[...]
        for t in range(CONV_WIDTH):
            acc = acc + cw_ref[t:t + 1, :] * ubuf[base + first + t:base + first + t + CONV_ROWS, :]
        mu = jnp.mean(acc, axis=-1, keepdims=True)
        xc = acc - mu
        y = xc * lax.rsqrt(jnp.mean(xc * xc, axis=-1, keepdims=True) + EPS)
        y = jax.nn.silu(y * lng_ref[...] + lnb_ref[...])
        y = y * lax.rsqrt(jnp.mean(y * y, axis=-1, keepdims=True) + EPS) * goc_ref[...]
        conv_ref[0, base:base + CONV_ROWS, :] = y.astype(BF16)
    ubuf[0:HALO, :] = ubuf[tm:tm + HALO, :]

    pq = jnp.dot(hb, wqkv_ref[...], preferred_element_type=F32)
    q = pq[:, :ATT_W]
    k = pq[:, ATT_W:2 * ATT_W]
    v = pq[:, 2 * ATT_W:]
    e2 = e2_ref[...]
    qs = jnp.dot(q * q, e2, precision=HIGHEST, preferred_element_type=F32) * (1.0 / HEAD_DIM)
    ks = jnp.dot(k * k, e2, precision=HIGHEST, preferred_element_type=F32) * (1.0 / HEAD_DIM)
    qn = q * lax.rsqrt(qs + EPS) * gq_ref[...] * (HEAD_DIM ** -0.5)
    kn = k * lax.rsqrt(ks + EPS) * gk_ref[...]
    lane = lax.broadcasted_iota(I32, (tm, LANES), 1)
    low = lane < HEAD_DIM
    for p in range(ATT_HEADS // 2):
        slab = qn[:, p * LANES:(p + 1) * LANES]
        q_ref[0, 2 * p] = jnp.where(low, slab, 0.0).astype(BF16)
        q_ref[0, 2 * p + 1] = jnp.where(low, 0.0, slab).astype(BF16)
        k_ref[0, p] = kn[:, p * LANES:(p + 1) * LANES].astype(BF16)
    vt_ref[0] = v.T.astype(BF16)

    pc = jnp.dot(hb, widx_ref[...], preferred_element_type=F32)
    for p in range(IDX_HEADS // 2):
        slab = pc[:, p * LANES:(p + 1) * LANES]
        qi_ref[0, 2 * p] = jnp.where(low, slab, 0.0).astype(BF16)
        qi_ref[0, 2 * p + 1] = jnp.where(low, 0.0, slab).astype(BF16)
    ki_ref[0] = pc[:, 2 * LANES:3 * LANES].astype(BF16)
    wit_ref[0] = _nt(wwi_ref[...], hb) * IDX_SCALE


def _inproj(x, mod3, g1, wa, wqkv, widx, wwi, gq, gk, e2, cw, cb, lng, lnb, goc):
    bsz, s, d = x.shape
    tm = TM_IN
    nt = s // tm
    full = lambda shape: pl.BlockSpec(shape, lambda b, j: (0,) * len(shape))
    out_shape = (
        jax.ShapeDtypeStruct((bsz, s, CONV_CH), BF16),
        jax.ShapeDtypeStruct((bsz, ATT_HEADS, s, LANES), BF16),
        jax.ShapeDtypeStruct((bsz, ATT_HEADS // 2, s, LANES), BF16),
        jax.ShapeDtypeStruct((bsz, ATT_W, s), BF16),
        jax.ShapeDtypeStruct((bsz, IDX_HEADS, s, LANES), BF16),
        jax.ShapeDtypeStruct((bsz, s, LANES), BF16),
        jax.ShapeDtypeStruct((bsz, SUBLANES, s), F32),
    )
    out_specs = (
        pl.BlockSpec((1, tm, CONV_CH), lambda b, j: (b, j, 0)),
        pl.BlockSpec((1, ATT_HEADS, tm, LANES), lambda b, j: (b, 0, j, 0)),
        pl.BlockSpec((1, ATT_HEADS // 2, tm, LANES), lambda b, j: (b, 0, j, 0)),
        pl.BlockSpec((1, ATT_W, tm), lambda b, j: (b, 0, j)),
        pl.BlockSpec((1, IDX_HEADS, tm, LANES), lambda b, j: (b, 0, j, 0)),
        pl.BlockSpec((1, tm, LANES), lambda b, j: (b, j, 0)),
        pl.BlockSpec((1, SUBLANES, tm), lambda b, j: (b, 0, j)),
    )
    return pl.pallas_call(
        _inproj_kernel,
        grid=(bsz, nt),
        in_specs=[
            pl.BlockSpec((1, tm, d), lambda b, j: (b, j, 0)),
            pl.BlockSpec((1, 6, d), lambda b, j: (b, 0, 0)),
            full(g1.shape), full(wa.shape), full(wqkv.shape), full(widx.shape), full(wwi.shape),
            full(gq.shape), full(gk.shape), full(e2.shape),
            full(cw.shape), full(cb.shape), full(lng.shape), full(lnb.shape), full(goc.shape),
        ],
        out_specs=out_specs,
        out_shape=out_shape,
        scratch_shapes=[pltpu.VMEM((tm + HALO, CONV_CH), F32)],
        compiler_params=pltpu.CompilerParams(
            dimension_semantics=("arbitrary", "arbitrary"), vmem_limit_bytes=VMEM_LIMIT),
        name="inproj",
    )(x, mod3, g1, wa, wqkv, widx, wwi, gq, gk, e2, cw, cb, lng, lnb, goc)


def _t5_bucket_np(rel):
    half = REL_BUCKETS // 2
    max_exact = half // 2
    ret = np.where(rel > 0, half, 0)
    n = np.abs(rel)
    nf = np.maximum(n, 1).astype(np.float64)
    large = max_exact + (np.log(nf / max_exact) / math.log(REL_MAX_DIST / max_exact)
                         * (half - max_exact)).astype(np.int32)
    large = np.minimum(large, half - 1)
    return (ret + np.where(n < max_exact, n, large)).astype(np.int32)


NEAR = REL_MAX_DIST + TQ
FAR_BUCKET = REL_BUCKETS // 2 - 1


def _near_buckets():
    r = np.arange(NEAR)[:, None]
    t = np.arange(TQ)[None, :]
    return _t5_bucket_np(r - REL_MAX_DIST - t)


def _attn_kernel(q_ref, k_ref, vt_ref, qi_ref, ki_ref, wit_ref, bkt_ref, rb_ref, goa_ref, o_ref,
                 keys_s, maskb_s, lg_s, relb_s, ot_s, *, nsel, jbits):
    b = pl.program_id(0)
    i = pl.program_id(1)
    t0 = i * TQ
    n_tiles = i + 1
    neg_inf = F32(-jnp.inf)

    @pl.when((b == 0) & (i == 0))
    def _():
        bk = bkt_ref[...]
        for h in range(ATT_HEADS):
            far = rb_ref[FAR_BUCKET, h]
            acc = jnp.zeros((NEAR, TQ), F32)
            for bb in range(REL_BUCKETS):
                acc = jnp.where(bk == bb, rb_ref[bb, h] - far, acc)
            relb_s[h, 0:2 * KT - NEAR, :] = jnp.zeros((2 * KT - NEAR, TQ), F32)
            relb_s[h, 2 * KT - NEAR:2 * KT, :] = acc

    qpos = t0 + lax.broadcasted_iota(I32, (1, TQ), 1)
    limit = (qpos // CHUNK + 1) * CHUNK
    row_iota = lax.broadcasted_iota(I32, (KT, TQ), 0)

    def tile_start(jt):
        return pl.multiple_of(jt * KT, KT)

    qia = qi_ref[0].reshape(IDX_HEADS * TQ, LANES)
    wi = wit_ref[0]

    def p1(jt, c):
        ks = tile_start(jt)
        lgt = _nt(ki_ref[0, pl.ds(ks, KT), :], qia)
        acc = jnp.zeros((KT, TQ), F32)
        for h in range(IDX_HEADS):
            acc = acc + jnp.maximum(lgt[:, h * TQ:(h + 1) * TQ], 0.0) * wi[h:h + 1, :]
        sc = jnp.where(row_iota + ks < limit, acc, neg_inf)
        bits = pltpu.bitcast(sc, I32)
        keys_s[pl.ds(ks, KT), :] = bits ^ (lax.shift_right_arithmetic(bits, 31) & 0x7FFFFFFF)
        return c

    lax.fori_loop(0, n_tiles, p1, 0)

    def count(pred):
        def body(jt, acc):
            ks = tile_start(jt)
            m = pred(keys_s[pl.ds(ks, KT), :], ks).astype(F32)
            return acc + jnp.sum(m.reshape(KT // SUBLANES, SUBLANES, TQ), axis=0)
        acc = lax.fori_loop(0, n_tiles, body, jnp.zeros((SUBLANES, TQ), F32))
        return jnp.sum(acc, axis=0, keepdims=True)

    def select_thr():
        c0 = count(lambda kt, ks: kt >= 0)
        t = jnp.where(c0 >= nsel, 0, INT_MIN).astype(I32)

        def bitstep(it, t):
            cand = t + lax.shift_left(I32(1), I32(30) - it)
            c = count(lambda kt, ks: kt >= cand)
            return jnp.where(c >= nsel, cand, t)

        return lax.fori_loop(0, 31, bitstep, t)

    thr = lax.cond(i > 0, select_thr, lambda: jnp.full((1, TQ), KEY_NEG_INF + 1, I32))

    c_gt = count(lambda kt, ks: kt > thr)
    c_ge = count(lambda kt, ks: kt >= thr)
    need = nsel - c_gt

    def tie_cut():
        def step(it, cut):
            cand = cut + lax.shift_left(I32(1), I32(jbits - 1) - it)
            f = count(lambda kt, ks: jnp.where(kt == thr, row_iota + ks, cand) < cand)
            return jnp.where(f <= need, cand, cut)
        return lax.fori_loop(0, jbits, step, jnp.zeros((1, TQ), I32))

    cut = lax.cond(jnp.max(c_ge) > nsel, tie_cut, lambda: jnp.full((1, TQ), 2 ** jbits - 1, I32))

    def p3(jt, c):
        ks = tile_start(jt)
        kt = keys_s[pl.ds(ks, KT), :]
        tied = jnp.where(kt == thr, jnp.where(row_iota + ks < cut, 0.0, neg_inf), neg_inf)
        maskb_s[pl.ds(ks, KT), :] = jnp.where(kt > thr, 0.0, tied)
        return c

    lax.fori_loop(0, n_tiles, p3, 0)

    n_far = jnp.maximum(n_tiles - 2, 0)
    for h in range(ATT_HEADS):
        qz = q_ref[0, h]

        def logits(jt):
            ks = tile_start(jt)
            s = _nt(k_ref[0, h // 2, pl.ds(ks, KT), :], qz)
            return ks, s + maskb_s[pl.ds(ks, KT), :]

        def tile_max(l):
            return jnp.max(l.reshape(KT // SUBLANES, SUBLANES, TQ), axis=0)

        def pa_far(jt, m):
            ks, l = logits(jt)
            lg_s[pl.ds(ks, KT), :] = l
            return jnp.maximum(m, tile_max(l))

        def pa_near(jt, m):
            ks, l = logits(jt)
            off = pl.multiple_of((jt - (n_tiles - 2)) * KT, KT)
            l = l + relb_s[h, pl.ds(off, KT), :]
            lg_s[pl.ds(ks, KT), :] = l
            return jnp.maximum(m, tile_max(l))

        m8 = lax.fori_loop(0, n_far, pa_far, jnp.full((SUBLANES, TQ), neg_inf, F32))
        m8 = lax.fori_loop(n_far, n_tiles, pa_near, m8)
        m = jnp.max(m8, axis=0, keepdims=True)

        def pb(jt, carry):
            s8, o = carry
            ks = tile_start(jt)
            p = jnp.exp(lg_s[pl.ds(ks, KT), :] - m)
            s8 = s8 + jnp.sum(p.reshape(KT // SUBLANES, SUBLANES, TQ), axis=0)
            vt = vt_ref[0, h * HEAD_DIM:(h + 1) * HEAD_DIM, pl.ds(ks, KT)]
            o = o + jnp.dot(vt, p.astype(BF16), preferred_element_type=F32)
            return s8, o

        s8, o = lax.fori_loop(0, n_tiles, pb, (jnp.zeros((SUBLANES, TQ), F32), jnp.zeros((HEAD_DIM, TQ), F32)))
        ot_s[h * HEAD_DIM:(h + 1) * HEAD_DIM, :] = o / jnp.sum(s8, axis=0, keepdims=True)

    ot = ot_s[...]
    ms = jnp.mean(ot * ot, axis=0, keepdims=True)
    y = (ot * lax.rsqrt(ms + EPS)).T * goa_ref[...]
    o_ref[0] = y.astype(BF16)


def _attn(q, k, vt, qi, ki, wit, rel_bias, goa):
    bsz, _, s, _ = q.shape
    nsel = min(TOPK_MAX, s // 4)
    assert nsel == TQ and s % TQ == 0, "attention kernel assumes TOPK_MAX-sized query blocks"
    jbits = int(s).bit_length()
    bkt = jnp.asarray(_near_buckets())
    kern = functools.partial(_attn_kernel, nsel=nsel, jbits=jbits)
    return pl.pallas_call(
        kern,
        grid=(bsz, s // TQ),
        in_specs=[
            pl.BlockSpec((1, ATT_HEADS, TQ, LANES), lambda b, i: (b, 0, i, 0)),
            pl.BlockSpec((1, ATT_HEADS // 2, s, LANES), lambda b, i: (b, 0, 0, 0)),
            pl.BlockSpec((1, ATT_W, s), lambda b, i: (b, 0, 0)),
            pl.BlockSpec((1, IDX_HEADS, TQ, LANES), lambda b, i: (b, 0, i, 0)),
            pl.BlockSpec((1, s, LANES), lambda b, i: (b, 0, 0)),
            pl.BlockSpec((1, SUBLANES, TQ), lambda b, i: (b, 0, i)),
            pl.BlockSpec((NEAR, TQ), lambda b, i: (0, 0)),
            pl.BlockSpec(memory_space=pltpu.SMEM),
            pl.BlockSpec((1, ATT_W), lambda b, i: (0, 0)),
        ],
        out_specs=pl.BlockSpec((1, TQ, ATT_W), lambda b, i: (b, i, 0)),
        out_shape=jax.ShapeDtypeStruct((bsz, s, ATT_W), BF16),
        scratch_shapes=[
            pltpu.VMEM((s, TQ), I32),
            pltpu.VMEM((s, TQ), F32),
            pltpu.VMEM((s, TQ), F32),
            pltpu.VMEM((ATT_HEADS, 2 * KT, TQ), F32),
            pltpu.VMEM((ATT_W, TQ), F32),
        ],
        compiler_params=pltpu.CompilerParams(
            dimension_semantics=("arbitrary", "arbitrary"), vmem_limit_bytes=VMEM_LIMIT),
        name="attn",
    )(q, k, vt, qi, ki, wit, bkt, rel_bias, goa)


def _topk_rows(s, payload, k):
    nrows = s.shape[0]
    rows = lax.broadcasted_iota(I32, s.shape, 0).astype(F32)
    vals, pays = [], []
    for _ in range(k):
        m = jnp.max(s, axis=0, keepdims=True)
        ix = jnp.min(jnp.where(s == m, rows, float(nrows)), axis=0, keepdims=True)
        hit = rows == ix
        if payload is None:
            pays.append(ix)
        else:
            pays.append(jnp.max(jnp.where(hit, payload, -1.0), axis=0, keepdims=True))
        vals.append(m)
        s = jnp.where(hit, -jnp.inf, s)
    return jnp.concatenate(vals, axis=0), jnp.concatenate(pays, axis=0)


def _mid_kernel(x_ref, cn_ref, an_ref, mod_ref, wo1_ref, wo2_ref, g2_ref, wpq_ref, k1_ref, k2_ref,
                x1_ref, h2_ref, idx_ref, gate_ref, qq_s, idt_s, gt_s):
    tm = x_ref.shape[1]
    x = x_ref[0]
    gt1 = mod_ref[0, 2:3, :]
    sh2 = mod_ref[0, 3:4, :]
    sc2 = mod_ref[0, 4:5, :]
    proj = (jnp.dot(cn_ref[0], wo1_ref[...], preferred_element_type=F32)
            + jnp.dot(an_ref[0], wo2_ref[...], preferred_element_type=F32))
    x1 = x + gt1 * proj
    x1_ref[0] = x1
    r = lax.rsqrt(jnp.mean(x1 * x1, axis=-1, keepdims=True) + EPS)
    h2 = (x1 * r) * g2_ref[...] * (1.0 + sc2) + sh2
    h2_ref[0] = h2
    qq_s[...] = jnp.dot(h2.astype(BF16), wpq_ref[...], preferred_element_type=F32).astype(BF16)

    def route(it, c):
        hh = it // (tm // LANES)
        lt = it % (tm // LANES)
        rows = pl.ds(pl.multiple_of(lt * LANES, LANES), LANES)
        q1 = qq_s[rows, pl.ds(pl.multiple_of(hh * 2 * N_KEYS, LANES), N_KEYS)]
        q2 = qq_s[rows, pl.ds(pl.multiple_of(hh * 2 * N_KEYS + N_KEYS, LANES), N_KEYS)]
        v1, i1 = _topk_rows(_nt(k1_ref[hh], q1), None, PEER_TOPK)
        v2, i2 = _topk_rows(_nt(k2_ref[hh], q2), None, PEER_TOPK)
        cand = jnp.concatenate([v1[a:a + 1, :] + v2 for a in range(PEER_TOPK)], axis=0)
        cidx = jnp.concatenate([i1[a:a + 1, :] * float(N_KEYS) + i2 for a in range(PEER_TOPK)], axis=0)
        best, experts = _topk_rows(cand, cidx, PEER_TOPK)
        e = jnp.exp(best - best[0:1, :])
        g = e / jnp.sum(e, axis=0, keepdims=True)
        slots = pl.ds(pl.multiple_of(hh * PEER_TOPK, PEER_TOPK), PEER_TOPK)
        cols = pl.ds(pl.multiple_of(lt * LANES, LANES), LANES)
        idt_s[slots, cols] = experts * float(ROWS_PER_EXPERT)
        gt_s[slots, cols] = g
        return c

    lax.fori_loop(0, PEER_HEADS * (tm // LANES), route, 0)
    idx_ref[0] = idt_s[...].T.astype(I32)
    gate_ref[0] = gt_s[...].T


def _mid(x, cn, an, mod3, wo1, wo2, g2, wpq, k1, k2):
    bsz, s, d = x.shape
    tm = TM_MID
    full = lambda shape: pl.BlockSpec(shape, lambda b, j: (0,) * len(shape))
    tok = lambda w: pl.BlockSpec((1, tm, w), lambda b, j: (b, j, 0))
    return pl.pallas_call(
        _mid_kernel,
        grid=(bsz, s // tm),
        in_specs=[tok(d), tok(CONV_CH), tok(ATT_W), pl.BlockSpec((1, 6, d), lambda b, j: (b, 0, 0)),
                  full(wo1.shape), full(wo2.shape), full(g2.shape), full(wpq.shape), full(k1.shape), full(k2.shape)],
        out_specs=(tok(d), tok(d), tok(PEER_SLOTS), tok(PEER_SLOTS)),
        out_shape=(
            jax.ShapeDtypeStruct((bsz, s, d), F32),
            jax.ShapeDtypeStruct((bsz, s, d), F32),
            jax.ShapeDtypeStruct((bsz, s, PEER_SLOTS), I32),
            jax.ShapeDtypeStruct((bsz, s, PEER_SLOTS), F32),
        ),
        scratch_shapes=[
            pltpu.VMEM((tm, PEER_HEADS * 2 * N_KEYS), BF16),
            pltpu.VMEM((PEER_SLOTS, tm), F32),
            pltpu.VMEM((PEER_SLOTS, tm), F32),
        ],
        compiler_params=pltpu.CompilerParams(
            dimension_semantics=("arbitrary", "arbitrary"), vmem_limit_bytes=VMEM_LIMIT),
        name="mid",
    )(x, cn, an, mod3, wo1, wo2, g2, wpq, k1, k2)


def _pack_table(t):
    tb = t.astype(BF16)
    lo = lax.bitcast_convert_type(tb[:, :HALF], jnp.uint16).astype(jnp.uint32)
    hi = lax.bitcast_convert_type(tb[:, HALF:], jnp.uint16).astype(jnp.uint32)
    return (lo | (hi << 16)).reshape(t.shape[0] * ROWS_PER_EXPERT, LANES)


def _unpack(w):
    lo = pltpu.bitcast(lax.shift_left(w, jnp.uint32(16)), F32)
    hi = pltpu.bitcast(w & jnp.uint32(0xFFFF0000), F32)
    return lo, hi


def _gather_row(tab_ref, row):
    return tab_ref[pl.ds(pl.multiple_of(row, ROWS_PER_EXPERT), ROWS_PER_EXPERT), :]


STAGE_ROWS = PEER_SLOTS * ROWS_PER_EXPERT


def _peer_u_kernel(idx_ref, tab_ref, x_ref, gate_ref, w_ref, *stages):
    tb = w_ref.shape[0]
    ones = jnp.ones((SUBLANES, LANES), F32)
    sub = lax.broadcasted_iota(I32, (SUBLANES, PEER_SLOTS), 0)

    def group(g, c):
        t0 = pl.multiple_of(g * SUBLANES, SUBLANES)
        act8 = jnp.zeros((SUBLANES, PEER_SLOTS), F32)
        for r in range(SUBLANES):
            t = t0 + r
            xv = x_ref[pl.ds(pl.multiple_of(t * SUBLANES, SUBLANES), SUBLANES), :]
            xlo = jnp.concatenate([xv[0:ROWS_PER_EXPERT]] * 2, axis=0)
            xhi = jnp.concatenate([xv[ROWS_PER_EXPERT:]] * 2, axis=0)
            stage = stages[r]
            for j in range(0, PEER_SLOTS, 2):
                pair = jnp.concatenate(
                    [_gather_row(tab_ref, idx_ref[t, j]), _gather_row(tab_ref, idx_ref[t, j + 1])], axis=0)
                lo, hi = _unpack(pair)
                row = j * ROWS_PER_EXPERT
                stage[row:row + 2 * ROWS_PER_EXPERT, :] = lo * xlo + hi * xhi
            cs = stage[pl.ds(0, PEER_SLOTS, stride=ROWS_PER_EXPERT), :]
            for q in range(1, ROWS_PER_EXPERT):
                cs = cs + stage[pl.ds(q, PEER_SLOTS, stride=ROWS_PER_EXPERT), :]
            act = jnp.sum(cs.T, axis=0, keepdims=True)
            act8 = jnp.where(sub == r, act, act8)
        rows = pl.ds(t0, SUBLANES)
        w_ref[rows, :] = gate_ref[rows, :] * jax.nn.gelu(act8)
        return c

    lax.fori_loop(0, tb // SUBLANES, group, 0)


def _peer_u(idx, tab, h2_rows, gate):
    n = idx.shape[0]
    tb = TB_PEER
    return pl.pallas_call(
        _peer_u_kernel,
        grid=(n // tb,),
        in_specs=[
            pl.BlockSpec((tb, PEER_SLOTS), lambda i: (i, 0), memory_space=pltpu.SMEM),
            pl.BlockSpec(tab.shape, lambda i: (0, 0), pipeline_mode=pl.Buffered(1)),
            pl.BlockSpec((tb * SUBLANES, LANES), lambda i: (i, 0)),
            pl.BlockSpec((tb, PEER_SLOTS), lambda i: (i, 0)),
        ],
        out_specs=pl.BlockSpec((tb, PEER_SLOTS), lambda i: (i, 0)),
        out_shape=jax.ShapeDtypeStruct((n, PEER_SLOTS), F32),
        scratch_shapes=[pltpu.VMEM((STAGE_ROWS, LANES), F32) for _ in range(SUBLANES)],
        compiler_params=pltpu.CompilerParams(
            dimension_semantics=("arbitrary",), vmem_limit_bytes=VMEM_LIMIT),
        name="peer_u",
    )(idx, tab, h2_rows, gate)


def _peer_v_kernel(idx_ref, w_ref, tab_ref, x1_ref, gt2_ref, o_ref):
    tb = idx_ref.shape[0]
    gt2 = gt2_ref[0]

    def token(t, c):
        acc_lo = jnp.zeros((ROWS_PER_EXPERT, LANES), F32)
        acc_hi = jnp.zeros((ROWS_PER_EXPERT, LANES), F32)
        for j in range(PEER_SLOTS):
            lo, hi = _unpack(_gather_row(tab_ref, idx_ref[t, j]))
            wj = w_ref[t, j]
            acc_lo = acc_lo + wj * lo
            acc_hi = acc_hi + wj * hi
        rows = pl.ds(pl.multiple_of(t * SUBLANES, SUBLANES), SUBLANES)
        o_ref[rows, :] = x1_ref[rows, :] + gt2 * jnp.concatenate([acc_lo, acc_hi], axis=0)
        return c

    lax.fori_loop(0, tb, token, 0)


def _peer_v(idx, w, tab, x1_rows, gt2_rows, blocks_per_batch):
    n = idx.shape[0]
    tb = TB_PEER
    return pl.pallas_call(
        _peer_v_kernel,
        grid=(n // tb,),
        in_specs=[
            pl.BlockSpec((tb, PEER_SLOTS), lambda i: (i, 0), memory_space=pltpu.SMEM),
            pl.BlockSpec((tb, PEER_SLOTS), lambda i: (i, 0), memory_space=pltpu.SMEM),
            pl.BlockSpec(tab.shape, lambda i: (0, 0), pipeline_mode=pl.Buffered(1)),
            pl.BlockSpec((tb * SUBLANES, LANES), lambda i: (i, 0)),
            pl.BlockSpec((1, SUBLANES, LANES), lambda i: (i // blocks_per_batch, 0, 0)),
        ],
        out_specs=pl.BlockSpec((tb * SUBLANES, LANES), lambda i: (i, 0)),
        out_shape=jax.ShapeDtypeStruct((n * SUBLANES, LANES), F32),
        compiler_params=pltpu.CompilerParams(
            dimension_semantics=("arbitrary",), vmem_limit_bytes=VMEM_LIMIT),
        name="peer_v",
    )(idx, w, tab, x1_rows, gt2_rows)


def _layer(x, mod, g_norm1, g_norm2, w_in, q_norm_g, k_norm_g, conv_w, conv_b, conv_ln_g, conv_ln_b,
           rel_bias, g_out_conv, g_out_attn, w_out, w_peer_q, peer_k1, peer_k2, peer_u, peer_v):
    bsz, s, d = x.shape
    n = bsz * s
    mod3 = mod.reshape(bsz, 6, d)
    row = lambda a: a.reshape(1, -1)

    c0 = 2 * CONV_CH
    c1 = c0 + 3 * ATT_W
    c2 = c1 + IDX_HEADS * IDX_DIM
    c3 = c2 + IDX_DIM
    wa = w_in[:, :c0].astype(BF16)
    wqkv = w_in[:, c0:c1].astype(BF16)
    widx = jnp.concatenate([w_in[:, c1:c2], w_in[:, c2:c3], w_in[:, c2:c3]], axis=1).astype(BF16)
    wwi = jnp.zeros((SUBLANES, d), F32).at[:IDX_HEADS].set(w_in[:, c3:c3 + IDX_HEADS].T).astype(BF16)
    head = np.arange(ATT_W) // HEAD_DIM
    e2 = jnp.asarray((head[:, None] == head[None, :]).astype(np.float32))

    conv_n, q, k, vt, qi, ki, wit = _inproj(
        x, mod3, row(g_norm1), wa, wqkv, widx, wwi,
        row(jnp.tile(q_norm_g, ATT_HEADS)), row(jnp.tile(k_norm_g, ATT_HEADS)), e2,
        conv_w.reshape(CONV_WIDTH, CONV_CH), row(conv_b), row(conv_ln_g), row(conv_ln_b), row(g_out_conv))
    attn_n = _attn(q, k, vt, qi, ki, wit, rel_bias, row(g_out_attn))

    x1, h2, idx, gate = _mid(
        x, conv_n, attn_n, mod3, w_out[:CONV_CH].astype(BF16), w_out[CONV_CH:].astype(BF16), row(g_norm2),
        w_peer_q.astype(BF16), peer_k1.astype(BF16), peer_k2.astype(BF16))

    idx = idx.reshape(n, PEER_SLOTS)
    w = _peer_u(idx, _pack_table(peer_u), h2.reshape(n * SUBLANES, LANES), gate.reshape(n, PEER_SLOTS))
    gt2_rows = mod3[:, 5, :].reshape(bsz, SUBLANES, LANES)
    out = _peer_v(idx, w, _pack_table(peer_v), x1.reshape(n * SUBLANES, LANES), gt2_rows, s // TB_PEER)
    return out.reshape(bsz, s, d)


def kernel(x, c, w_ada, b_ada, g_norm1, g_norm2, w_in, q_norm_g, k_norm_g, conv_w, conv_b, conv_ln_g,
           conv_ln_b, rel_bias, g_out_conv, g_out_attn, w_out, w_peer_q, peer_k1, peer_k2, peer_u, peer_v):
    depth = w_ada.shape[0]
    for l in range(depth):
        mod = _ada(c, w_ada[l], b_ada[l])
        x = _layer(x, mod, g_norm1[l], g_norm2[l], w_in[l], q_norm_g[l], k_norm_g[l], conv_w[l], conv_b[l],
                   conv_ln_g[l], conv_ln_b[l], rel_bias, g_out_conv[l], g_out_attn[l], w_out[l],
                   w_peer_q[l], peer_k1[l], peer_k2[l], peer_u[l], peer_v[l])
    return x
```

```python
import functools
import math

import numpy as np
import jax
import jax.numpy as jnp
from jax import lax
from jax.experimental import pallas as pl
from jax.experimental.pallas import tpu as pltpu

F32 = jnp.float32
BF16 = jnp.bfloat16
I32 = jnp.int32
HIGHEST = lax.Precision.HIGHEST

D_MODEL = 1024
CHUNK = 64
CONV_CH = 512
CONV_WIDTH = 31
ATT_HEADS = 8
HEAD_DIM = 64
ATT_W = ATT_HEADS * HEAD_DIM
IDX_HEADS = 4
IDX_DIM = 64
IDX_SCALE = (IDX_HEADS * IDX_DIM) ** -0.5
TOPK_MAX = 256
REL_BUCKETS = 32
REL_MAX_DIST = 128
PEER_HEADS = 8
N_KEYS = 128
N_EXPERTS = N_KEYS * N_KEYS
PEER_TOPK = 16
PEER_SLOTS = PEER_HEADS * PEER_TOPK
EPS = 1e-6

LANES = 128
SUBLANES = 8
VMEM_LIMIT = 56 * 1024 * 1024

TM_IN = 512
CONV_ROWS = 64
HALO = 32
TQ = 256
KT = 256
TM_MID = 256
TB_PEER = 128
HALF = D_MODEL // 2
ROWS_PER_EXPERT = HALF // LANES

NT_DIMS = (((1,), (1,)), ((), ()))

_NEG_INF_BITS = int(np.array(-np.inf, np.float32).view(np.int32))
KEY_NEG_INF = _NEG_INF_BITS ^ 0x7FFFFFFF
INT_MIN = -(2 ** 31)


def _nt(a, b, precision=None):
    return lax.dot_general(a, b, NT_DIMS, precision=precision, preferred_element_type=F32)


def _ada_kernel(c_ref, w_ref, b_ref, o_ref):
    a = jax.nn.silu(c_ref[...])
    o_ref[...] = jnp.dot(a, w_ref[...], precision=HIGHEST, preferred_element_type=F32) + b_ref[...]


def _ada(c, w_ada, b_ada):
    bsz, d = c.shape
    return pl.pallas_call(
        _ada_kernel,
        grid=(6,),
        in_specs=[
            pl.BlockSpec((bsz, d), lambda j: (0, 0)),
            pl.BlockSpec((d, d), lambda j: (0, j)),
            pl.BlockSpec((1, d), lambda j: (0, j)),
        ],
        out_specs=pl.BlockSpec((bsz, d), lambda j: (0, j)),
        out_shape=jax.ShapeDtypeStruct((bsz, 6 * d), F32),
        name="ada",
    )(c, w_ada, b_ada.reshape(1, 6 * d))


def _inproj_kernel(x_ref, mod_ref, g1_ref, wa_ref, wqkv_ref, widx_ref, wwi_ref, gq_ref, gk_ref, e2_ref,
                   cw_ref, cb_ref, lng_ref, lnb_ref, goc_ref,
                   conv_ref, q_ref, k_ref, vt_ref, qi_ref, ki_ref, wit_ref, ubuf):
    j = pl.program_id(1)
    tm = x_ref.shape[1]
    x = x_ref[0]
    sh1 = mod_ref[0, 0:1, :]
    sc1 = mod_ref[0, 1:2, :]
    r = lax.rsqrt(jnp.mean(x * x, axis=-1, keepdims=True) + EPS)
    h = (x * r) * g1_ref[...] * (1.0 + sc1) + sh1
    hb = h.astype(BF16)

    pa = jnp.dot(hb, wa_ref[...], preferred_element_type=F32)
    u = pa[:, :CONV_CH] * jax.nn.sigmoid(pa[:, CONV_CH:])

    @pl.when(j == 0)
    def _():
        ubuf[0:HALO, :] = jnp.zeros((HALO, CONV_CH), F32)

    ubuf[HALO:HALO + tm, :] = u
    first = HALO - (CONV_WIDTH - 1)
    for rb in range(tm // CONV_ROWS):
        base = rb * CONV_ROWS
        acc = jnp.zeros((CONV_ROWS, CONV_CH), F32) + cb_ref[...]
        for t in range(CONV_WIDTH):
            acc = acc + cw_ref[t:t + 1, :] * ubuf[base + first + t:base + first + t + CONV_ROWS, :]
        mu = jnp.mean(acc, axis=-1, keepdims=True)
        xc = acc - mu
        y = xc * lax.rsqrt(jnp.mean(xc * xc, axis=-1, keepdims=True) + EPS)
        y = jax.nn.silu(y * lng_ref[...] + lnb_ref[...])
        y = y * lax.rsqrt(jnp.mean(y * y, axis=-1, keepdims=True) + EPS) * goc_ref[...]
        conv_ref[0, base:base + CONV_ROWS, :] = y.astype(BF16)
    ubuf[0:HALO, :] = ubuf[tm:tm + HALO, :]

    pq = jnp.dot(hb, wqkv_ref[...], preferred_element_type=F32)
    q = pq[:, :ATT_W]
    k = pq[:, ATT_W:2 * ATT_W]
    v = pq[:, 2 * ATT_W:]
    e2 = e2_ref[...]
    qs = jnp.dot(q * q, e2, precision=HIGHEST, preferred_element_type=F32) * (1.0 / HEAD_DIM)
    ks = jnp.dot(k * k, e2, precision=HIGHEST, preferred_element_type=F32) * (1.0 / HEAD_DIM)
    qn = q * lax.rsqrt(qs + EPS) * gq_ref[...] * (HEAD_DIM ** -0.5)
    kn = k * lax.rsqrt(ks + EPS) * gk_ref[...]
    lane = lax.broadcasted_iota(I32, (tm, LANES), 1)
    low = lane < HEAD_DIM
    for p in range(ATT_HEADS // 2):
        slab = qn[:, p * LANES:(p + 1) * LANES]
        q_ref[0, 2 * p] = jnp.where(low, slab, 0.0).T.astype(BF16)
        q_ref[0, 2 * p + 1] = jnp.where(low, 0.0, slab).T.astype(BF16)
        k_ref[0, p] = kn[:, p * LANES:(p + 1) * LANES].astype(BF16)
    vt_ref[0] = v.T.astype(BF16)

    pc = jnp.dot(hb, widx_ref[...], preferred_element_type=F32)
    for p in range(IDX_HEADS // 2):
        slab = pc[:, p * LANES:(p + 1) * LANES]
        qi_ref[0, 2 * p] = jnp.where(low, slab, 0.0).T.astype(BF16)
        qi_ref[0, 2 * p + 1] = jnp.where(low, 0.0, slab).T.astype(BF16)
    ki_ref[0] = pc[:, 2 * LANES:3 * LANES].astype(BF16)
    wit_ref[0] = _nt(wwi_ref[...], hb) * IDX_SCALE


def _inproj(x, mod3, g1, wa, wqkv, widx, wwi, gq, gk, e2, cw, cb, lng, lnb, goc):
    bsz, s, d = x.shape
    tm = TM_IN
    nt = s // tm
    full = lambda shape: pl.BlockSpec(shape, lambda b, j: (0,) * len(shape))
    out_shape = (
        jax.ShapeDtypeStruct((bsz, s, CONV_CH), BF16),
        jax.ShapeDtypeStruct((bsz, ATT_HEADS, LANES, s), BF16),
        jax.ShapeDtypeStruct((bsz, ATT_HEADS // 2, s, LANES), BF16),
        jax.ShapeDtypeStruct((bsz, ATT_W, s), BF16),
        jax.ShapeDtypeStruct((bsz, IDX_HEADS, LANES, s), BF16),
        jax.ShapeDtypeStruct((bsz, s, LANES), BF16),
        jax.ShapeDtypeStruct((bsz, SUBLANES, s), F32),
    )
    out_specs = (
        pl.BlockSpec((1, tm, CONV_CH), lambda b, j: (b, j, 0)),
        pl.BlockSpec((1, ATT_HEADS, LANES, tm), lambda b, j: (b, 0, 0, j)),
        pl.BlockSpec((1, ATT_HEADS // 2, tm, LANES), lambda b, j: (b, 0, j, 0)),
        pl.BlockSpec((1, ATT_W, tm), lambda b, j: (b, 0, j)),
        pl.BlockSpec((1, IDX_HEADS, LANES, tm), lambda b, j: (b, 0, 0, j)),
        pl.BlockSpec((1, tm, LANES), lambda b, j: (b, j, 0)),
        pl.BlockSpec((1, SUBLANES, tm), lambda b, j: (b, 0, j)),
    )
    return pl.pallas_call(
        _inproj_kernel,
        grid=(bsz, nt),
        in_specs=[
            pl.BlockSpec((1, tm, d), lambda b, j: (b, j, 0)),
            pl.BlockSpec((1, 6, d), lambda b, j: (b, 0, 0)),
            full(g1.shape), full(wa.shape), full(wqkv.shape), full(widx.shape), full(wwi.shape),
            full(gq.shape), full(gk.shape), full(e2.shape),
            full(cw.shape), full(cb.shape), full(lng.shape), full(lnb.shape), full(goc.shape),
        ],
        out_specs=out_specs,
        out_shape=out_shape,
        scratch_shapes=[pltpu.VMEM((tm + HALO, CONV_CH), F32)],
        compiler_params=pltpu.CompilerParams(
            dimension_semantics=("arbitrary", "arbitrary"), vmem_limit_bytes=VMEM_LIMIT),
        name="inproj",
    )(x, mod3, g1, wa, wqkv, widx, wwi, gq, gk, e2, cw, cb, lng, lnb, goc)


def _t5_bucket_np(rel):
    half = REL_BUCKETS // 2
    max_exact = half // 2
    ret = np.where(rel > 0, half, 0)
    n = np.abs(rel)
    nf = np.maximum(n, 1).astype(np.float64)
    large = max_exact + (np.log(nf / max_exact) / math.log(REL_MAX_DIST / max_exact)
                         * (half - max_exact)).astype(np.int32)
    large = np.minimum(large, half - 1)
    return (ret + np.where(n < max_exact, n, large)).astype(np.int32)


NEAR = REL_MAX_DIST + TQ
FAR_BUCKET = REL_BUCKETS // 2 - 1


def _near_buckets():
    r = np.arange(NEAR)[:, None]
    t = np.arange(TQ)[None, :]
    return _t5_bucket_np(r - REL_MAX_DIST - t)


def _attn_kernel(q_ref, k_ref, vt_ref, qi_ref, ki_ref, wit_ref, bkt_ref, rb_ref, goa_ref, o_ref,
                 keys_s, maskb_s, relb_s, ot_s, m_s, mo_s, l_s, lg_s, *, nsel, jbits):
    b = pl.program_id(0)
    i = pl.program_id(1)
    t0 = i * TQ
    n_tiles = i + 1
    neg_inf = F32(-jnp.inf)

    @pl.when((b == 0) & (i == 0))
    def _():
        bk = bkt_ref[...]
        for h in range(ATT_HEADS):
            far = rb_ref[FAR_BUCKET, h]
            acc = jnp.zeros((NEAR, TQ), F32)
            for bb in range(REL_BUCKETS):
                acc = jnp.where(bk == bb, rb_ref[bb, h] - far, acc)
            relb_s[h, 0:2 * KT - NEAR, :] = jnp.zeros((2 * KT - NEAR, TQ), F32)
            relb_s[h, 2 * KT - NEAR:2 * KT, :] = acc

    qpos = t0 + lax.broadcasted_iota(I32, (1, TQ), 1)
    limit = (qpos // CHUNK + 1) * CHUNK
    row_iota = lax.broadcasted_iota(I32, (KT, TQ), 0)

    def tile_start(jt):
        return pl.multiple_of(jt * KT, KT)

    wi = wit_ref[0]

    def p1(jt, c):
        ks = tile_start(jt)
        kit = ki_ref[0, pl.ds(ks, KT), :]
        acc = jnp.zeros((KT, TQ), F32)
        for h in range(IDX_HEADS):
            lgt = jnp.dot(kit, qi_ref[0, h], preferred_element_type=F32)
            acc = acc + jnp.maximum(lgt, 0.0) * wi[h:h + 1, :]
        sc = jnp.where(row_iota + ks < limit, acc, neg_inf)
        bits = pltpu.bitcast(sc, I32)
        keys_s[pl.ds(ks, KT), :] = bits ^ (lax.shift_right_arithmetic(bits, 31) & 0x7FFFFFFF)
        return c

    lax.fori_loop(0, n_tiles, p1, 0)

    def count(pred):
        def body(jt, acc):
            ks = tile_start(jt)
            m = pred(keys_s[pl.ds(ks, KT), :], ks).astype(F32)
            return acc + jnp.sum(m.reshape(KT // SUBLANES, SUBLANES, TQ), axis=0)
        acc = lax.fori_loop(0, n_tiles, body, jnp.zeros((SUBLANES, TQ), F32))
        return jnp.sum(acc, axis=0, keepdims=True)

    def select_thr():
        c0 = count(lambda kt, ks: kt >= 0)
        t = jnp.where(c0 >= nsel, 0, INT_MIN).astype(I32)

        def bitstep(it, t):
            cand = t + lax.shift_left(I32(1), I32(30) - it)
            c = count(lambda kt, ks: kt >= cand)
            return jnp.where(c >= nsel, cand, t)

        return lax.fori_loop(0, 31, bitstep, t)

    thr = lax.cond(i > 0, select_thr, lambda: jnp.full((1, TQ), KEY_NEG_INF + 1, I32))

    c_gt = count(lambda kt, ks: kt > thr)
    c_ge = count(lambda kt, ks: kt >= thr)
    need = nsel - c_gt

    def tie_cut():
        def step(it, cut):
            cand = cut + lax.shift_left(I32(1), I32(jbits - 1) - it)
            f = count(lambda kt, ks: jnp.where(kt == thr, row_iota + ks, cand) < cand)
            return jnp.where(f <= need, cand, cut)
        return lax.fori_loop(0, jbits, step, jnp.zeros((1, TQ), I32))

    cut = lax.cond(jnp.max(c_ge) > nsel, tie_cut, lambda: jnp.full((1, TQ), 2 ** jbits - 1, I32))

    def p3(jt, c):
        ks = tile_start(jt)
        kt = keys_s[pl.ds(ks, KT), :]
        tied = jnp.where(kt == thr, jnp.where(row_iota + ks < cut, 0.0, neg_inf), neg_inf)
        maskb_s[pl.ds(ks, KT), :] = jnp.where(kt > thr, 0.0, tied)
        return c

    lax.fori_loop(0, n_tiles, p3, 0)

    m_s[...] = jnp.full((ATT_HEADS, TQ), neg_inf, F32)
    l_s[...] = jnp.zeros((ATT_HEADS, TQ), F32)
    ot_s[...] = jnp.zeros((ATT_W, TQ), F32)

    def att_tile(jt, near):
        ks = tile_start(jt)
        mb = maskb_s[pl.ds(ks, KT), :]
        for h in range(ATT_HEADS):
            l = jnp.dot(k_ref[0, h // 2, pl.ds(ks, KT), :], q_ref[0, h], preferred_element_type=F32) + mb
            if near:
                off = pl.multiple_of((jt - (n_tiles - 2)) * KT, KT)
                l = l + relb_s[h, pl.ds(off, KT), :]
            lg_s[h] = l
            m_old = m_s[h:h + 1, :]
            mo_s[h:h + 1, :] = m_old
            m_s[h:h + 1, :] = jnp.maximum(m_old, jnp.max(l, axis=0, keepdims=True))
        for h in range(ATT_HEADS):
            m_new = m_s[h:h + 1, :]
            m_ref = jnp.where(m_new == neg_inf, 0.0, m_new)
            alpha = jnp.exp(mo_s[h:h + 1, :] - m_ref)
            p = jnp.exp(lg_s[h] - m_ref)
            l_s[h:h + 1, :] = alpha * l_s[h:h + 1, :] + jnp.sum(p, axis=0, keepdims=True)
            vt = vt_ref[0, h * HEAD_DIM:(h + 1) * HEAD_DIM, pl.ds(ks, KT)]
            rows = slice(h * HEAD_DIM, (h + 1) * HEAD_DIM)
            ot_s[rows, :] = ot_s[rows, :] * alpha + jnp.dot(vt, p.astype(BF16), preferred_element_type=F32)

    def far_tile(jt, c):
        att_tile(jt, False)
        return c

    def near_tile(jt, c):
        att_tile(jt, True)
        return c

    n_far = jnp.maximum(n_tiles - 2, 0)
    lax.fori_loop(0, n_far, far_tile, 0)
    lax.fori_loop(n_far, n_tiles, near_tile, 0)
    for h in range(ATT_HEADS):
        rows = slice(h * HEAD_DIM, (h + 1) * HEAD_DIM)
        ot_s[rows, :] = ot_s[rows, :] / l_s[h:h + 1, :]

    ot = ot_s[...]
    ms = jnp.mean(ot * ot, axis=0, keepdims=True)
    y = (ot * lax.rsqrt(ms + EPS)).T * goa_ref[...]
    o_ref[0] = y.astype(BF16)


def _attn(q, k, vt, qi, ki, wit, rel_bias, goa):
    bsz, _, _, s = q.shape
    nsel = min(TOPK_MAX, s // 4)
    assert nsel == TQ and s % TQ == 0, "attention kernel assumes TOPK_MAX-sized query blocks"
    jbits = int(s).bit_length()
    bkt = jnp.asarray(_near_buckets())
    kern = functools.partial(_attn_kernel, nsel=nsel, jbits=jbits)
    return pl.pallas_call(
        kern,
        grid=(bsz, s // TQ),
        in_specs=[
            pl.BlockSpec((1, ATT_HEADS, LANES, TQ), lambda b, i: (b, 0, 0, i)),
            pl.BlockSpec((1, ATT_HEADS // 2, s, LANES), lambda b, i: (b, 0, 0, 0)),
            pl.BlockSpec((1, ATT_W, s), lambda b, i: (b, 0, 0)),
            pl.BlockSpec((1, IDX_HEADS, LANES, TQ), lambda b, i: (b, 0, 0, i)),
            pl.BlockSpec((1, s, LANES), lambda b, i: (b, 0, 0)),
            pl.BlockSpec((1, SUBLANES, TQ), lambda b, i: (b, 0, i)),
            pl.BlockSpec((NEAR, TQ), lambda b, i: (0, 0)),
            pl.BlockSpec(memory_space=pltpu.SMEM),
            pl.BlockSpec((1, ATT_W), lambda b, i: (0, 0)),
        ],
        out_specs=pl.BlockSpec((1, TQ, ATT_W), lambda b, i: (b, i, 0)),
        out_shape=jax.ShapeDtypeStruct((bsz, s, ATT_W), BF16),
        scratch_shapes=[
            pltpu.VMEM((s, TQ), I32),
            pltpu.VMEM((s, TQ), F32),
            pltpu.VMEM((ATT_HEADS, 2 * KT, TQ), F32),
            pltpu.VMEM((ATT_W, TQ), F32),
            pltpu.VMEM((ATT_HEADS, TQ), F32),
            pltpu.VMEM((ATT_HEADS, TQ), F32),
            pltpu.VMEM((ATT_HEADS, TQ), F32),
            pltpu.VMEM((ATT_HEADS, KT, TQ), F32),
        ],
        compiler_params=pltpu.CompilerParams(
            dimension_semantics=("arbitrary", "arbitrary"), vmem_limit_bytes=VMEM_LIMIT),
        name="attn",
    )(q, k, vt, qi, ki, wit, bkt, rel_bias, goa)


def _topk_rows(s, payload, k):
    nrows = s.shape[0]
    rows = lax.broadcasted_iota(I32, s.shape, 0).astype(F32)
    vals, pays = [], []
    for _ in range(k):
        m = jnp.max(s, axis=0, keepdims=True)
        ix = jnp.min(jnp.where(s == m, rows, float(nrows)), axis=0, keepdims=True)
        hit = rows == ix
        if payload is None:
            pays.append(ix)
        else:
            pays.append(jnp.max(jnp.where(hit, payload, -1.0), axis=0, keepdims=True))
        vals.append(m)
        s = jnp.where(hit, -jnp.inf, s)
    return jnp.concatenate(vals, axis=0), jnp.concatenate(pays, axis=0)


def _mid_kernel(x_ref, cn_ref, an_ref, mod_ref, wo1_ref, wo2_ref, g2_ref, wpq_ref, k1_ref, k2_ref,
                x1_ref, h2_ref, idx_ref, gate_ref, qq_s, idt_s, gt_s):
    tm = x_ref.shape[1]
    x = x_ref[0]
    gt1 = mod_ref[0, 2:3, :]
    sh2 = mod_ref[0, 3:4, :]
    sc2 = mod_ref[0, 4:5, :]
    proj = (jnp.dot(cn_ref[0], wo1_ref[...], preferred_element_type=F32)
            + jnp.dot(an_ref[0], wo2_ref[...], preferred_element_type=F32))
    x1 = x + gt1 * proj
    x1_ref[0] = x1
    r = lax.rsqrt(jnp.mean(x1 * x1, axis=-1, keepdims=True) + EPS)
    h2 = (x1 * r) * g2_ref[...] * (1.0 + sc2) + sh2
    h2_ref[0] = h2
    qq_s[...] = jnp.dot(h2.astype(BF16), wpq_ref[...], preferred_element_type=F32).astype(BF16)

    def route(it, c):
        hh = it // (tm // LANES)
        lt = it % (tm // LANES)
        rows = pl.ds(pl.multiple_of(lt * LANES, LANES), LANES)
        q1 = qq_s[rows, pl.ds(pl.multiple_of(hh * 2 * N_KEYS, LANES), N_KEYS)]
        q2 = qq_s[rows, pl.ds(pl.multiple_of(hh * 2 * N_KEYS + N_KEYS, LANES), N_KEYS)]
        v1, i1 = _topk_rows(_nt(k1_ref[hh], q1), None, PEER_TOPK)
        v2, i2 = _topk_rows(_nt(k2_ref[hh], q2), None, PEER_TOPK)
        cand = jnp.concatenate([v1[a:a + 1, :] + v2 for a in range(PEER_TOPK)], axis=0)
        cidx = jnp.concatenate([i1[a:a + 1, :] * float(N_KEYS) + i2 for a in range(PEER_TOPK)], axis=0)
        best, experts = _topk_rows(cand, cidx, PEER_TOPK)
        e = jnp.exp(best - best[0:1, :])
        g = e / jnp.sum(e, axis=0, keepdims=True)
        slots = pl.ds(pl.multiple_of(hh * PEER_TOPK, PEER_TOPK), PEER_TOPK)
        cols = pl.ds(pl.multiple_of(lt * LANES, LANES), LANES)
        idt_s[slots, cols] = experts * float(ROWS_PER_EXPERT)
        gt_s[slots, cols] = g
        return c

    lax.fori_loop(0, PEER_HEADS * (tm // LANES), route, 0)
    idx_ref[0] = idt_s[...].T.astype(I32)
    gate_ref[0] = gt_s[...].T


def _mid(x, cn, an, mod3, wo1, wo2, g2, wpq, k1, k2):
    bsz, s, d = x.shape
    tm = TM_MID
    full = lambda shape: pl.BlockSpec(shape, lambda b, j: (0,) * len(shape))
    tok = lambda w: pl.BlockSpec((1, tm, w), lambda b, j: (b, j, 0))
    return pl.pallas_call(
        _mid_kernel,
        grid=(bsz, s // tm),
        in_specs=[tok(d), tok(CONV_CH), tok(ATT_W), pl.BlockSpec((1, 6, d), lambda b, j: (b, 0, 0)),
                  full(wo1.shape), full(wo2.shape), full(g2.shape), full(wpq.shape), full(k1.shape), full(k2.shape)],
        out_specs=(tok(d), tok(d), tok(PEER_SLOTS), tok(PEER_SLOTS)),
        out_shape=(
            jax.ShapeDtypeStruct((bsz, s, d), F32),
            jax.ShapeDtypeStruct((bsz, s, d), F32),
            jax.ShapeDtypeStruct((bsz, s, PEER_SLOTS), I32),
            jax.ShapeDtypeStruct((bsz, s, PEER_SLOTS), F32),
        ),
        scratch_shapes=[
            pltpu.VMEM((tm, PEER_HEADS * 2 * N_KEYS), BF16),
            pltpu.VMEM((PEER_SLOTS, tm), F32),
            pltpu.VMEM((PEER_SLOTS, tm), F32),
        ],
        compiler_params=pltpu.CompilerParams(
            dimension_semantics=("arbitrary", "arbitrary"), vmem_limit_bytes=VMEM_LIMIT),
        name="mid",
    )(x, cn, an, mod3, wo1, wo2, g2, wpq, k1, k2)


def _pack_table(t):
    tb = t.astype(BF16)
    lo = lax.bitcast_convert_type(tb[:, :HALF], jnp.uint16).astype(jnp.uint32)
    hi = lax.bitcast_convert_type(tb[:, HALF:], jnp.uint16).astype(jnp.uint32)
    return (lo | (hi << 16)).reshape(t.shape[0] * ROWS_PER_EXPERT, LANES)


def _unpack(w):
    lo = pltpu.bitcast(lax.shift_left(w, jnp.uint32(16)), F32)
    hi = pltpu.bitcast(w & jnp.uint32(0xFFFF0000), F32)
    return lo, hi


def _gather_row(tab_ref, row):
    return tab_ref[pl.ds(pl.multiple_of(row, ROWS_PER_EXPERT), ROWS_PER_EXPERT), :]


STAGE_ROWS = PEER_SLOTS * ROWS_PER_EXPERT


def _peer_u_kernel(idx_ref, tab_ref, x_ref, gate_ref, w_ref, *stages):
    tb = w_ref.shape[0]
    ones = jnp.ones((SUBLANES, LANES), F32)
    sub = lax.broadcasted_iota(I32, (SUBLANES, PEER_SLOTS), 0)

    def group(g, c):
        t0 = pl.multiple_of(g * SUBLANES, SUBLANES)
        act8 = jnp.zeros((SUBLANES, PEER_SLOTS), F32)
        for r in range(SUBLANES):
            t = t0 + r
            xv = x_ref[pl.ds(pl.multiple_of(t * SUBLANES, SUBLANES), SUBLANES), :]
            xlo = jnp.concatenate([xv[0:ROWS_PER_EXPERT]] * 2, axis=0)
            xhi = jnp.concatenate([xv[ROWS_PER_EXPERT:]] * 2, axis=0)
            stage = stages[r]
            for j in range(0, PEER_SLOTS, 2):
                pair = jnp.concatenate(
                    [_gather_row(tab_ref, idx_ref[t, j]), _gather_row(tab_ref, idx_ref[t, j + 1])], axis=0)
                lo, hi = _unpack(pair)
                row = j * ROWS_PER_EXPERT
                stage[row:row + 2 * ROWS_PER_EXPERT, :] = lo * xlo + hi * xhi
            cs = stage[pl.ds(0, PEER_SLOTS, stride=ROWS_PER_EXPERT), :]
            for q in range(1, ROWS_PER_EXPERT):
                cs = cs + stage[pl.ds(q, PEER_SLOTS, stride=ROWS_PER_EXPERT), :]
            act = jnp.sum(cs.T, axis=0, keepdims=True)
            act8 = jnp.where(sub == r, act, act8)
        rows = pl.ds(t0, SUBLANES)
        w_ref[rows, :] = gate_ref[rows, :] * jax.nn.gelu(act8)
        return c

    lax.fori_loop(0, tb // SUBLANES, group, 0)


def _peer_u(idx, tab, h2_rows, gate):
    n = idx.shape[0]
    tb = TB_PEER
    return pl.pallas_call(
        _peer_u_kernel,
        grid=(n // tb,),
        in_specs=[
            pl.BlockSpec((tb, PEER_SLOTS), lambda i: (i, 0), memory_space=pltpu.SMEM),
            pl.BlockSpec(tab.shape, lambda i: (0, 0), pipeline_mode=pl.Buffered(1)),
            pl.BlockSpec((tb * SUBLANES, LANES), lambda i: (i, 0)),
            pl.BlockSpec((tb, PEER_SLOTS), lambda i: (i, 0)),
        ],
        out_specs=pl.BlockSpec((tb, PEER_SLOTS), lambda i: (i, 0)),
        out_shape=jax.ShapeDtypeStruct((n, PEER_SLOTS), F32),
        scratch_shapes=[pltpu.VMEM((STAGE_ROWS, LANES), F32) for _ in range(SUBLANES)],
        compiler_params=pltpu.CompilerParams(
            dimension_semantics=("arbitrary",), vmem_limit_bytes=VMEM_LIMIT),
        name="peer_u",
    )(idx, tab, h2_rows, gate)


def _peer_v_kernel(idx_ref, w_ref, tab_ref, x1_ref, gt2_ref, o_ref):
    tb = idx_ref.shape[0]
    gt2 = gt2_ref[0]

    def token(t, c):
        acc_lo = jnp.zeros((ROWS_PER_EXPERT, LANES), F32)
        acc_hi = jnp.zeros((ROWS_PER_EXPERT, LANES), F32)
        for j in range(PEER_SLOTS):
            lo, hi = _unpack(_gather_row(tab_ref, idx_ref[t, j]))
            wj = w_ref[t, j]
            acc_lo = acc_lo + wj * lo
            acc_hi = acc_hi + wj * hi
        rows = pl.ds(pl.multiple_of(t * SUBLANES, SUBLANES), SUBLANES)
        o_ref[rows, :] = x1_ref[rows, :] + gt2 * jnp.concatenate([acc_lo, acc_hi], axis=0)
        return c

    lax.fori_loop(0, tb, token, 0)


def _peer_v(idx, w, tab, x1_rows, gt2_rows, blocks_per_batch):
    n = idx.shape[0]
    tb = TB_PEER
    return pl.pallas_call(
        _peer_v_kernel,
        grid=(n // tb,),
        in_specs=[
            pl.BlockSpec((tb, PEER_SLOTS), lambda i: (i, 0), memory_space=pltpu.SMEM),
            pl.BlockSpec((tb, PEER_SLOTS), lambda i: (i, 0), memory_space=pltpu.SMEM),
            pl.BlockSpec(tab.shape, lambda i: (0, 0), pipeline_mode=pl.Buffered(1)),
            pl.BlockSpec((tb * SUBLANES, LANES), lambda i: (i, 0)),
            pl.BlockSpec((1, SUBLANES, LANES), lambda i: (i // blocks_per_batch, 0, 0)),
        ],
        out_specs=pl.BlockSpec((tb * SUBLANES, LANES), lambda i: (i, 0)),
        out_shape=jax.ShapeDtypeStruct((n * SUBLANES, LANES), F32),
        compiler_params=pltpu.CompilerParams(
            dimension_semantics=("arbitrary",), vmem_limit_bytes=VMEM_LIMIT),
        name="peer_v",
    )(idx, w, tab, x1_rows, gt2_rows)


def _layer(x, mod, g_norm1, g_norm2, w_in, q_norm_g, k_norm_g, conv_w, conv_b, conv_ln_g, conv_ln_b,
           rel_bias, g_out_conv, g_out_attn, w_out, w_peer_q, peer_k1, peer_k2, peer_u, peer_v):
    bsz, s, d = x.shape
    n = bsz * s
    mod3 = mod.reshape(bsz, 6, d)
    row = lambda a: a.reshape(1, -1)

    c0 = 2 * CONV_CH
    c1 = c0 + 3 * ATT_W
    c2 = c1 + IDX_HEADS * IDX_DIM
    c3 = c2 + IDX_DIM
    wa = w_in[:, :c0].astype(BF16)
    wqkv = w_in[:, c0:c1].astype(BF16)
    widx = jnp.concatenate([w_in[:, c1:c2], w_in[:, c2:c3], w_in[:, c2:c3]], axis=1).astype(BF16)
    wwi = jnp.zeros((SUBLANES, d), F32).at[:IDX_HEADS].set(w_in[:, c3:c3 + IDX_HEADS].T).astype(BF16)
    head = np.arange(ATT_W) // HEAD_DIM
    e2 = jnp.asarray((head[:, None] == head[None, :]).astype(np.float32))

    conv_n, q, k, vt, qi, ki, wit = _inproj(
        x, mod3, row(g_norm1), wa, wqkv, widx, wwi,
        row(jnp.tile(q_norm_g, ATT_HEADS)), row(jnp.tile(k_norm_g, ATT_HEADS)), e2,
        conv_w.reshape(CONV_WIDTH, CONV_CH), row(conv_b), row(conv_ln_g), row(conv_ln_b), row(g_out_conv))
    attn_n = _attn(q, k, vt, qi, ki, wit, rel_bias, row(g_out_attn))

    x1, h2, idx, gate = _mid(
        x, conv_n, attn_n, mod3, w_out[:CONV_CH].astype(BF16), w_out[CONV_CH:].astype(BF16), row(g_norm2),
        w_peer_q.astype(BF16), peer_k1.astype(BF16), peer_k2.astype(BF16))

    idx = idx.reshape(n, PEER_SLOTS)
    w = _peer_u(idx, _pack_table(peer_u), h2.reshape(n * SUBLANES, LANES), gate.reshape(n, PEER_SLOTS))
    gt2_rows = mod3[:, 5, :].reshape(bsz, SUBLANES, LANES)
    out = _peer_v(idx, w, _pack_table(peer_v), x1.reshape(n * SUBLANES, LANES), gt2_rows, s // TB_PEER)
    return out.reshape(bsz, s, d)


def kernel(x, c, w_ada, b_ada, g_norm1, g_norm2, w_in, q_norm_g, k_norm_g, conv_w, conv_b, conv_ln_g,
           conv_ln_b, rel_bias, g_out_conv, g_out_attn, w_out, w_peer_q, peer_k1, peer_k2, peer_u, peer_v):
    depth = w_ada.shape[0]
    for l in range(depth):
        mod = _ada(c, w_ada[l], b_ada[l])
        x = _layer(x, mod, g_norm1[l], g_norm2[l], w_in[l], q_norm_g[l], k_norm_g[l], conv_w[l], conv_b[l],
                   conv_ln_g[l], conv_ln_b[l], rel_bias, g_out_conv[l], g_out_attn[l], w_out[l],
                   w_peer_q[l], peer_k1[l], peer_k2[l], peer_u[l], peer_v[l])
    return x
```

```python
import functools
import math

import numpy as np
import jax
import jax.numpy as jnp
from jax import lax
from jax.experimental import pallas as pl
from jax.experimental.pallas import tpu as pltpu

F32 = jnp.float32
BF16 = jnp.bfloat16
I32 = jnp.int32
HIGHEST = lax.Precision.HIGHEST

D_MODEL = 1024
CHUNK = 64
CONV_CH = 512
CONV_WIDTH = 31
ATT_HEADS = 8
HEAD_DIM = 64
ATT_W = ATT_HEADS * HEAD_DIM
IDX_HEADS = 4
IDX_DIM = 64
IDX_SCALE = (IDX_HEADS * IDX_DIM) ** -0.5
TOPK_MAX = 256
REL_BUCKETS = 32
REL_MAX_DIST = 128
PEER_HEADS = 8
N_KEYS = 128
N_EXPERTS = N_KEYS * N_KEYS
PEER_TOPK = 16
PEER_SLOTS = PEER_HEADS * PEER_TOPK
EPS = 1e-6

LANES = 128
SUBLANES = 8
VMEM_LIMIT = 56 * 1024 * 1024

TM_IN = 512
CONV_ROWS = 64
HALO = 32
TQ = 256
KT = 256
TM_MID = 256
TB_PEER = 128
HALF = D_MODEL // 2
ROWS_PER_EXPERT = HALF // LANES

NT_DIMS = (((1,), (1,)), ((), ()))

_NEG_INF_BITS = int(np.array(-np.inf, np.float32).view(np.int32))
KEY_NEG_INF = _NEG_INF_BITS ^ 0x7FFFFFFF
INT_MIN = -(2 ** 31)


def _nt(a, b, precision=None):
    return lax.dot_general(a, b, NT_DIMS, precision=precision, preferred_element_type=F32)


def _ada_kernel(c_ref, w_ref, b_ref, o_ref):
    a = jax.nn.silu(c_ref[...])
    o_ref[...] = jnp.dot(a, w_ref[...], precision=HIGHEST, preferred_element_type=F32) + b_ref[...]


def _ada(c, w_ada, b_ada):
    bsz, d = c.shape
    return pl.pallas_call(
        _ada_kernel,
        grid=(6,),
        in_specs=[
            pl.BlockSpec((bsz, d), lambda j: (0, 0)),
            pl.BlockSpec((d, d), lambda j: (0, j)),
            pl.BlockSpec((1, d), lambda j: (0, j)),
        ],
        out_specs=pl.BlockSpec((bsz, d), lambda j: (0, j)),
        out_shape=jax.ShapeDtypeStruct((bsz, 6 * d), F32),
        name="ada",
    )(c, w_ada, b_ada.reshape(1, 6 * d))


def _inproj_kernel(x_ref, mod_ref, g1_ref, wa_ref, wqkv_ref, widx_ref, wwi_ref, gq_ref, gk_ref, e2_ref,
                   cw_ref, cb_ref, lng_ref, lnb_ref, goc_ref,
                   conv_ref, q_ref, k_ref, vt_ref, qi_ref, ki_ref, wit_ref, ubuf):
    j = pl.program_id(1)
    tm = x_ref.shape[1]
    x = x_ref[0]
    sh1 = mod_ref[0, 0:1, :]
    sc1 = mod_ref[0, 1:2, :]
    r = lax.rsqrt(jnp.mean(x * x, axis=-1, keepdims=True) + EPS)
    h = (x * r) * g1_ref[...] * (1.0 + sc1) + sh1
    hb = h.astype(BF16)

    pa = jnp.dot(hb, wa_ref[...], preferred_element_type=F32)
    u = pa[:, :CONV_CH] * jax.nn.sigmoid(pa[:, CONV_CH:])

    @pl.when(j == 0)
    def _():
        ubuf[0:HALO, :] = jnp.zeros((HALO, CONV_CH), F32)

    ubuf[HALO:HALO + tm, :] = u
    first = HALO - (CONV_WIDTH - 1)
    for rb in range(tm // CONV_ROWS):
        base = rb * CONV_ROWS
        acc = jnp.zeros((CONV_ROWS, CONV_CH), F32) + cb_ref[...]
        for t in range(CONV_WIDTH):
            acc = acc + cw_ref[t:t + 1, :] * ubuf[base + first + t:base + first + t + CONV_ROWS, :]
        mu = jnp.mean(acc, axis=-1, keepdims=True)
        xc = acc - mu
        y = xc * lax.rsqrt(jnp.mean(xc * xc, axis=-1, keepdims=True) + EPS)
        y = jax.nn.silu(y * lng_ref[...] + lnb_ref[...])
        y = y * lax.rsqrt(jnp.mean(y * y, axis=-1, keepdims=True) + EPS) * goc_ref[...]
        conv_ref[0, base:base + CONV_ROWS, :] = y.astype(BF16)
    ubuf[0:HALO, :] = ubuf[tm:tm + HALO, :]

    pq = jnp.dot(hb, wqkv_ref[...], preferred_element_type=F32)
    q = pq[:, :ATT_W]
    k = pq[:, ATT_W:2 * ATT_W]
    v = pq[:, 2 * ATT_W:]
    e2 = e2_ref[...]
    qs = jnp.dot(q * q, e2, precision=HIGHEST, preferred_element_type=F32) * (1.0 / HEAD_DIM)
    ks = jnp.dot(k * k, e2, precision=HIGHEST, preferred_element_type=F32) * (1.0 / HEAD_DIM)
    qn = q * lax.rsqrt(qs + EPS) * gq_ref[...] * (HEAD_DIM ** -0.5)
    kn = k * lax.rsqrt(ks + EPS) * gk_ref[...]
    lane = lax.broadcasted_iota(I32, (tm, LANES), 1)
    low = lane < HEAD_DIM
    for p in range(ATT_HEADS // 2):
        slab = qn[:, p * LANES:(p + 1) * LANES]
        q_ref[0, 2 * p] = jnp.where(low, slab, 0.0).T.astype(BF16)
        q_ref[0, 2 * p + 1] = jnp.where(low, 0.0, slab).T.astype(BF16)
        k_ref[0, p] = kn[:, p * LANES:(p + 1) * LANES].astype(BF16)
    vt_ref[0] = v.T.astype(BF16)

    pc = jnp.dot(hb, widx_ref[...], preferred_element_type=F32)
    for p in range(IDX_HEADS // 2):
        slab = pc[:, p * LANES:(p + 1) * LANES]
        qi_ref[0, 2 * p] = jnp.where(low, slab, 0.0).T.astype(BF16)
        qi_ref[0, 2 * p + 1] = jnp.where(low, 0.0, slab).T.astype(BF16)
    ki_ref[0] = pc[:, 2 * LANES:3 * LANES].astype(BF16)
    wit_ref[0] = _nt(wwi_ref[...], hb) * IDX_SCALE


def _inproj(x, mod3, g1, wa, wqkv, widx, wwi, gq, gk, e2, cw, cb, lng, lnb, goc):
    bsz, s, d = x.shape
    tm = TM_IN
    nt = s // tm
    full = lambda shape: pl.BlockSpec(shape, lambda b, j: (0,) * len(shape))
    out_shape = (
        jax.ShapeDtypeStruct((bsz, s, CONV_CH), BF16),
        jax.ShapeDtypeStruct((bsz, ATT_HEADS, LANES, s), BF16),
        jax.ShapeDtypeStruct((bsz, ATT_HEADS // 2, s, LANES), BF16),
        jax.ShapeDtypeStruct((bsz, ATT_W, s), BF16),
        jax.ShapeDtypeStruct((bsz, IDX_HEADS, LANES, s), BF16),
        jax.ShapeDtypeStruct((bsz, s, LANES), BF16),
        jax.ShapeDtypeStruct((bsz, SUBLANES, s), F32),
    )
    out_specs = (
        pl.BlockSpec((1, tm, CONV_CH), lambda b, j: (b, j, 0)),
        pl.BlockSpec((1, ATT_HEADS, LANES, tm), lambda b, j: (b, 0, 0, j)),
        pl.BlockSpec((1, ATT_HEADS // 2, tm, LANES), lambda b, j: (b, 0, j, 0)),
        pl.BlockSpec((1, ATT_W, tm), lambda b, j: (b, 0, j)),
        pl.BlockSpec((1, IDX_HEADS, LANES, tm), lambda b, j: (b, 0, 0, j)),
        pl.BlockSpec((1, tm, LANES), lambda b, j: (b, j, 0)),
        pl.BlockSpec((1, SUBLANES, tm), lambda b, j: (b, 0, j)),
    )
    return pl.pallas_call(
        _inproj_kernel,
        grid=(bsz, nt),
        in_specs=[
            pl.BlockSpec((1, tm, d), lambda b, j: (b, j, 0)),
            pl.BlockSpec((1, 6, d), lambda b, j: (b, 0, 0)),
            full(g1.shape), full(wa.shape), full(wqkv.shape), full(widx.shape), full(wwi.shape),
            full(gq.shape), full(gk.shape), full(e2.shape),
            full(cw.shape), full(cb.shape), full(lng.shape), full(lnb.shape), full(goc.shape),
        ],
        out_specs=out_specs,
        out_shape=out_shape,
        scratch_shapes=[pltpu.VMEM((tm + HALO, CONV_CH), F32)],
        compiler_params=pltpu.CompilerParams(
            dimension_semantics=("arbitrary", "arbitrary"), vmem_limit_bytes=VMEM_LIMIT),
        name="inproj",
    )(x, mod3, g1, wa, wqkv, widx, wwi, gq, gk, e2, cw, cb, lng, lnb, goc)


def _t5_bucket_np(rel):
    half = REL_BUCKETS // 2
    max_exact = half // 2
    ret = np.where(rel > 0, half, 0)
    n = np.abs(rel)
    nf = np.maximum(n, 1).astype(np.float64)
    large = max_exact + (np.log(nf / max_exact) / math.log(REL_MAX_DIST / max_exact)
                         * (half - max_exact)).astype(np.int32)
    large = np.minimum(large, half - 1)
    return (ret + np.where(n < max_exact, n, large)).astype(np.int32)


NEAR = REL_MAX_DIST + TQ
FAR_BUCKET = REL_BUCKETS // 2 - 1


def _near_buckets():
    r = np.arange(NEAR)[:, None]
    t = np.arange(TQ)[None, :]
    return _t5_bucket_np(r - REL_MAX_DIST - t)


def _attn_kernel(q_ref, k_ref, vt_ref, qi_ref, ki_ref, wit_ref, bkt_ref, rb_ref, goa_ref, o_ref,
                 keys_s, maskb_s, relb_s, ot_s, m_s, mo_s, l_s, lg_s, *, nsel, jbits):
    b = pl.program_id(0)
    i = pl.program_id(1)
    t0 = i * TQ
    n_tiles = i + 1
    neg_inf = F32(-jnp.inf)

    @pl.when((b == 0) & (i == 0))
    def _():
        bk = bkt_ref[...]
        for h in range(ATT_HEADS):
            far = rb_ref[FAR_BUCKET, h]
            acc = jnp.zeros((NEAR, TQ), F32)
            for bb in range(REL_BUCKETS):
                acc = jnp.where(bk == bb, rb_ref[bb, h] - far, acc)
            relb_s[h, 0:2 * KT - NEAR, :] = jnp.zeros((2 * KT - NEAR, TQ), F32)
            relb_s[h, 2 * KT - NEAR:2 * KT, :] = acc

    qpos = t0 + lax.broadcasted_iota(I32, (1, TQ), 1)
    limit = (qpos // CHUNK + 1) * CHUNK
    row_iota = lax.broadcasted_iota(I32, (KT, TQ), 0)

    def tile_start(jt):
        return pl.multiple_of(jt * KT, KT)

    wi = wit_ref[0]

    def p1(jt, c):
        ks = tile_start(jt)
        kit = ki_ref[0, pl.ds(ks, KT), :]
        acc = jnp.zeros((KT, TQ), F32)
        for h in range(IDX_HEADS):
            lgt = jnp.dot(kit, qi_ref[0, h], preferred_element_type=F32)
            acc = acc + jnp.maximum(lgt, 0.0) * wi[h:h + 1, :]
        sc = jnp.where(row_iota + ks < limit, acc, neg_inf)
        bits = pltpu.bitcast(sc, I32)
        keys_s[pl.ds(ks, KT), :] = bits ^ (lax.shift_right_arithmetic(bits, 31) & 0x7FFFFFFF)
        return c

    lax.fori_loop(0, n_tiles, p1, 0)

    def count(pred):
        def body(jt, acc):
            ks = tile_start(jt)
            m = pred(keys_s[pl.ds(ks, KT), :], ks).astype(F32)
            return acc + jnp.sum(m.reshape(KT // SUBLANES, SUBLANES, TQ), axis=0)
        acc = lax.fori_loop(0, n_tiles, body, jnp.zeros((SUBLANES, TQ), F32))
        return jnp.sum(acc, axis=0, keepdims=True)

    def select_thr():
        c0 = count(lambda kt, ks: kt >= 0)
        t = jnp.where(c0 >= nsel, 0, INT_MIN).astype(I32)

        def bitstep(it, t):
            cand = t + lax.shift_left(I32(1), I32(30) - it)
            c = count(lambda kt, ks: kt >= cand)
            return jnp.where(c >= nsel, cand, t)

        return lax.fori_loop(0, 31, bitstep, t)

    thr = lax.cond(i > 0, select_thr, lambda: jnp.full((1, TQ), KEY_NEG_INF + 1, I32))

    c_gt = count(lambda kt, ks: kt > thr)
    c_ge = count(lambda kt, ks: kt >= thr)
    need = nsel - c_gt

    def tie_cut():
        def step(it, cut):
            cand = cut + lax.shift_left(I32(1), I32(jbits - 1) - it)
            f = count(lambda kt, ks: jnp.where(kt == thr, row_iota + ks, cand) < cand)
            return jnp.where(f <= need, cand, cut)
        return lax.fori_loop(0, jbits, step, jnp.zeros((1, TQ), I32))

    cut = lax.cond(jnp.max(c_ge) > nsel, tie_cut, lambda: jnp.full((1, TQ), 2 ** jbits - 1, I32))

    def p3(jt, c):
        ks = tile_start(jt)
        kt = keys_s[pl.ds(ks, KT), :]
        tied = jnp.where(kt == thr, jnp.where(row_iota + ks < cut, 0.0, neg_inf), neg_inf)
        maskb_s[pl.ds(ks, KT), :] = jnp.where(kt > thr, 0.0, tied)
        return c

    lax.fori_loop(0, n_tiles, p3, 0)

    m_s[...] = jnp.full((ATT_HEADS, TQ), neg_inf, F32)
    l_s[...] = jnp.zeros((ATT_HEADS, TQ), F32)
    ot_s[...] = jnp.zeros((ATT_W, TQ), F32)

    def att_tile(jt, near):
        ks = tile_start(jt)
        mb = maskb_s[pl.ds(ks, KT), :]
        for h in range(ATT_HEADS):
            l = jnp.dot(k_ref[0, h // 2, pl.ds(ks, KT), :], q_ref[0, h], preferred_element_type=F32) + mb
            if near:
                off = pl.multiple_of((jt - (n_tiles - 2)) * KT, KT)
                l = l + relb_s[h, pl.ds(off, KT), :]
            lg_s[h] = l
            m_old = m_s[h:h + 1, :]
            mo_s[h:h + 1, :] = m_old
            m_s[h:h + 1, :] = jnp.maximum(m_old, jnp.max(l, axis=0, keepdims=True))
        for h in range(ATT_HEADS):
            m_new = m_s[h:h + 1, :]
            m_ref = jnp.where(m_new == neg_inf, 0.0, m_new)
            alpha = jnp.exp(mo_s[h:h + 1, :] - m_ref)
            p = jnp.exp(lg_s[h] - m_ref)
            l_s[h:h + 1, :] = alpha * l_s[h:h + 1, :] + jnp.sum(p, axis=0, keepdims=True)
            vt = vt_ref[0, h * HEAD_DIM:(h + 1) * HEAD_DIM, pl.ds(ks, KT)]
            rows = slice(h * HEAD_DIM, (h + 1) * HEAD_DIM)
            ot_s[rows, :] = ot_s[rows, :] * alpha + jnp.dot(vt, p.astype(BF16), preferred_element_type=F32)

    def far_tile(jt, c):
        att_tile(jt, False)
        return c

    def near_tile(jt, c):
        att_tile(jt, True)
        return c

    n_far = jnp.maximum(n_tiles - 2, 0)
    lax.fori_loop(0, n_far, far_tile, 0)
    lax.fori_loop(n_far, n_tiles, near_tile, 0)
    for h in range(ATT_HEADS):
        rows = slice(h * HEAD_DIM, (h + 1) * HEAD_DIM)
        ot_s[rows, :] = ot_s[rows, :] / l_s[h:h + 1, :]

    ot = ot_s[...]
    ms = jnp.mean(ot * ot, axis=0, keepdims=True)
    y = (ot * lax.rsqrt(ms + EPS)).T * goa_ref[...]
    o_ref[0] = y.astype(BF16)


def _attn(q, k, vt, qi, ki, wit, rel_bias, goa):
    bsz, _, _, s = q.shape
    nsel = min(TOPK_MAX, s // 4)
    assert nsel == TQ and s % TQ == 0, "attention kernel assumes TOPK_MAX-sized query blocks"
    jbits = int(s).bit_length()
    bkt = jnp.asarray(_near_buckets())
    kern = functools.partial(_attn_kernel, nsel=nsel, jbits=jbits)
    return pl.pallas_call(
        kern,
        grid=(bsz, s // TQ),
        in_specs=[
            pl.BlockSpec((1, ATT_HEADS, LANES, TQ), lambda b, i: (b, 0, 0, i)),
            pl.BlockSpec((1, ATT_HEADS // 2, s, LANES), lambda b, i: (b, 0, 0, 0)),
            pl.BlockSpec((1, ATT_W, s), lambda b, i: (b, 0, 0)),
            pl.BlockSpec((1, IDX_HEADS, LANES, TQ), lambda b, i: (b, 0, 0, i)),
            pl.BlockSpec((1, s, LANES), lambda b, i: (b, 0, 0)),
            pl.BlockSpec((1, SUBLANES, TQ), lambda b, i: (b, 0, i)),
            pl.BlockSpec((NEAR, TQ), lambda b, i: (0, 0)),
            pl.BlockSpec(memory_space=pltpu.SMEM),
            pl.BlockSpec((1, ATT_W), lambda b, i: (0, 0)),
        ],
        out_specs=pl.BlockSpec((1, TQ, ATT_W), lambda b, i: (b, i, 0)),
        out_shape=jax.ShapeDtypeStruct((bsz, s, ATT_W), BF16),
        scratch_shapes=[
            pltpu.VMEM((s, TQ), I32),
            pltpu.VMEM((s, TQ), F32),
            pltpu.VMEM((ATT_HEADS, 2 * KT, TQ), F32),
            pltpu.VMEM((ATT_W, TQ), F32),
            pltpu.VMEM((ATT_HEADS, TQ), F32),
            pltpu.VMEM((ATT_HEADS, TQ), F32),
            pltpu.VMEM((ATT_HEADS, TQ), F32),
            pltpu.VMEM((ATT_HEADS, KT, TQ), F32),
        ],
        compiler_params=pltpu.CompilerParams(
            dimension_semantics=("arbitrary", "arbitrary"), vmem_limit_bytes=VMEM_LIMIT),
        name="attn",
    )(q, k, vt, qi, ki, wit, bkt, rel_bias, goa)


def _topk_rows(s, payload, k):
    nrows = s.shape[0]
    rows = lax.broadcasted_iota(I32, s.shape, 0).astype(F32)
    vals, pays = [], []
    for _ in range(k):
        m = jnp.max(s, axis=0, keepdims=True)
        ix = jnp.min(jnp.where(s == m, rows, float(nrows)), axis=0, keepdims=True)
        hit = rows == ix
        if payload is None:
            pays.append(ix)
        else:
            pays.append(jnp.max(jnp.where(hit, payload, -1.0), axis=0, keepdims=True))
        vals.append(m)
        s = jnp.where(hit, -jnp.inf, s)
    return jnp.concatenate(vals, axis=0), jnp.concatenate(pays, axis=0)


_PAIR_ROWS = tuple((a, PEER_TOPK // (a + 1)) for a in range(PEER_TOPK // 2))


def _mid_kernel(x_ref, cn_ref, an_ref, mod_ref, wo1_ref, wo2_ref, g2_ref, wpq_ref, k1_ref, k2_ref,
                x1_ref, h2_ref, idx_ref, gate_ref, qq_s, idt_s, gt_s):
    tm = x_ref.shape[1]
    x = x_ref[0]
    gt1 = mod_ref[0, 2:3, :]
    sh2 = mod_ref[0, 3:4, :]
    sc2 = mod_ref[0, 4:5, :]
    proj = (jnp.dot(cn_ref[0], wo1_ref[...], preferred_element_type=F32)
            + jnp.dot(an_ref[0], wo2_ref[...], preferred_element_type=F32))
    x1 = x + gt1 * proj
    x1_ref[0] = x1
    r = lax.rsqrt(jnp.mean(x1 * x1, axis=-1, keepdims=True) + EPS)
    h2 = (x1 * r) * g2_ref[...] * (1.0 + sc2) + sh2
    h2_ref[0] = h2
    qq_s[...] = jnp.dot(h2.astype(BF16), wpq_ref[...], preferred_element_type=F32).astype(BF16)

    def route_unit(hh, lt):
        rows = pl.ds(lt * LANES, LANES)
        q1 = qq_s[rows, pl.ds(pl.multiple_of(hh * 2 * N_KEYS, LANES), N_KEYS)]
        q2 = qq_s[rows, pl.ds(pl.multiple_of(hh * 2 * N_KEYS + N_KEYS, LANES), N_KEYS)]
        v1, i1 = _topk_rows(_nt(k1_ref[hh], q1), None, PEER_TOPK)
        v2, i2 = _topk_rows(_nt(k2_ref[hh], q2), None, PEER_TOPK)
        sub = lax.broadcasted_iota(I32, (SUBLANES, LANES), 0)
        cands, cidxs = [], []
        for a, nb in _PAIR_ROWS:
            nrows = max(nb, SUBLANES)
            val = v1[a:a + 1, :] + v2[0:nrows, :]
            if nb < SUBLANES:
                val = jnp.where(sub < nb, val, -jnp.inf)
            cands.append(val)
            cidxs.append(i1[a:a + 1, :] * float(N_KEYS) + i2[0:nrows, :])
        half = PEER_TOPK // 2
        cands.append(v1[half:, :] + v2[0:1, :])
        cidxs.append(i1[half:, :] * float(N_KEYS) + i2[0:1, :])
        best, experts = _topk_rows(jnp.concatenate(cands, axis=0), jnp.concatenate(cidxs, axis=0), PEER_TOPK)
        e = jnp.exp(best - best[0:1, :])
        g = e / jnp.sum(e, axis=0, keepdims=True)
        slots = pl.ds(pl.multiple_of(hh * PEER_TOPK, PEER_TOPK), PEER_TOPK)
        cols = pl.ds(lt * LANES, LANES)
        idt_s[slots, cols] = experts * float(ROWS_PER_EXPERT)
        gt_s[slots, cols] = g

    def route(hh, c):
        for lt in range(tm // LANES):
            route_unit(hh, lt)
        return c

    lax.fori_loop(0, PEER_HEADS, route, 0)
    idx_ref[0] = idt_s[...].T.astype(I32)
    gate_ref[0] = gt_s[...].T


def _mid(x, cn, an, mod3, wo1, wo2, g2, wpq, k1, k2):
    bsz, s, d = x.shape
    tm = TM_MID
    full = lambda shape: pl.BlockSpec(shape, lambda b, j: (0,) * len(shape))
    tok = lambda w: pl.BlockSpec((1, tm, w), lambda b, j: (b, j, 0))
    return pl.pallas_call(
        _mid_kernel,
        grid=(bsz, s // tm),
        in_specs=[tok(d), tok(CONV_CH), tok(ATT_W), pl.BlockSpec((1, 6, d), lambda b, j: (b, 0, 0)),
                  full(wo1.shape), full(wo2.shape), full(g2.shape), full(wpq.shape), full(k1.shape), full(k2.shape)],
        out_specs=(tok(d), tok(d), tok(PEER_SLOTS), tok(PEER_SLOTS)),
        out_shape=(
            jax.ShapeDtypeStruct((bsz, s, d), F32),
            jax.ShapeDtypeStruct((bsz, s, d), F32),
            jax.ShapeDtypeStruct((bsz, s, PEER_SLOTS), I32),
            jax.ShapeDtypeStruct((bsz, s, PEER_SLOTS), F32),
        ),
        scratch_shapes=[
            pltpu.VMEM((tm, PEER_HEADS * 2 * N_KEYS), BF16),
            pltpu.VMEM((PEER_SLOTS, tm), F32),
            pltpu.VMEM((PEER_SLOTS, tm), F32),
        ],
        compiler_params=pltpu.CompilerParams(
            dimension_semantics=("arbitrary", "arbitrary"), vmem_limit_bytes=VMEM_LIMIT),
        name="mid",
    )(x, cn, an, mod3, wo1, wo2, g2, wpq, k1, k2)


def _pack_table(t):
    tb = t.astype(BF16)
    lo = lax.bitcast_convert_type(tb[:, :HALF], jnp.uint16).astype(jnp.uint32)
    hi = lax.bitcast_convert_type(tb[:, HALF:], jnp.uint16).astype(jnp.uint32)
    return (lo | (hi << 16)).reshape(t.shape[0] * ROWS_PER_EXPERT, LANES)


def _unpack(w):
    lo = pltpu.bitcast(lax.shift_left(w, jnp.uint32(16)), F32)
    hi = pltpu.bitcast(w & jnp.uint32(0xFFFF0000), F32)
    return lo, hi


def _gather_row(tab_ref, row):
    return tab_ref[pl.ds(pl.multiple_of(row, ROWS_PER_EXPERT), ROWS_PER_EXPERT), :]


STAGE_ROWS = PEER_SLOTS * ROWS_PER_EXPERT


def _peer_u_kernel(idx_ref, tab_ref, x_ref, gate_ref, w_ref, *stages):
    tb = w_ref.shape[0]
    ones = jnp.ones((SUBLANES, LANES), F32)
    sub = lax.broadcasted_iota(I32, (SUBLANES, PEER_SLOTS), 0)

    def group(g, c):
        t0 = pl.multiple_of(g * SUBLANES, SUBLANES)
        act8 = jnp.zeros((SUBLANES, PEER_SLOTS), F32)
        xlo, xhi, rows_t = [], [], []
        for r in range(SUBLANES):
            t = t0 + r
            xv = x_ref[pl.ds(pl.multiple_of(t * SUBLANES, SUBLANES), SUBLANES), :]
            xlo.append(jnp.concatenate([xv[0:ROWS_PER_EXPERT]] * 2, axis=0))
            xhi.append(jnp.concatenate([xv[ROWS_PER_EXPERT:]] * 2, axis=0))
            rows_t.append(idx_ref.at[t])
        for j in range(0, PEER_SLOTS, 2):
            for r in range(SUBLANES):
                pair = jnp.concatenate(
                    [_gather_row(tab_ref, rows_t[r][j]), _gather_row(tab_ref, rows_t[r][j + 1])], axis=0)
                lo, hi = _unpack(pair)
                row = j * ROWS_PER_EXPERT
                stages[r][row:row + 2 * ROWS_PER_EXPERT, :] = lo * xlo[r] + hi * xhi[r]
        for r in range(SUBLANES):
            stage = stages[r]
            cs = stage[pl.ds(0, PEER_SLOTS, stride=ROWS_PER_EXPERT), :]
            for q in range(1, ROWS_PER_EXPERT):
                cs = cs + stage[pl.ds(q, PEER_SLOTS, stride=ROWS_PER_EXPERT), :]
            act = jnp.sum(cs.T, axis=0, keepdims=True)
            act8 = jnp.where(sub == r, act, act8)
        rows = pl.ds(t0, SUBLANES)
        w_ref[rows, :] = gate_ref[rows, :] * jax.nn.gelu(act8)
        return c

    lax.fori_loop(0, tb // SUBLANES, group, 0)


def _peer_u(idx, tab, h2_rows, gate):
    n = idx.shape[0]
    tb = TB_PEER
    return pl.pallas_call(
        _peer_u_kernel,
        grid=(n // tb,),
        in_specs=[
            pl.BlockSpec((tb, PEER_SLOTS), lambda i: (i, 0), memory_space=pltpu.SMEM),
            pl.BlockSpec(tab.shape, lambda i: (0, 0), pipeline_mode=pl.Buffered(1)),
            pl.BlockSpec((tb * SUBLANES, LANES), lambda i: (i, 0)),
            pl.BlockSpec((tb, PEER_SLOTS), lambda i: (i, 0)),
        ],
        out_specs=pl.BlockSpec((tb, PEER_SLOTS), lambda i: (i, 0)),
        out_shape=jax.ShapeDtypeStruct((n, PEER_SLOTS), F32),
        scratch_shapes=[pltpu.VMEM((STAGE_ROWS, LANES), F32) for _ in range(SUBLANES)],
        compiler_params=pltpu.CompilerParams(
            dimension_semantics=("arbitrary",), vmem_limit_bytes=VMEM_LIMIT),
        name="peer_u",
    )(idx, tab, h2_rows, gate)


def _peer_v_kernel(idx_ref, w_ref, tab_ref, x1_ref, gt2_ref, o_ref):
    tb = idx_ref.shape[0]
    gt2 = gt2_ref[0]

    first = lax.broadcasted_iota(I32, (SUBLANES, LANES), 0) < ROWS_PER_EXPERT

    def token(t, c):
        rows_t = idx_ref.at[t]
        w_t = w_ref.at[t]
        acc_lo = jnp.zeros((SUBLANES, LANES), F32)
        acc_hi = jnp.zeros((SUBLANES, LANES), F32)
        for j in range(0, PEER_SLOTS, 2):
            pair = jnp.concatenate(
                [_gather_row(tab_ref, rows_t[j]), _gather_row(tab_ref, rows_t[j + 1])], axis=0)
            lo, hi = _unpack(pair)
            wv = jnp.where(first, w_t[j], w_t[j + 1])
            acc_lo = acc_lo + wv * lo
            acc_hi = acc_hi + wv * hi
        peer = jnp.concatenate([acc_lo[:ROWS_PER_EXPERT] + acc_lo[ROWS_PER_EXPERT:],
                                acc_hi[:ROWS_PER_EXPERT] + acc_hi[ROWS_PER_EXPERT:]], axis=0)
        rows = pl.ds(pl.multiple_of(t * SUBLANES, SUBLANES), SUBLANES)
        o_ref[rows, :] = x1_ref[rows, :] + gt2 * peer
        return c

    lax.fori_loop(0, tb, token, 0)


def _peer_v(idx, w, tab, x1_rows, gt2_rows, blocks_per_batch):
    n = idx.shape[0]
    tb = TB_PEER
    return pl.pallas_call(
        _peer_v_kernel,
        grid=(n // tb,),
        in_specs=[
            pl.BlockSpec((tb, PEER_SLOTS), lambda i: (i, 0), memory_space=pltpu.SMEM),
            pl.BlockSpec((tb, PEER_SLOTS), lambda i: (i, 0), memory_space=pltpu.SMEM),
            pl.BlockSpec(tab.shape, lambda i: (0, 0), pipeline_mode=pl.Buffered(1)),
            pl.BlockSpec((tb * SUBLANES, LANES), lambda i: (i, 0)),
            pl.BlockSpec((1, SUBLANES, LANES), lambda i: (i // blocks_per_batch, 0, 0)),
        ],
        out_specs=pl.BlockSpec((tb * SUBLANES, LANES), lambda i: (i, 0)),
        out_shape=jax.ShapeDtypeStruct((n * SUBLANES, LANES), F32),
        compiler_params=pltpu.CompilerParams(
            dimension_semantics=("arbitrary",), vmem_limit_bytes=VMEM_LIMIT),
        name="peer_v",
    )(idx, w, tab, x1_rows, gt2_rows)


def _layer(x, mod, g_norm1, g_norm2, w_in, q_norm_g, k_norm_g, conv_w, conv_b, conv_ln_g, conv_ln_b,
           rel_bias, g_out_conv, g_out_attn, w_out, w_peer_q, peer_k1, peer_k2, peer_u, peer_v):
    bsz, s, d = x.shape
    n = bsz * s
    mod3 = mod.reshape(bsz, 6, d)
    row = lambda a: a.reshape(1, -1)

    c0 = 2 * CONV_CH
    c1 = c0 + 3 * ATT_W
    c2 = c1 + IDX_HEADS * IDX_DIM
    c3 = c2 + IDX_DIM
    wa = w_in[:, :c0].astype(BF16)
    wqkv = w_in[:, c0:c1].astype(BF16)
    widx = jnp.concatenate([w_in[:, c1:c2], w_in[:, c2:c3], w_in[:, c2:c3]], axis=1).astype(BF16)
    wwi = jnp.zeros((SUBLANES, d), F32).at[:IDX_HEADS].set(w_in[:, c3:c3 + IDX_HEADS].T).astype(BF16)
    head = np.arange(ATT_W) // HEAD_DIM
    e2 = jnp.asarray((head[:, None] == head[None, :]).astype(np.float32))

    conv_n, q, k, vt, qi, ki, wit = _inproj(
        x, mod3, row(g_norm1), wa, wqkv, widx, wwi,
        row(jnp.tile(q_norm_g, ATT_HEADS)), row(jnp.tile(k_norm_g, ATT_HEADS)), e2,
        conv_w.reshape(CONV_WIDTH, CONV_CH), row(conv_b), row(conv_ln_g), row(conv_ln_b), row(g_out_conv))
    attn_n = _attn(q, k, vt, qi, ki, wit, rel_bias, row(g_out_attn))

    x1, h2, idx, gate = _mid(
        x, conv_n, attn_n, mod3, w_out[:CONV_CH].astype(BF16), w_out[CONV_CH:].astype(BF16), row(g_norm2),
        w_peer_q.astype(BF16), peer_k1.astype(BF16), peer_k2.astype(BF16))

    idx = idx.reshape(n, PEER_SLOTS)
    w = _peer_u(idx, _pack_table(peer_u), h2.reshape(n * SUBLANES, LANES), gate.reshape(n, PEER_SLOTS))
    gt2_rows = mod3[:, 5, :].reshape(bsz, SUBLANES, LANES)
    out = _peer_v(idx, w, _pack_table(peer_v), x1.reshape(n * SUBLANES, LANES), gt2_rows, s // TB_PEER)
    return out.reshape(bsz, s, d)


def kernel(x, c, w_ada, b_ada, g_norm1, g_norm2, w_in, q_norm_g, k_norm_g, conv_w, conv_b, conv_ln_g,
           conv_ln_b, rel_bias, g_out_conv, g_out_attn, w_out, w_peer_q, peer_k1, peer_k2, peer_u, peer_v):
    depth = w_ada.shape[0]
    for l in range(depth):
        mod = _ada(c, w_ada[l], b_ada[l])
        x = _layer(x, mod, g_norm1[l], g_norm2[l], w_in[l], q_norm_g[l], k_norm_g[l], conv_w[l], conv_b[l],
                   conv_ln_g[l], conv_ln_b[l], rel_bias, g_out_conv[l], g_out_attn[l], w_out[l],
                   w_peer_q[l], peer_k1[l], peer_k2[l], peer_u[l], peer_v[l])
    return x
```

```python
import functools
import math

import numpy as np
import jax
import jax.numpy as jnp
from jax import lax
from jax.experimental import pallas as pl
from jax.experimental.pallas import tpu as pltpu

F32 = jnp.float32
BF16 = jnp.bfloat16
I32 = jnp.int32
HIGHEST = lax.Precision.HIGHEST

D_MODEL = 1024
CHUNK = 64
CONV_CH = 512
CONV_WIDTH = 31
ATT_HEADS = 8
HEAD_DIM = 64
ATT_W = ATT_HEADS * HEAD_DIM
IDX_HEADS = 4
IDX_DIM = 64
IDX_SCALE = (IDX_HEADS * IDX_DIM) ** -0.5
TOPK_MAX = 256
REL_BUCKETS = 32
REL_MAX_DIST = 128
PEER_HEADS = 8
N_KEYS = 128
N_EXPERTS = N_KEYS * N_KEYS
PEER_TOPK = 16
PEER_SLOTS = PEER_HEADS * PEER_TOPK
EPS = 1e-6

LANES = 128
SUBLANES = 8
VMEM_LIMIT = 56 * 1024 * 1024

TM_IN = 512
CONV_ROWS = 64
HALO = 32
TQ = 256
KT = 256
TM_MID = 256
TB_PEER = 128
HALF = D_MODEL // 2
ROWS_PER_EXPERT = HALF // LANES

NT_DIMS = (((1,), (1,)), ((), ()))

_NEG_INF_BITS = int(np.array(-np.inf, np.float32).view(np.int32))
KEY_NEG_INF = _NEG_INF_BITS ^ 0x7FFFFFFF
INT_MIN = -(2 ** 31)


def _nt(a, b, precision=None):
    return lax.dot_general(a, b, NT_DIMS, precision=precision, preferred_element_type=F32)


def _ada_kernel(c_ref, w_ref, b_ref, o_ref):
    a = jax.nn.silu(c_ref[...])
    o_ref[...] = jnp.dot(a, w_ref[...], precision=HIGHEST, preferred_element_type=F32) + b_ref[...]


def _ada(c, w_ada, b_ada):
    bsz, d = c.shape
    return pl.pallas_call(
        _ada_kernel,
        grid=(6,),
        in_specs=[
            pl.BlockSpec((bsz, d), lambda j: (0, 0)),
            pl.BlockSpec((d, d), lambda j: (0, j)),
            pl.BlockSpec((1, d), lambda j: (0, j)),
        ],
        out_specs=pl.BlockSpec((bsz, d), lambda j: (0, j)),
        out_shape=jax.ShapeDtypeStruct((bsz, 6 * d), F32),
        name="ada",
    )(c, w_ada, b_ada.reshape(1, 6 * d))


def _inproj_kernel(x_ref, mod_ref, g1_ref, wa_ref, wqkv_ref, widx_ref, wwi_ref, gq_ref, gk_ref, e2_ref,
                   cw_ref, cb_ref, lng_ref, lnb_ref, goc_ref,
                   conv_ref, q_ref, k_ref, vt_ref, qi_ref, ki_ref, wit_ref, ubuf):
    j = pl.program_id(1)
    tm = x_ref.shape[1]
    x = x_ref[0]
    sh1 = mod_ref[0, 0:1, :]
    sc1 = mod_ref[0, 1:2, :]
    r = lax.rsqrt(jnp.mean(x * x, axis=-1, keepdims=True) + EPS)
    h = (x * r) * g1_ref[...] * (1.0 + sc1) + sh1
    hb = h.astype(BF16)

    pa = jnp.dot(hb, wa_ref[...], preferred_element_type=F32)
    u = pa[:, :CONV_CH] * jax.nn.sigmoid(pa[:, CONV_CH:])

    @pl.when(j == 0)
    def _():
        ubuf[0:HALO, :] = jnp.zeros((HALO, CONV_CH), F32)

    ubuf[HALO:HALO + tm, :] = u
    first = HALO - (CONV_WIDTH - 1)
    for rb in range(tm // CONV_ROWS):
        base = rb * CONV_ROWS
        acc = jnp.zeros((CONV_ROWS, CONV_CH), F32) + cb_ref[...]
        for t in range(CONV_WIDTH):
            acc = acc + cw_ref[t:t + 1, :] * ubuf[base + first + t:base + first + t + CONV_ROWS, :]
        mu = jnp.mean(acc, axis=-1, keepdims=True)
        xc = acc - mu
        y = xc * lax.rsqrt(jnp.mean(xc * xc, axis=-1, keepdims=True) + EPS)
        y = jax.nn.silu(y * lng_ref[...] + lnb_ref[...])
        y = y * lax.rsqrt(jnp.mean(y * y, axis=-1, keepdims=True) + EPS) * goc_ref[...]
        conv_ref[0, base:base + CONV_ROWS, :] = y.astype(BF16)
    ubuf[0:HALO, :] = ubuf[tm:tm + HALO, :]

    pq = jnp.dot(hb, wqkv_ref[...], preferred_element_type=F32)
    q = pq[:, :ATT_W]
    k = pq[:, ATT_W:2 * ATT_W]
    v = pq[:, 2 * ATT_W:]
    e2 = e2_ref[...]
    qs = jnp.dot(q * q, e2, precision=HIGHEST, preferred_element_type=F32) * (1.0 / HEAD_DIM)
    ks = jnp.dot(k * k, e2, precision=HIGHEST, preferred_element_type=F32) * (1.0 / HEAD_DIM)
    qn = q * lax.rsqrt(qs + EPS) * gq_ref[...] * (HEAD_DIM ** -0.5)
    kn = k * lax.rsqrt(ks + EPS) * gk_ref[...]
    lane = lax.broadcasted_iota(I32, (tm, LANES), 1)
    low = lane < HEAD_DIM
    for p in range(ATT_HEADS // 2):
        slab = qn[:, p * LANES:(p + 1) * LANES]
        q_ref[0, 2 * p] = jnp.where(low, slab, 0.0).T.astype(BF16)
        q_ref[0, 2 * p + 1] = jnp.where(low, 0.0, slab).T.astype(BF16)
        k_ref[0, p] = kn[:, p * LANES:(p + 1) * LANES].astype(BF16)
    vt_ref[0] = v.T.astype(BF16)

    pc = jnp.dot(hb, widx_ref[...], preferred_element_type=F32)
    for p in range(IDX_HEADS // 2):
        slab = pc[:, p * LANES:(p + 1) * LANES]
        qi_ref[0, 2 * p] = jnp.where(low, slab, 0.0).T.astype(BF16)
        qi_ref[0, 2 * p + 1] = jnp.where(low, 0.0, slab).T.astype(BF16)
    ki_ref[0] = pc[:, 2 * LANES:3 * LANES].astype(BF16)
    wit_ref[0] = _nt(wwi_ref[...], hb) * IDX_SCALE


def _inproj(x, mod3, g1, wa, wqkv, widx, wwi, gq, gk, e2, cw, cb, lng, lnb, goc):
    bsz, s, d = x.shape
    tm = TM_IN
    nt = s // tm
    full = lambda shape: pl.BlockSpec(shape, lambda b, j: (0,) * len(shape))
    out_shape = (
        jax.ShapeDtypeStruct((bsz, s, CONV_CH), BF16),
        jax.ShapeDtypeStruct((bsz, ATT_HEADS, LANES, s), BF16),
        jax.ShapeDtypeStruct((bsz, ATT_HEADS // 2, s, LANES), BF16),
        jax.ShapeDtypeStruct((bsz, ATT_W, s), BF16),
        jax.ShapeDtypeStruct((bsz, IDX_HEADS, LANES, s), BF16),
        jax.ShapeDtypeStruct((bsz, s, LANES), BF16),
        jax.ShapeDtypeStruct((bsz, SUBLANES, s), F32),
    )
    out_specs = (
        pl.BlockSpec((1, tm, CONV_CH), lambda b, j: (b, j, 0)),
        pl.BlockSpec((1, ATT_HEADS, LANES, tm), lambda b, j: (b, 0, 0, j)),
        pl.BlockSpec((1, ATT_HEADS // 2, tm, LANES), lambda b, j: (b, 0, j, 0)),
        pl.BlockSpec((1, ATT_W, tm), lambda b, j: (b, 0, j)),
        pl.BlockSpec((1, IDX_HEADS, LANES, tm), lambda b, j: (b, 0, 0, j)),
        pl.BlockSpec((1, tm, LANES), lambda b, j: (b, j, 0)),
        pl.BlockSpec((1, SUBLANES, tm), lambda b, j: (b, 0, j)),
    )
    return pl.pallas_call(
        _inproj_kernel,
        grid=(bsz, nt),
        in_specs=[
            pl.BlockSpec((1, tm, d), lambda b, j: (b, j, 0)),
            pl.BlockSpec((1, 6, d), lambda b, j: (b, 0, 0)),
            full(g1.shape), full(wa.shape), full(wqkv.shape), full(widx.shape), full(wwi.shape),
            full(gq.shape), full(gk.shape), full(e2.shape),
            full(cw.shape), full(cb.shape), full(lng.shape), full(lnb.shape), full(goc.shape),
        ],
        out_specs=out_specs,
        out_shape=out_shape,
        scratch_shapes=[pltpu.VMEM((tm + HALO, CONV_CH), F32)],
        compiler_params=pltpu.CompilerParams(
            dimension_semantics=("arbitrary", "arbitrary"), vmem_limit_bytes=VMEM_LIMIT),
        name="inproj",
    )(x, mod3, g1, wa, wqkv, widx, wwi, gq, gk, e2, cw, cb, lng, lnb, goc)


def _t5_bucket_np(rel):
    half = REL_BUCKETS // 2
    max_exact = half // 2
    ret = np.where(rel > 0, half, 0)
    n = np.abs(rel)
    nf = np.maximum(n, 1).astype(np.float64)
    large = max_exact + (np.log(nf / max_exact) / math.log(REL_MAX_DIST / max_exact)
                         * (half - max_exact)).astype(np.int32)
    large = np.minimum(large, half - 1)
    return (ret + np.where(n < max_exact, n, large)).astype(np.int32)


NEAR = REL_MAX_DIST + TQ
FAR_BUCKET = REL_BUCKETS // 2 - 1


def _near_buckets():
    r = np.arange(NEAR)[:, None]
    t = np.arange(TQ)[None, :]
    return _t5_bucket_np(r - REL_MAX_DIST - t)


def _attn_kernel(q_ref, k_ref, vt_ref, qi_ref, ki_ref, wit_ref, bkt_ref, rb_ref, goa_ref, o_ref,
                 keys_s, maskb_s, relb_s, ot_s, m_s, mo_s, l_s, lg_s, *, nsel, jbits):
    b = pl.program_id(0)
    i = pl.program_id(1)
    t0 = i * TQ
    n_tiles = i + 1
    neg_inf = F32(-jnp.inf)

    @pl.when((b == 0) & (i == 0))
    def _():
        bk = bkt_ref[...]
        for h in range(ATT_HEADS):
            far = rb_ref[FAR_BUCKET, h]
            acc = jnp.zeros((NEAR, TQ), F32)
            for bb in range(REL_BUCKETS):
                acc = jnp.where(bk == bb, rb_ref[bb, h] - far, acc)
            relb_s[h, 0:2 * KT - NEAR, :] = jnp.zeros((2 * KT - NEAR, TQ), F32)
            relb_s[h, 2 * KT - NEAR:2 * KT, :] = acc

    qpos = t0 + lax.broadcasted_iota(I32, (1, TQ), 1)
    limit = (qpos // CHUNK + 1) * CHUNK
    row_iota = lax.broadcasted_iota(I32, (KT, TQ), 0)

    def tile_start(jt):
        return pl.multiple_of(jt * KT, KT)

    wi = wit_ref[0]

    def p1(jt, c):
        ks = tile_start(jt)
        kit = ki_ref[0, pl.ds(ks, KT), :]
        acc = jnp.zeros((KT, TQ), F32)
        for h in range(IDX_HEADS):
            lgt = jnp.dot(kit, qi_ref[0, h], preferred_element_type=F32)
            acc = acc + jnp.maximum(lgt, 0.0) * wi[h:h + 1, :]
        sc = jnp.where(row_iota + ks < limit, acc, neg_inf)
        bits = pltpu.bitcast(sc, I32)
        keys_s[pl.ds(ks, KT), :] = bits ^ (lax.shift_right_arithmetic(bits, 31) & 0x7FFFFFFF)
        return c

    lax.fori_loop(0, n_tiles, p1, 0)

    def count(pred):
        def body(jt, acc):
            ks = tile_start(jt)
            m = pred(keys_s[pl.ds(ks, KT), :], ks).astype(F32)
            return acc + jnp.sum(m.reshape(KT // SUBLANES, SUBLANES, TQ), axis=0)
        acc = lax.fori_loop(0, n_tiles, body, jnp.zeros((SUBLANES, TQ), F32))
        return jnp.sum(acc, axis=0, keepdims=True)

    def select_thr():
        c0 = count(lambda kt, ks: kt >= 0)
        t = jnp.where(c0 >= nsel, 0, INT_MIN).astype(I32)

        def bitstep(it, t):
            cand = t + lax.shift_left(I32(1), I32(30) - it)
            c = count(lambda kt, ks: kt >= cand)
            return jnp.where(c >= nsel, cand, t)

        return lax.fori_loop(0, 31, bitstep, t)

    thr = lax.cond(i > 0, select_thr, lambda: jnp.full((1, TQ), KEY_NEG_INF + 1, I32))

    c_gt = count(lambda kt, ks: kt > thr)
    c_ge = count(lambda kt, ks: kt >= thr)
    need = nsel - c_gt

    def tie_cut():
        def step(it, cut):
            cand = cut + lax.shift_left(I32(1), I32(jbits - 1) - it)
            f = count(lambda kt, ks: jnp.where(kt == thr, row_iota + ks, cand) < cand)
            return jnp.where(f <= need, cand, cut)
        return lax.fori_loop(0, jbits, step, jnp.zeros((1, TQ), I32))

    cut = lax.cond(jnp.max(c_ge) > nsel, tie_cut, lambda: jnp.full((1, TQ), 2 ** jbits - 1, I32))

    def p3(jt, c):
        ks = tile_start(jt)
        kt = keys_s[pl.ds(ks, KT), :]
        tied = jnp.where(kt == thr, jnp.where(row_iota + ks < cut, 0.0, neg_inf), neg_inf)
        maskb_s[pl.ds(ks, KT), :] = jnp.where(kt > thr, 0.0, tied)
        return c

    lax.fori_loop(0, n_tiles, p3, 0)

    m_s[...] = jnp.full((ATT_HEADS, TQ), neg_inf, F32)
    l_s[...] = jnp.zeros((ATT_HEADS, TQ), F32)
    ot_s[...] = jnp.zeros((ATT_W, TQ), F32)

    def att_tile(jt, near):
        ks = tile_start(jt)
        mb = maskb_s[pl.ds(ks, KT), :]
        for h in range(ATT_HEADS):
            l = jnp.dot(k_ref[0, h // 2, pl.ds(ks, KT), :], q_ref[0, h], preferred_element_type=F32) + mb
            if near:
                off = pl.multiple_of((jt - (n_tiles - 2)) * KT, KT)
                l = l + relb_s[h, pl.ds(off, KT), :]
            lg_s[h] = l
            m_old = m_s[h:h + 1, :]
            mo_s[h:h + 1, :] = m_old
            m_s[h:h + 1, :] = jnp.maximum(m_old, jnp.max(l, axis=0, keepdims=True))
        for h in range(ATT_HEADS):
            m_new = m_s[h:h + 1, :]
            m_ref = jnp.where(m_new == neg_inf, 0.0, m_new)
            alpha = jnp.exp(mo_s[h:h + 1, :] - m_ref)
            p = jnp.exp(lg_s[h] - m_ref)
            l_s[h:h + 1, :] = alpha * l_s[h:h + 1, :] + jnp.sum(p, axis=0, keepdims=True)
            vt = vt_ref[0, h * HEAD_DIM:(h + 1) * HEAD_DIM, pl.ds(ks, KT)]
            rows = slice(h * HEAD_DIM, (h + 1) * HEAD_DIM)
            ot_s[rows, :] = ot_s[rows, :] * alpha + jnp.dot(vt, p.astype(BF16), preferred_element_type=F32)

    def far_tile(jt, c):
        att_tile(jt, False)
        return c

    def near_tile(jt, c):
        att_tile(jt, True)
        return c

    n_far = jnp.maximum(n_tiles - 2, 0)
    lax.fori_loop(0, n_far, far_tile, 0)
    lax.fori_loop(n_far, n_tiles, near_tile, 0)
    for h in range(ATT_HEADS):
        rows = slice(h * HEAD_DIM, (h + 1) * HEAD_DIM)
        ot_s[rows, :] = ot_s[rows, :] / l_s[h:h + 1, :]

    ot = ot_s[...]
    ms = jnp.mean(ot * ot, axis=0, keepdims=True)
    y = (ot * lax.rsqrt(ms + EPS)).T * goa_ref[...]
    o_ref[0] = y.astype(BF16)


def _attn(q, k, vt, qi, ki, wit, rel_bias, goa):
    bsz, _, _, s = q.shape
    nsel = min(TOPK_MAX, s // 4)
    assert nsel == TQ and s % TQ == 0, "attention kernel assumes TOPK_MAX-sized query blocks"
    jbits = int(s).bit_length()
    bkt = jnp.asarray(_near_buckets())
    kern = functools.partial(_attn_kernel, nsel=nsel, jbits=jbits)
    return pl.pallas_call(
        kern,
        grid=(bsz, s // TQ),
        in_specs=[
            pl.BlockSpec((1, ATT_HEADS, LANES, TQ), lambda b, i: (b, 0, 0, i)),
            pl.BlockSpec((1, ATT_HEADS // 2, s, LANES), lambda b, i: (b, 0, 0, 0)),
            pl.BlockSpec((1, ATT_W, s), lambda b, i: (b, 0, 0)),
            pl.BlockSpec((1, IDX_HEADS, LANES, TQ), lambda b, i: (b, 0, 0, i)),
            pl.BlockSpec((1, s, LANES), lambda b, i: (b, 0, 0)),
            pl.BlockSpec((1, SUBLANES, TQ), lambda b, i: (b, 0, i)),
            pl.BlockSpec((NEAR, TQ), lambda b, i: (0, 0)),
            pl.BlockSpec(memory_space=pltpu.SMEM),
            pl.BlockSpec((1, ATT_W), lambda b, i: (0, 0)),
        ],
        out_specs=pl.BlockSpec((1, TQ, ATT_W), lambda b, i: (b, i, 0)),
        out_shape=jax.ShapeDtypeStruct((bsz, s, ATT_W), BF16),
        scratch_shapes=[
            pltpu.VMEM((s, TQ), I32),
            pltpu.VMEM((s, TQ), F32),
            pltpu.VMEM((ATT_HEADS, 2 * KT, TQ), F32),
            pltpu.VMEM((ATT_W, TQ), F32),
            pltpu.VMEM((ATT_HEADS, TQ), F32),
            pltpu.VMEM((ATT_HEADS, TQ), F32),
            pltpu.VMEM((ATT_HEADS, TQ), F32),
            pltpu.VMEM((ATT_HEADS, KT, TQ), F32),
        ],
        compiler_params=pltpu.CompilerParams(
            dimension_semantics=("arbitrary", "arbitrary"), vmem_limit_bytes=VMEM_LIMIT),
        name="attn",
    )(q, k, vt, qi, ki, wit, bkt, rel_bias, goa)


def _topk_rows(s, payload, k):
    nrows = s.shape[0]
    rows = lax.broadcasted_iota(I32, s.shape, 0).astype(F32)
    vals, pays = [], []
    for _ in range(k):
        m = jnp.max(s, axis=0, keepdims=True)
        ix = jnp.min(jnp.where(s == m, rows, float(nrows)), axis=0, keepdims=True)
        hit = rows == ix
        if payload is None:
            pays.append(ix)
        else:
            pays.append(jnp.max(jnp.where(hit, payload, -1.0), axis=0, keepdims=True))
        vals.append(m)
        s = jnp.where(hit, -jnp.inf, s)
    return jnp.concatenate(vals, axis=0), jnp.concatenate(pays, axis=0)


_PAIR_ROWS = tuple((a, PEER_TOPK // (a + 1)) for a in range(PEER_TOPK // 2))


def _mid_kernel(x_ref, cn_ref, an_ref, mod_ref, wo1_ref, wo2_ref, g2_ref, wpq_ref, k1_ref, k2_ref,
                x1_ref, h2_ref, idx_ref, gate_ref, qq_s, idt_s, gt_s):
    tm = x_ref.shape[1]
    x = x_ref[0]
    gt1 = mod_ref[0, 2:3, :]
    sh2 = mod_ref[0, 3:4, :]
    sc2 = mod_ref[0, 4:5, :]
    proj = (jnp.dot(cn_ref[0], wo1_ref[...], preferred_element_type=F32)
            + jnp.dot(an_ref[0], wo2_ref[...], preferred_element_type=F32))
    x1 = x + gt1 * proj
    x1_ref[0] = x1
    r = lax.rsqrt(jnp.mean(x1 * x1, axis=-1, keepdims=True) + EPS)
    h2 = (x1 * r) * g2_ref[...] * (1.0 + sc2) + sh2
    h2_ref[0] = h2
    qq_s[...] = jnp.dot(h2.astype(BF16), wpq_ref[...], preferred_element_type=F32).astype(BF16)

    def route_unit(hh, lt):
        rows = pl.ds(lt * LANES, LANES)
        q1 = qq_s[rows, pl.ds(pl.multiple_of(hh * 2 * N_KEYS, LANES), N_KEYS)]
        q2 = qq_s[rows, pl.ds(pl.multiple_of(hh * 2 * N_KEYS + N_KEYS, LANES), N_KEYS)]
        v1, i1 = _topk_rows(_nt(k1_ref[hh], q1), None, PEER_TOPK)
        v2, i2 = _topk_rows(_nt(k2_ref[hh], q2), None, PEER_TOPK)
        sub = lax.broadcasted_iota(I32, (SUBLANES, LANES), 0)
        cands, cidxs = [], []
        for a, nb in _PAIR_ROWS:
            nrows = max(nb, SUBLANES)
            val = v1[a:a + 1, :] + v2[0:nrows, :]
            if nb < SUBLANES:
                val = jnp.where(sub < nb, val, -jnp.inf)
            cands.append(val)
            cidxs.append(i1[a:a + 1, :] * float(N_KEYS) + i2[0:nrows, :])
        half = PEER_TOPK // 2
        cands.append(v1[half:, :] + v2[0:1, :])
        cidxs.append(i1[half:, :] * float(N_KEYS) + i2[0:1, :])
        best, experts = _topk_rows(jnp.concatenate(cands, axis=0), jnp.concatenate(cidxs, axis=0), PEER_TOPK)
        e = jnp.exp(best - best[0:1, :])
        g = e / jnp.sum(e, axis=0, keepdims=True)
        slots = pl.ds(pl.multiple_of(hh * PEER_TOPK, PEER_TOPK), PEER_TOPK)
        cols = pl.ds(lt * LANES, LANES)
        idt_s[slots, cols] = experts * float(ROWS_PER_EXPERT)
        gt_s[slots, cols] = g

    def route(hh, c):
        for lt in range(tm // LANES):
            route_unit(hh, lt)
        return c

    lax.fori_loop(0, PEER_HEADS, route, 0)
    idx_ref[0] = idt_s[...].T.astype(I32)
    gate_ref[0] = gt_s[...].T


def _mid(x, cn, an, mod3, wo1, wo2, g2, wpq, k1, k2):
    bsz, s, d = x.shape
    tm = TM_MID
    full = lambda shape: pl.BlockSpec(shape, lambda b, j: (0,) * len(shape))
    tok = lambda w: pl.BlockSpec((1, tm, w), lambda b, j: (b, j, 0))
    return pl.pallas_call(
        _mid_kernel,
        grid=(bsz, s // tm),
        in_specs=[tok(d), tok(CONV_CH), tok(ATT_W), pl.BlockSpec((1, 6, d), lambda b, j: (b, 0, 0)),
                  full(wo1.shape), full(wo2.shape), full(g2.shape), full(wpq.shape), full(k1.shape), full(k2.shape)],
        out_specs=(tok(d), tok(d), tok(PEER_SLOTS), tok(PEER_SLOTS)),
        out_shape=(
            jax.ShapeDtypeStruct((bsz, s, d), F32),
            jax.ShapeDtypeStruct((bsz, s, d), F32),
            jax.ShapeDtypeStruct((bsz, s, PEER_SLOTS), I32),
            jax.ShapeDtypeStruct((bsz, s, PEER_SLOTS), F32),
        ),
        scratch_shapes=[
            pltpu.VMEM((tm, PEER_HEADS * 2 * N_KEYS), BF16),
            pltpu.VMEM((PEER_SLOTS, tm), F32),
            pltpu.VMEM((PEER_SLOTS, tm), F32),
        ],
        compiler_params=pltpu.CompilerParams(
            dimension_semantics=("arbitrary", "arbitrary"), vmem_limit_bytes=VMEM_LIMIT),
        name="mid",
    )(x, cn, an, mod3, wo1, wo2, g2, wpq, k1, k2)


def _pack_table(t):
    tb = t.astype(BF16)
    lo = lax.bitcast_convert_type(tb[:, :HALF], jnp.uint16).astype(jnp.uint32)
    hi = lax.bitcast_convert_type(tb[:, HALF:], jnp.uint16).astype(jnp.uint32)
    return (lo | (hi << 16)).reshape(t.shape[0] * ROWS_PER_EXPERT, LANES)


def _unpack(w):
    lo = pltpu.bitcast(lax.shift_left(w, jnp.uint32(16)), F32)
    hi = pltpu.bitcast(w & jnp.uint32(0xFFFF0000), F32)
    return lo, hi


def _gather_row(tab_ref, row):
    return tab_ref[pl.ds(pl.multiple_of(row, ROWS_PER_EXPERT), ROWS_PER_EXPERT), :]


STAGE_ROWS = PEER_SLOTS * ROWS_PER_EXPERT


WREP = SUBLANES


def _peer_u_kernel(idx_ref, tab_ref, x_ref, gate_ref, e8_ref, w_ref, *stages):
    tb = w_ref.shape[0]
    sub = lax.broadcasted_iota(I32, (SUBLANES, PEER_SLOTS), 0)

    def group(g, c):
        t0 = pl.multiple_of(g * SUBLANES, SUBLANES)
        act8 = jnp.zeros((SUBLANES, PEER_SLOTS), F32)
        xlo, xhi, rows_t = [], [], []
        for r in range(SUBLANES):
            t = t0 + r
            xv = x_ref[pl.ds(pl.multiple_of(t * SUBLANES, SUBLANES), SUBLANES), :]
            xlo.append(jnp.concatenate([xv[0:ROWS_PER_EXPERT]] * 2, axis=0))
            xhi.append(jnp.concatenate([xv[ROWS_PER_EXPERT:]] * 2, axis=0))
            rows_t.append(idx_ref.at[t])
        for j in range(0, PEER_SLOTS, 2):
            for r in range(SUBLANES):
                pair = jnp.concatenate(
                    [_gather_row(tab_ref, rows_t[r][j]), _gather_row(tab_ref, rows_t[r][j + 1])], axis=0)
                lo, hi = _unpack(pair)
                row = j * ROWS_PER_EXPERT
                stages[r][row:row + 2 * ROWS_PER_EXPERT, :] = lo * xlo[r] + hi * xhi[r]
        for r in range(SUBLANES):
            stage = stages[r]
            cs = stage[pl.ds(0, PEER_SLOTS, stride=ROWS_PER_EXPERT), :]
            for q in range(1, ROWS_PER_EXPERT):
                cs = cs + stage[pl.ds(q, PEER_SLOTS, stride=ROWS_PER_EXPERT), :]
            act = jnp.sum(cs.T, axis=0, keepdims=True)
            act8 = jnp.where(sub == r, act, act8)
        rows = pl.ds(t0, SUBLANES)
        w8 = gate_ref[rows, :] * jax.nn.gelu(act8)
        w_ref[rows, :] = jnp.dot(w8, e8_ref[...], precision=HIGHEST, preferred_element_type=F32)
        return c

    lax.fori_loop(0, tb // SUBLANES, group, 0)


def _peer_u(idx, tab, h2_rows, gate):
    n = idx.shape[0]
    tb = TB_PEER
    slot = np.arange(PEER_SLOTS * WREP) // WREP
    e8 = jnp.asarray((np.arange(PEER_SLOTS)[:, None] == slot[None, :]).astype(np.float32))
    return pl.pallas_call(
        _peer_u_kernel,
        grid=(n // tb,),
        in_specs=[
            pl.BlockSpec((tb, PEER_SLOTS), lambda i: (i, 0), memory_space=pltpu.SMEM),
            pl.BlockSpec(tab.shape, lambda i: (0, 0), pipeline_mode=pl.Buffered(1)),
            pl.BlockSpec((tb * SUBLANES, LANES), lambda i: (i, 0)),
            pl.BlockSpec((tb, PEER_SLOTS), lambda i: (i, 0)),
            pl.BlockSpec(e8.shape, lambda i: (0, 0)),
        ],
        out_specs=pl.BlockSpec((tb, PEER_SLOTS * WREP), lambda i: (i, 0)),
        out_shape=jax.ShapeDtypeStruct((n, PEER_SLOTS * WREP), F32),
        scratch_shapes=[pltpu.VMEM((STAGE_ROWS, LANES), F32) for _ in range(SUBLANES)],
        compiler_params=pltpu.CompilerParams(
            dimension_semantics=("arbitrary",), vmem_limit_bytes=VMEM_LIMIT),
        name="peer_u",
    )(idx, tab, h2_rows, gate, e8)


PV_PAIRS = 4


def _peer_v_kernel(idx_ref, wrep_ref, tab_ref, x1_ref, gt2_ref, mask_ref, o_ref, *stages):
    tb = idx_ref.shape[0]
    gt2 = gt2_ref[0]
    mask = mask_ref[...]

    def group(g, c):
        for p in range(PV_PAIRS):
            stage = stages[p]
            tokens = [(g * PV_PAIRS + p) * 2 + u for u in range(2)]
            for u, t in enumerate(tokens):
                rows_t = idx_ref.at[t]
                for j in range(PEER_SLOTS):
                    stage[j * ROWS_PER_EXPERT:(j + 1) * ROWS_PER_EXPERT, u * LANES:(u + 1) * LANES] = (
                        _gather_row(tab_ref, rows_t[j]))
            w16 = jnp.concatenate([wrep_ref[pl.ds(t, 1), :] * mask for t in tokens], axis=0)
            hi = w16.astype(BF16)
            lo = (w16 - hi.astype(F32)).astype(BF16)
            out = jnp.dot(jnp.concatenate([hi, lo], axis=0), pltpu.bitcast(stage[...], BF16),
                          preferred_element_type=F32)
            for u, t in enumerate(tokens):
                r0 = u * SUBLANES
                peer = (out[r0:r0 + SUBLANES, u * LANES:(u + 1) * LANES]
                        + out[2 * SUBLANES + r0:3 * SUBLANES + r0, u * LANES:(u + 1) * LANES])
                rows = pl.ds(pl.multiple_of(t * SUBLANES, SUBLANES), SUBLANES)
                o_ref[rows, :] = x1_ref[rows, :] + gt2 * peer
        return c

    lax.fori_loop(0, tb // (2 * PV_PAIRS), group, 0)


def _peer_v(idx, wrep, tab, x1_rows, gt2_rows, blocks_per_batch):
    n = idx.shape[0]
    tb = TB_PEER
    q = np.arange(SUBLANES)
    piece = 2 * (q % ROWS_PER_EXPERT) + q // ROWS_PER_EXPERT
    mask = jnp.asarray((np.arange(PEER_SLOTS * WREP)[None, :] % WREP == piece[:, None]).astype(np.float32))
    return pl.pallas_call(
        _peer_v_kernel,
        grid=(n // tb,),
        in_specs=[
            pl.BlockSpec((tb, PEER_SLOTS), lambda i: (i, 0), memory_space=pltpu.SMEM),
            pl.BlockSpec((tb, PEER_SLOTS * WREP), lambda i: (i, 0)),
            pl.BlockSpec(tab.shape, lambda i: (0, 0), pipeline_mode=pl.Buffered(1)),
            pl.BlockSpec((tb * SUBLANES, LANES), lambda i: (i, 0)),
            pl.BlockSpec((1, SUBLANES, LANES), lambda i: (i // blocks_per_batch, 0, 0)),
            pl.BlockSpec(mask.shape, lambda i: (0, 0)),
        ],
        out_specs=pl.BlockSpec((tb * SUBLANES, LANES), lambda i: (i, 0)),
        out_shape=jax.ShapeDtypeStruct((n * SUBLANES, LANES), F32),
        scratch_shapes=[pltpu.VMEM((STAGE_ROWS, 2 * LANES), jnp.uint32) for _ in range(PV_PAIRS)],
        compiler_params=pltpu.CompilerParams(
            dimension_semantics=("arbitrary",), vmem_limit_bytes=VMEM_LIMIT),
        name="peer_v",
    )(idx, wrep, tab, x1_rows, gt2_rows, mask)


def _layer(x, mod, g_norm1, g_norm2, w_in, q_norm_g, k_norm_g, conv_w, conv_b, conv_ln_g, conv_ln_b,
           rel_bias, g_out_conv, g_out_attn, w_out, w_peer_q, peer_k1, peer_k2, peer_u, peer_v):
    bsz, s, d = x.shape
    n = bsz * s
    mod3 = mod.reshape(bsz, 6, d)
    row = lambda a: a.reshape(1, -1)

    c0 = 2 * CONV_CH
    c1 = c0 + 3 * ATT_W
    c2 = c1 + IDX_HEADS * IDX_DIM
    c3 = c2 + IDX_DIM
    wa = w_in[:, :c0].astype(BF16)
    wqkv = w_in[:, c0:c1].astype(BF16)
    widx = jnp.concatenate([w_in[:, c1:c2], w_in[:, c2:c3], w_in[:, c2:c3]], axis=1).astype(BF16)
    wwi = jnp.zeros((SUBLANES, d), F32).at[:IDX_HEADS].set(w_in[:, c3:c3 + IDX_HEADS].T).astype(BF16)
    head = np.arange(ATT_W) // HEAD_DIM
    e2 = jnp.asarray((head[:, None] == head[None, :]).astype(np.float32))

    conv_n, q, k, vt, qi, ki, wit = _inproj(
        x, mod3, row(g_norm1), wa, wqkv, widx, wwi,
        row(jnp.tile(q_norm_g, ATT_HEADS)), row(jnp.tile(k_norm_g, ATT_HEADS)), e2,
        conv_w.reshape(CONV_WIDTH, CONV_CH), row(conv_b), row(conv_ln_g), row(conv_ln_b), row(g_out_conv))
    attn_n = _attn(q, k, vt, qi, ki, wit, rel_bias, row(g_out_attn))

    x1, h2, idx, gate = _mid(
        x, conv_n, attn_n, mod3, w_out[:CONV_CH].astype(BF16), w_out[CONV_CH:].astype(BF16), row(g_norm2),
        w_peer_q.astype(BF16), peer_k1.astype(BF16), peer_k2.astype(BF16))

    idx = idx.reshape(n, PEER_SLOTS)
    w = _peer_u(idx, _pack_table(peer_u), h2.reshape(n * SUBLANES, LANES), gate.reshape(n, PEER_SLOTS))
    gt2_rows = mod3[:, 5, :].reshape(bsz, SUBLANES, LANES)
    out = _peer_v(idx, w, _pack_table(peer_v), x1.reshape(n * SUBLANES, LANES), gt2_rows, s // TB_PEER)
    return out.reshape(bsz, s, d)


def kernel(x, c, w_ada, b_ada, g_norm1, g_norm2, w_in, q_norm_g, k_norm_g, conv_w, conv_b, conv_ln_g,
           conv_ln_b, rel_bias, g_out_conv, g_out_attn, w_out, w_peer_q, peer_k1, peer_k2, peer_u, peer_v):
    depth = w_ada.shape[0]
    for l in range(depth):
        mod = _ada(c, w_ada[l], b_ada[l])
        x = _layer(x, mod, g_norm1[l], g_norm2[l], w_in[l], q_norm_g[l], k_norm_g[l], conv_w[l], conv_b[l],
                   conv_ln_g[l], conv_ln_b[l], rel_bias, g_out_conv[l], g_out_attn[l], w_out[l],
                   w_peer_q[l], peer_k1[l], peer_k2[l], peer_u[l], peer_v[l])
    return x
```

```python
import functools
import math

import numpy as np
import jax
import jax.numpy as jnp
from jax import lax
from jax.experimental import pallas as pl
from jax.experimental.pallas import tpu as pltpu

F32 = jnp.float32
BF16 = jnp.bfloat16
I32 = jnp.int32
HIGHEST = lax.Precision.HIGHEST

D_MODEL = 1024
CHUNK = 64
CONV_CH = 512
CONV_WIDTH = 31
ATT_HEADS = 8
HEAD_DIM = 64
ATT_W = ATT_HEADS * HEAD_DIM
IDX_HEADS = 4
IDX_DIM = 64
IDX_SCALE = (IDX_HEADS * IDX_DIM) ** -0.5
TOPK_MAX = 256
REL_BUCKETS = 32
REL_MAX_DIST = 128
PEER_HEADS = 8
N_KEYS = 128
N_EXPERTS = N_KEYS * N_KEYS
PEER_TOPK = 16
PEER_SLOTS = PEER_HEADS * PEER_TOPK
EPS = 1e-6

LANES = 128
SUBLANES = 8
VMEM_LIMIT = 56 * 1024 * 1024

TM_IN = 512
CONV_ROWS = 64
HALO = 32
TQ = 256
KT = 256
TM_MID = 256
TB_PEER = 128
HALF = D_MODEL // 2
ROWS_PER_EXPERT = HALF // LANES

NT_DIMS = (((1,), (1,)), ((), ()))

_NEG_INF_BITS = int(np.array(-np.inf, np.float32).view(np.int32))
KEY_NEG_INF = _NEG_INF_BITS ^ 0x7FFFFFFF
INT_MIN = -(2 ** 31)


def _nt(a, b, precision=None):
    return lax.dot_general(a, b, NT_DIMS, precision=precision, preferred_element_type=F32)


def _ada_kernel(c_ref, w_ref, b_ref, o_ref):
    a = jax.nn.silu(c_ref[...])
    o_ref[...] = jnp.dot(a, w_ref[...], precision=HIGHEST, preferred_element_type=F32) + b_ref[...]


def _ada(c, w_ada, b_ada):
    bsz, d = c.shape
    return pl.pallas_call(
        _ada_kernel,
        grid=(6,),
        in_specs=[
            pl.BlockSpec((bsz, d), lambda j: (0, 0)),
            pl.BlockSpec((d, d), lambda j: (0, j)),
            pl.BlockSpec((1, d), lambda j: (0, j)),
        ],
        out_specs=pl.BlockSpec((bsz, d), lambda j: (0, j)),
        out_shape=jax.ShapeDtypeStruct((bsz, 6 * d), F32),
        name="ada",
    )(c, w_ada, b_ada.reshape(1, 6 * d))


def _inproj_kernel(x_ref, mod_ref, g1_ref, wa_ref, wqkv_ref, widx_ref, wwi_ref, gq_ref, gk_ref, e2_ref,
                   cw_ref, cb_ref, lng_ref, lnb_ref, goc_ref,
                   conv_ref, q_ref, k_ref, vt_ref, qi_ref, ki_ref, wit_ref, ubuf):
    j = pl.program_id(1)
    tm = x_ref.shape[1]
    x = x_ref[0]
    sh1 = mod_ref[0, 0:1, :]
    sc1 = mod_ref[0, 1:2, :]
    r = lax.rsqrt(jnp.mean(x * x, axis=-1, keepdims=True) + EPS)
    h = (x * r) * g1_ref[...] * (1.0 + sc1) + sh1
    hb = h.astype(BF16)

    pa = jnp.dot(hb, wa_ref[...], preferred_element_type=F32)
    u = pa[:, :CONV_CH] * jax.nn.sigmoid(pa[:, CONV_CH:])

    @pl.when(j == 0)
    def _():
        ubuf[0:HALO, :] = jnp.zeros((HALO, CONV_CH), F32)

    ubuf[HALO:HALO + tm, :] = u
    first = HALO - (CONV_WIDTH - 1)
    for rb in range(tm // CONV_ROWS):
        base = rb * CONV_ROWS
        acc = jnp.zeros((CONV_ROWS, CONV_CH), F32) + cb_ref[...]
        for t in range(CONV_WIDTH):
            acc = acc + cw_ref[t:t + 1, :] * ubuf[base + first + t:base + first + t + CONV_ROWS, :]
        mu = jnp.mean(acc, axis=-1, keepdims=True)
        xc = acc - mu
        y = xc * lax.rsqrt(jnp.mean(xc * xc, axis=-1, keepdims=True) + EPS)
        y = jax.nn.silu(y * lng_ref[...] + lnb_ref[...])
        y = y * lax.rsqrt(jnp.mean(y * y, axis=-1, keepdims=True) + EPS) * goc_ref[...]
        conv_ref[0, base:base + CONV_ROWS, :] = y.astype(BF16)
    ubuf[0:HALO, :] = ubuf[tm:tm + HALO, :]

    pq = jnp.dot(hb, wqkv_ref[...], preferred_element_type=F32)
    q = pq[:, :ATT_W]
    k = pq[:, ATT_W:2 * ATT_W]
    v = pq[:, 2 * ATT_W:]
    e2 = e2_ref[...]
    qs = jnp.dot(q * q, e2, precision=HIGHEST, preferred_element_type=F32) * (1.0 / HEAD_DIM)
    ks = jnp.dot(k * k, e2, precision=HIGHEST, preferred_element_type=F32) * (1.0 / HEAD_DIM)
    qn = q * lax.rsqrt(qs + EPS) * gq_ref[...] * (HEAD_DIM ** -0.5)
    kn = k * lax.rsqrt(ks + EPS) * gk_ref[...]
    lane = lax.broadcasted_iota(I32, (tm, LANES), 1)
    low = lane < HEAD_DIM
    for p in range(ATT_HEADS // 2):
        slab = qn[:, p * LANES:(p + 1) * LANES]
        q_ref[0, 2 * p] = jnp.where(low, slab, 0.0).T.astype(BF16)
        q_ref[0, 2 * p + 1] = jnp.where(low, 0.0, slab).T.astype(BF16)
        k_ref[0, p] = kn[:, p * LANES:(p + 1) * LANES].astype(BF16)
    vt_ref[0] = v.T.astype(BF16)

    pc = jnp.dot(hb, widx_ref[...], preferred_element_type=F32)
    for p in range(IDX_HEADS // 2):
        slab = pc[:, p * LANES:(p + 1) * LANES]
        qi_ref[0, 2 * p] = jnp.where(low, slab, 0.0).T.astype(BF16)
        qi_ref[0, 2 * p + 1] = jnp.where(low, 0.0, slab).T.astype(BF16)
    ki_ref[0] = pc[:, 2 * LANES:3 * LANES].astype(BF16)
    wit_ref[0] = _nt(wwi_ref[...], hb) * IDX_SCALE


def _inproj(x, mod3, g1, wa, wqkv, widx, wwi, gq, gk, e2, cw, cb, lng, lnb, goc):
    bsz, s, d = x.shape
    tm = TM_IN
    nt = s // tm
    full = lambda shape: pl.BlockSpec(shape, lambda b, j: (0,) * len(shape))
    out_shape = (
        jax.ShapeDtypeStruct((bsz, s, CONV_CH), BF16),
        jax.ShapeDtypeStruct((bsz, ATT_HEADS, LANES, s), BF16),
        jax.ShapeDtypeStruct((bsz, ATT_HEADS // 2, s, LANES), BF16),
        jax.ShapeDtypeStruct((bsz, ATT_W, s), BF16),
        jax.ShapeDtypeStruct((bsz, IDX_HEADS, LANES, s), BF16),
        jax.ShapeDtypeStruct((bsz, s, LANES), BF16),
        jax.ShapeDtypeStruct((bsz, SUBLANES, s), F32),
    )
    out_specs = (
        pl.BlockSpec((1, tm, CONV_CH), lambda b, j: (b, j, 0)),
        pl.BlockSpec((1, ATT_HEADS, LANES, tm), lambda b, j: (b, 0, 0, j)),
        pl.BlockSpec((1, ATT_HEADS // 2, tm, LANES), lambda b, j: (b, 0, j, 0)),
        pl.BlockSpec((1, ATT_W, tm), lambda b, j: (b, 0, j)),
        pl.BlockSpec((1, IDX_HEADS, LANES, tm), lambda b, j: (b, 0, 0, j)),
        pl.BlockSpec((1, tm, LANES), lambda b, j: (b, j, 0)),
        pl.BlockSpec((1, SUBLANES, tm), lambda b, j: (b, 0, j)),
    )
    return pl.pallas_call(
        _inproj_kernel,
        grid=(bsz, nt),
        in_specs=[
            pl.BlockSpec((1, tm, d), lambda b, j: (b, j, 0)),
            pl.BlockSpec((1, 6, d), lambda b, j: (b, 0, 0)),
            full(g1.shape), full(wa.shape), full(wqkv.shape), full(widx.shape), full(wwi.shape),
            full(gq.shape), full(gk.shape), full(e2.shape),
            full(cw.shape), full(cb.shape), full(lng.shape), full(lnb.shape), full(goc.shape),
        ],
        out_specs=out_specs,
        out_shape=out_shape,
        scratch_shapes=[pltpu.VMEM((tm + HALO, CONV_CH), F32)],
        compiler_params=pltpu.CompilerParams(
            dimension_semantics=("arbitrary", "arbitrary"), vmem_limit_bytes=VMEM_LIMIT),
        name="inproj",
    )(x, mod3, g1, wa, wqkv, widx, wwi, gq, gk, e2, cw, cb, lng, lnb, goc)


def _t5_bucket_np(rel):
    half = REL_BUCKETS // 2
    max_exact = half // 2
    ret = np.where(rel > 0, half, 0)
    n = np.abs(rel)
    nf = np.maximum(n, 1).astype(np.float64)
    large = max_exact + (np.log(nf / max_exact) / math.log(REL_MAX_DIST / max_exact)
                         * (half - max_exact)).astype(np.int32)
    large = np.minimum(large, half - 1)
    return (ret + np.where(n < max_exact, n, large)).astype(np.int32)


NEAR = REL_MAX_DIST + TQ
FAR_BUCKET = REL_BUCKETS // 2 - 1


def _near_buckets():
    r = np.arange(NEAR)[:, None]
    t = np.arange(TQ)[None, :]
    return _t5_bucket_np(r - REL_MAX_DIST - t)


def _attn_kernel(q_ref, k_ref, vt_ref, qi_ref, ki_ref, wit_ref, bkt_ref, rb_ref, goa_ref, o_ref,
                 keys_s, maskb_s, relb_s, ot_s, m_s, mo_s, l_s, lg_s, *, nsel, jbits):
    b = pl.program_id(0)
    i = pl.program_id(1)
    t0 = i * TQ
    n_tiles = i + 1
    neg_inf = F32(-jnp.inf)

    @pl.when((b == 0) & (i == 0))
    def _():
        bk = bkt_ref[...]
        for h in range(ATT_HEADS):
            far = rb_ref[FAR_BUCKET, h]
            acc = jnp.zeros((NEAR, TQ), F32)
            for bb in range(REL_BUCKETS):
                acc = jnp.where(bk == bb, rb_ref[bb, h] - far, acc)
            relb_s[h, 0:2 * KT - NEAR, :] = jnp.zeros((2 * KT - NEAR, TQ), F32)
            relb_s[h, 2 * KT - NEAR:2 * KT, :] = acc

    qpos = t0 + lax.broadcasted_iota(I32, (1, TQ), 1)
    limit = (qpos // CHUNK + 1) * CHUNK
    row_iota = lax.broadcasted_iota(I32, (KT, TQ), 0)

    def tile_start(jt):
        return pl.multiple_of(jt * KT, KT)

    wi = wit_ref[0]

    def p1(jt, c):
        ks = tile_start(jt)
        kit = ki_ref[0, pl.ds(ks, KT), :]
        acc = jnp.zeros((KT, TQ), F32)
        for h in range(IDX_HEADS):
            lgt = jnp.dot(kit, qi_ref[0, h], preferred_element_type=F32)
            acc = acc + jnp.maximum(lgt, 0.0) * wi[h:h + 1, :]
        sc = jnp.where(row_iota + ks < limit, acc, neg_inf)
        bits = pltpu.bitcast(sc, I32)
        keys_s[pl.ds(ks, KT), :] = bits ^ (lax.shift_right_arithmetic(bits, 31) & 0x7FFFFFFF)
        return c

    lax.fori_loop(0, n_tiles, p1, 0)

    def count(pred):
        def body(jt, acc):
            ks = tile_start(jt)
            m = pred(keys_s[pl.ds(ks, KT), :], ks).astype(F32)
            return acc + jnp.sum(m.reshape(KT // SUBLANES, SUBLANES, TQ), axis=0)
        acc = lax.fori_loop(0, n_tiles, body, jnp.zeros((SUBLANES, TQ), F32))
        return jnp.sum(acc, axis=0, keepdims=True)

    def select_thr():
        c0 = count(lambda kt, ks: kt >= 0)
        t = jnp.where(c0 >= nsel, 0, INT_MIN).astype(I32)

        def bitstep(it, t):
            cand = t + lax.shift_left(I32(1), I32(30) - it)
            c = count(lambda kt, ks: kt >= cand)
            return jnp.where(c >= nsel, cand, t)

        return lax.fori_loop(0, 31, bitstep, t)

    thr = lax.cond(i > 0, select_thr, lambda: jnp.full((1, TQ), KEY_NEG_INF + 1, I32))

    c_gt = count(lambda kt, ks: kt > thr)
    c_ge = count(lambda kt, ks: kt >= thr)
    need = nsel - c_gt

    def tie_cut():
        def step(it, cut):
            cand = cut + lax.shift_left(I32(1), I32(jbits - 1) - it)
            f = count(lambda kt, ks: jnp.where(kt == thr, row_iota + ks, cand) < cand)
            return jnp.where(f <= need, cand, cut)
        return lax.fori_loop(0, jbits, step, jnp.zeros((1, TQ), I32))

    cut = lax.cond(jnp.max(c_ge) > nsel, tie_cut, lambda: jnp.full((1, TQ), 2 ** jbits - 1, I32))

    def p3(jt, c):
        ks = tile_start(jt)
        kt = keys_s[pl.ds(ks, KT), :]
        tied = jnp.where(kt == thr, jnp.where(row_iota + ks < cut, 0.0, neg_inf), neg_inf)
        maskb_s[pl.ds(ks, KT), :] = jnp.where(kt > thr, 0.0, tied)
        return c

    lax.fori_loop(0, n_tiles, p3, 0)

    m_s[...] = jnp.full((ATT_HEADS, TQ), neg_inf, F32)
    l_s[...] = jnp.zeros((ATT_HEADS, TQ), F32)
    ot_s[...] = jnp.zeros((ATT_W, TQ), F32)

    def att_tile(jt, near):
        ks = tile_start(jt)
        mb = maskb_s[pl.ds(ks, KT), :]
        for h in range(ATT_HEADS):
            l = jnp.dot(k_ref[0, h // 2, pl.ds(ks, KT), :], q_ref[0, h], preferred_element_type=F32) + mb
            if near:
                off = pl.multiple_of((jt - (n_tiles - 2)) * KT, KT)
                l = l + relb_s[h, pl.ds(off, KT), :]
            lg_s[h] = l
            m_old = m_s[h:h + 1, :]
            mo_s[h:h + 1, :] = m_old
            m_s[h:h + 1, :] = jnp.maximum(m_old, jnp.max(l, axis=0, keepdims=True))
        for h in range(ATT_HEADS):
            m_new = m_s[h:h + 1, :]
            m_ref = jnp.where(m_new == neg_inf, 0.0, m_new)
            alpha = jnp.exp(mo_s[h:h + 1, :] - m_ref)
            p = jnp.exp(lg_s[h] - m_ref)
            l_s[h:h + 1, :] = alpha * l_s[h:h + 1, :] + jnp.sum(p, axis=0, keepdims=True)
            vt = vt_ref[0, h * HEAD_DIM:(h + 1) * HEAD_DIM, pl.ds(ks, KT)]
            rows = slice(h * HEAD_DIM, (h + 1) * HEAD_DIM)
            ot_s[rows, :] = ot_s[rows, :] * alpha + jnp.dot(vt, p.astype(BF16), preferred_element_type=F32)

    def far_tile(jt, c):
        att_tile(jt, False)
        return c

    def near_tile(jt, c):
        att_tile(jt, True)
        return c

    n_far = jnp.maximum(n_tiles - 2, 0)
    lax.fori_loop(0, n_far, far_tile, 0)
    lax.fori_loop(n_far, n_tiles, near_tile, 0)
    for h in range(ATT_HEADS):
        rows = slice(h * HEAD_DIM, (h + 1) * HEAD_DIM)
        ot_s[rows, :] = ot_s[rows, :] / l_s[h:h + 1, :]

    ot = ot_s[...]
    ms = jnp.mean(ot * ot, axis=0, keepdims=True)
    y = (ot * lax.rsqrt(ms + EPS)).T * goa_ref[...]
    o_ref[0] = y.astype(BF16)


def _attn(q, k, vt, qi, ki, wit, rel_bias, goa):
    bsz, _, _, s = q.shape
    nsel = min(TOPK_MAX, s // 4)
    assert nsel == TQ and s % TQ == 0, "attention kernel assumes TOPK_MAX-sized query blocks"
    jbits = int(s).bit_length()
    bkt = jnp.asarray(_near_buckets())
    kern = functools.partial(_attn_kernel, nsel=nsel, jbits=jbits)
    return pl.pallas_call(
        kern,
        grid=(bsz, s // TQ),
        in_specs=[
            pl.BlockSpec((1, ATT_HEADS, LANES, TQ), lambda b, i: (b, 0, 0, i)),
            pl.BlockSpec((1, ATT_HEADS // 2, s, LANES), lambda b, i: (b, 0, 0, 0)),
            pl.BlockSpec((1, ATT_W, s), lambda b, i: (b, 0, 0)),
            pl.BlockSpec((1, IDX_HEADS, LANES, TQ), lambda b, i: (b, 0, 0, i)),
            pl.BlockSpec((1, s, LANES), lambda b, i: (b, 0, 0)),
            pl.BlockSpec((1, SUBLANES, TQ), lambda b, i: (b, 0, i)),
            pl.BlockSpec((NEAR, TQ), lambda b, i: (0, 0)),
            pl.BlockSpec(memory_space=pltpu.SMEM),
            pl.BlockSpec((1, ATT_W), lambda b, i: (0, 0)),
        ],
        out_specs=pl.BlockSpec((1, TQ, ATT_W), lambda b, i: (b, i, 0)),
        out_shape=jax.ShapeDtypeStruct((bsz, s, ATT_W), BF16),
        scratch_shapes=[
            pltpu.VMEM((s, TQ), I32),
            pltpu.VMEM((s, TQ), F32),
            pltpu.VMEM((ATT_HEADS, 2 * KT, TQ), F32),
            pltpu.VMEM((ATT_W, TQ), F32),
            pltpu.VMEM((ATT_HEADS, TQ), F32),
            pltpu.VMEM((ATT_HEADS, TQ), F32),
            pltpu.VMEM((ATT_HEADS, TQ), F32),
            pltpu.VMEM((ATT_HEADS, KT, TQ), F32),
        ],
        compiler_params=pltpu.CompilerParams(
            dimension_semantics=("arbitrary", "arbitrary"), vmem_limit_bytes=VMEM_LIMIT),
        name="attn",
    )(q, k, vt, qi, ki, wit, bkt, rel_bias, goa)


def _topk_rows(s, payload, k):
    nrows = s.shape[0]
    rows = lax.broadcasted_iota(I32, s.shape, 0).astype(F32)
    vals, pays = [], []
    for _ in range(k):
        m = jnp.max(s, axis=0, keepdims=True)
        ix = jnp.min(jnp.where(s == m, rows, float(nrows)), axis=0, keepdims=True)
        hit = rows == ix
        if payload is None:
            pays.append(ix)
        else:
            pays.append(jnp.max(jnp.where(hit, payload, -1.0), axis=0, keepdims=True))
        vals.append(m)
        s = jnp.where(hit, -jnp.inf, s)
    return jnp.concatenate(vals, axis=0), jnp.concatenate(pays, axis=0)


_PAIR_ROWS = tuple((a, PEER_TOPK // (a + 1)) for a in range(PEER_TOPK // 2))


def _mid_kernel(x_ref, cn_ref, an_ref, mod_ref, wo1_ref, wo2_ref, g2_ref, wpq_ref, k1_ref, k2_ref,
                x1_ref, h2_ref, idx_ref, gate_ref, qq_s, idt_s, gt_s):
    tm = x_ref.shape[1]
    x = x_ref[0]
    gt1 = mod_ref[0, 2:3, :]
    sh2 = mod_ref[0, 3:4, :]
    sc2 = mod_ref[0, 4:5, :]
    proj = (jnp.dot(cn_ref[0], wo1_ref[...], preferred_element_type=F32)
            + jnp.dot(an_ref[0], wo2_ref[...], preferred_element_type=F32))
    x1 = x + gt1 * proj
    x1_ref[0] = x1
    r = lax.rsqrt(jnp.mean(x1 * x1, axis=-1, keepdims=True) + EPS)
    h2 = (x1 * r) * g2_ref[...] * (1.0 + sc2) + sh2
    h2_ref[0] = h2
    qq_s[...] = jnp.dot(h2.astype(BF16), wpq_ref[...], preferred_element_type=F32).astype(BF16)

    def route_unit(hh, lt):
        rows = pl.ds(lt * LANES, LANES)
        q1 = qq_s[rows, pl.ds(pl.multiple_of(hh * 2 * N_KEYS, LANES), N_KEYS)]
        q2 = qq_s[rows, pl.ds(pl.multiple_of(hh * 2 * N_KEYS + N_KEYS, LANES), N_KEYS)]
        v1, i1 = _topk_rows(_nt(k1_ref[hh], q1), None, PEER_TOPK)
        v2, i2 = _topk_rows(_nt(k2_ref[hh], q2), None, PEER_TOPK)
        sub = lax.broadcasted_iota(I32, (SUBLANES, LANES), 0)
        cands, cidxs = [], []
        for a, nb in _PAIR_ROWS:
            nrows = max(nb, SUBLANES)
            val = v1[a:a + 1, :] + v2[0:nrows, :]
            if nb < SUBLANES:
                val = jnp.where(sub < nb, val, -jnp.inf)
            cands.append(val)
            cidxs.append(i1[a:a + 1, :] * float(N_KEYS) + i2[0:nrows, :])
        half = PEER_TOPK // 2
        cands.append(v1[half:, :] + v2[0:1, :])
        cidxs.append(i1[half:, :] * float(N_KEYS) + i2[0:1, :])
        best, experts = _topk_rows(jnp.concatenate(cands, axis=0), jnp.concatenate(cidxs, axis=0), PEER_TOPK)
        e = jnp.exp(best - best[0:1, :])
        g = e / jnp.sum(e, axis=0, keepdims=True)
        slots = pl.ds(pl.multiple_of(hh * PEER_TOPK, PEER_TOPK), PEER_TOPK)
        cols = pl.ds(lt * LANES, LANES)
        idt_s[slots, cols] = experts * float(ROWS_PER_EXPERT)
        gt_s[slots, cols] = g

    def route(hh, c):
        for lt in range(tm // LANES):
            route_unit(hh, lt)
        return c

    lax.fori_loop(0, PEER_HEADS, route, 0)
    idx_ref[0] = idt_s[...].T.astype(I32)
    gate_ref[0] = gt_s[...].T


def _mid(x, cn, an, mod3, wo1, wo2, g2, wpq, k1, k2):
    bsz, s, d = x.shape
    tm = TM_MID
    full = lambda shape: pl.BlockSpec(shape, lambda b, j: (0,) * len(shape))
    tok = lambda w: pl.BlockSpec((1, tm, w), lambda b, j: (b, j, 0))
    return pl.pallas_call(
        _mid_kernel,
        grid=(bsz, s // tm),
        in_specs=[tok(d), tok(CONV_CH), tok(ATT_W), pl.BlockSpec((1, 6, d), lambda b, j: (b, 0, 0)),
                  full(wo1.shape), full(wo2.shape), full(g2.shape), full(wpq.shape), full(k1.shape), full(k2.shape)],
        out_specs=(tok(d), tok(d), tok(PEER_SLOTS), tok(PEER_SLOTS)),
        out_shape=(
            jax.ShapeDtypeStruct((bsz, s, d), F32),
            jax.ShapeDtypeStruct((bsz, s, d), F32),
            jax.ShapeDtypeStruct((bsz, s, PEER_SLOTS), I32),
            jax.ShapeDtypeStruct((bsz, s, PEER_SLOTS), F32),
        ),
        scratch_shapes=[
            pltpu.VMEM((tm, PEER_HEADS * 2 * N_KEYS), BF16),
            pltpu.VMEM((PEER_SLOTS, tm), F32),
            pltpu.VMEM((PEER_SLOTS, tm), F32),
        ],
        compiler_params=pltpu.CompilerParams(
            dimension_semantics=("arbitrary", "arbitrary"), vmem_limit_bytes=VMEM_LIMIT),
        name="mid",
    )(x, cn, an, mod3, wo1, wo2, g2, wpq, k1, k2)


def _pack_table(t):
    tb = t.astype(BF16)
    lo = lax.bitcast_convert_type(tb[:, :HALF], jnp.uint16).astype(jnp.uint32)
    hi = lax.bitcast_convert_type(tb[:, HALF:], jnp.uint16).astype(jnp.uint32)
    return (lo | (hi << 16)).reshape(t.shape[0] * ROWS_PER_EXPERT, LANES)


def _unpack(w):
    lo = pltpu.bitcast(lax.shift_left(w, jnp.uint32(16)), F32)
    hi = pltpu.bitcast(w & jnp.uint32(0xFFFF0000), F32)
    return lo, hi


def _gather_row(tab_ref, row):
    return tab_ref[pl.ds(pl.multiple_of(row, ROWS_PER_EXPERT), ROWS_PER_EXPERT), :]


STAGE_ROWS = PEER_SLOTS * ROWS_PER_EXPERT


def _token_pieces(xg, r, first, count):
    return jnp.concatenate(
        [xg[r:r + 1, (first + q) * LANES:(first + q + 1) * LANES] for q in range(count)], axis=0)


def _peer_u_kernel(idx_ref, tab_ref, x_ref, gate_ref, w_ref, *stages):
    tb = w_ref.shape[0]
    sub = lax.broadcasted_iota(I32, (SUBLANES, PEER_SLOTS), 0)

    def group(g, c):
        t0 = pl.multiple_of(g * SUBLANES, SUBLANES)
        act8 = jnp.zeros((SUBLANES, PEER_SLOTS), F32)
        xg = x_ref[pl.ds(t0, SUBLANES), :]
        xlo, xhi, rows_t = [], [], []
        for r in range(SUBLANES):
            xlo.append(jnp.concatenate([_token_pieces(xg, r, 0, ROWS_PER_EXPERT)] * 2, axis=0))
            xhi.append(jnp.concatenate([_token_pieces(xg, r, ROWS_PER_EXPERT, ROWS_PER_EXPERT)] * 2, axis=0))
            rows_t.append(idx_ref.at[t0 + r])
        for j in range(0, PEER_SLOTS, 2):
            for r in range(SUBLANES):
                pair = jnp.concatenate(
                    [_gather_row(tab_ref, rows_t[r][j]), _gather_row(tab_ref, rows_t[r][j + 1])], axis=0)
                lo, hi = _unpack(pair)
                row = j * ROWS_PER_EXPERT
                stages[r][row:row + 2 * ROWS_PER_EXPERT, :] = lo * xlo[r] + hi * xhi[r]
        for r in range(SUBLANES):
            stage = stages[r]
            cs = stage[pl.ds(0, PEER_SLOTS, stride=ROWS_PER_EXPERT), :]
            for q in range(1, ROWS_PER_EXPERT):
                cs = cs + stage[pl.ds(q, PEER_SLOTS, stride=ROWS_PER_EXPERT), :]
            act = jnp.sum(cs.T, axis=0, keepdims=True)
            act8 = jnp.where(sub == r, act, act8)
        rows = pl.ds(t0, SUBLANES)
        w_ref[rows, :] = gate_ref[rows, :] * jax.nn.gelu(act8)
        return c

    lax.fori_loop(0, tb // SUBLANES, group, 0)


def _peer_u(idx, tab, h2, gate):
    n, d = h2.shape
    tb = TB_PEER
    return pl.pallas_call(
        _peer_u_kernel,
        grid=(n // tb,),
        in_specs=[
            pl.BlockSpec((tb, PEER_SLOTS), lambda i: (i, 0), memory_space=pltpu.SMEM),
            pl.BlockSpec(tab.shape, lambda i: (0, 0), pipeline_mode=pl.Buffered(1)),
            pl.BlockSpec((tb, d), lambda i: (i, 0)),
            pl.BlockSpec((tb, PEER_SLOTS), lambda i: (i, 0)),
        ],
        out_specs=pl.BlockSpec((tb, PEER_SLOTS), lambda i: (i, 0)),
        out_shape=jax.ShapeDtypeStruct((n, PEER_SLOTS), F32),
        scratch_shapes=[pltpu.VMEM((STAGE_ROWS, LANES), F32) for _ in range(SUBLANES)],
        compiler_params=pltpu.CompilerParams(
            dimension_semantics=("arbitrary",), vmem_limit_bytes=VMEM_LIMIT),
        name="peer_u",
    )(idx, tab, h2, gate)


PV_PAIRS = 4


def _peer_v_kernel(idx_ref, w_ref, tab_ref, x1_ref, gt2_ref, e8_ref, mask_ref, o_ref, *stages):
    tb = idx_ref.shape[0]
    gt2 = gt2_ref[0]
    mask = mask_ref[...]
    e8 = e8_ref[...]

    def group(g, c):
        g0 = pl.multiple_of(g * 2 * PV_PAIRS, 2 * PV_PAIRS)
        peers = []
        for p in range(PV_PAIRS):
            stage = stages[p]
            t0 = g0 + 2 * p
            for u in range(2):
                rows_t = idx_ref.at[t0 + u]
                for j in range(PEER_SLOTS):
                    stage[j * ROWS_PER_EXPERT:(j + 1) * ROWS_PER_EXPERT, u * LANES:(u + 1) * LANES] = (
                        _gather_row(tab_ref, rows_t[j]))
            w2 = w_ref[pl.ds(t0, 2), :]
            hi = w2.astype(BF16).astype(F32)
            rep = jnp.dot(jnp.concatenate([hi, w2 - hi], axis=0).astype(BF16), e8,
                          preferred_element_type=F32)
            lhs = jnp.concatenate([rep[i:i + 1, :] * mask for i in range(4)], axis=0).astype(BF16)
            out = jnp.dot(lhs, pltpu.bitcast(stage[...], BF16), preferred_element_type=F32)
            for u in range(2):
                r0 = u * SUBLANES
                peers.append(out[r0:r0 + SUBLANES, u * LANES:(u + 1) * LANES]
                             + out[2 * SUBLANES + r0:3 * SUBLANES + r0, u * LANES:(u + 1) * LANES])
        rows = pl.ds(g0, 2 * PV_PAIRS)
        for q in range(SUBLANES):
            cols = slice(q * LANES, (q + 1) * LANES)
            piece = jnp.concatenate([peer[q:q + 1, :] for peer in peers], axis=0)
            o_ref[rows, cols] = x1_ref[rows, cols] + gt2[:, cols] * piece
        return c

    lax.fori_loop(0, tb // (2 * PV_PAIRS), group, 0)


def _peer_v(idx, w, tab, x1, gt2, blocks_per_batch):
    n, d = x1.shape
    tb = TB_PEER
    q = np.arange(SUBLANES)
    piece = 2 * (q % ROWS_PER_EXPERT) + q // ROWS_PER_EXPERT
    lane = np.arange(PEER_SLOTS * SUBLANES)
    mask = jnp.asarray((lane[None, :] % SUBLANES == piece[:, None]).astype(np.float32))
    e8 = jnp.asarray(np.arange(PEER_SLOTS)[:, None] == lane[None, :] // SUBLANES, BF16)
    return pl.pallas_call(
        _peer_v_kernel,
        grid=(n // tb,),
        in_specs=[
            pl.BlockSpec((tb, PEER_SLOTS), lambda i: (i, 0), memory_space=pltpu.SMEM),
            pl.BlockSpec((tb, PEER_SLOTS), lambda i: (i, 0)),
            pl.BlockSpec(tab.shape, lambda i: (0, 0), pipeline_mode=pl.Buffered(1)),
            pl.BlockSpec((tb, d), lambda i: (i, 0)),
            pl.BlockSpec((1, 1, d), lambda i: (i // blocks_per_batch, 0, 0)),
            pl.BlockSpec(e8.shape, lambda i: (0, 0)),
            pl.BlockSpec(mask.shape, lambda i: (0, 0)),
        ],
        out_specs=pl.BlockSpec((tb, d), lambda i: (i, 0)),
        out_shape=jax.ShapeDtypeStruct((n, d), F32),
        scratch_shapes=[pltpu.VMEM((STAGE_ROWS, 2 * LANES), jnp.uint32) for _ in range(PV_PAIRS)],
        compiler_params=pltpu.CompilerParams(
            dimension_semantics=("arbitrary",), vmem_limit_bytes=VMEM_LIMIT),
        name="peer_v",
    )(idx, w, tab, x1, gt2, e8, mask)


def _layer(x, mod, g_norm1, g_norm2, w_in, q_norm_g, k_norm_g, conv_w, conv_b, conv_ln_g, conv_ln_b,
           rel_bias, g_out_conv, g_out_attn, w_out, w_peer_q, peer_k1, peer_k2, peer_u, peer_v):
    bsz, s, d = x.shape
    n = bsz * s
    mod3 = mod.reshape(bsz, 6, d)
    row = lambda a: a.reshape(1, -1)

    c0 = 2 * CONV_CH
    c1 = c0 + 3 * ATT_W
    c2 = c1 + IDX_HEADS * IDX_DIM
    c3 = c2 + IDX_DIM
    wa = w_in[:, :c0].astype(BF16)
    wqkv = w_in[:, c0:c1].astype(BF16)
    widx = jnp.concatenate([w_in[:, c1:c2], w_in[:, c2:c3], w_in[:, c2:c3]], axis=1).astype(BF16)
    wwi = jnp.zeros((SUBLANES, d), F32).at[:IDX_HEADS].set(w_in[:, c3:c3 + IDX_HEADS].T).astype(BF16)
    head = np.arange(ATT_W) // HEAD_DIM
    e2 = jnp.asarray((head[:, None] == head[None, :]).astype(np.float32))

    conv_n, q, k, vt, qi, ki, wit = _inproj(
        x, mod3, row(g_norm1), wa, wqkv, widx, wwi,
        row(jnp.tile(q_norm_g, ATT_HEADS)), row(jnp.tile(k_norm_g, ATT_HEADS)), e2,
        conv_w.reshape(CONV_WIDTH, CONV_CH), row(conv_b), row(conv_ln_g), row(conv_ln_b), row(g_out_conv))
    attn_n = _attn(q, k, vt, qi, ki, wit, rel_bias, row(g_out_attn))

    x1, h2, idx, gate = _mid(
        x, conv_n, attn_n, mod3, w_out[:CONV_CH].astype(BF16), w_out[CONV_CH:].astype(BF16), row(g_norm2),
        w_peer_q.astype(BF16), peer_k1.astype(BF16), peer_k2.astype(BF16))

    idx = idx.reshape(n, PEER_SLOTS)
    w = _peer_u(idx, _pack_table(peer_u), h2.reshape(n, d), gate.reshape(n, PEER_SLOTS))
    out = _peer_v(idx, w, _pack_table(peer_v), x1.reshape(n, d), mod3[:, 5:6, :], s // TB_PEER)
    return out.reshape(bsz, s, d)


def kernel(x, c, w_ada, b_ada, g_norm1, g_norm2, w_in, q_norm_g, k_norm_g, conv_w, conv_b, conv_ln_g,
           conv_ln_b, rel_bias, g_out_conv, g_out_attn, w_out, w_peer_q, peer_k1, peer_k2, peer_u, peer_v):
    depth = w_ada.shape[0]
    for l in range(depth):
        mod = _ada(c, w_ada[l], b_ada[l])
        x = _layer(x, mod, g_norm1[l], g_norm2[l], w_in[l], q_norm_g[l], k_norm_g[l], conv_w[l], conv_b[l],
                   conv_ln_g[l], conv_ln_b[l], rel_bias, g_out_conv[l], g_out_attn[l], w_out[l],
                   w_peer_q[l], peer_k1[l], peer_k2[l], peer_u[l], peer_v[l])
    return x
```

```python
import functools
import math

import numpy as np
import jax
import jax.numpy as jnp
from jax import lax
from jax.experimental import pallas as pl
from jax.experimental.pallas import tpu as pltpu

F32 = jnp.float32
BF16 = jnp.bfloat16
I32 = jnp.int32
HIGHEST = lax.Precision.HIGHEST

D_MODEL = 1024
CHUNK = 64
CONV_CH = 512
CONV_WIDTH = 31
ATT_HEADS = 8
HEAD_DIM = 64
ATT_W = ATT_HEADS * HEAD_DIM
IDX_HEADS = 4
IDX_DIM = 64
IDX_SCALE = (IDX_HEADS * IDX_DIM) ** -0.5
TOPK_MAX = 256
REL_BUCKETS = 32
REL_MAX_DIST = 128
PEER_HEADS = 8
N_KEYS = 128
N_EXPERTS = N_KEYS * N_KEYS
PEER_TOPK = 16
PEER_SLOTS = PEER_HEADS * PEER_TOPK
EPS = 1e-6

LANES = 128
SUBLANES = 8
PACK = 16
VMEM_LIMIT = 56 * 1024 * 1024

TM_IN = 512
CONV_ROWS = 64
HALO = 32
TQ = 256
KT = 256
TM_MID = 256
TB_PEER = 128
HALF = D_MODEL // 2
ROWS_PER_EXPERT = HALF // LANES

NT_DIMS = (((1,), (1,)), ((), ()))

_NEG_INF_BITS = int(np.array(-np.inf, np.float32).view(np.int32))
KEY_NEG_INF = _NEG_INF_BITS ^ 0x7FFFFFFF
MIN_NORMAL_BITS = 0x00800000
BAND = MIN_NORMAL_BITS >> 16
INT_MIN = -(2 ** 31)


def _nt(a, b, precision=None):
    return lax.dot_general(a, b, NT_DIMS, precision=precision, preferred_element_type=F32)


def _ada_kernel(c_ref, w_ref, b_ref, o_ref):
    a = jax.nn.silu(c_ref[...])
    o_ref[...] = jnp.dot(a, w_ref[...], precision=HIGHEST, preferred_element_type=F32) + b_ref[...]


def _ada(c, w_ada, b_ada):
    bsz, d = c.shape
    return pl.pallas_call(
        _ada_kernel,
        grid=(6,),
        in_specs=[
            pl.BlockSpec((bsz, d), lambda j: (0, 0)),
            pl.BlockSpec((d, d), lambda j: (0, j)),
            pl.BlockSpec((1, d), lambda j: (0, j)),
        ],
        out_specs=pl.BlockSpec((bsz, d), lambda j: (0, j)),
        out_shape=jax.ShapeDtypeStruct((bsz, 6 * d), F32),
        name="ada",
    )(c, w_ada, b_ada.reshape(1, 6 * d))


def _inproj_kernel(x_ref, mod_ref, g1_ref, wa_ref, wqkv_ref, widx_ref, wwi_ref, gq_ref, gk_ref, e2_ref,
                   cw_ref, cb_ref, lng_ref, lnb_ref, goc_ref,
                   conv_ref, q_ref, k_ref, vt_ref, qi_ref, ki_ref, wit_ref, ubuf):
    j = pl.program_id(1)
    tm = x_ref.shape[1]
    x = x_ref[0]
    sh1 = mod_ref[0, 0:1, :]
    sc1 = mod_ref[0, 1:2, :]
    r = lax.rsqrt(jnp.mean(x * x, axis=-1, keepdims=True) + EPS)
    h = (x * r) * g1_ref[...] * (1.0 + sc1) + sh1
    hb = h.astype(BF16)

    pa = jnp.dot(hb, wa_ref[...], preferred_element_type=F32)
    u = pa[:, :CONV_CH] * jax.nn.sigmoid(pa[:, CONV_CH:])

    @pl.when(j == 0)
    def _():
        ubuf[0:HALO, :] = jnp.zeros((HALO, CONV_CH), F32)

    ubuf[HALO:HALO + tm, :] = u
    first = HALO - (CONV_WIDTH - 1)
    for rb in range(tm // CONV_ROWS):
        base = rb * CONV_ROWS
        acc = jnp.zeros((CONV_ROWS, CONV_CH), F32) + cb_ref[...]
        for t in range(CONV_WIDTH):
            acc = acc + cw_ref[t:t + 1, :] * ubuf[base + first + t:base + first + t + CONV_ROWS, :]
        mu = jnp.mean(acc, axis=-1, keepdims=True)
        xc = acc - mu
        y = xc * lax.rsqrt(jnp.mean(xc * xc, axis=-1, keepdims=True) + EPS)
        y = jax.nn.silu(y * lng_ref[...] + lnb_ref[...])
        y = y * lax.rsqrt(jnp.mean(y * y, axis=-1, keepdims=True) + EPS) * goc_ref[...]
        conv_ref[0, base:base + CONV_ROWS, :] = y.astype(BF16)
    ubuf[0:HALO, :] = ubuf[tm:tm + HALO, :]

    pq = jnp.dot(hb, wqkv_ref[...], preferred_element_type=F32)
    q = pq[:, :ATT_W]
    k = pq[:, ATT_W:2 * ATT_W]
    v = pq[:, 2 * ATT_W:]
    e2 = e2_ref[...]
    qs = jnp.dot(q * q, e2, precision=HIGHEST, preferred_element_type=F32) * (1.0 / HEAD_DIM)
    ks = jnp.dot(k * k, e2, precision=HIGHEST, preferred_element_type=F32) * (1.0 / HEAD_DIM)
    qn = q * lax.rsqrt(qs + EPS) * gq_ref[...] * (HEAD_DIM ** -0.5)
    kn = k * lax.rsqrt(ks + EPS) * gk_ref[...]
    lane = lax.broadcasted_iota(I32, (tm, LANES), 1)
    low = lane < HEAD_DIM
    for p in range(ATT_HEADS // 2):
        slab = qn[:, p * LANES:(p + 1) * LANES]
        q_ref[0, 2 * p] = jnp.where(low, slab, 0.0).T.astype(BF16)
        q_ref[0, 2 * p + 1] = jnp.where(low, 0.0, slab).T.astype(BF16)
        k_ref[0, p] = kn[:, p * LANES:(p + 1) * LANES].astype(BF16)
    vt_ref[0] = v.T.astype(BF16)

    pc = jnp.dot(hb, widx_ref[...], preferred_element_type=F32)
    for p in range(IDX_HEADS // 2):
        slab = pc[:, p * LANES:(p + 1) * LANES]
        qi_ref[0, 2 * p] = jnp.where(low, slab, 0.0).T.astype(BF16)
        qi_ref[0, 2 * p + 1] = jnp.where(low, 0.0, slab).T.astype(BF16)
    ki_ref[0] = pc[:, 2 * LANES:3 * LANES].astype(BF16)
    wit_ref[0] = _nt(wwi_ref[...], hb) * IDX_SCALE


def _inproj(x, mod3, g1, wa, wqkv, widx, wwi, gq, gk, e2, cw, cb, lng, lnb, goc):
    bsz, s, d = x.shape
    tm = TM_IN
    nt = s // tm
    full = lambda shape: pl.BlockSpec(shape, lambda b, j: (0,) * len(shape))
    out_shape = (
        jax.ShapeDtypeStruct((bsz, s, CONV_CH), BF16),
        jax.ShapeDtypeStruct((bsz, ATT_HEADS, LANES, s), BF16),
        jax.ShapeDtypeStruct((bsz, ATT_HEADS // 2, s, LANES), BF16),
        jax.ShapeDtypeStruct((bsz, ATT_W, s), BF16),
        jax.ShapeDtypeStruct((bsz, IDX_HEADS, LANES, s), BF16),
        jax.ShapeDtypeStruct((bsz, s, LANES), BF16),
        jax.ShapeDtypeStruct((bsz, SUBLANES, s), F32),
    )
    out_specs = (
        pl.BlockSpec((1, tm, CONV_CH), lambda b, j: (b, j, 0)),
        pl.BlockSpec((1, ATT_HEADS, LANES, tm), lambda b, j: (b, 0, 0, j)),
        pl.BlockSpec((1, ATT_HEADS // 2, tm, LANES), lambda b, j: (b, 0, j, 0)),
        pl.BlockSpec((1, ATT_W, tm), lambda b, j: (b, 0, j)),
        pl.BlockSpec((1, IDX_HEADS, LANES, tm), lambda b, j: (b, 0, 0, j)),
        pl.BlockSpec((1, tm, LANES), lambda b, j: (b, j, 0)),
        pl.BlockSpec((1, SUBLANES, tm), lambda b, j: (b, 0, j)),
    )
    return pl.pallas_call(
        _inproj_kernel,
        grid=(bsz, nt),
        in_specs=[
            pl.BlockSpec((1, tm, d), lambda b, j: (b, j, 0)),
            pl.BlockSpec((1, 6, d), lambda b, j: (b, 0, 0)),
            full(g1.shape), full(wa.shape), full(wqkv.shape), full(widx.shape), full(wwi.shape),
            full(gq.shape), full(gk.shape), full(e2.shape),
            full(cw.shape), full(cb.shape), full(lng.shape), full(lnb.shape), full(goc.shape),
        ],
        out_specs=out_specs,
        out_shape=out_shape,
        scratch_shapes=[pltpu.VMEM((tm + HALO, CONV_CH), F32)],
        compiler_params=pltpu.CompilerParams(
            dimension_semantics=("arbitrary", "arbitrary"), vmem_limit_bytes=VMEM_LIMIT),
        name="inproj",
    )(x, mod3, g1, wa, wqkv, widx, wwi, gq, gk, e2, cw, cb, lng, lnb, goc)


def _t5_bucket_np(rel):
    half = REL_BUCKETS // 2
    max_exact = half // 2
    ret = np.where(rel > 0, half, 0)
    n = np.abs(rel)
    nf = np.maximum(n, 1).astype(np.float64)
    large = max_exact + (np.log(nf / max_exact) / math.log(REL_MAX_DIST / max_exact)
                         * (half - max_exact)).astype(np.int32)
    large = np.minimum(large, half - 1)
    return (ret + np.where(n < max_exact, n, large)).astype(np.int32)


NEAR = REL_MAX_DIST + TQ
FAR_BUCKET = REL_BUCKETS // 2 - 1


def _near_buckets():
    r = np.arange(NEAR)[:, None]
    t = np.arange(TQ)[None, :]
    return _t5_bucket_np(r - REL_MAX_DIST - t)


def _attn_kernel(q_ref, k_ref, vt_ref, qi_ref, ki_ref, wit_ref, bkt_ref, rb_ref, goa_ref, o_ref,
                 keys_s, khi_s, maskb_s, relb_s, ot_s, m_s, mo_s, l_s, lg_s, *, nsel, jbits):
    b = pl.program_id(0)
    i = pl.program_id(1)
    t0 = i * TQ
    n_tiles = i + 1
    neg_inf = F32(-jnp.inf)

    @pl.when((b == 0) & (i == 0))
    def _():
        bk = bkt_ref[...]
        for h in range(ATT_HEADS):
            far = rb_ref[FAR_BUCKET, h]
            acc = jnp.zeros((NEAR, TQ), F32)
            for bb in range(REL_BUCKETS):
                acc = jnp.where(bk == bb, rb_ref[bb, h] - far, acc)
            relb_s[h, 0:2 * KT - NEAR, :] = jnp.zeros((2 * KT - NEAR, TQ), F32)
            relb_s[h, 2 * KT - NEAR:2 * KT, :] = acc

    qpos = t0 + lax.broadcasted_iota(I32, (1, TQ), 1)
    limit = (qpos // CHUNK + 1) * CHUNK
    row_iota = lax.broadcasted_iota(I32, (KT, TQ), 0)

    def tile_start(jt):
        return pl.multiple_of(jt * KT, KT)

    wi = wit_ref[0]

    def p1(jt, c):
        ks = tile_start(jt)
        kit = ki_ref[0, pl.ds(ks, KT), :]
        acc = jnp.zeros((KT, TQ), F32)
        for h in range(IDX_HEADS):
            lgt = jnp.dot(kit, qi_ref[0, h], preferred_element_type=F32)
            acc = acc + jnp.maximum(lgt, 0.0) * wi[h:h + 1, :]
        sc = jnp.where(row_iota + ks < limit, acc, neg_inf)
        bits = pltpu.bitcast(sc, I32)
        keys_s[pl.ds(ks, KT), :] = bits ^ (lax.shift_right_arithmetic(bits, 31) & 0x7FFFFFFF)
        top = jnp.where((bits & 0x7FFFFFFF) < MIN_NORMAL_BITS, 0, bits & I32(-65536))
        khi_s[pl.ds(ks, KT), :] = pltpu.bitcast(top, F32).astype(BF16)
        return c

    lax.fori_loop(0, n_tiles, p1, 0)

    def count(pred):
        def body(jt, acc):
            ks = tile_start(jt)
            m = pred(keys_s[pl.ds(ks, KT), :], ks).astype(F32)
            return acc + jnp.sum(m.reshape(KT // SUBLANES, SUBLANES, TQ), axis=0)
        acc = lax.fori_loop(0, n_tiles, body, jnp.zeros((SUBLANES, TQ), F32))
        return jnp.sum(acc, axis=0, keepdims=True)

    def count_top(cand_hi):
        cand_hi = jnp.where((cand_hi > 0) & (cand_hi < BAND), BAND, cand_hi)
        pat = cand_hi ^ (lax.shift_right_arithmetic(cand_hi, 15) & 0x7FFF)
        cb = pltpu.bitcast(lax.shift_left(pat, 16), F32).astype(BF16)
        one = jnp.ones((KT, TQ), BF16)
        zero = jnp.zeros((KT, TQ), BF16)

        def body(jt, acc):
            ks = tile_start(jt)
            hit = jnp.where(khi_s[pl.ds(ks, KT), :] >= cb, one, zero)
            parts = [hit[r:r + PACK, :] for r in range(0, KT, PACK)]
            while len(parts) > 1:
                parts = [parts[k] + parts[k + 1] for k in range(0, len(parts), 2)]
            return acc + parts[0].astype(F32)
        acc = lax.fori_loop(0, n_tiles, body, jnp.zeros((PACK, TQ), F32))
        return jnp.sum(acc, axis=0, keepdims=True)

    def select_thr():
        c0 = count_top(jnp.zeros((1, TQ), I32))
        ok = c0 >= nsel
        t = jnp.where(ok, 0, -(2 ** 15)).astype(I32)
        cnt = jnp.where(ok, c0, F32(2 ** 30))

        def top_step(it, carry):
            t, cnt = carry
            cand = t + lax.shift_left(I32(1), I32(14) - it)
            c = count_top(cand)
            ok = c >= nsel
            return jnp.where(ok, cand, t), jnp.where(ok, c, cnt)

        t, cnt = lax.fori_loop(0, 15, top_step, (t, cnt))
        in_band = jnp.abs(2 * t + 1) < 2 * BAND
        any_band = jnp.max(jnp.where(in_band, 1.0, 0.0)) > 0.0
        t = lax.shift_left(jnp.where(in_band, -BAND, t), 16)
        cnt = jnp.where(in_band, F32(2 ** 30), cnt)
        first_bit = jnp.where(any_band, 23, 15).astype(I32)

        def low_cond(carry):
            bit, _, cnt = carry
            return (bit >= 0) & (jnp.max(cnt) > nsel)

        def low_step(carry):
            bit, t, cnt = carry
            cand = t + lax.shift_left(I32(1), bit)
            c = count(lambda kt, ks: kt >= cand)
            ok = c >= nsel
            return bit - 1, jnp.where(ok, cand, t), jnp.where(ok, c, cnt)

        _, t, cnt = lax.while_loop(low_cond, low_step, (first_bit, t, cnt))
        return t, cnt

    thr, c_ge = lax.cond(
        i > 0, select_thr,
        lambda: (jnp.full((1, TQ), KEY_NEG_INF + 1, I32), jnp.full((1, TQ), nsel, F32)))

    def tie_cut():
        need = nsel - count(lambda kt, ks: kt > thr)

        def step(it, cut):
            cand = cut + lax.shift_left(I32(1), I32(jbits - 1) - it)
            f = count(lambda kt, ks: jnp.where(kt == thr, row_iota + ks, cand) < cand)
            return jnp.where(f <= need, cand, cut)
        return lax.fori_loop(0, jbits, step, jnp.zeros((1, TQ), I32))

    cut = lax.cond(jnp.max(c_ge) > nsel, tie_cut, lambda: jnp.full((1, TQ), 2 ** jbits - 1, I32))

    def p3(jt, c):
        ks = tile_start(jt)
        kt = keys_s[pl.ds(ks, KT), :]
        tied = jnp.where(kt == thr, jnp.where(row_iota + ks < cut, 0.0, neg_inf), neg_inf)
        maskb_s[pl.ds(ks, KT), :] = jnp.where(kt > thr, 0.0, tied)
        return c

    lax.fori_loop(0, n_tiles, p3, 0)

    m_s[...] = jnp.full((ATT_HEADS, TQ), neg_inf, F32)
    l_s[...] = jnp.zeros((ATT_HEADS, TQ), F32)
    ot_s[...] = jnp.zeros((ATT_W, TQ), F32)

    def att_tile(jt, near):
        ks = tile_start(jt)
        mb = maskb_s[pl.ds(ks, KT), :]
        for h in range(ATT_HEADS):
            l = jnp.dot(k_ref[0, h // 2, pl.ds(ks, KT), :], q_ref[0, h], preferred_element_type=F32) + mb
            if near:
                off = pl.multiple_of((jt - (n_tiles - 2)) * KT, KT)
                l = l + relb_s[h, pl.ds(off, KT), :]
            lg_s[h] = l
            m_old = m_s[h:h + 1, :]
            mo_s[h:h + 1, :] = m_old
            m_s[h:h + 1, :] = jnp.maximum(m_old, jnp.max(l, axis=0, keepdims=True))
        for h in range(ATT_HEADS):
            m_new = m_s[h:h + 1, :]
            m_ref = jnp.where(m_new == neg_inf, 0.0, m_new)
            alpha = jnp.exp(mo_s[h:h + 1, :] - m_ref)
            p = jnp.exp(lg_s[h] - m_ref)
            l_s[h:h + 1, :] = alpha * l_s[h:h + 1, :] + jnp.sum(p, axis=0, keepdims=True)
            vt = vt_ref[0, h * HEAD_DIM:(h + 1) * HEAD_DIM, pl.ds(ks, KT)]
            rows = slice(h * HEAD_DIM, (h + 1) * HEAD_DIM)
            ot_s[rows, :] = ot_s[rows, :] * alpha + jnp.dot(vt, p.astype(BF16), preferred_element_type=F32)

    def far_tile(jt, c):
        att_tile(jt, False)
        return c

    def near_tile(jt, c):
        att_tile(jt, True)
        return c

    n_far = jnp.maximum(n_tiles - 2, 0)
    lax.fori_loop(0, n_far, far_tile, 0)
    lax.fori_loop(n_far, n_tiles, near_tile, 0)
    for h in range(ATT_HEADS):
        rows = slice(h * HEAD_DIM, (h + 1) * HEAD_DIM)
        ot_s[rows, :] = ot_s[rows, :] / l_s[h:h + 1, :]

    ot = ot_s[...]
    ms = jnp.mean(ot * ot, axis=0, keepdims=True)
    y = (ot * lax.rsqrt(ms + EPS)).T * goa_ref[...]
    o_ref[0] = y.astype(BF16)


def _attn(q, k, vt, qi, ki, wit, rel_bias, goa):
    bsz, _, _, s = q.shape
    nsel = min(TOPK_MAX, s // 4)
    assert nsel == TQ and s % TQ == 0, "attention kernel assumes TOPK_MAX-sized query blocks"
    jbits = int(s).bit_length()
    bkt = jnp.asarray(_near_buckets())
    kern = functools.partial(_attn_kernel, nsel=nsel, jbits=jbits)
    return pl.pallas_call(
        kern,
        grid=(bsz, s // TQ),
        in_specs=[
            pl.BlockSpec((1, ATT_HEADS, LANES, TQ), lambda b, i: (b, 0, 0, i)),
            pl.BlockSpec((1, ATT_HEADS // 2, s, LANES), lambda b, i: (b, 0, 0, 0)),
            pl.BlockSpec((1, ATT_W, s), lambda b, i: (b, 0, 0)),
            pl.BlockSpec((1, IDX_HEADS, LANES, TQ), lambda b, i: (b, 0, 0, i)),
            pl.BlockSpec((1, s, LANES), lambda b, i: (b, 0, 0)),
            pl.BlockSpec((1, SUBLANES, TQ), lambda b, i: (b, 0, i)),
            pl.BlockSpec((NEAR, TQ), lambda b, i: (0, 0)),
            pl.BlockSpec(memory_space=pltpu.SMEM),
            pl.BlockSpec((1, ATT_W), lambda b, i: (0, 0)),
        ],
        out_specs=pl.BlockSpec((1, TQ, ATT_W), lambda b, i: (b, i, 0)),
        out_shape=jax.ShapeDtypeStruct((bsz, s, ATT_W), BF16),
        scratch_shapes=[
            pltpu.VMEM((s, TQ), I32),
            pltpu.VMEM((s, TQ), BF16),
            pltpu.VMEM((s, TQ), F32),
            pltpu.VMEM((ATT_HEADS, 2 * KT, TQ), F32),
            pltpu.VMEM((ATT_W, TQ), F32),
            pltpu.VMEM((ATT_HEADS, TQ), F32),
            pltpu.VMEM((ATT_HEADS, TQ), F32),
            pltpu.VMEM((ATT_HEADS, TQ), F32),
            pltpu.VMEM((ATT_HEADS, KT, TQ), F32),
        ],
        compiler_params=pltpu.CompilerParams(
            dimension_semantics=("arbitrary", "arbitrary"), vmem_limit_bytes=VMEM_LIMIT),
        name="attn",
    )(q, k, vt, qi, ki, wit, bkt, rel_bias, goa)


def _topk_rows(s, payload, k):
    nrows = s.shape[0]
    rows = lax.broadcasted_iota(I32, s.shape, 0).astype(F32)
    vals, pays = [], []
    for _ in range(k):
        m = jnp.max(s, axis=0, keepdims=True)
        ix = jnp.min(jnp.where(s == m, rows, float(nrows)), axis=0, keepdims=True)
        hit = rows == ix
        if payload is None:
            pays.append(ix)
        else:
            pays.append(jnp.max(jnp.where(hit, payload, -1.0), axis=0, keepdims=True))
        vals.append(m)
        s = jnp.where(hit, -jnp.inf, s)
    return jnp.concatenate(vals, axis=0), jnp.concatenate(pays, axis=0)


_PAIR_ROWS = tuple((a, PEER_TOPK // (a + 1)) for a in range(PEER_TOPK // 2))


def _mid_kernel(x_ref, cn_ref, an_ref, mod_ref, wo1_ref, wo2_ref, g2_ref, wpq_ref, k1_ref, k2_ref,
                x1_ref, h2_ref, idx_ref, gate_ref, qq_s, idt_s, gt_s):
    tm = x_ref.shape[1]
    x = x_ref[0]
    gt1 = mod_ref[0, 2:3, :]
    sh2 = mod_ref[0, 3:4, :]
    sc2 = mod_ref[0, 4:5, :]
    proj = (jnp.dot(cn_ref[0], wo1_ref[...], preferred_element_type=F32)
            + jnp.dot(an_ref[0], wo2_ref[...], preferred_element_type=F32))
    x1 = x + gt1 * proj
    x1_ref[0] = x1
    r = lax.rsqrt(jnp.mean(x1 * x1, axis=-1, keepdims=True) + EPS)
    h2 = (x1 * r) * g2_ref[...] * (1.0 + sc2) + sh2
    h2_ref[0] = h2
    qq_s[...] = jnp.dot(h2.astype(BF16), wpq_ref[...], preferred_element_type=F32).astype(BF16)

    def route_unit(hh, lt):
        rows = pl.ds(lt * LANES, LANES)
        q1 = qq_s[rows, pl.ds(pl.multiple_of(hh * 2 * N_KEYS, LANES), N_KEYS)]
        q2 = qq_s[rows, pl.ds(pl.multiple_of(hh * 2 * N_KEYS + N_KEYS, LANES), N_KEYS)]
        v1, i1 = _topk_rows(_nt(k1_ref[hh], q1), None, PEER_TOPK)
        v2, i2 = _topk_rows(_nt(k2_ref[hh], q2), None, PEER_TOPK)
        sub = lax.broadcasted_iota(I32, (SUBLANES, LANES), 0)
        cands, cidxs = [], []
        for a, nb in _PAIR_ROWS:
            nrows = max(nb, SUBLANES)
            val = v1[a:a + 1, :] + v2[0:nrows, :]
            if nb < SUBLANES:
                val = jnp.where(sub < nb, val, -jnp.inf)
            cands.append(val)
            cidxs.append(i1[a:a + 1, :] * float(N_KEYS) + i2[0:nrows, :])
        half = PEER_TOPK // 2
        cands.append(v1[half:, :] + v2[0:1, :])
        cidxs.append(i1[half:, :] * float(N_KEYS) + i2[0:1, :])
        best, experts = _topk_rows(jnp.concatenate(cands, axis=0), jnp.concatenate(cidxs, axis=0), PEER_TOPK)
        e = jnp.exp(best - best[0:1, :])
        g = e / jnp.sum(e, axis=0, keepdims=True)
        slots = pl.ds(pl.multiple_of(hh * PEER_TOPK, PEER_TOPK), PEER_TOPK)
        cols = pl.ds(lt * LANES, LANES)
        idt_s[slots, cols] = experts * float(ROWS_PER_EXPERT)
        gt_s[slots, cols] = g

    def route(hh, c):
        for lt in range(tm // LANES):
            route_unit(hh, lt)
        return c

    lax.fori_loop(0, PEER_HEADS, route, 0)
    idx_ref[0] = idt_s[...].T.astype(I32)
    gate_ref[0] = gt_s[...].T


def _mid(x, cn, an, mod3, wo1, wo2, g2, wpq, k1, k2):
    bsz, s, d = x.shape
    tm = TM_MID
    full = lambda shape: pl.BlockSpec(shape, lambda b, j: (0,) * len(shape))
    tok = lambda w: pl.BlockSpec((1, tm, w), lambda b, j: (b, j, 0))
    return pl.pallas_call(
        _mid_kernel,
        grid=(bsz, s // tm),
        in_specs=[tok(d), tok(CONV_CH), tok(ATT_W), pl.BlockSpec((1, 6, d), lambda b, j: (b, 0, 0)),
                  full(wo1.shape), full(wo2.shape), full(g2.shape), full(wpq.shape), full(k1.shape), full(k2.shape)],
        out_specs=(tok(d), tok(d), tok(PEER_SLOTS), tok(PEER_SLOTS)),
        out_shape=(
            jax.ShapeDtypeStruct((bsz, s, d), F32),
            jax.ShapeDtypeStruct((bsz, s, d), F32),
            jax.ShapeDtypeStruct((bsz, s, PEER_SLOTS), I32),
            jax.ShapeDtypeStruct((bsz, s, PEER_SLOTS), F32),
        ),
        scratch_shapes=[
            pltpu.VMEM((tm, PEER_HEADS * 2 * N_KEYS), BF16),
            pltpu.VMEM((PEER_SLOTS, tm), F32),
            pltpu.VMEM((PEER_SLOTS, tm), F32),
        ],
        compiler_params=pltpu.CompilerParams(
            dimension_semantics=("arbitrary", "arbitrary"), vmem_limit_bytes=VMEM_LIMIT),
        name="mid",
    )(x, cn, an, mod3, wo1, wo2, g2, wpq, k1, k2)


def _pack_table(t):
    tb = t.astype(BF16)
    lo = lax.bitcast_convert_type(tb[:, :HALF], jnp.uint16).astype(jnp.uint32)
    hi = lax.bitcast_convert_type(tb[:, HALF:], jnp.uint16).astype(jnp.uint32)
    return (lo | (hi << 16)).reshape(t.shape[0] * ROWS_PER_EXPERT, LANES)


def _unpack(w):
    lo = pltpu.bitcast(lax.shift_left(w, jnp.uint32(16)), F32)
    hi = pltpu.bitcast(w & jnp.uint32(0xFFFF0000), F32)
    return lo, hi


def _gather_row(tab_ref, row):
    return tab_ref[pl.ds(pl.multiple_of(row, ROWS_PER_EXPERT), ROWS_PER_EXPERT), :]


STAGE_ROWS = PEER_SLOTS * ROWS_PER_EXPERT


def _token_pieces(xg, r, first, count):
    return jnp.concatenate(
        [xg[r:r + 1, (first + q) * LANES:(first + q + 1) * LANES] for q in range(count)], axis=0)


def _peer_u_kernel(idx_ref, tab_ref, x_ref, gate_ref, w_ref, *stages):
    tb = w_ref.shape[0]
    sub = lax.broadcasted_iota(I32, (SUBLANES, PEER_SLOTS), 0)

    def group(g, c):
        t0 = pl.multiple_of(g * SUBLANES, SUBLANES)
        act8 = jnp.zeros((SUBLANES, PEER_SLOTS), F32)
        xg = x_ref[pl.ds(t0, SUBLANES), :]
        xlo, xhi, rows_t = [], [], []
        for r in range(SUBLANES):
            xlo.append(jnp.concatenate([_token_pieces(xg, r, 0, ROWS_PER_EXPERT)] * 2, axis=0))
            xhi.append(jnp.concatenate([_token_pieces(xg, r, ROWS_PER_EXPERT, ROWS_PER_EXPERT)] * 2, axis=0))
            rows_t.append(idx_ref.at[t0 + r])
        for j in range(0, PEER_SLOTS, 2):
            for r in range(SUBLANES):
                pair = jnp.concatenate(
                    [_gather_row(tab_ref, rows_t[r][j]), _gather_row(tab_ref, rows_t[r][j + 1])], axis=0)
                lo, hi = _unpack(pair)
                row = j * ROWS_PER_EXPERT
                stages[r][row:row + 2 * ROWS_PER_EXPERT, :] = lo * xlo[r] + hi * xhi[r]
        for r in range(SUBLANES):
            stage = stages[r]
            cs = stage[pl.ds(0, PEER_SLOTS, stride=ROWS_PER_EXPERT), :]
            for q in range(1, ROWS_PER_EXPERT):
                cs = cs + stage[pl.ds(q, PEER_SLOTS, stride=ROWS_PER_EXPERT), :]
            act = jnp.sum(cs.T, axis=0, keepdims=True)
            act8 = jnp.where(sub == r, act, act8)
        rows = pl.ds(t0, SUBLANES)
        w_ref[rows, :] = gate_ref[rows, :] * jax.nn.gelu(act8)
        return c

    lax.fori_loop(0, tb // SUBLANES, group, 0)


def _peer_u(idx, tab, h2, gate):
    n, d = h2.shape
    tb = TB_PEER
    return pl.pallas_call(
        _peer_u_kernel,
        grid=(n // tb,),
        in_specs=[
            pl.BlockSpec((tb, PEER_SLOTS), lambda i: (i, 0), memory_space=pltpu.SMEM),
            pl.BlockSpec(tab.shape, lambda i: (0, 0), pipeline_mode=pl.Buffered(1)),
            pl.BlockSpec((tb, d), lambda i: (i, 0)),
            pl.BlockSpec((tb, PEER_SLOTS), lambda i: (i, 0)),
        ],
        out_specs=pl.BlockSpec((tb, PEER_SLOTS), lambda i: (i, 0)),
        out_shape=jax.ShapeDtypeStruct((n, PEER_SLOTS), F32),
        scratch_shapes=[pltpu.VMEM((STAGE_ROWS, LANES), F32) for _ in range(SUBLANES)],
        compiler_params=pltpu.CompilerParams(
            dimension_semantics=("arbitrary",), vmem_limit_bytes=VMEM_LIMIT),
        name="peer_u",
    )(idx, tab, h2, gate)


PV_PAIRS = 4


def _peer_v_kernel(idx_ref, w_ref, tab_ref, x1_ref, gt2_ref, e8_ref, mask_ref, o_ref, *stages):
    tb = idx_ref.shape[0]
    gt2 = gt2_ref[0]
    mask = mask_ref[...]
    e8 = e8_ref[...]

    def group(g, c):
        g0 = pl.multiple_of(g * 2 * PV_PAIRS, 2 * PV_PAIRS)
        peers = []
        for p in range(PV_PAIRS):
            stage = stages[p]
            t0 = g0 + 2 * p
            for u in range(2):
                rows_t = idx_ref.at[t0 + u]
                for j in range(PEER_SLOTS):
                    stage[j * ROWS_PER_EXPERT:(j + 1) * ROWS_PER_EXPERT, u * LANES:(u + 1) * LANES] = (
                        _gather_row(tab_ref, rows_t[j]))
            w2 = w_ref[pl.ds(t0, 2), :]
            hi = w2.astype(BF16).astype(F32)
            rep = jnp.dot(jnp.concatenate([hi, w2 - hi], axis=0).astype(BF16), e8,
                          preferred_element_type=F32)
            lhs = jnp.concatenate([rep[i:i + 1, :] * mask for i in range(4)], axis=0).astype(BF16)
            out = jnp.dot(lhs, pltpu.bitcast(stage[...], BF16), preferred_element_type=F32)
            for u in range(2):
                r0 = u * SUBLANES
                peers.append(out[r0:r0 + SUBLANES, u * LANES:(u + 1) * LANES]
                             + out[2 * SUBLANES + r0:3 * SUBLANES + r0, u * LANES:(u + 1) * LANES])
        rows = pl.ds(g0, 2 * PV_PAIRS)
        for q in range(SUBLANES):
            cols = slice(q * LANES, (q + 1) * LANES)
            piece = jnp.concatenate([peer[q:q + 1, :] for peer in peers], axis=0)
            o_ref[rows, cols] = x1_ref[rows, cols] + gt2[:, cols] * piece
        return c

    lax.fori_loop(0, tb // (2 * PV_PAIRS), group, 0)


def _peer_v(idx, w, tab, x1, gt2, blocks_per_batch):
    n, d = x1.shape
    tb = TB_PEER
    q = np.arange(SUBLANES)
    piece = 2 * (q % ROWS_PER_EXPERT) + q // ROWS_PER_EXPERT
    lane = np.arange(PEER_SLOTS * SUBLANES)
    mask = jnp.asarray((lane[None, :] % SUBLANES == piece[:, None]).astype(np.float32))
    e8 = jnp.asarray(np.arange(PEER_SLOTS)[:, None] == lane[None, :] // SUBLANES, BF16)
    return pl.pallas_call(
        _peer_v_kernel,
        grid=(n // tb,),
        in_specs=[
            pl.BlockSpec((tb, PEER_SLOTS), lambda i: (i, 0), memory_space=pltpu.SMEM),
            pl.BlockSpec((tb, PEER_SLOTS), lambda i: (i, 0)),
            pl.BlockSpec(tab.shape, lambda i: (0, 0), pipeline_mode=pl.Buffered(1)),
            pl.BlockSpec((tb, d), lambda i: (i, 0)),
            pl.BlockSpec((1, 1, d), lambda i: (i // blocks_per_batch, 0, 0)),
            pl.BlockSpec(e8.shape, lambda i: (0, 0)),
            pl.BlockSpec(mask.shape, lambda i: (0, 0)),
        ],
        out_specs=pl.BlockSpec((tb, d), lambda i: (i, 0)),
        out_shape=jax.ShapeDtypeStruct((n, d), F32),
        scratch_shapes=[pltpu.VMEM((STAGE_ROWS, 2 * LANES), jnp.uint32) for _ in range(PV_PAIRS)],
        compiler_params=pltpu.CompilerParams(
            dimension_semantics=("arbitrary",), vmem_limit_bytes=VMEM_LIMIT),
        name="peer_v",
    )(idx, w, tab, x1, gt2, e8, mask)


def _layer(x, mod, g_norm1, g_norm2, w_in, q_norm_g, k_norm_g, conv_w, conv_b, conv_ln_g, conv_ln_b,
           rel_bias, g_out_conv, g_out_attn, w_out, w_peer_q, peer_k1, peer_k2, peer_u, peer_v):
    bsz, s, d = x.shape
    n = bsz * s
    mod3 = mod.reshape(bsz, 6, d)
    row = lambda a: a.reshape(1, -1)

    c0 = 2 * CONV_CH
    c1 = c0 + 3 * ATT_W
    c2 = c1 + IDX_HEADS * IDX_DIM
    c3 = c2 + IDX_DIM
    wa = w_in[:, :c0].astype(BF16)
    wqkv = w_in[:, c0:c1].astype(BF16)
    widx = jnp.concatenate([w_in[:, c1:c2], w_in[:, c2:c3], w_in[:, c2:c3]], axis=1).astype(BF16)
    wwi = jnp.zeros((SUBLANES, d), F32).at[:IDX_HEADS].set(w_in[:, c3:c3 + IDX_HEADS].T).astype(BF16)
    head = np.arange(ATT_W) // HEAD_DIM
    e2 = jnp.asarray((head[:, None] == head[None, :]).astype(np.float32))

    conv_n, q, k, vt, qi, ki, wit = _inproj(
        x, mod3, row(g_norm1), wa, wqkv, widx, wwi,
        row(jnp.tile(q_norm_g, ATT_HEADS)), row(jnp.tile(k_norm_g, ATT_HEADS)), e2,
        conv_w.reshape(CONV_WIDTH, CONV_CH), row(conv_b), row(conv_ln_g), row(conv_ln_b), row(g_out_conv))
    attn_n = _attn(q, k, vt, qi, ki, wit, rel_bias, row(g_out_attn))

    x1, h2, idx, gate = _mid(
        x, conv_n, attn_n, mod3, w_out[:CONV_CH].astype(BF16), w_out[CONV_CH:].astype(BF16), row(g_norm2),
        w_peer_q.astype(BF16), peer_k1.astype(BF16), peer_k2.astype(BF16))

    idx = idx.reshape(n, PEER_SLOTS)
    w = _peer_u(idx, _pack_table(peer_u), h2.reshape(n, d), gate.reshape(n, PEER_SLOTS))
    out = _peer_v(idx, w, _pack_table(peer_v), x1.reshape(n, d), mod3[:, 5:6, :], s // TB_PEER)
    return out.reshape(bsz, s, d)


def kernel(x, c, w_ada, b_ada, g_norm1, g_norm2, w_in, q_norm_g, k_norm_g, conv_w, conv_b, conv_ln_g,
           conv_ln_b, rel_bias, g_out_conv, g_out_attn, w_out, w_peer_q, peer_k1, peer_k2, peer_u, peer_v):
    depth = w_ada.shape[0]
    for l in range(depth):
        mod = _ada(c, w_ada[l], b_ada[l])
        x = _layer(x, mod, g_norm1[l], g_norm2[l], w_in[l], q_norm_g[l], k_norm_g[l], conv_w[l], conv_b[l],
                   conv_ln_g[l], conv_ln_b[l], rel_bias, g_out_conv[l], g_out_attn[l], w_out[l],
                   w_peer_q[l], peer_k1[l], peer_k2[l], peer_u[l], peer_v[l])
    return x
```

```python
import functools
import math

import numpy as np
import jax
import jax.numpy as jnp
from jax import lax
from jax.experimental import pallas as pl
from jax.experimental.pallas import tpu as pltpu

F32 = jnp.float32
BF16 = jnp.bfloat16
I32 = jnp.int32
HIGHEST = lax.Precision.HIGHEST

D_MODEL = 1024
CHUNK = 64
CONV_CH = 512
CONV_WIDTH = 31
ATT_HEADS = 8
HEAD_DIM = 64
ATT_W = ATT_HEADS * HEAD_DIM
IDX_HEADS = 4
IDX_DIM = 64
IDX_SCALE = (IDX_HEADS * IDX_DIM) ** -0.5
TOPK_MAX = 256
REL_BUCKETS = 32
REL_MAX_DIST = 128
PEER_HEADS = 8
N_KEYS = 128
N_EXPERTS = N_KEYS * N_KEYS
PEER_TOPK = 16
PEER_SLOTS = PEER_HEADS * PEER_TOPK
EPS = 1e-6

LANES = 128
SUBLANES = 8
PACK = 16
VMEM_LIMIT = 56 * 1024 * 1024

TM_IN = 512
CONV_ROWS = 64
HALO = 32
TQ = 256
KT = 256
TM_MID = 256
TB_PEER = 128
HALF = D_MODEL // 2
ROWS_PER_EXPERT = HALF // LANES

NT_DIMS = (((1,), (1,)), ((), ()))

_NEG_INF_BITS = int(np.array(-np.inf, np.float32).view(np.int32))
KEY_NEG_INF = _NEG_INF_BITS ^ 0x7FFFFFFF
MIN_NORMAL_BITS = 0x00800000
BAND = MIN_NORMAL_BITS >> 16
INT_MIN = -(2 ** 31)


def _nt(a, b, precision=None):
    return lax.dot_general(a, b, NT_DIMS, precision=precision, preferred_element_type=F32)


def _ada_kernel(c_ref, w_ref, b_ref, o_ref):
    a = jax.nn.silu(c_ref[...])
    o_ref[...] = jnp.dot(a, w_ref[...], precision=HIGHEST, preferred_element_type=F32) + b_ref[...]


def _ada(c, w_ada, b_ada):
    bsz, d = c.shape
    return pl.pallas_call(
        _ada_kernel,
        grid=(6,),
        in_specs=[
            pl.BlockSpec((bsz, d), lambda j: (0, 0)),
            pl.BlockSpec((d, d), lambda j: (0, j)),
            pl.BlockSpec((1, d), lambda j: (0, j)),
        ],
        out_specs=pl.BlockSpec((bsz, d), lambda j: (0, j)),
        out_shape=jax.ShapeDtypeStruct((bsz, 6 * d), F32),
        name="ada",
    )(c, w_ada, b_ada.reshape(1, 6 * d))


def _inproj_kernel(x_ref, mod_ref, g1_ref, wa_ref, wqkv_ref, widx_ref, wwi_ref, gq_ref, gk_ref, e2_ref,
                   cw_ref, cb_ref, lng_ref, lnb_ref, goc_ref,
                   conv_ref, q_ref, k_ref, vt_ref, qi_ref, ki_ref, wit_ref, ubuf):
    j = pl.program_id(1)
    tm = x_ref.shape[1]
    x = x_ref[0]
    sh1 = mod_ref[0, 0:1, :]
    sc1 = mod_ref[0, 1:2, :]
    r = lax.rsqrt(jnp.mean(x * x, axis=-1, keepdims=True) + EPS)
    h = (x * r) * g1_ref[...] * (1.0 + sc1) + sh1
    hb = h.astype(BF16)

    pa = jnp.dot(hb, wa_ref[...], preferred_element_type=F32)
    u = pa[:, :CONV_CH] * jax.nn.sigmoid(pa[:, CONV_CH:])

    @pl.when(j == 0)
    def _():
        ubuf[0:HALO, :] = jnp.zeros((HALO, CONV_CH), F32)

    ubuf[HALO:HALO + tm, :] = u
    first = HALO - (CONV_WIDTH - 1)
    for rb in range(tm // CONV_ROWS):
        base = rb * CONV_ROWS
        acc = jnp.zeros((CONV_ROWS, CONV_CH), F32) + cb_ref[...]
        for t in range(CONV_WIDTH):
            acc = acc + cw_ref[t:t + 1, :] * ubuf[base + first + t:base + first + t + CONV_ROWS, :]
        mu = jnp.mean(acc, axis=-1, keepdims=True)
        xc = acc - mu
        y = xc * lax.rsqrt(jnp.mean(xc * xc, axis=-1, keepdims=True) + EPS)
        y = jax.nn.silu(y * lng_ref[...] + lnb_ref[...])
        y = y * lax.rsqrt(jnp.mean(y * y, axis=-1, keepdims=True) + EPS) * goc_ref[...]
        conv_ref[0, base:base + CONV_ROWS, :] = y.astype(BF16)
    ubuf[0:HALO, :] = ubuf[tm:tm + HALO, :]

    pq = jnp.dot(hb, wqkv_ref[...], preferred_element_type=F32)
    q = pq[:, :ATT_W]
    k = pq[:, ATT_W:2 * ATT_W]
    v = pq[:, 2 * ATT_W:]
    e2 = e2_ref[...]
    qs = jnp.dot(q * q, e2, precision=HIGHEST, preferred_element_type=F32) * (1.0 / HEAD_DIM)
    ks = jnp.dot(k * k, e2, precision=HIGHEST, preferred_element_type=F32) * (1.0 / HEAD_DIM)
    qn = q * lax.rsqrt(qs + EPS) * gq_ref[...] * (HEAD_DIM ** -0.5)
    kn = k * lax.rsqrt(ks + EPS) * gk_ref[...]
    lane = lax.broadcasted_iota(I32, (tm, LANES), 1)
    low = lane < HEAD_DIM
    for p in range(ATT_HEADS // 2):
        slab = qn[:, p * LANES:(p + 1) * LANES]
        q_ref[0, 2 * p] = jnp.where(low, slab, 0.0).T.astype(BF16)
        q_ref[0, 2 * p + 1] = jnp.where(low, 0.0, slab).T.astype(BF16)
        k_ref[0, p] = kn[:, p * LANES:(p + 1) * LANES].astype(BF16)
    vt_ref[0] = v.T.astype(BF16)

    pc = jnp.dot(hb, widx_ref[...], preferred_element_type=F32)
    for p in range(IDX_HEADS // 2):
        slab = pc[:, p * LANES:(p + 1) * LANES]
        qi_ref[0, 2 * p] = jnp.where(low, slab, 0.0).T.astype(BF16)
        qi_ref[0, 2 * p + 1] = jnp.where(low, 0.0, slab).T.astype(BF16)
    ki_ref[0] = pc[:, 2 * LANES:3 * LANES].astype(BF16)
    wit_ref[0] = _nt(wwi_ref[...], hb) * IDX_SCALE


def _inproj(x, mod3, g1, wa, wqkv, widx, wwi, gq, gk, e2, cw, cb, lng, lnb, goc):
    bsz, s, d = x.shape
    tm = TM_IN
    nt = s // tm
    full = lambda shape: pl.BlockSpec(shape, lambda b, j: (0,) * len(shape))
    out_shape = (
        jax.ShapeDtypeStruct((bsz, s, CONV_CH), BF16),
        jax.ShapeDtypeStruct((bsz, ATT_HEADS, LANES, s), BF16),
        jax.ShapeDtypeStruct((bsz, ATT_HEADS // 2, s, LANES), BF16),
        jax.ShapeDtypeStruct((bsz, ATT_W, s), BF16),
        jax.ShapeDtypeStruct((bsz, IDX_HEADS, LANES, s), BF16),
        jax.ShapeDtypeStruct((bsz, s, LANES), BF16),
        jax.ShapeDtypeStruct((bsz, SUBLANES, s), F32),
    )
    out_specs = (
        pl.BlockSpec((1, tm, CONV_CH), lambda b, j: (b, j, 0)),
        pl.BlockSpec((1, ATT_HEADS, LANES, tm), lambda b, j: (b, 0, 0, j)),
        pl.BlockSpec((1, ATT_HEADS // 2, tm, LANES), lambda b, j: (b, 0, j, 0)),
        pl.BlockSpec((1, ATT_W, tm), lambda b, j: (b, 0, j)),
        pl.BlockSpec((1, IDX_HEADS, LANES, tm), lambda b, j: (b, 0, 0, j)),
        pl.BlockSpec((1, tm, LANES), lambda b, j: (b, j, 0)),
        pl.BlockSpec((1, SUBLANES, tm), lambda b, j: (b, 0, j)),
    )
    return pl.pallas_call(
        _inproj_kernel,
        grid=(bsz, nt),
        in_specs=[
            pl.BlockSpec((1, tm, d), lambda b, j: (b, j, 0)),
            pl.BlockSpec((1, 6, d), lambda b, j: (b, 0, 0)),
            full(g1.shape), full(wa.shape), full(wqkv.shape), full(widx.shape), full(wwi.shape),
            full(gq.shape), full(gk.shape), full(e2.shape),
            full(cw.shape), full(cb.shape), full(lng.shape), full(lnb.shape), full(goc.shape),
        ],
        out_specs=out_specs,
        out_shape=out_shape,
        scratch_shapes=[pltpu.VMEM((tm + HALO, CONV_CH), F32)],
        compiler_params=pltpu.CompilerParams(
            dimension_semantics=("arbitrary", "arbitrary"), vmem_limit_bytes=VMEM_LIMIT),
        name="inproj",
    )(x, mod3, g1, wa, wqkv, widx, wwi, gq, gk, e2, cw, cb, lng, lnb, goc)


def _t5_bucket_np(rel):
    half = REL_BUCKETS // 2
    max_exact = half // 2
    ret = np.where(rel > 0, half, 0)
    n = np.abs(rel)
    nf = np.maximum(n, 1).astype(np.float64)
    large = max_exact + (np.log(nf / max_exact) / math.log(REL_MAX_DIST / max_exact)
                         * (half - max_exact)).astype(np.int32)
    large = np.minimum(large, half - 1)
    return (ret + np.where(n < max_exact, n, large)).astype(np.int32)


NEAR = REL_MAX_DIST + TQ
FAR_BUCKET = REL_BUCKETS // 2 - 1


def _near_buckets():
    r = np.arange(NEAR)[:, None]
    t = np.arange(TQ)[None, :]
    return _t5_bucket_np(r - REL_MAX_DIST - t)


def _attn_kernel(q_ref, k_ref, vt_ref, qi_ref, ki_ref, wit_ref, bkt_ref, rb_ref, goa_ref, o_ref,
                 keys_s, khi_s, maskb_s, relb_s, ot_s, m_s, mo_s, l_s, lg_s, *, nsel):
    b = pl.program_id(0)
    i = pl.program_id(1)
    t0 = i * TQ
    n_tiles = i + 1
    neg_inf = F32(-jnp.inf)

    @pl.when((b == 0) & (i == 0))
    def _():
        bk = bkt_ref[...]
        for h in range(ATT_HEADS):
            far = rb_ref[FAR_BUCKET, h]
            acc = jnp.zeros((NEAR, TQ), F32)
            for bb in range(REL_BUCKETS):
                acc = jnp.where(bk == bb, rb_ref[bb, h] - far, acc)
            relb_s[h, 0:2 * KT - NEAR, :] = jnp.zeros((2 * KT - NEAR, TQ), F32)
            relb_s[h, 2 * KT - NEAR:2 * KT, :] = acc

    qpos = t0 + lax.broadcasted_iota(I32, (1, TQ), 1)
    limit = (qpos // CHUNK + 1) * CHUNK
    row_iota = lax.broadcasted_iota(I32, (KT, TQ), 0)

    def tile_start(jt):
        return pl.multiple_of(jt * KT, KT)

    wi = wit_ref[0]

    def p1(jt, c):
        ks = tile_start(jt)
        kit = ki_ref[0, pl.ds(ks, KT), :]
        acc = jnp.zeros((KT, TQ), F32)
        for h in range(IDX_HEADS):
            lgt = jnp.dot(kit, qi_ref[0, h], preferred_element_type=F32)
            acc = acc + jnp.maximum(lgt, 0.0) * wi[h:h + 1, :]
        sc = jnp.where(row_iota + ks < limit, acc, neg_inf)
        bits = pltpu.bitcast(sc, I32)
        keys_s[pl.ds(ks, KT), :] = bits ^ (lax.shift_right_arithmetic(bits, 31) & 0x7FFFFFFF)
        top = jnp.where((bits & 0x7FFFFFFF) < MIN_NORMAL_BITS, 0, bits & I32(-65536))
        khi_s[pl.ds(ks, KT), :] = pltpu.bitcast(top, F32).astype(BF16)
        return c

    lax.fori_loop(0, n_tiles, p1, 0)

    def count(pred):
        def body(jt, acc):
            ks = tile_start(jt)
            m = pred(keys_s[pl.ds(ks, KT), :], ks).astype(F32)
            return acc + jnp.sum(m.reshape(KT // SUBLANES, SUBLANES, TQ), axis=0)
        acc = lax.fori_loop(0, n_tiles, body, jnp.zeros((SUBLANES, TQ), F32))
        return jnp.sum(acc, axis=0, keepdims=True)

    def count_top(cand_hi):
        cand_hi = jnp.where((cand_hi > 0) & (cand_hi < BAND), BAND, cand_hi)
        pat = cand_hi ^ (lax.shift_right_arithmetic(cand_hi, 15) & 0x7FFF)
        cb = pltpu.bitcast(lax.shift_left(pat, 16), F32).astype(BF16)
        one = jnp.ones((KT, TQ), BF16)
        zero = jnp.zeros((KT, TQ), BF16)

        def body(jt, acc):
            ks = tile_start(jt)
            hit = jnp.where(khi_s[pl.ds(ks, KT), :] >= cb, one, zero)
            parts = [hit[r:r + PACK, :] for r in range(0, KT, PACK)]
            while len(parts) > 1:
                parts = [parts[k] + parts[k + 1] for k in range(0, len(parts), 2)]
            return acc + parts[0].astype(F32)
        acc = lax.fori_loop(0, n_tiles, body, jnp.zeros((PACK, TQ), F32))
        return jnp.sum(acc, axis=0, keepdims=True)

    def select_thr():
        c0 = count_top(jnp.zeros((1, TQ), I32))
        ok = c0 >= nsel
        t = jnp.where(ok, 0, -(2 ** 15)).astype(I32)
        cnt = jnp.where(ok, c0, F32(2 ** 30))

        def top_step(it, carry):
            t, cnt = carry
            cand = t + lax.shift_left(I32(1), I32(14) - it)
            c = count_top(cand)
            ok = c >= nsel
            return jnp.where(ok, cand, t), jnp.where(ok, c, cnt)

        t, cnt = lax.fori_loop(0, 15, top_step, (t, cnt))

        band = jnp.where(jnp.abs(2 * t + 1) < 2 * BAND, 1.0, 0.0)

        def band_counts():
            c_zero = count(lambda kt, ks: kt >= 0)
            c_tiny = count(lambda kt, ks: kt >= 1)
            c_norm = count(lambda kt, ks: kt >= MIN_NORMAL_BITS)
            return c_zero, jnp.where(c_tiny == c_norm, jnp.where(c_zero >= nsel, 1.0, 0.0), 0.0)

        c_zero, plain = lax.cond(jnp.max(band) > 0.0, band_counts,
                                 lambda: (jnp.zeros((1, TQ), F32), jnp.ones((1, TQ), F32)))
        settled = band * plain > 0.0
        hard = band * (1.0 - plain) > 0.0
        t = jnp.where(settled, 0, lax.shift_left(jnp.where(hard, -BAND, t), 16))
        cnt = jnp.where(settled, c_zero, jnp.where(hard, F32(2 ** 30), cnt))
        first_bit = jnp.where(jnp.max(band * (1.0 - plain)) > 0.0, 23, 15).astype(I32)

        def low_cond(carry):
            bit, _, cnt = carry
            return (bit >= 0) & (jnp.max(jnp.where(settled, F32(nsel), cnt)) > nsel)

        def low_step(carry):
            bit, t, cnt = carry
            cand = t + lax.shift_left(I32(1), bit)
            c = count(lambda kt, ks: kt >= cand)
            ok = c >= nsel
            return bit - 1, jnp.where(ok, cand, t), jnp.where(ok, c, cnt)

        _, t, cnt = lax.while_loop(low_cond, low_step, (first_bit, t, cnt))
        return t, cnt

    thr, c_ge = lax.cond(
        i > 0, select_thr,
        lambda: (jnp.full((1, TQ), KEY_NEG_INF + 1, I32), jnp.full((1, TQ), nsel, F32)))

    def mask_plain():
        def p3(jt, c):
            ks = tile_start(jt)
            maskb_s[pl.ds(ks, KT), :] = jnp.where(keys_s[pl.ds(ks, KT), :] >= thr, 0.0, neg_inf)
            return c
        lax.fori_loop(0, n_tiles, p3, 0)
        return I32(0)

    def mask_ties():
        quota = nsel - count(lambda kt, ks: kt > thr)
        tri = jnp.where(lax.broadcasted_iota(I32, (KT, KT), 0) >= lax.broadcasted_iota(I32, (KT, KT), 1),
                        1.0, 0.0).astype(BF16)

        def p3(jt, before):
            ks = tile_start(jt)
            kt = keys_s[pl.ds(ks, KT), :]
            eq = kt == thr
            rank = before + jnp.dot(tri, jnp.where(eq, 1.0, 0.0).astype(BF16), preferred_element_type=F32)
            tied = jnp.where(eq, jnp.where(rank <= quota, 0.0, neg_inf), neg_inf)
            maskb_s[pl.ds(ks, KT), :] = jnp.where(kt > thr, 0.0, tied)
            return rank[KT - 1:KT, :]

        lax.fori_loop(0, n_tiles, p3, jnp.zeros((1, TQ), F32))
        return I32(0)

    lax.cond(jnp.max(c_ge) > nsel, mask_ties, mask_plain)

    m_s[...] = jnp.full((ATT_HEADS, TQ), neg_inf, F32)
    l_s[...] = jnp.zeros((ATT_HEADS, TQ), F32)
    ot_s[...] = jnp.zeros((ATT_W, TQ), F32)

    def att_tile(jt, near):
        ks = tile_start(jt)
        mb = maskb_s[pl.ds(ks, KT), :]
        for h in range(ATT_HEADS):
            l = jnp.dot(k_ref[0, h // 2, pl.ds(ks, KT), :], q_ref[0, h], preferred_element_type=F32) + mb
            if near:
                off = pl.multiple_of((jt - (n_tiles - 2)) * KT, KT)
                l = l + relb_s[h, pl.ds(off, KT), :]
            lg_s[h] = l
            m_old = m_s[h:h + 1, :]
            mo_s[h:h + 1, :] = m_old
            m_s[h:h + 1, :] = jnp.maximum(m_old, jnp.max(l, axis=0, keepdims=True))
        for h in range(ATT_HEADS):
            m_new = m_s[h:h + 1, :]
            m_ref = jnp.where(m_new == neg_inf, 0.0, m_new)
            alpha = jnp.exp(mo_s[h:h + 1, :] - m_ref)
            p = jnp.exp(lg_s[h] - m_ref)
            l_s[h:h + 1, :] = alpha * l_s[h:h + 1, :] + jnp.sum(p, axis=0, keepdims=True)
            vt = vt_ref[0, h * HEAD_DIM:(h + 1) * HEAD_DIM, pl.ds(ks, KT)]
            rows = slice(h * HEAD_DIM, (h + 1) * HEAD_DIM)
            ot_s[rows, :] = ot_s[rows, :] * alpha + jnp.dot(vt, p.astype(BF16), preferred_element_type=F32)

    def far_tile(jt, c):
        att_tile(jt, False)
        return c

    def near_tile(jt, c):
        att_tile(jt, True)
        return c

    n_far = jnp.maximum(n_tiles - 2, 0)
    lax.fori_loop(0, n_far, far_tile, 0)
    lax.fori_loop(n_far, n_tiles, near_tile, 0)
    for h in range(ATT_HEADS):
        rows = slice(h * HEAD_DIM, (h + 1) * HEAD_DIM)
        ot_s[rows, :] = ot_s[rows, :] / l_s[h:h + 1, :]

    ot = ot_s[...]
    ms = jnp.mean(ot * ot, axis=0, keepdims=True)
    y = (ot * lax.rsqrt(ms + EPS)).T * goa_ref[...]
    o_ref[0] = y.astype(BF16)


def _attn(q, k, vt, qi, ki, wit, rel_bias, goa):
    bsz, _, _, s = q.shape
    nsel = min(TOPK_MAX, s // 4)
    assert nsel == TQ and s % TQ == 0, "attention kernel assumes TOPK_MAX-sized query blocks"
    bkt = jnp.asarray(_near_buckets())
    kern = functools.partial(_attn_kernel, nsel=nsel)
    return pl.pallas_call(
        kern,
        grid=(bsz, s // TQ),
        in_specs=[
            pl.BlockSpec((1, ATT_HEADS, LANES, TQ), lambda b, i: (b, 0, 0, i)),
            pl.BlockSpec((1, ATT_HEADS // 2, s, LANES), lambda b, i: (b, 0, 0, 0)),
            pl.BlockSpec((1, ATT_W, s), lambda b, i: (b, 0, 0)),
            pl.BlockSpec((1, IDX_HEADS, LANES, TQ), lambda b, i: (b, 0, 0, i)),
            pl.BlockSpec((1, s, LANES), lambda b, i: (b, 0, 0)),
            pl.BlockSpec((1, SUBLANES, TQ), lambda b, i: (b, 0, i)),
            pl.BlockSpec((NEAR, TQ), lambda b, i: (0, 0)),
            pl.BlockSpec(memory_space=pltpu.SMEM),
            pl.BlockSpec((1, ATT_W), lambda b, i: (0, 0)),
        ],
        out_specs=pl.BlockSpec((1, TQ, ATT_W), lambda b, i: (b, i, 0)),
        out_shape=jax.ShapeDtypeStruct((bsz, s, ATT_W), BF16),
        scratch_shapes=[
            pltpu.VMEM((s, TQ), I32),
            pltpu.VMEM((s, TQ), BF16),
            pltpu.VMEM((s, TQ), F32),
            pltpu.VMEM((ATT_HEADS, 2 * KT, TQ), F32),
            pltpu.VMEM((ATT_W, TQ), F32),
            pltpu.VMEM((ATT_HEADS, TQ), F32),
            pltpu.VMEM((ATT_HEADS, TQ), F32),
            pltpu.VMEM((ATT_HEADS, TQ), F32),
            pltpu.VMEM((ATT_HEADS, KT, TQ), F32),
        ],
        compiler_params=pltpu.CompilerParams(
            dimension_semantics=("arbitrary", "arbitrary"), vmem_limit_bytes=VMEM_LIMIT),
        name="attn",
    )(q, k, vt, qi, ki, wit, bkt, rel_bias, goa)


def _topk_rows(s, payload, k):
    nrows = s.shape[0]
    rows = lax.broadcasted_iota(I32, s.shape, 0).astype(F32)
    vals, pays = [], []
    for _ in range(k):
        m = jnp.max(s, axis=0, keepdims=True)
        ix = jnp.min(jnp.where(s == m, rows, float(nrows)), axis=0, keepdims=True)
        hit = rows == ix
        if payload is None:
            pays.append(ix)
        else:
            pays.append(jnp.max(jnp.where(hit, payload, -1.0), axis=0, keepdims=True))
        vals.append(m)
        s = jnp.where(hit, -jnp.inf, s)
    return jnp.concatenate(vals, axis=0), jnp.concatenate(pays, axis=0)


_PAIR_ROWS = tuple((a, PEER_TOPK // (a + 1)) for a in range(PEER_TOPK // 2))


def _mid_kernel(x_ref, cn_ref, an_ref, mod_ref, wo1_ref, wo2_ref, g2_ref, wpq_ref, k1_ref, k2_ref,
                x1_ref, h2_ref, idx_ref, gate_ref, qq_s, idt_s, gt_s):
    tm = x_ref.shape[1]
    x = x_ref[0]
    gt1 = mod_ref[0, 2:3, :]
    sh2 = mod_ref[0, 3:4, :]
    sc2 = mod_ref[0, 4:5, :]
    proj = (jnp.dot(cn_ref[0], wo1_ref[...], preferred_element_type=F32)
            + jnp.dot(an_ref[0], wo2_ref[...], preferred_element_type=F32))
    x1 = x + gt1 * proj
    x1_ref[0] = x1
    r = lax.rsqrt(jnp.mean(x1 * x1, axis=-1, keepdims=True) + EPS)
    h2 = (x1 * r) * g2_ref[...] * (1.0 + sc2) + sh2
    h2_ref[0] = h2
    qq_s[...] = jnp.dot(h2.astype(BF16), wpq_ref[...], preferred_element_type=F32).astype(BF16)

    def route_unit(hh, lt):
        rows = pl.ds(lt * LANES, LANES)
        q1 = qq_s[rows, pl.ds(pl.multiple_of(hh * 2 * N_KEYS, LANES), N_KEYS)]
        q2 = qq_s[rows, pl.ds(pl.multiple_of(hh * 2 * N_KEYS + N_KEYS, LANES), N_KEYS)]
        v1, i1 = _topk_rows(_nt(k1_ref[hh], q1), None, PEER_TOPK)
        v2, i2 = _topk_rows(_nt(k2_ref[hh], q2), None, PEER_TOPK)
        sub = lax.broadcasted_iota(I32, (SUBLANES, LANES), 0)
        cands, cidxs = [], []
        for a, nb in _PAIR_ROWS:
            nrows = max(nb, SUBLANES)
            val = v1[a:a + 1, :] + v2[0:nrows, :]
            if nb < SUBLANES:
                val = jnp.where(sub < nb, val, -jnp.inf)
            cands.append(val)
            cidxs.append(i1[a:a + 1, :] * float(N_KEYS) + i2[0:nrows, :])
        half = PEER_TOPK // 2
        cands.append(v1[half:, :] + v2[0:1, :])
        cidxs.append(i1[half:, :] * float(N_KEYS) + i2[0:1, :])
        best, experts = _topk_rows(jnp.concatenate(cands, axis=0), jnp.concatenate(cidxs, axis=0), PEER_TOPK)
        e = jnp.exp(best - best[0:1, :])
        g = e / jnp.sum(e, axis=0, keepdims=True)
        slots = pl.ds(pl.multiple_of(hh * PEER_TOPK, PEER_TOPK), PEER_TOPK)
        cols = pl.ds(lt * LANES, LANES)
        idt_s[slots, cols] = experts * float(ROWS_PER_EXPERT)
        gt_s[slots, cols] = g

    def route(hh, c):
        for lt in range(tm // LANES):
            route_unit(hh, lt)
        return c

    lax.fori_loop(0, PEER_HEADS, route, 0)
    idx_ref[0] = idt_s[...].T.astype(I32)
    gate_ref[0] = gt_s[...].T


def _mid(x, cn, an, mod3, wo1, wo2, g2, wpq, k1, k2):
    bsz, s, d = x.shape
    tm = TM_MID
    full = lambda shape: pl.BlockSpec(shape, lambda b, j: (0,) * len(shape))
    tok = lambda w: pl.BlockSpec((1, tm, w), lambda b, j: (b, j, 0))
    return pl.pallas_call(
        _mid_kernel,
        grid=(bsz, s // tm),
        in_specs=[tok(d), tok(CONV_CH), tok(ATT_W), pl.BlockSpec((1, 6, d), lambda b, j: (b, 0, 0)),
                  full(wo1.shape), full(wo2.shape), full(g2.shape), full(wpq.shape), full(k1.shape), full(k2.shape)],
        out_specs=(tok(d), tok(d), tok(PEER_SLOTS), tok(PEER_SLOTS)),
        out_shape=(
            jax.ShapeDtypeStruct((bsz, s, d), F32),
            jax.ShapeDtypeStruct((bsz, s, d), F32),
            jax.ShapeDtypeStruct((bsz, s, PEER_SLOTS), I32),
            jax.ShapeDtypeStruct((bsz, s, PEER_SLOTS), F32),
        ),
        scratch_shapes=[
            pltpu.VMEM((tm, PEER_HEADS * 2 * N_KEYS), BF16),
            pltpu.VMEM((PEER_SLOTS, tm), F32),
            pltpu.VMEM((PEER_SLOTS, tm), F32),
        ],
        compiler_params=pltpu.CompilerParams(
            dimension_semantics=("arbitrary", "arbitrary"), vmem_limit_bytes=VMEM_LIMIT),
        name="mid",
    )(x, cn, an, mod3, wo1, wo2, g2, wpq, k1, k2)


def _pack_table(t):
    tb = t.astype(BF16)
    lo = lax.bitcast_convert_type(tb[:, :HALF], jnp.uint16).astype(jnp.uint32)
    hi = lax.bitcast_convert_type(tb[:, HALF:], jnp.uint16).astype(jnp.uint32)
    return (lo | (hi << 16)).reshape(t.shape[0] * ROWS_PER_EXPERT, LANES)


def _unpack(w):
    lo = pltpu.bitcast(lax.shift_left(w, jnp.uint32(16)), F32)
    hi = pltpu.bitcast(w & jnp.uint32(0xFFFF0000), F32)
    return lo, hi


def _gather_row(tab_ref, row):
    return tab_ref[pl.ds(pl.multiple_of(row, ROWS_PER_EXPERT), ROWS_PER_EXPERT), :]


STAGE_ROWS = PEER_SLOTS * ROWS_PER_EXPERT


def _token_pieces(xg, r, first, count):
    return jnp.concatenate(
        [xg[r:r + 1, (first + q) * LANES:(first + q + 1) * LANES] for q in range(count)], axis=0)


def _peer_u_kernel(idx_ref, tab_ref, x_ref, gate_ref, w_ref, *stages):
    tb = w_ref.shape[0]
    sub = lax.broadcasted_iota(I32, (SUBLANES, PEER_SLOTS), 0)

    def group(g, c):
        t0 = pl.multiple_of(g * SUBLANES, SUBLANES)
        act8 = jnp.zeros((SUBLANES, PEER_SLOTS), F32)
        xg = x_ref[pl.ds(t0, SUBLANES), :]
        xlo, xhi, rows_t = [], [], []
        for r in range(SUBLANES):
            xlo.append(jnp.concatenate([_token_pieces(xg, r, 0, ROWS_PER_EXPERT)] * 2, axis=0))
            xhi.append(jnp.concatenate([_token_pieces(xg, r, ROWS_PER_EXPERT, ROWS_PER_EXPERT)] * 2, axis=0))
            rows_t.append(idx_ref.at[t0 + r])
        for j in range(0, PEER_SLOTS, 2):
            for r in range(SUBLANES):
                pair = jnp.concatenate(
                    [_gather_row(tab_ref, rows_t[r][j]), _gather_row(tab_ref, rows_t[r][j + 1])], axis=0)
                lo, hi = _unpack(pair)
                row = j * ROWS_PER_EXPERT
                stages[r][row:row + 2 * ROWS_PER_EXPERT, :] = lo * xlo[r] + hi * xhi[r]
        for r in range(SUBLANES):
            stage = stages[r]
            cs = stage[pl.ds(0, PEER_SLOTS, stride=ROWS_PER_EXPERT), :]
            for q in range(1, ROWS_PER_EXPERT):
                cs = cs + stage[pl.ds(q, PEER_SLOTS, stride=ROWS_PER_EXPERT), :]
            act = jnp.sum(cs.T, axis=0, keepdims=True)
            act8 = jnp.where(sub == r, act, act8)
        rows = pl.ds(t0, SUBLANES)
        w_ref[rows, :] = gate_ref[rows, :] * jax.nn.gelu(act8)
        return c

    lax.fori_loop(0, tb // SUBLANES, group, 0)


def _peer_u(idx, tab, h2, gate):
    n, d = h2.shape
    tb = TB_PEER
    return pl.pallas_call(
        _peer_u_kernel,
        grid=(n // tb,),
        in_specs=[
            pl.BlockSpec((tb, PEER_SLOTS), lambda i: (i, 0), memory_space=pltpu.SMEM),
            pl.BlockSpec(tab.shape, lambda i: (0, 0), pipeline_mode=pl.Buffered(1)),
            pl.BlockSpec((tb, d), lambda i: (i, 0)),
            pl.BlockSpec((tb, PEER_SLOTS), lambda i: (i, 0)),
        ],
        out_specs=pl.BlockSpec((tb, PEER_SLOTS), lambda i: (i, 0)),
        out_shape=jax.ShapeDtypeStruct((n, PEER_SLOTS), F32),
        scratch_shapes=[pltpu.VMEM((STAGE_ROWS, LANES), F32) for _ in range(SUBLANES)],
        compiler_params=pltpu.CompilerParams(
            dimension_semantics=("arbitrary",), vmem_limit_bytes=VMEM_LIMIT),
        name="peer_u",
    )(idx, tab, h2, gate)


PV_PAIRS = 4


def _peer_v_kernel(idx_ref, w_ref, tab_ref, x1_ref, gt2_ref, e8_ref, mask_ref, o_ref, *stages):
    tb = idx_ref.shape[0]
    gt2 = gt2_ref[0]
    mask = mask_ref[...]
    e8 = e8_ref[...]

    def group(g, c):
        g0 = pl.multiple_of(g * 2 * PV_PAIRS, 2 * PV_PAIRS)
        peers = []
        for p in range(PV_PAIRS):
            stage = stages[p]
            t0 = g0 + 2 * p
            for u in range(2):
                rows_t = idx_ref.at[t0 + u]
                for j in range(PEER_SLOTS):
                    stage[j * ROWS_PER_EXPERT:(j + 1) * ROWS_PER_EXPERT, u * LANES:(u + 1) * LANES] = (
                        _gather_row(tab_ref, rows_t[j]))
            w2 = w_ref[pl.ds(t0, 2), :]
            hi = w2.astype(BF16).astype(F32)
            rep = jnp.dot(jnp.concatenate([hi, w2 - hi], axis=0).astype(BF16), e8,
                          preferred_element_type=F32)
            lhs = jnp.concatenate([rep[i:i + 1, :] * mask for i in range(4)], axis=0).astype(BF16)
            out = jnp.dot(lhs, pltpu.bitcast(stage[...], BF16), preferred_element_type=F32)
            for u in range(2):
                r0 = u * SUBLANES
                peers.append(out[r0:r0 + SUBLANES, u * LANES:(u + 1) * LANES]
                             + out[2 * SUBLANES + r0:3 * SUBLANES + r0, u * LANES:(u + 1) * LANES])
        rows = pl.ds(g0, 2 * PV_PAIRS)
        for q in range(SUBLANES):
            cols = slice(q * LANES, (q + 1) * LANES)
            piece = jnp.concatenate([peer[q:q + 1, :] for peer in peers], axis=0)
            o_ref[rows, cols] = x1_ref[rows, cols] + gt2[:, cols] * piece
        return c

    lax.fori_loop(0, tb // (2 * PV_PAIRS), group, 0)


def _peer_v(idx, w, tab, x1, gt2, blocks_per_batch):
    n, d = x1.shape
    tb = TB_PEER
    q = np.arange(SUBLANES)
    piece = 2 * (q % ROWS_PER_EXPERT) + q // ROWS_PER_EXPERT
    lane = np.arange(PEER_SLOTS * SUBLANES)
    mask = jnp.asarray((lane[None, :] % SUBLANES == piece[:, None]).astype(np.float32))
    e8 = jnp.asarray(np.arange(PEER_SLOTS)[:, None] == lane[None, :] // SUBLANES, BF16)
    return pl.pallas_call(
        _peer_v_kernel,
        grid=(n // tb,),
        in_specs=[
            pl.BlockSpec((tb, PEER_SLOTS), lambda i: (i, 0), memory_space=pltpu.SMEM),
            pl.BlockSpec((tb, PEER_SLOTS), lambda i: (i, 0)),
            pl.BlockSpec(tab.shape, lambda i: (0, 0), pipeline_mode=pl.Buffered(1)),
            pl.BlockSpec((tb, d), lambda i: (i, 0)),
            pl.BlockSpec((1, 1, d), lambda i: (i // blocks_per_batch, 0, 0)),
            pl.BlockSpec(e8.shape, lambda i: (0, 0)),
            pl.BlockSpec(mask.shape, lambda i: (0, 0)),
        ],
        out_specs=pl.BlockSpec((tb, d), lambda i: (i, 0)),
        out_shape=jax.ShapeDtypeStruct((n, d), F32),
        scratch_shapes=[pltpu.VMEM((STAGE_ROWS, 2 * LANES), jnp.uint32) for _ in range(PV_PAIRS)],
        compiler_params=pltpu.CompilerParams(
            dimension_semantics=("arbitrary",), vmem_limit_bytes=VMEM_LIMIT),
        name="peer_v",
    )(idx, w, tab, x1, gt2, e8, mask)


def _layer(x, mod, g_norm1, g_norm2, w_in, q_norm_g, k_norm_g, conv_w, conv_b, conv_ln_g, conv_ln_b,
           rel_bias, g_out_conv, g_out_attn, w_out, w_peer_q, peer_k1, peer_k2, peer_u, peer_v):
    bsz, s, d = x.shape
    n = bsz * s
    mod3 = mod.reshape(bsz, 6, d)
    row = lambda a: a.reshape(1, -1)

    c0 = 2 * CONV_CH
    c1 = c0 + 3 * ATT_W
    c2 = c1 + IDX_HEADS * IDX_DIM
    c3 = c2 + IDX_DIM
    wa = w_in[:, :c0].astype(BF16)
    wqkv = w_in[:, c0:c1].astype(BF16)
    widx = jnp.concatenate([w_in[:, c1:c2], w_in[:, c2:c3], w_in[:, c2:c3]], axis=1).astype(BF16)
    wwi = jnp.zeros((SUBLANES, d), F32).at[:IDX_HEADS].set(w_in[:, c3:c3 + IDX_HEADS].T).astype(BF16)
    head = np.arange(ATT_W) // HEAD_DIM
    e2 = jnp.asarray((head[:, None] == head[None, :]).astype(np.float32))

    conv_n, q, k, vt, qi, ki, wit = _inproj(
        x, mod3, row(g_norm1), wa, wqkv, widx, wwi,
        row(jnp.tile(q_norm_g, ATT_HEADS)), row(jnp.tile(k_norm_g, ATT_HEADS)), e2,
        conv_w.reshape(CONV_WIDTH, CONV_CH), row(conv_b), row(conv_ln_g), row(conv_ln_b), row(g_out_conv))
    attn_n = _attn(q, k, vt, qi, ki, wit, rel_bias, row(g_out_attn))

    x1, h2, idx, gate = _mid(
        x, conv_n, attn_n, mod3, w_out[:CONV_CH].astype(BF16), w_out[CONV_CH:].astype(BF16), row(g_norm2),
        w_peer_q.astype(BF16), peer_k1.astype(BF16), peer_k2.astype(BF16))

    idx = idx.reshape(n, PEER_SLOTS)
    w = _peer_u(idx, _pack_table(peer_u), h2.reshape(n, d), gate.reshape(n, PEER_SLOTS))
    out = _peer_v(idx, w, _pack_table(peer_v), x1.reshape(n, d), mod3[:, 5:6, :], s // TB_PEER)
    return out.reshape(bsz, s, d)


def kernel(x, c, w_ada, b_ada, g_norm1, g_norm2, w_in, q_norm_g, k_norm_g, conv_w, conv_b, conv_ln_g,
           conv_ln_b, rel_bias, g_out_conv, g_out_attn, w_out, w_peer_q, peer_k1, peer_k2, peer_u, peer_v):
    depth = w_ada.shape[0]
    for l in range(depth):
        mod = _ada(c, w_ada[l], b_ada[l])
        x = _layer(x, mod, g_norm1[l], g_norm2[l], w_in[l], q_norm_g[l], k_norm_g[l], conv_w[l], conv_b[l],
                   conv_ln_g[l], conv_ln_b[l], rel_bias, g_out_conv[l], g_out_attn[l], w_out[l],
                   w_peer_q[l], peer_k1[l], peer_k2[l], peer_u[l], peer_v[l])
    return x
```

```python
import functools
import math

import numpy as np
import jax
import jax.numpy as jnp
from jax import lax
from jax.experimental import pallas as pl
from jax.experimental.pallas import tpu as pltpu

F32 = jnp.float32
BF16 = jnp.bfloat16
I32 = jnp.int32
HIGHEST = lax.Precision.HIGHEST

D_MODEL = 1024
CHUNK = 64
CONV_CH = 512
CONV_WIDTH = 31
ATT_HEADS = 8
HEAD_DIM = 64
ATT_W = ATT_HEADS * HEAD_DIM
IDX_HEADS = 4
IDX_DIM = 64
IDX_SCALE = (IDX_HEADS * IDX_DIM) ** -0.5
TOPK_MAX = 256
REL_BUCKETS = 32
REL_MAX_DIST = 128
PEER_HEADS = 8
N_KEYS = 128
N_EXPERTS = N_KEYS * N_KEYS
PEER_TOPK = 16
PEER_SLOTS = PEER_HEADS * PEER_TOPK
EPS = 1e-6

LANES = 128
SUBLANES = 8
PACK = 16
VMEM_LIMIT = 56 * 1024 * 1024

TM_IN = 512
CONV_ROWS = 64
HALO = 32
TQ = 256
KT = 256
TM_MID = 256
TB_PEER = 128
PEER_GROUPS = 2
HALF = D_MODEL // 2
ROWS_PER_EXPERT = HALF // LANES

NT_DIMS = (((1,), (1,)), ((), ()))

_NEG_INF_BITS = int(np.array(-np.inf, np.float32).view(np.int32))
KEY_NEG_INF = _NEG_INF_BITS ^ 0x7FFFFFFF
MIN_NORMAL_BITS = 0x00800000
BAND = MIN_NORMAL_BITS >> 16
INT_MIN = -(2 ** 31)


def _nt(a, b, precision=None):
    return lax.dot_general(a, b, NT_DIMS, precision=precision, preferred_element_type=F32)


def _ada_kernel(c_ref, w_ref, b_ref, o_ref):
    a = jax.nn.silu(c_ref[...])
    o_ref[...] = jnp.dot(a, w_ref[...], precision=HIGHEST, preferred_element_type=F32) + b_ref[...]


def _ada(c, w_ada, b_ada):
    bsz, d = c.shape
    return pl.pallas_call(
        _ada_kernel,
        grid=(6,),
        in_specs=[
            pl.BlockSpec((bsz, d), lambda j: (0, 0)),
            pl.BlockSpec((d, d), lambda j: (0, j)),
            pl.BlockSpec((1, d), lambda j: (0, j)),
        ],
        out_specs=pl.BlockSpec((bsz, d), lambda j: (0, j)),
        out_shape=jax.ShapeDtypeStruct((bsz, 6 * d), F32),
        name="ada",
    )(c, w_ada, b_ada.reshape(1, 6 * d))


def _inproj_kernel(x_ref, mod_ref, g1_ref, wa_ref, wqkv_ref, widx_ref, wwi_ref, gq_ref, gk_ref, e2_ref,
                   cw_ref, cb_ref, lng_ref, lnb_ref, goc_ref,
                   conv_ref, q_ref, k_ref, vt_ref, qi_ref, ki_ref, wit_ref, ubuf):
    j = pl.program_id(1)
    tm = x_ref.shape[1]
    x = x_ref[0]
    sh1 = mod_ref[0, 0:1, :]
    sc1 = mod_ref[0, 1:2, :]
    r = lax.rsqrt(jnp.mean(x * x, axis=-1, keepdims=True) + EPS)
    h = (x * r) * g1_ref[...] * (1.0 + sc1) + sh1
    hb = h.astype(BF16)

    pa = jnp.dot(hb, wa_ref[...], preferred_element_type=F32)
    u = pa[:, :CONV_CH] * jax.nn.sigmoid(pa[:, CONV_CH:])

    @pl.when(j == 0)
    def _():
        ubuf[0:HALO, :] = jnp.zeros((HALO, CONV_CH), F32)

    ubuf[HALO:HALO + tm, :] = u
    first = HALO - (CONV_WIDTH - 1)
    for rb in range(tm // CONV_ROWS):
        base = rb * CONV_ROWS
        acc = jnp.zeros((CONV_ROWS, CONV_CH), F32) + cb_ref[...]
        for t in range(CONV_WIDTH):
            acc = acc + cw_ref[t:t + 1, :] * ubuf[base + first + t:base + first + t + CONV_ROWS, :]
        mu = jnp.mean(acc, axis=-1, keepdims=True)
        xc = acc - mu
        y = xc * lax.rsqrt(jnp.mean(xc * xc, axis=-1, keepdims=True) + EPS)
        y = jax.nn.silu(y * lng_ref[...] + lnb_ref[...])
        y = y * lax.rsqrt(jnp.mean(y * y, axis=-1, keepdims=True) + EPS) * goc_ref[...]
        conv_ref[0, base:base + CONV_ROWS, :] = y.astype(BF16)
    ubuf[0:HALO, :] = ubuf[tm:tm + HALO, :]

    pq = jnp.dot(hb, wqkv_ref[...], preferred_element_type=F32)
    q = pq[:, :ATT_W]
    k = pq[:, ATT_W:2 * ATT_W]
    v = pq[:, 2 * ATT_W:]
    e2 = e2_ref[...]
    qs = jnp.dot(q * q, e2, precision=HIGHEST, preferred_element_type=F32) * (1.0 / HEAD_DIM)
    ks = jnp.dot(k * k, e2, precision=HIGHEST, preferred_element_type=F32) * (1.0 / HEAD_DIM)
    qn = q * lax.rsqrt(qs + EPS) * gq_ref[...] * (HEAD_DIM ** -0.5)
    kn = k * lax.rsqrt(ks + EPS) * gk_ref[...]
    lane = lax.broadcasted_iota(I32, (tm, LANES), 1)
    low = lane < HEAD_DIM
    for p in range(ATT_HEADS // 2):
        slab = qn[:, p * LANES:(p + 1) * LANES]
        q_ref[0, 2 * p] = jnp.where(low, slab, 0.0).T.astype(BF16)
        q_ref[0, 2 * p + 1] = jnp.where(low, 0.0, slab).T.astype(BF16)
        k_ref[0, p] = kn[:, p * LANES:(p + 1) * LANES].astype(BF16)
    vt_ref[0] = v.T.astype(BF16)

    pc = jnp.dot(hb, widx_ref[...], preferred_element_type=F32)
    for p in range(IDX_HEADS // 2):
        slab = pc[:, p * LANES:(p + 1) * LANES]
        qi_ref[0, 2 * p] = jnp.where(low, slab, 0.0).T.astype(BF16)
        qi_ref[0, 2 * p + 1] = jnp.where(low, 0.0, slab).T.astype(BF16)
    ki_ref[0] = pc[:, 2 * LANES:3 * LANES].astype(BF16)
    wit_ref[0] = _nt(wwi_ref[...], hb) * IDX_SCALE


def _inproj(x, mod3, g1, wa, wqkv, widx, wwi, gq, gk, e2, cw, cb, lng, lnb, goc):
    bsz, s, d = x.shape
    tm = TM_IN
    nt = s // tm
    full = lambda shape: pl.BlockSpec(shape, lambda b, j: (0,) * len(shape))
    out_shape = (
        jax.ShapeDtypeStruct((bsz, s, CONV_CH), BF16),
        jax.ShapeDtypeStruct((bsz, ATT_HEADS, LANES, s), BF16),
        jax.ShapeDtypeStruct((bsz, ATT_HEADS // 2, s, LANES), BF16),
        jax.ShapeDtypeStruct((bsz, ATT_W, s), BF16),
        jax.ShapeDtypeStruct((bsz, IDX_HEADS, LANES, s), BF16),
        jax.ShapeDtypeStruct((bsz, s, LANES), BF16),
        jax.ShapeDtypeStruct((bsz, SUBLANES, s), F32),
    )
    out_specs = (
        pl.BlockSpec((1, tm, CONV_CH), lambda b, j: (b, j, 0)),
        pl.BlockSpec((1, ATT_HEADS, LANES, tm), lambda b, j: (b, 0, 0, j)),
        pl.BlockSpec((1, ATT_HEADS // 2, tm, LANES), lambda b, j: (b, 0, j, 0)),
        pl.BlockSpec((1, ATT_W, tm), lambda b, j: (b, 0, j)),
        pl.BlockSpec((1, IDX_HEADS, LANES, tm), lambda b, j: (b, 0, 0, j)),
        pl.BlockSpec((1, tm, LANES), lambda b, j: (b, j, 0)),
        pl.BlockSpec((1, SUBLANES, tm), lambda b, j: (b, 0, j)),
    )
    return pl.pallas_call(
        _inproj_kernel,
        grid=(bsz, nt),
        in_specs=[
            pl.BlockSpec((1, tm, d), lambda b, j: (b, j, 0)),
            pl.BlockSpec((1, 6, d), lambda b, j: (b, 0, 0)),
            full(g1.shape), full(wa.shape), full(wqkv.shape), full(widx.shape), full(wwi.shape),
            full(gq.shape), full(gk.shape), full(e2.shape),
            full(cw.shape), full(cb.shape), full(lng.shape), full(lnb.shape), full(goc.shape),
        ],
        out_specs=out_specs,
        out_shape=out_shape,
        scratch_shapes=[pltpu.VMEM((tm + HALO, CONV_CH), F32)],
        compiler_params=pltpu.CompilerParams(
            dimension_semantics=("arbitrary", "arbitrary"), vmem_limit_bytes=VMEM_LIMIT),
        name="inproj",
    )(x, mod3, g1, wa, wqkv, widx, wwi, gq, gk, e2, cw, cb, lng, lnb, goc)


def _t5_bucket_np(rel):
    half = REL_BUCKETS // 2
    max_exact = half // 2
    ret = np.where(rel > 0, half, 0)
    n = np.abs(rel)
    nf = np.maximum(n, 1).astype(np.float64)
    large = max_exact + (np.log(nf / max_exact) / math.log(REL_MAX_DIST / max_exact)
                         * (half - max_exact)).astype(np.int32)
    large = np.minimum(large, half - 1)
    return (ret + np.where(n < max_exact, n, large)).astype(np.int32)


NEAR = REL_MAX_DIST + TQ
FAR_BUCKET = REL_BUCKETS // 2 - 1


def _near_buckets():
    r = np.arange(NEAR)[:, None]
    t = np.arange(TQ)[None, :]
    return _t5_bucket_np(r - REL_MAX_DIST - t)


def _attn_kernel(q_ref, k_ref, vt_ref, qi_ref, ki_ref, wit_ref, bkt_ref, rb_ref, goa_ref, o_ref,
                 keys_s, khi_s, maskb_s, relb_s, ot_s, m_s, mo_s, l_s, lg_s, *, nsel):
    b = pl.program_id(0)
    i = pl.program_id(1)
    t0 = i * TQ
    n_tiles = i + 1
    neg_inf = F32(-jnp.inf)

    @pl.when((b == 0) & (i == 0))
    def _():
        bk = bkt_ref[...]
        for h in range(ATT_HEADS):
            far = rb_ref[FAR_BUCKET, h]
            acc = jnp.zeros((NEAR, TQ), F32)
            for bb in range(REL_BUCKETS):
                acc = jnp.where(bk == bb, rb_ref[bb, h] - far, acc)
            relb_s[h, 0:2 * KT - NEAR, :] = jnp.zeros((2 * KT - NEAR, TQ), F32)
            relb_s[h, 2 * KT - NEAR:2 * KT, :] = acc

    qpos = t0 + lax.broadcasted_iota(I32, (1, TQ), 1)
    limit = (qpos // CHUNK + 1) * CHUNK
    row_iota = lax.broadcasted_iota(I32, (KT, TQ), 0)

    def tile_start(jt):
        return pl.multiple_of(jt * KT, KT)

    wi = wit_ref[0]

    def p1(jt, c):
        ks = tile_start(jt)
        kit = ki_ref[0, pl.ds(ks, KT), :]
        acc = jnp.zeros((KT, TQ), F32)
        for h in range(IDX_HEADS):
            lgt = jnp.dot(kit, qi_ref[0, h], preferred_element_type=F32)
            acc = acc + jnp.maximum(lgt, 0.0) * wi[h:h + 1, :]
        sc = jnp.where(row_iota + ks < limit, acc, neg_inf)
        bits = pltpu.bitcast(sc, I32)
        keys_s[pl.ds(ks, KT), :] = bits ^ (lax.shift_right_arithmetic(bits, 31) & 0x7FFFFFFF)
        top = jnp.where((bits & 0x7FFFFFFF) < MIN_NORMAL_BITS, 0, bits & I32(-65536))
        khi_s[pl.ds(ks, KT), :] = pltpu.bitcast(top, F32).astype(BF16)
        return c

    lax.fori_loop(0, n_tiles, p1, 0)

    def count(pred):
        def body(jt, acc):
            ks = tile_start(jt)
            m = pred(keys_s[pl.ds(ks, KT), :], ks).astype(F32)
            return acc + jnp.sum(m.reshape(KT // SUBLANES, SUBLANES, TQ), axis=0)
        acc = lax.fori_loop(0, n_tiles, body, jnp.zeros((SUBLANES, TQ), F32))
        return jnp.sum(acc, axis=0, keepdims=True)

    def count_top(cand_hi):
        cand_hi = jnp.where((cand_hi > 0) & (cand_hi < BAND), BAND, cand_hi)
        pat = cand_hi ^ (lax.shift_right_arithmetic(cand_hi, 15) & 0x7FFF)
        cb = pltpu.bitcast(lax.shift_left(pat, 16), F32).astype(BF16)
        one = jnp.ones((KT, TQ), BF16)
        zero = jnp.zeros((KT, TQ), BF16)

        def body(jt, acc):
            ks = tile_start(jt)
            hit = jnp.where(khi_s[pl.ds(ks, KT), :] >= cb, one, zero)
            parts = [hit[r:r + PACK, :] for r in range(0, KT, PACK)]
            while len(parts) > 1:
                parts = [parts[k] + parts[k + 1] for k in range(0, len(parts), 2)]
            return acc + parts[0].astype(F32)
        acc = lax.fori_loop(0, n_tiles, body, jnp.zeros((PACK, TQ), F32))
        return jnp.sum(acc, axis=0, keepdims=True)

    def select_thr():
        c0 = count_top(jnp.zeros((1, TQ), I32))
        ok = c0 >= nsel
        t = jnp.where(ok, 0, -(2 ** 15)).astype(I32)
        cnt = jnp.where(ok, c0, F32(2 ** 30))

        def top_step(it, carry):
            t, cnt = carry
            cand = t + lax.shift_left(I32(1), I32(14) - it)
            c = count_top(cand)
            ok = c >= nsel
            return jnp.where(ok, cand, t), jnp.where(ok, c, cnt)

        t, cnt = lax.fori_loop(0, 15, top_step, (t, cnt))

        band = jnp.where(jnp.abs(2 * t + 1) < 2 * BAND, 1.0, 0.0)

        def band_counts():
            c_zero = count(lambda kt, ks: kt >= 0)
            c_tiny = count(lambda kt, ks: kt >= 1)
            c_norm = count(lambda kt, ks: kt >= MIN_NORMAL_BITS)
            return c_zero, jnp.where(c_tiny == c_norm, jnp.where(c_zero >= nsel, 1.0, 0.0), 0.0)

        c_zero, plain = lax.cond(jnp.max(band) > 0.0, band_counts,
                                 lambda: (jnp.zeros((1, TQ), F32), jnp.ones((1, TQ), F32)))
        settled = band * plain > 0.0
        hard = band * (1.0 - plain) > 0.0
        t = jnp.where(settled, 0, lax.shift_left(jnp.where(hard, -BAND, t), 16))
        cnt = jnp.where(settled, c_zero, jnp.where(hard, F32(2 ** 30), cnt))
        first_bit = jnp.where(jnp.max(band * (1.0 - plain)) > 0.0, 23, 15).astype(I32)

        def low_cond(carry):
            bit, _, cnt = carry
            return (bit >= 0) & (jnp.max(jnp.where(settled, F32(nsel), cnt)) > nsel)

        def low_step(carry):
            bit, t, cnt = carry
            cand = t + lax.shift_left(I32(1), bit)
            c = count(lambda kt, ks: kt >= cand)
            ok = c >= nsel
            return bit - 1, jnp.where(ok, cand, t), jnp.where(ok, c, cnt)

        _, t, cnt = lax.while_loop(low_cond, low_step, (first_bit, t, cnt))
        return t, cnt

    thr, c_ge = lax.cond(
        i > 0, select_thr,
        lambda: (jnp.full((1, TQ), KEY_NEG_INF + 1, I32), jnp.full((1, TQ), nsel, F32)))

    def mask_plain():
        def p3(jt, c):
            ks = tile_start(jt)
            maskb_s[pl.ds(ks, KT), :] = jnp.where(keys_s[pl.ds(ks, KT), :] >= thr, 0.0, neg_inf)
            return c
        lax.fori_loop(0, n_tiles, p3, 0)
        return I32(0)

    def mask_ties():
        quota = nsel - count(lambda kt, ks: kt > thr)
        tri = jnp.where(lax.broadcasted_iota(I32, (KT, KT), 0) >= lax.broadcasted_iota(I32, (KT, KT), 1),
                        1.0, 0.0).astype(BF16)

        def p3(jt, before):
            ks = tile_start(jt)
            kt = keys_s[pl.ds(ks, KT), :]
            eq = kt == thr
            rank = before + jnp.dot(tri, jnp.where(eq, 1.0, 0.0).astype(BF16), preferred_element_type=F32)
            tied = jnp.where(eq, jnp.where(rank <= quota, 0.0, neg_inf), neg_inf)
            maskb_s[pl.ds(ks, KT), :] = jnp.where(kt > thr, 0.0, tied)
            return rank[KT - 1:KT, :]

        lax.fori_loop(0, n_tiles, p3, jnp.zeros((1, TQ), F32))
        return I32(0)

    lax.cond(jnp.max(c_ge) > nsel, mask_ties, mask_plain)

    m_s[...] = jnp.full((ATT_HEADS, TQ), neg_inf, F32)
    l_s[...] = jnp.zeros((ATT_HEADS, TQ), F32)
    ot_s[...] = jnp.zeros((ATT_W, TQ), F32)

    def att_tile(jt, near):
        ks = tile_start(jt)
        mb = maskb_s[pl.ds(ks, KT), :]
        for h in range(ATT_HEADS):
            l = jnp.dot(k_ref[0, h // 2, pl.ds(ks, KT), :], q_ref[0, h], preferred_element_type=F32) + mb
            if near:
                off = pl.multiple_of((jt - (n_tiles - 2)) * KT, KT)
                l = l + relb_s[h, pl.ds(off, KT), :]
            lg_s[h] = l
            m_old = m_s[h:h + 1, :]
            mo_s[h:h + 1, :] = m_old
            m_s[h:h + 1, :] = jnp.maximum(m_old, jnp.max(l, axis=0, keepdims=True))
        for h in range(ATT_HEADS):
            m_new = m_s[h:h + 1, :]
            m_ref = jnp.where(m_new == neg_inf, 0.0, m_new)
            alpha = jnp.exp(mo_s[h:h + 1, :] - m_ref)
            p = jnp.exp(lg_s[h] - m_ref)
            l_s[h:h + 1, :] = alpha * l_s[h:h + 1, :] + jnp.sum(p, axis=0, keepdims=True)
            vt = vt_ref[0, h * HEAD_DIM:(h + 1) * HEAD_DIM, pl.ds(ks, KT)]
            rows = slice(h * HEAD_DIM, (h + 1) * HEAD_DIM)
            ot_s[rows, :] = ot_s[rows, :] * alpha + jnp.dot(vt, p.astype(BF16), preferred_element_type=F32)

    def far_tile(jt, c):
        att_tile(jt, False)
        return c

    def near_tile(jt, c):
        att_tile(jt, True)
        return c

    n_far = jnp.maximum(n_tiles - 2, 0)
    lax.fori_loop(0, n_far, far_tile, 0)
    lax.fori_loop(n_far, n_tiles, near_tile, 0)
    for h in range(ATT_HEADS):
        rows = slice(h * HEAD_DIM, (h + 1) * HEAD_DIM)
        ot_s[rows, :] = ot_s[rows, :] / l_s[h:h + 1, :]

    ot = ot_s[...]
    ms = jnp.mean(ot * ot, axis=0, keepdims=True)
    y = (ot * lax.rsqrt(ms + EPS)).T * goa_ref[...]
    o_ref[0] = y.astype(BF16)


def _attn(q, k, vt, qi, ki, wit, rel_bias, goa):
    bsz, _, _, s = q.shape
    nsel = min(TOPK_MAX, s // 4)
    assert nsel == TQ and s % TQ == 0, "attention kernel assumes TOPK_MAX-sized query blocks"
    bkt = jnp.asarray(_near_buckets())
    kern = functools.partial(_attn_kernel, nsel=nsel)
    return pl.pallas_call(
        kern,
        grid=(bsz, s // TQ),
        in_specs=[
            pl.BlockSpec((1, ATT_HEADS, LANES, TQ), lambda b, i: (b, 0, 0, i)),
            pl.BlockSpec((1, ATT_HEADS // 2, s, LANES), lambda b, i: (b, 0, 0, 0)),
            pl.BlockSpec((1, ATT_W, s), lambda b, i: (b, 0, 0)),
            pl.BlockSpec((1, IDX_HEADS, LANES, TQ), lambda b, i: (b, 0, 0, i)),
            pl.BlockSpec((1, s, LANES), lambda b, i: (b, 0, 0)),
            pl.BlockSpec((1, SUBLANES, TQ), lambda b, i: (b, 0, i)),
            pl.BlockSpec((NEAR, TQ), lambda b, i: (0, 0)),
            pl.BlockSpec(memory_space=pltpu.SMEM),
            pl.BlockSpec((1, ATT_W), lambda b, i: (0, 0)),
        ],
        out_specs=pl.BlockSpec((1, TQ, ATT_W), lambda b, i: (b, i, 0)),
        out_shape=jax.ShapeDtypeStruct((bsz, s, ATT_W), BF16),
        scratch_shapes=[
            pltpu.VMEM((s, TQ), I32),
            pltpu.VMEM((s, TQ), BF16),
            pltpu.VMEM((s, TQ), F32),
            pltpu.VMEM((ATT_HEADS, 2 * KT, TQ), F32),
            pltpu.VMEM((ATT_W, TQ), F32),
            pltpu.VMEM((ATT_HEADS, TQ), F32),
            pltpu.VMEM((ATT_HEADS, TQ), F32),
            pltpu.VMEM((ATT_HEADS, TQ), F32),
            pltpu.VMEM((ATT_HEADS, KT, TQ), F32),
        ],
        compiler_params=pltpu.CompilerParams(
            dimension_semantics=("arbitrary", "arbitrary"), vmem_limit_bytes=VMEM_LIMIT),
        name="attn",
    )(q, k, vt, qi, ki, wit, bkt, rel_bias, goa)


def _topk_rows(s, payload, k):
    nrows = s.shape[0]
    rows = lax.broadcasted_iota(I32, s.shape, 0).astype(F32)
    vals, pays = [], []
    for _ in range(k):
        m = jnp.max(s, axis=0, keepdims=True)
        ix = jnp.min(jnp.where(s == m, rows, float(nrows)), axis=0, keepdims=True)
        hit = rows == ix
        if payload is None:
            pays.append(ix)
        else:
            pays.append(jnp.max(jnp.where(hit, payload, -1.0), axis=0, keepdims=True))
        vals.append(m)
        s = jnp.where(hit, -jnp.inf, s)
    return jnp.concatenate(vals, axis=0), jnp.concatenate(pays, axis=0)


_PAIR_ROWS = tuple((a, PEER_TOPK // (a + 1)) for a in range(PEER_TOPK // 2))


def _mid_kernel(x_ref, cn_ref, an_ref, mod_ref, wo1_ref, wo2_ref, g2_ref, wpq_ref, k1_ref, k2_ref,
                x1_ref, h2_ref, idx_ref, gate_ref, qq_s, idt_s, gt_s):
    tm = x_ref.shape[1]
    x = x_ref[0]
    gt1 = mod_ref[0, 2:3, :]
    sh2 = mod_ref[0, 3:4, :]
    sc2 = mod_ref[0, 4:5, :]
    proj = (jnp.dot(cn_ref[0], wo1_ref[...], preferred_element_type=F32)
            + jnp.dot(an_ref[0], wo2_ref[...], preferred_element_type=F32))
    x1 = x + gt1 * proj
    x1_ref[0] = x1
    r = lax.rsqrt(jnp.mean(x1 * x1, axis=-1, keepdims=True) + EPS)
    h2 = (x1 * r) * g2_ref[...] * (1.0 + sc2) + sh2
    h2_ref[0] = h2
    qq_s[...] = jnp.dot(h2.astype(BF16), wpq_ref[...], preferred_element_type=F32).astype(BF16)

    def route_unit(hh, lt):
        rows = pl.ds(lt * LANES, LANES)
        q1 = qq_s[rows, pl.ds(pl.multiple_of(hh * 2 * N_KEYS, LANES), N_KEYS)]
        q2 = qq_s[rows, pl.ds(pl.multiple_of(hh * 2 * N_KEYS + N_KEYS, LANES), N_KEYS)]
        v1, i1 = _topk_rows(_nt(k1_ref[hh], q1), None, PEER_TOPK)
        v2, i2 = _topk_rows(_nt(k2_ref[hh], q2), None, PEER_TOPK)
        sub = lax.broadcasted_iota(I32, (SUBLANES, LANES), 0)
        cands, cidxs = [], []
        for a, nb in _PAIR_ROWS:
            nrows = max(nb, SUBLANES)
            val = v1[a:a + 1, :] + v2[0:nrows, :]
            if nb < SUBLANES:
                val = jnp.where(sub < nb, val, -jnp.inf)
            cands.append(val)
            cidxs.append(i1[a:a + 1, :] * float(N_KEYS) + i2[0:nrows, :])
        half = PEER_TOPK // 2
        cands.append(v1[half:, :] + v2[0:1, :])
        cidxs.append(i1[half:, :] * float(N_KEYS) + i2[0:1, :])
        best, experts = _topk_rows(jnp.concatenate(cands, axis=0), jnp.concatenate(cidxs, axis=0), PEER_TOPK)
        e = jnp.exp(best - best[0:1, :])
        g = e / jnp.sum(e, axis=0, keepdims=True)
        slots = pl.ds(pl.multiple_of(hh * PEER_TOPK, PEER_TOPK), PEER_TOPK)
        cols = pl.ds(lt * LANES, LANES)
        idt_s[slots, cols] = experts * float(ROWS_PER_EXPERT)
        gt_s[slots, cols] = g

    def route(hh, c):
        for lt in range(tm // LANES):
            route_unit(hh, lt)
        return c

    lax.fori_loop(0, PEER_HEADS, route, 0)
    idx_ref[0] = idt_s[...].T.astype(I32)
    gate_ref[0] = gt_s[...].T


def _mid(x, cn, an, mod3, wo1, wo2, g2, wpq, k1, k2):
    bsz, s, d = x.shape
    tm = TM_MID
    full = lambda shape: pl.BlockSpec(shape, lambda b, j: (0,) * len(shape))
    tok = lambda w: pl.BlockSpec((1, tm, w), lambda b, j: (b, j, 0))
    return pl.pallas_call(
        _mid_kernel,
        grid=(bsz, s // tm),
        in_specs=[tok(d), tok(CONV_CH), tok(ATT_W), pl.BlockSpec((1, 6, d), lambda b, j: (b, 0, 0)),
                  full(wo1.shape), full(wo2.shape), full(g2.shape), full(wpq.shape), full(k1.shape), full(k2.shape)],
        out_specs=(tok(d), tok(d), tok(PEER_SLOTS), tok(PEER_SLOTS)),
        out_shape=(
            jax.ShapeDtypeStruct((bsz, s, d), F32),
            jax.ShapeDtypeStruct((bsz, s, d), F32),
            jax.ShapeDtypeStruct((bsz, s, PEER_SLOTS), I32),
            jax.ShapeDtypeStruct((bsz, s, PEER_SLOTS), F32),
        ),
        scratch_shapes=[
            pltpu.VMEM((tm, PEER_HEADS * 2 * N_KEYS), BF16),
            pltpu.VMEM((PEER_SLOTS, tm), F32),
            pltpu.VMEM((PEER_SLOTS, tm), F32),
        ],
        compiler_params=pltpu.CompilerParams(
            dimension_semantics=("arbitrary", "arbitrary"), vmem_limit_bytes=VMEM_LIMIT),
        name="mid",
    )(x, cn, an, mod3, wo1, wo2, g2, wpq, k1, k2)


def _pack_table(t):
    tb = t.astype(BF16)
    lo = lax.bitcast_convert_type(tb[:, :HALF], jnp.uint16).astype(jnp.uint32)
    hi = lax.bitcast_convert_type(tb[:, HALF:], jnp.uint16).astype(jnp.uint32)
    return (lo | (hi << 16)).reshape(t.shape[0] * ROWS_PER_EXPERT, LANES)


def _unpack(w):
    lo = pltpu.bitcast(lax.shift_left(w, jnp.uint32(16)), F32)
    hi = pltpu.bitcast(w & jnp.uint32(0xFFFF0000), F32)
    return lo, hi


def _gather_row(tab_ref, row):
    return tab_ref[pl.ds(pl.multiple_of(row, ROWS_PER_EXPERT), ROWS_PER_EXPERT), :]


STAGE_ROWS = PEER_SLOTS * ROWS_PER_EXPERT


def _token_pieces(xg, r, first, count):
    return jnp.concatenate(
        [xg[r:r + 1, (first + q) * LANES:(first + q + 1) * LANES] for q in range(count)], axis=0)


def _peer_u_kernel(idx_ref, tab_ref, x_ref, gate_ref, w_ref, *stages):
    tb = w_ref.shape[0]
    sub = lax.broadcasted_iota(I32, (SUBLANES, PEER_SLOTS), 0)

    def group(g, c):
        for k in range(PEER_GROUPS):
            eight(pl.multiple_of((g * PEER_GROUPS + k) * SUBLANES, SUBLANES), stages[k * SUBLANES:(k + 1) * SUBLANES])
        return c

    def eight(t0, stages):
        act8 = jnp.zeros((SUBLANES, PEER_SLOTS), F32)
        xg = x_ref[pl.ds(t0, SUBLANES), :]
        xlo, xhi, rows_t = [], [], []
        for r in range(SUBLANES):
            xlo.append(jnp.concatenate([_token_pieces(xg, r, 0, ROWS_PER_EXPERT)] * 2, axis=0))
            xhi.append(jnp.concatenate([_token_pieces(xg, r, ROWS_PER_EXPERT, ROWS_PER_EXPERT)] * 2, axis=0))
            rows_t.append(idx_ref.at[t0 + r])
        for j in range(0, PEER_SLOTS, 2):
            for r in range(SUBLANES):
                pair = jnp.concatenate(
                    [_gather_row(tab_ref, rows_t[r][j]), _gather_row(tab_ref, rows_t[r][j + 1])], axis=0)
                lo, hi = _unpack(pair)
                row = j * ROWS_PER_EXPERT
                stages[r][row:row + 2 * ROWS_PER_EXPERT, :] = lo * xlo[r] + hi * xhi[r]
        for r in range(SUBLANES):
            stage = stages[r]
            cs = stage[pl.ds(0, PEER_SLOTS, stride=ROWS_PER_EXPERT), :]
            for q in range(1, ROWS_PER_EXPERT):
                cs = cs + stage[pl.ds(q, PEER_SLOTS, stride=ROWS_PER_EXPERT), :]
            act = jnp.sum(cs.T, axis=0, keepdims=True)
            act8 = jnp.where(sub == r, act, act8)
        rows = pl.ds(t0, SUBLANES)
        w_ref[rows, :] = gate_ref[rows, :] * jax.nn.gelu(act8)

    lax.fori_loop(0, tb // (PEER_GROUPS * SUBLANES), group, 0)


def _peer_u(idx, tab, h2, gate):
    n, d = h2.shape
    tb = TB_PEER
    return pl.pallas_call(
        _peer_u_kernel,
        grid=(n // tb,),
        in_specs=[
            pl.BlockSpec((tb, PEER_SLOTS), lambda i: (i, 0), memory_space=pltpu.SMEM),
            pl.BlockSpec(tab.shape, lambda i: (0, 0), pipeline_mode=pl.Buffered(1)),
            pl.BlockSpec((tb, d), lambda i: (i, 0)),
            pl.BlockSpec((tb, PEER_SLOTS), lambda i: (i, 0)),
        ],
        out_specs=pl.BlockSpec((tb, PEER_SLOTS), lambda i: (i, 0)),
        out_shape=jax.ShapeDtypeStruct((n, PEER_SLOTS), F32),
        scratch_shapes=[pltpu.VMEM((STAGE_ROWS, LANES), F32) for _ in range(PEER_GROUPS * SUBLANES)],
        compiler_params=pltpu.CompilerParams(
            dimension_semantics=("arbitrary",), vmem_limit_bytes=VMEM_LIMIT),
        name="peer_u",
    )(idx, tab, h2, gate)


PV_PAIRS = SUBLANES // 2


def _peer_v_kernel(idx_ref, w_ref, tab_ref, x1_ref, gt2_ref, e8_ref, mask_ref, o_ref, *stages):
    tb = idx_ref.shape[0]
    gt2 = gt2_ref[0]
    mask = mask_ref[...]
    e8 = e8_ref[...]

    def group(g, c):
        for k in range(PEER_GROUPS):
            eight(pl.multiple_of((g * PEER_GROUPS + k) * SUBLANES, SUBLANES), stages[k * PV_PAIRS:(k + 1) * PV_PAIRS])
        return c

    def eight(g0, stages):
        peers = []
        for p in range(PV_PAIRS):
            stage = stages[p]
            t0 = g0 + 2 * p
            for u in range(2):
                rows_t = idx_ref.at[t0 + u]
                for j in range(PEER_SLOTS):
                    stage[j * ROWS_PER_EXPERT:(j + 1) * ROWS_PER_EXPERT, u * LANES:(u + 1) * LANES] = (
                        _gather_row(tab_ref, rows_t[j]))
            w2 = w_ref[pl.ds(t0, 2), :]
            hi = w2.astype(BF16).astype(F32)
            rep = jnp.dot(jnp.concatenate([hi, w2 - hi], axis=0).astype(BF16), e8,
                          preferred_element_type=F32)
            lhs = jnp.concatenate([rep[i:i + 1, :] * mask for i in range(4)], axis=0).astype(BF16)
            out = jnp.dot(lhs, pltpu.bitcast(stage[...], BF16), preferred_element_type=F32)
            for u in range(2):
                r0 = u * SUBLANES
                peers.append(out[r0:r0 + SUBLANES, u * LANES:(u + 1) * LANES]
                             + out[2 * SUBLANES + r0:3 * SUBLANES + r0, u * LANES:(u + 1) * LANES])
        rows = pl.ds(g0, SUBLANES)
        for q in range(SUBLANES):
            cols = slice(q * LANES, (q + 1) * LANES)
            piece = jnp.concatenate([peer[q:q + 1, :] for peer in peers], axis=0)
            o_ref[rows, cols] = x1_ref[rows, cols] + gt2[:, cols] * piece

    lax.fori_loop(0, tb // (PEER_GROUPS * SUBLANES), group, 0)


def _peer_v(idx, w, tab, x1, gt2, blocks_per_batch):
    n, d = x1.shape
    tb = TB_PEER
    q = np.arange(SUBLANES)
    piece = 2 * (q % ROWS_PER_EXPERT) + q // ROWS_PER_EXPERT
    lane = np.arange(PEER_SLOTS * SUBLANES)
    mask = jnp.asarray((lane[None, :] % SUBLANES == piece[:, None]).astype(np.float32))
    e8 = jnp.asarray(np.arange(PEER_SLOTS)[:, None] == lane[None, :] // SUBLANES, BF16)
    return pl.pallas_call(
        _peer_v_kernel,
        grid=(n // tb,),
        in_specs=[
            pl.BlockSpec((tb, PEER_SLOTS), lambda i: (i, 0), memory_space=pltpu.SMEM),
            pl.BlockSpec((tb, PEER_SLOTS), lambda i: (i, 0)),
            pl.BlockSpec(tab.shape, lambda i: (0, 0), pipeline_mode=pl.Buffered(1)),
            pl.BlockSpec((tb, d), lambda i: (i, 0)),
            pl.BlockSpec((1, 1, d), lambda i: (i // blocks_per_batch, 0, 0)),
            pl.BlockSpec(e8.shape, lambda i: (0, 0)),
            pl.BlockSpec(mask.shape, lambda i: (0, 0)),
        ],
        out_specs=pl.BlockSpec((tb, d), lambda i: (i, 0)),
        out_shape=jax.ShapeDtypeStruct((n, d), F32),
        scratch_shapes=[pltpu.VMEM((STAGE_ROWS, 2 * LANES), jnp.uint32) for _ in range(PEER_GROUPS * PV_PAIRS)],
        compiler_params=pltpu.CompilerParams(
            dimension_semantics=("arbitrary",), vmem_limit_bytes=VMEM_LIMIT),
        name="peer_v",
    )(idx, w, tab, x1, gt2, e8, mask)


def _layer(x, mod, g_norm1, g_norm2, w_in, q_norm_g, k_norm_g, conv_w, conv_b, conv_ln_g, conv_ln_b,
           rel_bias, g_out_conv, g_out_attn, w_out, w_peer_q, peer_k1, peer_k2, peer_u, peer_v):
    bsz, s, d = x.shape
    n = bsz * s
    mod3 = mod.reshape(bsz, 6, d)
    row = lambda a: a.reshape(1, -1)

    c0 = 2 * CONV_CH
    c1 = c0 + 3 * ATT_W
    c2 = c1 + IDX_HEADS * IDX_DIM
    c3 = c2 + IDX_DIM
    wa = w_in[:, :c0].astype(BF16)
    wqkv = w_in[:, c0:c1].astype(BF16)
    widx = jnp.concatenate([w_in[:, c1:c2], w_in[:, c2:c3], w_in[:, c2:c3]], axis=1).astype(BF16)
    wwi = jnp.zeros((SUBLANES, d), F32).at[:IDX_HEADS].set(w_in[:, c3:c3 + IDX_HEADS].T).astype(BF16)
    head = np.arange(ATT_W) // HEAD_DIM
    e2 = jnp.asarray((head[:, None] == head[None, :]).astype(np.float32))

    conv_n, q, k, vt, qi, ki, wit = _inproj(
        x, mod3, row(g_norm1), wa, wqkv, widx, wwi,
        row(jnp.tile(q_norm_g, ATT_HEADS)), row(jnp.tile(k_norm_g, ATT_HEADS)), e2,
        conv_w.reshape(CONV_WIDTH, CONV_CH), row(conv_b), row(conv_ln_g), row(conv_ln_b), row(g_out_conv))
    attn_n = _attn(q, k, vt, qi, ki, wit, rel_bias, row(g_out_attn))

    x1, h2, idx, gate = _mid(
        x, conv_n, attn_n, mod3, w_out[:CONV_CH].astype(BF16), w_out[CONV_CH:].astype(BF16), row(g_norm2),
        w_peer_q.astype(BF16), peer_k1.astype(BF16), peer_k2.astype(BF16))

    idx = idx.reshape(n, PEER_SLOTS)
    w = _peer_u(idx, _pack_table(peer_u), h2.reshape(n, d), gate.reshape(n, PEER_SLOTS))
    out = _peer_v(idx, w, _pack_table(peer_v), x1.reshape(n, d), mod3[:, 5:6, :], s // TB_PEER)
    return out.reshape(bsz, s, d)


def kernel(x, c, w_ada, b_ada, g_norm1, g_norm2, w_in, q_norm_g, k_norm_g, conv_w, conv_b, conv_ln_g,
           conv_ln_b, rel_bias, g_out_conv, g_out_attn, w_out, w_peer_q, peer_k1, peer_k2, peer_u, peer_v):
    depth = w_ada.shape[0]
    for l in range(depth):
        mod = _ada(c, w_ada[l], b_ada[l])
        x = _layer(x, mod, g_norm1[l], g_norm2[l], w_in[l], q_norm_g[l], k_norm_g[l], conv_w[l], conv_b[l],
                   conv_ln_g[l], conv_ln_b[l], rel_bias, g_out_conv[l], g_out_attn[l], w_out[l],
                   w_peer_q[l], peer_k1[l], peer_k2[l], peer_u[l], peer_v[l])
    return x
```

```python
import functools
import math

import numpy as np
import jax
import jax.numpy as jnp
from jax import lax
from jax.experimental import pallas as pl
from jax.experimental.pallas import tpu as pltpu

F32 = jnp.float32
BF16 = jnp.bfloat16
I32 = jnp.int32
HIGHEST = lax.Precision.HIGHEST

D_MODEL = 1024
CHUNK = 64
CONV_CH = 512
CONV_WIDTH = 31
ATT_HEADS = 8
HEAD_DIM = 64
ATT_W = ATT_HEADS * HEAD_DIM
IDX_HEADS = 4
IDX_DIM = 64
IDX_SCALE = (IDX_HEADS * IDX_DIM) ** -0.5
TOPK_MAX = 256
REL_BUCKETS = 32
REL_MAX_DIST = 128
PEER_HEADS = 8
N_KEYS = 128
N_EXPERTS = N_KEYS * N_KEYS
PEER_TOPK = 16
PEER_SLOTS = PEER_HEADS * PEER_TOPK
EPS = 1e-6

LANES = 128
SUBLANES = 8
PACK = 16
VMEM_LIMIT = 56 * 1024 * 1024

TM_IN = 512
CONV_ROWS = 64
HALO = 32
TQ = 256
KT = 256
TM_MID = 512
TB_PEER = 128
PEER_GROUPS = 2
HALF = D_MODEL // 2
ROWS_PER_EXPERT = HALF // LANES

NT_DIMS = (((1,), (1,)), ((), ()))

_NEG_INF_BITS = int(np.array(-np.inf, np.float32).view(np.int32))
KEY_NEG_INF = _NEG_INF_BITS ^ 0x7FFFFFFF
MIN_NORMAL_BITS = 0x00800000
BAND = MIN_NORMAL_BITS >> 16
INT_MIN = -(2 ** 31)


def _nt(a, b, precision=None):
    return lax.dot_general(a, b, NT_DIMS, precision=precision, preferred_element_type=F32)


def _ada_kernel(c_ref, w_ref, b_ref, o_ref):
    a = jax.nn.silu(c_ref[...])
    o_ref[...] = jnp.dot(a, w_ref[...], precision=HIGHEST, preferred_element_type=F32) + b_ref[...]


def _ada(c, w_ada, b_ada):
    bsz, d = c.shape
    return pl.pallas_call(
        _ada_kernel,
        grid=(6,),
        in_specs=[
            pl.BlockSpec((bsz, d), lambda j: (0, 0)),
            pl.BlockSpec((d, d), lambda j: (0, j)),
            pl.BlockSpec((1, d), lambda j: (0, j)),
        ],
        out_specs=pl.BlockSpec((bsz, d), lambda j: (0, j)),
        out_shape=jax.ShapeDtypeStruct((bsz, 6 * d), F32),
        name="ada",
    )(c, w_ada, b_ada.reshape(1, 6 * d))


def _inproj_kernel(x_ref, mod_ref, g1_ref, wa_ref, wqkv_ref, widx_ref, wwi_ref, gq_ref, gk_ref, e2_ref,
                   cw_ref, cb_ref, lng_ref, lnb_ref, goc_ref,
                   conv_ref, q_ref, k_ref, vt_ref, qi_ref, ki_ref, wit_ref, ubuf):
    j = pl.program_id(1)
    tm = x_ref.shape[1]
    x = x_ref[0]
    sh1 = mod_ref[0, 0:1, :]
    sc1 = mod_ref[0, 1:2, :]
    r = lax.rsqrt(jnp.mean(x * x, axis=-1, keepdims=True) + EPS)
    h = (x * r) * g1_ref[...] * (1.0 + sc1) + sh1
    hb = h.astype(BF16)

    pa = jnp.dot(hb, wa_ref[...], preferred_element_type=F32)
    u = pa[:, :CONV_CH] * jax.nn.sigmoid(pa[:, CONV_CH:])

    @pl.when(j == 0)
    def _():
        ubuf[0:HALO, :] = jnp.zeros((HALO, CONV_CH), F32)

    ubuf[HALO:HALO + tm, :] = u
    first = HALO - (CONV_WIDTH - 1)
    for rb in range(tm // CONV_ROWS):
        base = rb * CONV_ROWS
        acc = jnp.zeros((CONV_ROWS, CONV_CH), F32) + cb_ref[...]
        for t in range(CONV_WIDTH):
            acc = acc + cw_ref[t:t + 1, :] * ubuf[base + first + t:base + first + t + CONV_ROWS, :]
        mu = jnp.mean(acc, axis=-1, keepdims=True)
        xc = acc - mu
        y = xc * lax.rsqrt(jnp.mean(xc * xc, axis=-1, keepdims=True) + EPS)
        y = jax.nn.silu(y * lng_ref[...] + lnb_ref[...])
        y = y * lax.rsqrt(jnp.mean(y * y, axis=-1, keepdims=True) + EPS) * goc_ref[...]
        conv_ref[0, base:base + CONV_ROWS, :] = y.astype(BF16)
    ubuf[0:HALO, :] = ubuf[tm:tm + HALO, :]

    pq = jnp.dot(hb, wqkv_ref[...], preferred_element_type=F32)
    q = pq[:, :ATT_W]
    k = pq[:, ATT_W:2 * ATT_W]
    v = pq[:, 2 * ATT_W:]
    e2 = e2_ref[...]
    qs = jnp.dot(q * q, e2, precision=HIGHEST, preferred_element_type=F32) * (1.0 / HEAD_DIM)
    ks = jnp.dot(k * k, e2, precision=HIGHEST, preferred_element_type=F32) * (1.0 / HEAD_DIM)
    qn = q * lax.rsqrt(qs + EPS) * gq_ref[...] * (HEAD_DIM ** -0.5)
    kn = k * lax.rsqrt(ks + EPS) * gk_ref[...]
    lane = lax.broadcasted_iota(I32, (tm, LANES), 1)
    low = lane < HEAD_DIM
    for p in range(ATT_HEADS // 2):
        slab = qn[:, p * LANES:(p + 1) * LANES]
        q_ref[0, 2 * p] = jnp.where(low, slab, 0.0).T.astype(BF16)
        q_ref[0, 2 * p + 1] = jnp.where(low, 0.0, slab).T.astype(BF16)
        k_ref[0, p] = kn[:, p * LANES:(p + 1) * LANES].astype(BF16)
    vt_ref[0] = v.T.astype(BF16)

    pc = jnp.dot(hb, widx_ref[...], preferred_element_type=F32)
    for p in range(IDX_HEADS // 2):
        slab = pc[:, p * LANES:(p + 1) * LANES]
        qi_ref[0, 2 * p] = jnp.where(low, slab, 0.0).T.astype(BF16)
        qi_ref[0, 2 * p + 1] = jnp.where(low, 0.0, slab).T.astype(BF16)
    ki_ref[0] = pc[:, 2 * LANES:3 * LANES].astype(BF16)
    wit_ref[0] = _nt(wwi_ref[...], hb) * IDX_SCALE


def _inproj(x, mod3, g1, wa, wqkv, widx, wwi, gq, gk, e2, cw, cb, lng, lnb, goc):
    bsz, s, d = x.shape
    tm = TM_IN
    nt = s // tm
    full = lambda shape: pl.BlockSpec(shape, lambda b, j: (0,) * len(shape))
    out_shape = (
        jax.ShapeDtypeStruct((bsz, s, CONV_CH), BF16),
        jax.ShapeDtypeStruct((bsz, ATT_HEADS, LANES, s), BF16),
        jax.ShapeDtypeStruct((bsz, ATT_HEADS // 2, s, LANES), BF16),
        jax.ShapeDtypeStruct((bsz, ATT_W, s), BF16),
        jax.ShapeDtypeStruct((bsz, IDX_HEADS, LANES, s), BF16),
        jax.ShapeDtypeStruct((bsz, s, LANES), BF16),
        jax.ShapeDtypeStruct((bsz, SUBLANES, s), F32),
    )
    out_specs = (
        pl.BlockSpec((1, tm, CONV_CH), lambda b, j: (b, j, 0)),
        pl.BlockSpec((1, ATT_HEADS, LANES, tm), lambda b, j: (b, 0, 0, j)),
        pl.BlockSpec((1, ATT_HEADS // 2, tm, LANES), lambda b, j: (b, 0, j, 0)),
        pl.BlockSpec((1, ATT_W, tm), lambda b, j: (b, 0, j)),
        pl.BlockSpec((1, IDX_HEADS, LANES, tm), lambda b, j: (b, 0, 0, j)),
        pl.BlockSpec((1, tm, LANES), lambda b, j: (b, j, 0)),
        pl.BlockSpec((1, SUBLANES, tm), lambda b, j: (b, 0, j)),
    )
    return pl.pallas_call(
        _inproj_kernel,
        grid=(bsz, nt),
        in_specs=[
            pl.BlockSpec((1, tm, d), lambda b, j: (b, j, 0)),
            pl.BlockSpec((1, 6, d), lambda b, j: (b, 0, 0)),
            full(g1.shape), full(wa.shape), full(wqkv.shape), full(widx.shape), full(wwi.shape),
            full(gq.shape), full(gk.shape), full(e2.shape),
            full(cw.shape), full(cb.shape), full(lng.shape), full(lnb.shape), full(goc.shape),
        ],
        out_specs=out_specs,
        out_shape=out_shape,
        scratch_shapes=[pltpu.VMEM((tm + HALO, CONV_CH), F32)],
        compiler_params=pltpu.CompilerParams(
            dimension_semantics=("arbitrary", "arbitrary"), vmem_limit_bytes=VMEM_LIMIT),
        name="inproj",
    )(x, mod3, g1, wa, wqkv, widx, wwi, gq, gk, e2, cw, cb, lng, lnb, goc)


def _t5_bucket_np(rel):
    half = REL_BUCKETS // 2
    max_exact = half // 2
    ret = np.where(rel > 0, half, 0)
    n = np.abs(rel)
    nf = np.maximum(n, 1).astype(np.float64)
    large = max_exact + (np.log(nf / max_exact) / math.log(REL_MAX_DIST / max_exact)
                         * (half - max_exact)).astype(np.int32)
    large = np.minimum(large, half - 1)
    return (ret + np.where(n < max_exact, n, large)).astype(np.int32)


NEAR = REL_MAX_DIST + TQ
FAR_BUCKET = REL_BUCKETS // 2 - 1


def _near_buckets():
    r = np.arange(NEAR)[:, None]
    t = np.arange(TQ)[None, :]
    return _t5_bucket_np(r - REL_MAX_DIST - t)


def _attn_kernel(q_ref, k_ref, vt_ref, qi_ref, ki_ref, wit_ref, bkt_ref, rb_ref, goa_ref, o_ref,
                 keys_s, khi_s, maskb_s, relb_s, ot_s, m_s, mo_s, l_s, lg_s, *, nsel):
    b = pl.program_id(0)
    i = pl.program_id(1)
    t0 = i * TQ
    n_tiles = i + 1
    neg_inf = F32(-jnp.inf)

    @pl.when((b == 0) & (i == 0))
    def _():
        bk = bkt_ref[...]
        for h in range(ATT_HEADS):
            far = rb_ref[FAR_BUCKET, h]
            acc = jnp.zeros((NEAR, TQ), F32)
            for bb in range(REL_BUCKETS):
                acc = jnp.where(bk == bb, rb_ref[bb, h] - far, acc)
            relb_s[h, 0:2 * KT - NEAR, :] = jnp.zeros((2 * KT - NEAR, TQ), F32)
            relb_s[h, 2 * KT - NEAR:2 * KT, :] = acc

    qpos = t0 + lax.broadcasted_iota(I32, (1, TQ), 1)
    limit = (qpos // CHUNK + 1) * CHUNK
    row_iota = lax.broadcasted_iota(I32, (KT, TQ), 0)

    def tile_start(jt):
        return pl.multiple_of(jt * KT, KT)

    wi = wit_ref[0]

    def p1(jt, c):
        ks = tile_start(jt)
        kit = ki_ref[0, pl.ds(ks, KT), :]
        acc = jnp.zeros((KT, TQ), F32)
        for h in range(IDX_HEADS):
            lgt = jnp.dot(kit, qi_ref[0, h], preferred_element_type=F32)
            acc = acc + jnp.maximum(lgt, 0.0) * wi[h:h + 1, :]
        sc = jnp.where(row_iota + ks < limit, acc, neg_inf)
        bits = pltpu.bitcast(sc, I32)
        keys_s[pl.ds(ks, KT), :] = bits ^ (lax.shift_right_arithmetic(bits, 31) & 0x7FFFFFFF)
        top = jnp.where((bits & 0x7FFFFFFF) < MIN_NORMAL_BITS, 0, bits & I32(-65536))
        khi_s[pl.ds(ks, KT), :] = pltpu.bitcast(top, F32).astype(BF16)
        return c

    lax.fori_loop(0, n_tiles, p1, 0)

    def count(pred):
        def body(jt, acc):
            ks = tile_start(jt)
            m = pred(keys_s[pl.ds(ks, KT), :], ks).astype(F32)
            return acc + jnp.sum(m.reshape(KT // SUBLANES, SUBLANES, TQ), axis=0)
        acc = lax.fori_loop(0, n_tiles, body, jnp.zeros((SUBLANES, TQ), F32))
        return jnp.sum(acc, axis=0, keepdims=True)

    def count_top(cand_hi):
        cand_hi = jnp.where((cand_hi > 0) & (cand_hi < BAND), BAND, cand_hi)
        pat = cand_hi ^ (lax.shift_right_arithmetic(cand_hi, 15) & 0x7FFF)
        cb = pltpu.bitcast(lax.shift_left(pat, 16), F32).astype(BF16)
        one = jnp.ones((KT, TQ), BF16)
        zero = jnp.zeros((KT, TQ), BF16)

        def body(jt, acc):
            ks = tile_start(jt)
            hit = jnp.where(khi_s[pl.ds(ks, KT), :] >= cb, one, zero)
            parts = [hit[r:r + PACK, :] for r in range(0, KT, PACK)]
            while len(parts) > 1:
                parts = [parts[k] + parts[k + 1] for k in range(0, len(parts), 2)]
            return acc + parts[0].astype(F32)
        acc = lax.fori_loop(0, n_tiles, body, jnp.zeros((PACK, TQ), F32))
        return jnp.sum(acc, axis=0, keepdims=True)

    def select_thr():
        c0 = count_top(jnp.zeros((1, TQ), I32))
        ok = c0 >= nsel
        t = jnp.where(ok, 0, -(2 ** 15)).astype(I32)
        cnt = jnp.where(ok, c0, F32(2 ** 30))

        def top_step(it, carry):
            t, cnt = carry
            cand = t + lax.shift_left(I32(1), I32(14) - it)
            c = count_top(cand)
            ok = c >= nsel
            return jnp.where(ok, cand, t), jnp.where(ok, c, cnt)

        t, cnt = lax.fori_loop(0, 15, top_step, (t, cnt))

        band = jnp.where(jnp.abs(2 * t + 1) < 2 * BAND, 1.0, 0.0)

        def band_counts():
            c_zero = count(lambda kt, ks: kt >= 0)
            c_tiny = count(lambda kt, ks: kt >= 1)
            c_norm = count(lambda kt, ks: kt >= MIN_NORMAL_BITS)
            return c_zero, jnp.where(c_tiny == c_norm, jnp.where(c_zero >= nsel, 1.0, 0.0), 0.0)

        c_zero, plain = lax.cond(jnp.max(band) > 0.0, band_counts,
                                 lambda: (jnp.zeros((1, TQ), F32), jnp.ones((1, TQ), F32)))
        settled = band * plain > 0.0
        hard = band * (1.0 - plain) > 0.0
        t = jnp.where(settled, 0, lax.shift_left(jnp.where(hard, -BAND, t), 16))
        cnt = jnp.where(settled, c_zero, jnp.where(hard, F32(2 ** 30), cnt))
        first_bit = jnp.where(jnp.max(band * (1.0 - plain)) > 0.0, 23, 15).astype(I32)

        def low_cond(carry):
            bit, _, cnt = carry
            return (bit >= 0) & (jnp.max(jnp.where(settled, F32(nsel), cnt)) > nsel)

        def low_step(carry):
            bit, t, cnt = carry
            cand = t + lax.shift_left(I32(1), bit)
            c = count(lambda kt, ks: kt >= cand)
            ok = c >= nsel
            return bit - 1, jnp.where(ok, cand, t), jnp.where(ok, c, cnt)

        _, t, cnt = lax.while_loop(low_cond, low_step, (first_bit, t, cnt))
        return t, cnt

    thr, c_ge = lax.cond(
        i > 0, select_thr,
        lambda: (jnp.full((1, TQ), KEY_NEG_INF + 1, I32), jnp.full((1, TQ), nsel, F32)))

    def mask_plain():
        def p3(jt, c):
            ks = tile_start(jt)
            maskb_s[pl.ds(ks, KT), :] = jnp.where(keys_s[pl.ds(ks, KT), :] >= thr, 0.0, neg_inf)
            return c
        lax.fori_loop(0, n_tiles, p3, 0)
        return I32(0)

    def mask_ties():
        quota = nsel - count(lambda kt, ks: kt > thr)
        tri = jnp.where(lax.broadcasted_iota(I32, (KT, KT), 0) >= lax.broadcasted_iota(I32, (KT, KT), 1),
                        1.0, 0.0).astype(BF16)

        def p3(jt, before):
            ks = tile_start(jt)
            kt = keys_s[pl.ds(ks, KT), :]
            eq = kt == thr
            rank = before + jnp.dot(tri, jnp.where(eq, 1.0, 0.0).astype(BF16), preferred_element_type=F32)
            tied = jnp.where(eq, jnp.where(rank <= quota, 0.0, neg_inf), neg_inf)
            maskb_s[pl.ds(ks, KT), :] = jnp.where(kt > thr, 0.0, tied)
            return rank[KT - 1:KT, :]

        lax.fori_loop(0, n_tiles, p3, jnp.zeros((1, TQ), F32))
        return I32(0)

    lax.cond(jnp.max(c_ge) > nsel, mask_ties, mask_plain)

    m_s[...] = jnp.full((ATT_HEADS, TQ), neg_inf, F32)
    l_s[...] = jnp.zeros((ATT_HEADS, TQ), F32)
    ot_s[...] = jnp.zeros((ATT_W, TQ), F32)

    def att_tile(jt, near):
        ks = tile_start(jt)
        mb = maskb_s[pl.ds(ks, KT), :]
        for h in range(ATT_HEADS):
            l = jnp.dot(k_ref[0, h // 2, pl.ds(ks, KT), :], q_ref[0, h], preferred_element_type=F32) + mb
            if near:
                off = pl.multiple_of((jt - (n_tiles - 2)) * KT, KT)
                l = l + relb_s[h, pl.ds(off, KT), :]
            lg_s[h] = l
            m_old = m_s[h:h + 1, :]
            mo_s[h:h + 1, :] = m_old
            m_s[h:h + 1, :] = jnp.maximum(m_old, jnp.max(l, axis=0, keepdims=True))
        for h in range(ATT_HEADS):
            m_new = m_s[h:h + 1, :]
            m_ref = jnp.where(m_new == neg_inf, 0.0, m_new)
            alpha = jnp.exp(mo_s[h:h + 1, :] - m_ref)
            p = jnp.exp(lg_s[h] - m_ref)
            l_s[h:h + 1, :] = alpha * l_s[h:h + 1, :] + jnp.sum(p, axis=0, keepdims=True)
            vt = vt_ref[0, h * HEAD_DIM:(h + 1) * HEAD_DIM, pl.ds(ks, KT)]
            rows = slice(h * HEAD_DIM, (h + 1) * HEAD_DIM)
            ot_s[rows, :] = ot_s[rows, :] * alpha + jnp.dot(vt, p.astype(BF16), preferred_element_type=F32)

    def far_tile(jt, c):
        att_tile(jt, False)
        return c

    def near_tile(jt, c):
        att_tile(jt, True)
        return c

    n_far = jnp.maximum(n_tiles - 2, 0)
    lax.fori_loop(0, n_far, far_tile, 0)
    lax.fori_loop(n_far, n_tiles, near_tile, 0)
    for h in range(ATT_HEADS):
        rows = slice(h * HEAD_DIM, (h + 1) * HEAD_DIM)
        ot_s[rows, :] = ot_s[rows, :] / l_s[h:h + 1, :]

    ot = ot_s[...]
    ms = jnp.mean(ot * ot, axis=0, keepdims=True)
    y = (ot * lax.rsqrt(ms + EPS)).T * goa_ref[...]
    o_ref[0] = y.astype(BF16)


def _attn(q, k, vt, qi, ki, wit, rel_bias, goa):
    bsz, _, _, s = q.shape
    nsel = min(TOPK_MAX, s // 4)
    assert nsel == TQ and s % TQ == 0, "attention kernel assumes TOPK_MAX-sized query blocks"
    bkt = jnp.asarray(_near_buckets())
    kern = functools.partial(_attn_kernel, nsel=nsel)
    return pl.pallas_call(
        kern,
        grid=(bsz, s // TQ),
        in_specs=[
            pl.BlockSpec((1, ATT_HEADS, LANES, TQ), lambda b, i: (b, 0, 0, i)),
            pl.BlockSpec((1, ATT_HEADS // 2, s, LANES), lambda b, i: (b, 0, 0, 0)),
            pl.BlockSpec((1, ATT_W, s), lambda b, i: (b, 0, 0)),
            pl.BlockSpec((1, IDX_HEADS, LANES, TQ), lambda b, i: (b, 0, 0, i)),
            pl.BlockSpec((1, s, LANES), lambda b, i: (b, 0, 0)),
            pl.BlockSpec((1, SUBLANES, TQ), lambda b, i: (b, 0, i)),
            pl.BlockSpec((NEAR, TQ), lambda b, i: (0, 0)),
            pl.BlockSpec(memory_space=pltpu.SMEM),
            pl.BlockSpec((1, ATT_W), lambda b, i: (0, 0)),
        ],
        out_specs=pl.BlockSpec((1, TQ, ATT_W), lambda b, i: (b, i, 0)),
        out_shape=jax.ShapeDtypeStruct((bsz, s, ATT_W), BF16),
        scratch_shapes=[
            pltpu.VMEM((s, TQ), I32),
            pltpu.VMEM((s, TQ), BF16),
            pltpu.VMEM((s, TQ), F32),
            pltpu.VMEM((ATT_HEADS, 2 * KT, TQ), F32),
            pltpu.VMEM((ATT_W, TQ), F32),
            pltpu.VMEM((ATT_HEADS, TQ), F32),
            pltpu.VMEM((ATT_HEADS, TQ), F32),
            pltpu.VMEM((ATT_HEADS, TQ), F32),
            pltpu.VMEM((ATT_HEADS, KT, TQ), F32),
        ],
        compiler_params=pltpu.CompilerParams(
            dimension_semantics=("arbitrary", "arbitrary"), vmem_limit_bytes=VMEM_LIMIT),
        name="attn",
    )(q, k, vt, qi, ki, wit, bkt, rel_bias, goa)


def _topk_rows(s, payload, k):
    nrows = s.shape[0]
    rows = lax.broadcasted_iota(I32, s.shape, 0).astype(F32)
    vals, pays = [], []
    for _ in range(k):
        m = jnp.max(s, axis=0, keepdims=True)
        ix = jnp.min(jnp.where(s == m, rows, float(nrows)), axis=0, keepdims=True)
        hit = rows == ix
        if payload is None:
            pays.append(ix)
        else:
            pays.append(jnp.max(jnp.where(hit, payload, -1.0), axis=0, keepdims=True))
        vals.append(m)
        s = jnp.where(hit, -jnp.inf, s)
    return jnp.concatenate(vals, axis=0), jnp.concatenate(pays, axis=0)


_PAIR_ROWS = tuple((a, PEER_TOPK // (a + 1)) for a in range(PEER_TOPK // 2))


def _mid_kernel(x_ref, cn_ref, an_ref, mod_ref, wo1_ref, wo2_ref, g2_ref, wpq_ref, k1_ref, k2_ref,
                x1_ref, h2_ref, idx_ref, gate_ref, qq_s, idt_s, gt_s):
    tm = x_ref.shape[1]
    x = x_ref[0]
    gt1 = mod_ref[0, 2:3, :]
    sh2 = mod_ref[0, 3:4, :]
    sc2 = mod_ref[0, 4:5, :]
    proj = (jnp.dot(cn_ref[0], wo1_ref[...], preferred_element_type=F32)
            + jnp.dot(an_ref[0], wo2_ref[...], preferred_element_type=F32))
    x1 = x + gt1 * proj
    x1_ref[0] = x1
    r = lax.rsqrt(jnp.mean(x1 * x1, axis=-1, keepdims=True) + EPS)
    h2 = (x1 * r) * g2_ref[...] * (1.0 + sc2) + sh2
    h2_ref[0] = h2
    qq_s[...] = jnp.dot(h2.astype(BF16), wpq_ref[...], preferred_element_type=F32).astype(BF16)

    def route_unit(hh, lt):
        rows = pl.ds(lt * LANES, LANES)
        q1 = qq_s[rows, pl.ds(pl.multiple_of(hh * 2 * N_KEYS, LANES), N_KEYS)]
        q2 = qq_s[rows, pl.ds(pl.multiple_of(hh * 2 * N_KEYS + N_KEYS, LANES), N_KEYS)]
        v1, i1 = _topk_rows(_nt(k1_ref[hh], q1), None, PEER_TOPK)
        v2, i2 = _topk_rows(_nt(k2_ref[hh], q2), None, PEER_TOPK)
        sub = lax.broadcasted_iota(I32, (SUBLANES, LANES), 0)
        cands, cidxs = [], []
        for a, nb in _PAIR_ROWS:
            nrows = max(nb, SUBLANES)
            val = v1[a:a + 1, :] + v2[0:nrows, :]
            if nb < SUBLANES:
                val = jnp.where(sub < nb, val, -jnp.inf)
            cands.append(val)
            cidxs.append(i1[a:a + 1, :] * float(N_KEYS) + i2[0:nrows, :])
        half = PEER_TOPK // 2
        cands.append(v1[half:, :] + v2[0:1, :])
        cidxs.append(i1[half:, :] * float(N_KEYS) + i2[0:1, :])
        best, experts = _topk_rows(jnp.concatenate(cands, axis=0), jnp.concatenate(cidxs, axis=0), PEER_TOPK)
        e = jnp.exp(best - best[0:1, :])
        g = e / jnp.sum(e, axis=0, keepdims=True)
        slots = pl.ds(pl.multiple_of(hh * PEER_TOPK, PEER_TOPK), PEER_TOPK)
        cols = pl.ds(lt * LANES, LANES)
        idt_s[slots, cols] = experts * float(ROWS_PER_EXPERT)
        gt_s[slots, cols] = g

    def route(hh, c):
        for lt in range(tm // LANES):
            route_unit(hh, lt)
        return c

    lax.fori_loop(0, PEER_HEADS, route, 0)
    idx_ref[0] = idt_s[...].T.astype(I32)
    gate_ref[0] = gt_s[...].T


def _mid(x, cn, an, mod3, wo1, wo2, g2, wpq, k1, k2):
    bsz, s, d = x.shape
    tm = TM_MID
    full = lambda shape: pl.BlockSpec(shape, lambda b, j: (0,) * len(shape))
    tok = lambda w: pl.BlockSpec((1, tm, w), lambda b, j: (b, j, 0))
    return pl.pallas_call(
        _mid_kernel,
        grid=(bsz, s // tm),
        in_specs=[tok(d), tok(CONV_CH), tok(ATT_W), pl.BlockSpec((1, 6, d), lambda b, j: (b, 0, 0)),
                  full(wo1.shape), full(wo2.shape), full(g2.shape), full(wpq.shape), full(k1.shape), full(k2.shape)],
        out_specs=(tok(d), tok(d), tok(PEER_SLOTS), tok(PEER_SLOTS)),
        out_shape=(
            jax.ShapeDtypeStruct((bsz, s, d), F32),
            jax.ShapeDtypeStruct((bsz, s, d), F32),
            jax.ShapeDtypeStruct((bsz, s, PEER_SLOTS), I32),
            jax.ShapeDtypeStruct((bsz, s, PEER_SLOTS), F32),
        ),
        scratch_shapes=[
            pltpu.VMEM((tm, PEER_HEADS * 2 * N_KEYS), BF16),
            pltpu.VMEM((PEER_SLOTS, tm), F32),
            pltpu.VMEM((PEER_SLOTS, tm), F32),
        ],
        compiler_params=pltpu.CompilerParams(
            dimension_semantics=("arbitrary", "arbitrary"), vmem_limit_bytes=VMEM_LIMIT),
        name="mid",
    )(x, cn, an, mod3, wo1, wo2, g2, wpq, k1, k2)


def _pack_table(t):
    tb = t.astype(BF16)
    lo = lax.bitcast_convert_type(tb[:, :HALF], jnp.uint16).astype(jnp.uint32)
    hi = lax.bitcast_convert_type(tb[:, HALF:], jnp.uint16).astype(jnp.uint32)
    return (lo | (hi << 16)).reshape(t.shape[0] * ROWS_PER_EXPERT, LANES)


def _unpack(w):
    lo = pltpu.bitcast(lax.shift_left(w, jnp.uint32(16)), F32)
    hi = pltpu.bitcast(w & jnp.uint32(0xFFFF0000), F32)
    return lo, hi


def _gather_row(tab_ref, row):
    return tab_ref[pl.ds(pl.multiple_of(row, ROWS_PER_EXPERT), ROWS_PER_EXPERT), :]


STAGE_ROWS = PEER_SLOTS * ROWS_PER_EXPERT


def _token_pieces(xg, r, first, count):
    return jnp.concatenate(
        [xg[r:r + 1, (first + q) * LANES:(first + q + 1) * LANES] for q in range(count)], axis=0)


def _peer_u_kernel(idx_ref, tab_ref, x_ref, gate_ref, w_ref, *stages):
    tb = w_ref.shape[0]
    sub = lax.broadcasted_iota(I32, (SUBLANES, PEER_SLOTS), 0)

    def group(g, c):
        for k in range(PEER_GROUPS):
            eight(pl.multiple_of((g * PEER_GROUPS + k) * SUBLANES, SUBLANES), stages[k * SUBLANES:(k + 1) * SUBLANES])
        return c

    def eight(t0, stages):
        act8 = jnp.zeros((SUBLANES, PEER_SLOTS), F32)
        xg = x_ref[pl.ds(t0, SUBLANES), :]
        xlo, xhi, rows_t = [], [], []
        for r in range(SUBLANES):
            xlo.append(jnp.concatenate([_token_pieces(xg, r, 0, ROWS_PER_EXPERT)] * 2, axis=0))
            xhi.append(jnp.concatenate([_token_pieces(xg, r, ROWS_PER_EXPERT, ROWS_PER_EXPERT)] * 2, axis=0))
            rows_t.append(idx_ref.at[t0 + r])
        for j in range(0, PEER_SLOTS, 2):
            for r in range(SUBLANES):
                pair = jnp.concatenate(
                    [_gather_row(tab_ref, rows_t[r][j]), _gather_row(tab_ref, rows_t[r][j + 1])], axis=0)
                lo, hi = _unpack(pair)
                row = j * ROWS_PER_EXPERT
                stages[r][row:row + 2 * ROWS_PER_EXPERT, :] = lo * xlo[r] + hi * xhi[r]
        for r in range(SUBLANES):
            stage = stages[r]
            cs = stage[pl.ds(0, PEER_SLOTS, stride=ROWS_PER_EXPERT), :]
            for q in range(1, ROWS_PER_EXPERT):
                cs = cs + stage[pl.ds(q, PEER_SLOTS, stride=ROWS_PER_EXPERT), :]
            act = jnp.sum(cs.T, axis=0, keepdims=True)
            act8 = jnp.where(sub == r, act, act8)
        rows = pl.ds(t0, SUBLANES)
        w_ref[rows, :] = gate_ref[rows, :] * jax.nn.gelu(act8)

    lax.fori_loop(0, tb // (PEER_GROUPS * SUBLANES), group, 0)


def _peer_u(idx, tab, h2, gate):
    n, d = h2.shape
    tb = TB_PEER
    return pl.pallas_call(
        _peer_u_kernel,
        grid=(n // tb,),
        in_specs=[
            pl.BlockSpec((tb, PEER_SLOTS), lambda i: (i, 0), memory_space=pltpu.SMEM),
            pl.BlockSpec(tab.shape, lambda i: (0, 0), pipeline_mode=pl.Buffered(1)),
            pl.BlockSpec((tb, d), lambda i: (i, 0)),
            pl.BlockSpec((tb, PEER_SLOTS), lambda i: (i, 0)),
        ],
        out_specs=pl.BlockSpec((tb, PEER_SLOTS), lambda i: (i, 0)),
        out_shape=jax.ShapeDtypeStruct((n, PEER_SLOTS), F32),
        scratch_shapes=[pltpu.VMEM((STAGE_ROWS, LANES), F32) for _ in range(PEER_GROUPS * SUBLANES)],
        compiler_params=pltpu.CompilerParams(
            dimension_semantics=("arbitrary",), vmem_limit_bytes=VMEM_LIMIT),
        name="peer_u",
    )(idx, tab, h2, gate)


PV_PAIRS = SUBLANES // 2


def _peer_v_kernel(idx_ref, w_ref, tab_ref, x1_ref, gt2_ref, e8_ref, mask_ref, o_ref, *stages):
    tb = idx_ref.shape[0]
    gt2 = gt2_ref[0]
    mask = mask_ref[...]
    e8 = e8_ref[...]

    def group(g, c):
        for k in range(PEER_GROUPS):
            eight(pl.multiple_of((g * PEER_GROUPS + k) * SUBLANES, SUBLANES), stages[k * PV_PAIRS:(k + 1) * PV_PAIRS])
        return c

    def eight(g0, stages):
        peers = []
        for p in range(PV_PAIRS):
            stage = stages[p]
            t0 = g0 + 2 * p
            for u in range(2):
                rows_t = idx_ref.at[t0 + u]
                for j in range(PEER_SLOTS):
                    stage[j * ROWS_PER_EXPERT:(j + 1) * ROWS_PER_EXPERT, u * LANES:(u + 1) * LANES] = (
                        _gather_row(tab_ref, rows_t[j]))
            w2 = w_ref[pl.ds(t0, 2), :]
            hi = w2.astype(BF16).astype(F32)
            rep = jnp.dot(jnp.concatenate([hi, w2 - hi], axis=0).astype(BF16), e8,
                          preferred_element_type=F32)
            lhs = jnp.concatenate([rep[i:i + 1, :] * mask for i in range(4)], axis=0).astype(BF16)
            out = jnp.dot(lhs, pltpu.bitcast(stage[...], BF16), preferred_element_type=F32)
            for u in range(2):
                r0 = u * SUBLANES
                peers.append(out[r0:r0 + SUBLANES, u * LANES:(u + 1) * LANES]
                             + out[2 * SUBLANES + r0:3 * SUBLANES + r0, u * LANES:(u + 1) * LANES])
        rows = pl.ds(g0, SUBLANES)
        for q in range(SUBLANES):
            cols = slice(q * LANES, (q + 1) * LANES)
            piece = jnp.concatenate([peer[q:q + 1, :] for peer in peers], axis=0)
            o_ref[rows, cols] = x1_ref[rows, cols] + gt2[:, cols] * piece

    lax.fori_loop(0, tb // (PEER_GROUPS * SUBLANES), group, 0)


def _peer_v(idx, w, tab, x1, gt2, blocks_per_batch):
    n, d = x1.shape
    tb = TB_PEER
    q = np.arange(SUBLANES)
    piece = 2 * (q % ROWS_PER_EXPERT) + q // ROWS_PER_EXPERT
    lane = np.arange(PEER_SLOTS * SUBLANES)
    mask = jnp.asarray((lane[None, :] % SUBLANES == piece[:, None]).astype(np.float32))
    e8 = jnp.asarray(np.arange(PEER_SLOTS)[:, None] == lane[None, :] // SUBLANES, BF16)
    return pl.pallas_call(
        _peer_v_kernel,
        grid=(n // tb,),
        in_specs=[
            pl.BlockSpec((tb, PEER_SLOTS), lambda i: (i, 0), memory_space=pltpu.SMEM),
            pl.BlockSpec((tb, PEER_SLOTS), lambda i: (i, 0)),
            pl.BlockSpec(tab.shape, lambda i: (0, 0), pipeline_mode=pl.Buffered(1)),
            pl.BlockSpec((tb, d), lambda i: (i, 0)),
            pl.BlockSpec((1, 1, d), lambda i: (i // blocks_per_batch, 0, 0)),
            pl.BlockSpec(e8.shape, lambda i: (0, 0)),
            pl.BlockSpec(mask.shape, lambda i: (0, 0)),
        ],
        out_specs=pl.BlockSpec((tb, d), lambda i: (i, 0)),
        out_shape=jax.ShapeDtypeStruct((n, d), F32),
        scratch_shapes=[pltpu.VMEM((STAGE_ROWS, 2 * LANES), jnp.uint32) for _ in range(PEER_GROUPS * PV_PAIRS)],
        compiler_params=pltpu.CompilerParams(
            dimension_semantics=("arbitrary",), vmem_limit_bytes=VMEM_LIMIT),
        name="peer_v",
    )(idx, w, tab, x1, gt2, e8, mask)


def _layer(x, mod, g_norm1, g_norm2, w_in, q_norm_g, k_norm_g, conv_w, conv_b, conv_ln_g, conv_ln_b,
           rel_bias, g_out_conv, g_out_attn, w_out, w_peer_q, peer_k1, peer_k2, peer_u, peer_v):
    bsz, s, d = x.shape
    n = bsz * s
    mod3 = mod.reshape(bsz, 6, d)
    row = lambda a: a.reshape(1, -1)

    c0 = 2 * CONV_CH
    c1 = c0 + 3 * ATT_W
    c2 = c1 + IDX_HEADS * IDX_DIM
    c3 = c2 + IDX_DIM
    wa = w_in[:, :c0].astype(BF16)
    wqkv = w_in[:, c0:c1].astype(BF16)
    widx = jnp.concatenate([w_in[:, c1:c2], w_in[:, c2:c3], w_in[:, c2:c3]], axis=1).astype(BF16)
    wwi = jnp.zeros((SUBLANES, d), F32).at[:IDX_HEADS].set(w_in[:, c3:c3 + IDX_HEADS].T).astype(BF16)
    head = np.arange(ATT_W) // HEAD_DIM
    e2 = jnp.asarray((head[:, None] == head[None, :]).astype(np.float32))

    conv_n, q, k, vt, qi, ki, wit = _inproj(
        x, mod3, row(g_norm1), wa, wqkv, widx, wwi,
        row(jnp.tile(q_norm_g, ATT_HEADS)), row(jnp.tile(k_norm_g, ATT_HEADS)), e2,
        conv_w.reshape(CONV_WIDTH, CONV_CH), row(conv_b), row(conv_ln_g), row(conv_ln_b), row(g_out_conv))
    attn_n = _attn(q, k, vt, qi, ki, wit, rel_bias, row(g_out_attn))

    x1, h2, idx, gate = _mid(
        x, conv_n, attn_n, mod3, w_out[:CONV_CH].astype(BF16), w_out[CONV_CH:].astype(BF16), row(g_norm2),
        w_peer_q.astype(BF16), peer_k1.astype(BF16), peer_k2.astype(BF16))

    idx = idx.reshape(n, PEER_SLOTS)
    w = _peer_u(idx, _pack_table(peer_u), h2.reshape(n, d), gate.reshape(n, PEER_SLOTS))
    out = _peer_v(idx, w, _pack_table(peer_v), x1.reshape(n, d), mod3[:, 5:6, :], s // TB_PEER)
    return out.reshape(bsz, s, d)


def kernel(x, c, w_ada, b_ada, g_norm1, g_norm2, w_in, q_norm_g, k_norm_g, conv_w, conv_b, conv_ln_g,
           conv_ln_b, rel_bias, g_out_conv, g_out_attn, w_out, w_peer_q, peer_k1, peer_k2, peer_u, peer_v):
    depth = w_ada.shape[0]
    for l in range(depth):
        mod = _ada(c, w_ada[l], b_ada[l])
        x = _layer(x, mod, g_norm1[l], g_norm2[l], w_in[l], q_norm_g[l], k_norm_g[l], conv_w[l], conv_b[l],
                   conv_ln_g[l], conv_ln_b[l], rel_bias, g_out_conv[l], g_out_attn[l], w_out[l],
                   w_peer_q[l], peer_k1[l], peer_k2[l], peer_u[l], peer_v[l])
    return x
```

```python
import functools
import math

import numpy as np
import jax
import jax.numpy as jnp
from jax import lax
from jax.experimental import pallas as pl
from jax.experimental.pallas import tpu as pltpu

F32 = jnp.float32
BF16 = jnp.bfloat16
I32 = jnp.int32
HIGHEST = lax.Precision.HIGHEST

D_MODEL = 1024
CHUNK = 64
CONV_CH = 512
CONV_WIDTH = 31
ATT_HEADS = 8
HEAD_DIM = 64
ATT_W = ATT_HEADS * HEAD_DIM
IDX_HEADS = 4
IDX_DIM = 64
IDX_SCALE = (IDX_HEADS * IDX_DIM) ** -0.5
TOPK_MAX = 256
REL_BUCKETS = 32
REL_MAX_DIST = 128
PEER_HEADS = 8
N_KEYS = 128
N_EXPERTS = N_KEYS * N_KEYS
PEER_TOPK = 16
PEER_SLOTS = PEER_HEADS * PEER_TOPK
EPS = 1e-6

LANES = 128
SUBLANES = 8
PACK = 16
VMEM_LIMIT = 56 * 1024 * 1024

TM_IN = 512
CONV_ROWS = 64
HALO = 32
TQ = 256
KT = 256
TM_MID = 512
TB_PEER = 128
PEER_GROUPS = 2
HALF = D_MODEL // 2
ROWS_PER_EXPERT = HALF // LANES

NT_DIMS = (((1,), (1,)), ((), ()))

_NEG_INF_BITS = int(np.array(-np.inf, np.float32).view(np.int32))
KEY_NEG_INF = _NEG_INF_BITS ^ 0x7FFFFFFF
MIN_NORMAL_BITS = 0x00800000
BAND = MIN_NORMAL_BITS >> 16
INT_MIN = -(2 ** 31)


def _nt(a, b, precision=None):
    return lax.dot_general(a, b, NT_DIMS, precision=precision, preferred_element_type=F32)


def _ada_kernel(c_ref, w_ref, b_ref, o_ref):
    a = jax.nn.silu(c_ref[...])
    o_ref[...] = jnp.dot(a, w_ref[...], precision=HIGHEST, preferred_element_type=F32) + b_ref[...]


def _ada(c, w_ada, b_ada):
    bsz, d = c.shape
    return pl.pallas_call(
        _ada_kernel,
        grid=(6,),
        in_specs=[
            pl.BlockSpec((bsz, d), lambda j: (0, 0)),
            pl.BlockSpec((d, d), lambda j: (0, j)),
            pl.BlockSpec((1, d), lambda j: (0, j)),
        ],
        out_specs=pl.BlockSpec((bsz, d), lambda j: (0, j)),
        out_shape=jax.ShapeDtypeStruct((bsz, 6 * d), F32),
        name="ada",
    )(c, w_ada, b_ada.reshape(1, 6 * d))


def _inproj_kernel(x_ref, mod_ref, g1_ref, wa_ref, wqkv_ref, widx_ref, wwi_ref, gq_ref, gk_ref, e2_ref,
                   cw_ref, cb_ref, lng_ref, lnb_ref, goc_ref,
                   conv_ref, q_ref, k_ref, vt_ref, qi_ref, ki_ref, wit_ref, ubuf):
    j = pl.program_id(1)
    tm = x_ref.shape[1]
    x = x_ref[0]
    sh1 = mod_ref[0, 0:1, :]
    sc1 = mod_ref[0, 1:2, :]
    r = lax.rsqrt(jnp.mean(x * x, axis=-1, keepdims=True) + EPS)
    h = (x * r) * g1_ref[...] * (1.0 + sc1) + sh1
    hb = h.astype(BF16)

    pa = jnp.dot(hb, wa_ref[...], preferred_element_type=F32)
    u = pa[:, :CONV_CH] * jax.nn.sigmoid(pa[:, CONV_CH:])

    @pl.when(j == 0)
    def _():
        ubuf[0:HALO, :] = jnp.zeros((HALO, CONV_CH), F32)

    ubuf[HALO:HALO + tm, :] = u
    first = HALO - (CONV_WIDTH - 1)
    for rb in range(tm // CONV_ROWS):
        base = rb * CONV_ROWS
        acc = jnp.zeros((CONV_ROWS, CONV_CH), F32) + cb_ref[...]
        for phase in range(SUBLANES):
            taps = [t for t in range(CONV_WIDTH) if (first + t) % SUBLANES == phase]
            span = (first + taps[-1]) - phase + CONV_ROWS
            slab = ubuf[base + phase:base + phase + span, :]
            for t in taps:
                off = first + t - phase
                acc = acc + cw_ref[t:t + 1, :] * slab[off:off + CONV_ROWS, :]
        mu = jnp.mean(acc, axis=-1, keepdims=True)
        xc = acc - mu
        y = xc * lax.rsqrt(jnp.mean(xc * xc, axis=-1, keepdims=True) + EPS)
        y = jax.nn.silu(y * lng_ref[...] + lnb_ref[...])
        y = y * lax.rsqrt(jnp.mean(y * y, axis=-1, keepdims=True) + EPS) * goc_ref[...]
        conv_ref[0, base:base + CONV_ROWS, :] = y.astype(BF16)
    ubuf[0:HALO, :] = ubuf[tm:tm + HALO, :]

    pq = jnp.dot(hb, wqkv_ref[...], preferred_element_type=F32)
    q = pq[:, :ATT_W]
    k = pq[:, ATT_W:2 * ATT_W]
    v = pq[:, 2 * ATT_W:]
    e2 = e2_ref[...]

    def head_mean(sq):
        hi = sq.astype(BF16)
        lo = (sq - hi.astype(F32)).astype(BF16)
        return (jnp.dot(hi, e2, preferred_element_type=F32)
                + jnp.dot(lo, e2, preferred_element_type=F32)) * (1.0 / HEAD_DIM)

    qs = head_mean(q * q)
    ks = head_mean(k * k)
    qn = q * lax.rsqrt(qs + EPS) * gq_ref[...] * (HEAD_DIM ** -0.5)
    kn = k * lax.rsqrt(ks + EPS) * gk_ref[...]
    lane = lax.broadcasted_iota(I32, (tm, LANES), 1)
    low = lane < HEAD_DIM
    for p in range(ATT_HEADS // 2):
        slab = qn[:, p * LANES:(p + 1) * LANES]
        q_ref[0, 2 * p] = jnp.where(low, slab, 0.0).T.astype(BF16)
        q_ref[0, 2 * p + 1] = jnp.where(low, 0.0, slab).T.astype(BF16)
        k_ref[0, p] = kn[:, p * LANES:(p + 1) * LANES].astype(BF16)
    vt_ref[0] = v.T.astype(BF16)

    pc = jnp.dot(hb, widx_ref[...], preferred_element_type=F32)
    for p in range(IDX_HEADS // 2):
        slab = pc[:, p * LANES:(p + 1) * LANES]
        qi_ref[0, 2 * p] = jnp.where(low, slab, 0.0).T.astype(BF16)
        qi_ref[0, 2 * p + 1] = jnp.where(low, 0.0, slab).T.astype(BF16)
    ki_ref[0] = pc[:, 2 * LANES:3 * LANES].astype(BF16)
    wit_ref[0] = _nt(wwi_ref[...], hb) * IDX_SCALE


def _inproj(x, mod3, g1, wa, wqkv, widx, wwi, gq, gk, e2, cw, cb, lng, lnb, goc):
    bsz, s, d = x.shape
    tm = TM_IN
    nt = s // tm
    full = lambda shape: pl.BlockSpec(shape, lambda b, j: (0,) * len(shape))
    out_shape = (
        jax.ShapeDtypeStruct((bsz, s, CONV_CH), BF16),
        jax.ShapeDtypeStruct((bsz, ATT_HEADS, LANES, s), BF16),
        jax.ShapeDtypeStruct((bsz, ATT_HEADS // 2, s, LANES), BF16),
        jax.ShapeDtypeStruct((bsz, ATT_W, s), BF16),
        jax.ShapeDtypeStruct((bsz, IDX_HEADS, LANES, s), BF16),
        jax.ShapeDtypeStruct((bsz, s, LANES), BF16),
        jax.ShapeDtypeStruct((bsz, SUBLANES, s), F32),
    )
    out_specs = (
        pl.BlockSpec((1, tm, CONV_CH), lambda b, j: (b, j, 0)),
        pl.BlockSpec((1, ATT_HEADS, LANES, tm), lambda b, j: (b, 0, 0, j)),
        pl.BlockSpec((1, ATT_HEADS // 2, tm, LANES), lambda b, j: (b, 0, j, 0)),
        pl.BlockSpec((1, ATT_W, tm), lambda b, j: (b, 0, j)),
        pl.BlockSpec((1, IDX_HEADS, LANES, tm), lambda b, j: (b, 0, 0, j)),
        pl.BlockSpec((1, tm, LANES), lambda b, j: (b, j, 0)),
        pl.BlockSpec((1, SUBLANES, tm), lambda b, j: (b, 0, j)),
    )
    return pl.pallas_call(
        _inproj_kernel,
        grid=(bsz, nt),
        in_specs=[
            pl.BlockSpec((1, tm, d), lambda b, j: (b, j, 0)),
            pl.BlockSpec((1, 6, d), lambda b, j: (b, 0, 0)),
            full(g1.shape), full(wa.shape), full(wqkv.shape), full(widx.shape), full(wwi.shape),
            full(gq.shape), full(gk.shape), full(e2.shape),
            full(cw.shape), full(cb.shape), full(lng.shape), full(lnb.shape), full(goc.shape),
        ],
        out_specs=out_specs,
        out_shape=out_shape,
        scratch_shapes=[pltpu.VMEM((tm + HALO, CONV_CH), F32)],
        compiler_params=pltpu.CompilerParams(
            dimension_semantics=("arbitrary", "arbitrary"), vmem_limit_bytes=VMEM_LIMIT),
        name="inproj",
    )(x, mod3, g1, wa, wqkv, widx, wwi, gq, gk, e2, cw, cb, lng, lnb, goc)


def _t5_bucket_np(rel):
    half = REL_BUCKETS // 2
    max_exact = half // 2
    ret = np.where(rel > 0, half, 0)
    n = np.abs(rel)
    nf = np.maximum(n, 1).astype(np.float64)
    large = max_exact + (np.log(nf / max_exact) / math.log(REL_MAX_DIST / max_exact)
                         * (half - max_exact)).astype(np.int32)
    large = np.minimum(large, half - 1)
    return (ret + np.where(n < max_exact, n, large)).astype(np.int32)


NEAR = REL_MAX_DIST + TQ
FAR_BUCKET = REL_BUCKETS // 2 - 1


def _near_buckets():
    r = np.arange(NEAR)[:, None]
    t = np.arange(TQ)[None, :]
    return _t5_bucket_np(r - REL_MAX_DIST - t)


def _attn_kernel(q_ref, k_ref, vt_ref, qi_ref, ki_ref, wit_ref, bkt_ref, rb_ref, goa_ref, o_ref,
                 keys_s, khi_s, maskb_s, relb_s, ot_s, m_s, mo_s, l_s, lg_s, *, nsel):
    b = pl.program_id(0)
    i = pl.program_id(1)
    t0 = i * TQ
    n_tiles = i + 1
    neg_inf = F32(-jnp.inf)

    @pl.when((b == 0) & (i == 0))
    def _():
        bk = bkt_ref[...]
        for h in range(ATT_HEADS):
            far = rb_ref[FAR_BUCKET, h]
            acc = jnp.zeros((NEAR, TQ), F32)
            for bb in range(REL_BUCKETS):
                acc = jnp.where(bk == bb, rb_ref[bb, h] - far, acc)
            relb_s[h, 0:2 * KT - NEAR, :] = jnp.zeros((2 * KT - NEAR, TQ), F32)
            relb_s[h, 2 * KT - NEAR:2 * KT, :] = acc

    qpos = t0 + lax.broadcasted_iota(I32, (1, TQ), 1)
    limit = (qpos // CHUNK + 1) * CHUNK
    row_iota = lax.broadcasted_iota(I32, (KT, TQ), 0)

    def tile_start(jt):
        return pl.multiple_of(jt * KT, KT)

    wi = wit_ref[0]

    def p1(jt, c):
        ks = tile_start(jt)
        kit = ki_ref[0, pl.ds(ks, KT), :]
        acc = jnp.zeros((KT, TQ), F32)
        for h in range(IDX_HEADS):
            lgt = jnp.dot(kit, qi_ref[0, h], preferred_element_type=F32)
            acc = acc + jnp.maximum(lgt, 0.0) * wi[h:h + 1, :]
        sc = jnp.where(row_iota + ks < limit, acc, neg_inf)
        bits = pltpu.bitcast(sc, I32)
        keys_s[pl.ds(ks, KT), :] = bits ^ (lax.shift_right_arithmetic(bits, 31) & 0x7FFFFFFF)
        top = jnp.where((bits & 0x7FFFFFFF) < MIN_NORMAL_BITS, 0, bits & I32(-65536))
        khi_s[pl.ds(ks, KT), :] = pltpu.bitcast(top, F32).astype(BF16)
        return c

    lax.fori_loop(0, n_tiles, p1, 0)

    def count(pred):
        def body(jt, acc):
            ks = tile_start(jt)
            m = pred(keys_s[pl.ds(ks, KT), :], ks).astype(F32)
            return acc + jnp.sum(m.reshape(KT // SUBLANES, SUBLANES, TQ), axis=0)
        acc = lax.fori_loop(0, n_tiles, body, jnp.zeros((SUBLANES, TQ), F32))
        return jnp.sum(acc, axis=0, keepdims=True)

    def count_top(cand_hi):
        cand_hi = jnp.where((cand_hi > 0) & (cand_hi < BAND), BAND, cand_hi)
        pat = cand_hi ^ (lax.shift_right_arithmetic(cand_hi, 15) & 0x7FFF)
        cb = pltpu.bitcast(lax.shift_left(pat, 16), F32).astype(BF16)
        one = jnp.ones((KT, TQ), BF16)
        zero = jnp.zeros((KT, TQ), BF16)

        def body(jt, acc):
            ks = tile_start(jt)
            hit = jnp.where(khi_s[pl.ds(ks, KT), :] >= cb, one, zero)
            parts = [hit[r:r + PACK, :] for r in range(0, KT, PACK)]
            while len(parts) > 1:
                parts = [parts[k] + parts[k + 1] for k in range(0, len(parts), 2)]
            return acc + parts[0].astype(F32)
        acc = lax.fori_loop(0, n_tiles, body, jnp.zeros((PACK, TQ), F32))
        return jnp.sum(acc, axis=0, keepdims=True)

    def select_thr():
        c0 = count_top(jnp.zeros((1, TQ), I32))
        ok = c0 >= nsel
        t = jnp.where(ok, 0, -(2 ** 15)).astype(I32)
        cnt = jnp.where(ok, c0, F32(2 ** 30))

        def top_step(it, carry):
            t, cnt = carry
            cand = t + lax.shift_left(I32(1), I32(14) - it)
            c = count_top(cand)
            ok = c >= nsel
            return jnp.where(ok, cand, t), jnp.where(ok, c, cnt)

        t, cnt = lax.fori_loop(0, 15, top_step, (t, cnt))

        band = jnp.where(jnp.abs(2 * t + 1) < 2 * BAND, 1.0, 0.0)

        def band_counts():
            c_zero = count(lambda kt, ks: kt >= 0)
            c_tiny = count(lambda kt, ks: kt >= 1)
            c_norm = count(lambda kt, ks: kt >= MIN_NORMAL_BITS)
            return c_zero, jnp.where(c_tiny == c_norm, jnp.where(c_zero >= nsel, 1.0, 0.0), 0.0)

        c_zero, plain = lax.cond(jnp.max(band) > 0.0, band_counts,
                                 lambda: (jnp.zeros((1, TQ), F32), jnp.ones((1, TQ), F32)))
        settled = band * plain > 0.0
        hard = band * (1.0 - plain) > 0.0
        t = jnp.where(settled, 0, lax.shift_left(jnp.where(hard, -BAND, t), 16))
        cnt = jnp.where(settled, c_zero, jnp.where(hard, F32(2 ** 30), cnt))
        first_bit = jnp.where(jnp.max(band * (1.0 - plain)) > 0.0, 23, 15).astype(I32)

        def low_cond(carry):
            bit, _, cnt = carry
            return (bit >= 0) & (jnp.max(jnp.where(settled, F32(nsel), cnt)) > nsel)

        def low_step(carry):
            bit, t, cnt = carry
            cand = t + lax.shift_left(I32(1), bit)
            c = count(lambda kt, ks: kt >= cand)
            ok = c >= nsel
            return bit - 1, jnp.where(ok, cand, t), jnp.where(ok, c, cnt)

        _, t, cnt = lax.while_loop(low_cond, low_step, (first_bit, t, cnt))
        return t, cnt

    thr, c_ge = lax.cond(
        i > 0, select_thr,
        lambda: (jnp.full((1, TQ), KEY_NEG_INF + 1, I32), jnp.full((1, TQ), nsel, F32)))

    def mask_plain():
        def p3(jt, c):
            ks = tile_start(jt)
            maskb_s[pl.ds(ks, KT), :] = jnp.where(keys_s[pl.ds(ks, KT), :] >= thr, 0.0, neg_inf)
            return c
        lax.fori_loop(0, n_tiles, p3, 0)
        return I32(0)

    def mask_ties():
        quota = nsel - count(lambda kt, ks: kt > thr)
        tri = jnp.where(lax.broadcasted_iota(I32, (KT, KT), 0) >= lax.broadcasted_iota(I32, (KT, KT), 1),
                        1.0, 0.0).astype(BF16)

        def p3(jt, before):
            ks = tile_start(jt)
            kt = keys_s[pl.ds(ks, KT), :]
            eq = kt == thr
            rank = before + jnp.dot(tri, jnp.where(eq, 1.0, 0.0).astype(BF16), preferred_element_type=F32)
            tied = jnp.where(eq, jnp.where(rank <= quota, 0.0, neg_inf), neg_inf)
            maskb_s[pl.ds(ks, KT), :] = jnp.where(kt > thr, 0.0, tied)
            return rank[KT - 1:KT, :]

        lax.fori_loop(0, n_tiles, p3, jnp.zeros((1, TQ), F32))
        return I32(0)

    lax.cond(jnp.max(c_ge) > nsel, mask_ties, mask_plain)

    m_s[...] = jnp.full((ATT_HEADS, TQ), neg_inf, F32)
    l_s[...] = jnp.zeros((ATT_HEADS, TQ), F32)
    ot_s[...] = jnp.zeros((ATT_W, TQ), F32)

    def att_tile(jt, near):
        ks = tile_start(jt)
        mb = maskb_s[pl.ds(ks, KT), :]
        for h in range(ATT_HEADS):
            l = jnp.dot(k_ref[0, h // 2, pl.ds(ks, KT), :], q_ref[0, h], preferred_element_type=F32) + mb
            if near:
                off = pl.multiple_of((jt - (n_tiles - 2)) * KT, KT)
                l = l + relb_s[h, pl.ds(off, KT), :]
            lg_s[h] = l
            m_old = m_s[h:h + 1, :]
            mo_s[h:h + 1, :] = m_old
            m_s[h:h + 1, :] = jnp.maximum(m_old, jnp.max(l, axis=0, keepdims=True))
        for h in range(ATT_HEADS):
            m_new = m_s[h:h + 1, :]
            m_ref = jnp.where(m_new == neg_inf, 0.0, m_new)
            alpha = jnp.exp(mo_s[h:h + 1, :] - m_ref)
            p = jnp.exp(lg_s[h] - m_ref)
            l_s[h:h + 1, :] = alpha * l_s[h:h + 1, :] + jnp.sum(p, axis=0, keepdims=True)
            vt = vt_ref[0, h * HEAD_DIM:(h + 1) * HEAD_DIM, pl.ds(ks, KT)]
            rows = slice(h * HEAD_DIM, (h + 1) * HEAD_DIM)
            ot_s[rows, :] = ot_s[rows, :] * alpha + jnp.dot(vt, p.astype(BF16), preferred_element_type=F32)

    def far_tile(jt, c):
        att_tile(jt, False)
        return c

    def near_tile(jt, c):
        att_tile(jt, True)
        return c

    n_far = jnp.maximum(n_tiles - 2, 0)
    lax.fori_loop(0, n_far, far_tile, 0)
    lax.fori_loop(n_far, n_tiles, near_tile, 0)
    for h in range(ATT_HEADS):
        rows = slice(h * HEAD_DIM, (h + 1) * HEAD_DIM)
        ot_s[rows, :] = ot_s[rows, :] / l_s[h:h + 1, :]

    ot = ot_s[...]
    ms = jnp.mean(ot * ot, axis=0, keepdims=True)
    y = (ot * lax.rsqrt(ms + EPS)).T * goa_ref[...]
    o_ref[0] = y.astype(BF16)


def _attn(q, k, vt, qi, ki, wit, rel_bias, goa):
    bsz, _, _, s = q.shape
    nsel = min(TOPK_MAX, s // 4)
    assert nsel == TQ and s % TQ == 0, "attention kernel assumes TOPK_MAX-sized query blocks"
    bkt = jnp.asarray(_near_buckets())
    kern = functools.partial(_attn_kernel, nsel=nsel)
    return pl.pallas_call(
        kern,
        grid=(bsz, s // TQ),
        in_specs=[
            pl.BlockSpec((1, ATT_HEADS, LANES, TQ), lambda b, i: (b, 0, 0, i)),
            pl.BlockSpec((1, ATT_HEADS // 2, s, LANES), lambda b, i: (b, 0, 0, 0)),
            pl.BlockSpec((1, ATT_W, s), lambda b, i: (b, 0, 0)),
            pl.BlockSpec((1, IDX_HEADS, LANES, TQ), lambda b, i: (b, 0, 0, i)),
            pl.BlockSpec((1, s, LANES), lambda b, i: (b, 0, 0)),
            pl.BlockSpec((1, SUBLANES, TQ), lambda b, i: (b, 0, i)),
            pl.BlockSpec((NEAR, TQ), lambda b, i: (0, 0)),
            pl.BlockSpec(memory_space=pltpu.SMEM),
            pl.BlockSpec((1, ATT_W), lambda b, i: (0, 0)),
        ],
        out_specs=pl.BlockSpec((1, TQ, ATT_W), lambda b, i: (b, i, 0)),
        out_shape=jax.ShapeDtypeStruct((bsz, s, ATT_W), BF16),
        scratch_shapes=[
            pltpu.VMEM((s, TQ), I32),
            pltpu.VMEM((s, TQ), BF16),
            pltpu.VMEM((s, TQ), F32),
            pltpu.VMEM((ATT_HEADS, 2 * KT, TQ), F32),
            pltpu.VMEM((ATT_W, TQ), F32),
            pltpu.VMEM((ATT_HEADS, TQ), F32),
            pltpu.VMEM((ATT_HEADS, TQ), F32),
            pltpu.VMEM((ATT_HEADS, TQ), F32),
            pltpu.VMEM((ATT_HEADS, KT, TQ), F32),
        ],
        compiler_params=pltpu.CompilerParams(
            dimension_semantics=("arbitrary", "arbitrary"), vmem_limit_bytes=VMEM_LIMIT),
        name="attn",
    )(q, k, vt, qi, ki, wit, bkt, rel_bias, goa)


def _topk_rows(s, payload, k):
    nrows = s.shape[0]
    rows = lax.broadcasted_iota(I32, s.shape, 0).astype(F32)
    vals, pays = [], []
    for _ in range(k):
        m = jnp.max(s, axis=0, keepdims=True)
        ix = jnp.min(jnp.where(s == m, rows, float(nrows)), axis=0, keepdims=True)
        hit = rows == ix
        if payload is None:
            pays.append(ix)
        else:
            pays.append(jnp.max(jnp.where(hit, payload, -1.0), axis=0, keepdims=True))
        vals.append(m)
        s = jnp.where(hit, -jnp.inf, s)
    return jnp.concatenate(vals, axis=0), jnp.concatenate(pays, axis=0)


_PAIR_ROWS = tuple((a, PEER_TOPK // (a + 1)) for a in range(PEER_TOPK // 2))


def _mid_kernel(x_ref, cn_ref, an_ref, mod_ref, wo1_ref, wo2_ref, g2_ref, wpq_ref, k1_ref, k2_ref,
                x1_ref, h2_ref, idx_ref, gate_ref, qq_s, idt_s, gt_s):
    tm = x_ref.shape[1]
    x = x_ref[0]
    gt1 = mod_ref[0, 2:3, :]
    sh2 = mod_ref[0, 3:4, :]
    sc2 = mod_ref[0, 4:5, :]
    proj = (jnp.dot(cn_ref[0], wo1_ref[...], preferred_element_type=F32)
            + jnp.dot(an_ref[0], wo2_ref[...], preferred_element_type=F32))
    x1 = x + gt1 * proj
    x1_ref[0] = x1
    r = lax.rsqrt(jnp.mean(x1 * x1, axis=-1, keepdims=True) + EPS)
    h2 = (x1 * r) * g2_ref[...] * (1.0 + sc2) + sh2
    h2_ref[0] = h2
    qq_s[...] = jnp.dot(h2.astype(BF16), wpq_ref[...], preferred_element_type=F32).astype(BF16)

    def route_unit(hh, lt):
        rows = pl.ds(lt * LANES, LANES)
        q1 = qq_s[rows, pl.ds(pl.multiple_of(hh * 2 * N_KEYS, LANES), N_KEYS)]
        q2 = qq_s[rows, pl.ds(pl.multiple_of(hh * 2 * N_KEYS + N_KEYS, LANES), N_KEYS)]
        v1, i1 = _topk_rows(_nt(k1_ref[hh], q1), None, PEER_TOPK)
        v2, i2 = _topk_rows(_nt(k2_ref[hh], q2), None, PEER_TOPK)
        sub = lax.broadcasted_iota(I32, (SUBLANES, LANES), 0)
        cands, cidxs = [], []
        for a, nb in _PAIR_ROWS:
            nrows = max(nb, SUBLANES)
            val = v1[a:a + 1, :] + v2[0:nrows, :]
            if nb < SUBLANES:
                val = jnp.where(sub < nb, val, -jnp.inf)
            cands.append(val)
            cidxs.append(i1[a:a + 1, :] * float(N_KEYS) + i2[0:nrows, :])
        half = PEER_TOPK // 2
        cands.append(v1[half:, :] + v2[0:1, :])
        cidxs.append(i1[half:, :] * float(N_KEYS) + i2[0:1, :])
        best, experts = _topk_rows(jnp.concatenate(cands, axis=0), jnp.concatenate(cidxs, axis=0), PEER_TOPK)
        e = jnp.exp(best - best[0:1, :])
        g = e / jnp.sum(e, axis=0, keepdims=True)
        slots = pl.ds(pl.multiple_of(hh * PEER_TOPK, PEER_TOPK), PEER_TOPK)
        cols = pl.ds(lt * LANES, LANES)
        idt_s[slots, cols] = experts * float(ROWS_PER_EXPERT)
        gt_s[slots, cols] = g

    def route(hh, c):
        for lt in range(tm // LANES):
            route_unit(hh, lt)
        return c

    lax.fori_loop(0, PEER_HEADS, route, 0)
    idx_ref[0] = idt_s[...].T.astype(I32)
    gate_ref[0] = gt_s[...].T


def _mid(x, cn, an, mod3, wo1, wo2, g2, wpq, k1, k2):
    bsz, s, d = x.shape
    tm = TM_MID
    full = lambda shape: pl.BlockSpec(shape, lambda b, j: (0,) * len(shape))
    tok = lambda w: pl.BlockSpec((1, tm, w), lambda b, j: (b, j, 0))
    return pl.pallas_call(
        _mid_kernel,
        grid=(bsz, s // tm),
        in_specs=[tok(d), tok(CONV_CH), tok(ATT_W), pl.BlockSpec((1, 6, d), lambda b, j: (b, 0, 0)),
                  full(wo1.shape), full(wo2.shape), full(g2.shape), full(wpq.shape), full(k1.shape), full(k2.shape)],
        out_specs=(tok(d), tok(d), tok(PEER_SLOTS), tok(PEER_SLOTS)),
        out_shape=(
            jax.ShapeDtypeStruct((bsz, s, d), F32),
            jax.ShapeDtypeStruct((bsz, s, d), F32),
            jax.ShapeDtypeStruct((bsz, s, PEER_SLOTS), I32),
            jax.ShapeDtypeStruct((bsz, s, PEER_SLOTS), F32),
        ),
        scratch_shapes=[
            pltpu.VMEM((tm, PEER_HEADS * 2 * N_KEYS), BF16),
            pltpu.VMEM((PEER_SLOTS, tm), F32),
            pltpu.VMEM((PEER_SLOTS, tm), F32),
        ],
        compiler_params=pltpu.CompilerParams(
            dimension_semantics=("arbitrary", "arbitrary"), vmem_limit_bytes=VMEM_LIMIT),
        name="mid",
    )(x, cn, an, mod3, wo1, wo2, g2, wpq, k1, k2)


def _pack_table(t):
    tb = t.astype(BF16)
    lo = lax.bitcast_convert_type(tb[:, :HALF], jnp.uint16).astype(jnp.uint32)
    hi = lax.bitcast_convert_type(tb[:, HALF:], jnp.uint16).astype(jnp.uint32)
    return (lo | (hi << 16)).reshape(t.shape[0] * ROWS_PER_EXPERT, LANES)


def _unpack(w):
    lo = pltpu.bitcast(lax.shift_left(w, jnp.uint32(16)), F32)
    hi = pltpu.bitcast(w & jnp.uint32(0xFFFF0000), F32)
    return lo, hi


def _gather_row(tab_ref, row):
    return tab_ref[pl.ds(pl.multiple_of(row, ROWS_PER_EXPERT), ROWS_PER_EXPERT), :]


STAGE_ROWS = PEER_SLOTS * ROWS_PER_EXPERT


def _token_pieces(xg, r, first, count):
    return jnp.concatenate(
        [xg[r:r + 1, (first + q) * LANES:(first + q + 1) * LANES] for q in range(count)], axis=0)


def _peer_u_kernel(idx_ref, tab_ref, x_ref, gate_ref, w_ref, *stages):
    tb = w_ref.shape[0]
    sub = lax.broadcasted_iota(I32, (SUBLANES, PEER_SLOTS), 0)

    def group(g, c):
        for k in range(PEER_GROUPS):
            eight(pl.multiple_of((g * PEER_GROUPS + k) * SUBLANES, SUBLANES), stages[k * SUBLANES:(k + 1) * SUBLANES])
        return c

    def eight(t0, stages):
        act8 = jnp.zeros((SUBLANES, PEER_SLOTS), F32)
        xg = x_ref[pl.ds(t0, SUBLANES), :]
        xlo, xhi, rows_t = [], [], []
        for r in range(SUBLANES):
            xlo.append(jnp.concatenate([_token_pieces(xg, r, 0, ROWS_PER_EXPERT)] * 2, axis=0))
            xhi.append(jnp.concatenate([_token_pieces(xg, r, ROWS_PER_EXPERT, ROWS_PER_EXPERT)] * 2, axis=0))
            rows_t.append(idx_ref.at[t0 + r])
        for j in range(0, PEER_SLOTS, 2):
            for r in range(SUBLANES):
                pair = jnp.concatenate(
                    [_gather_row(tab_ref, rows_t[r][j]), _gather_row(tab_ref, rows_t[r][j + 1])], axis=0)
                lo, hi = _unpack(pair)
                row = j * ROWS_PER_EXPERT
                stages[r][row:row + 2 * ROWS_PER_EXPERT, :] = lo * xlo[r] + hi * xhi[r]
        for r in range(SUBLANES):
            stage = stages[r]
            cs = stage[pl.ds(0, PEER_SLOTS, stride=ROWS_PER_EXPERT), :]
            for q in range(1, ROWS_PER_EXPERT):
                cs = cs + stage[pl.ds(q, PEER_SLOTS, stride=ROWS_PER_EXPERT), :]
            act = jnp.sum(cs.T, axis=0, keepdims=True)
            act8 = jnp.where(sub == r, act, act8)
        rows = pl.ds(t0, SUBLANES)
        w_ref[rows, :] = gate_ref[rows, :] * jax.nn.gelu(act8)

    lax.fori_loop(0, tb // (PEER_GROUPS * SUBLANES), group, 0)


def _peer_u(idx, tab, h2, gate):
    n, d = h2.shape
    tb = TB_PEER
    return pl.pallas_call(
        _peer_u_kernel,
        grid=(n // tb,),
        in_specs=[
            pl.BlockSpec((tb, PEER_SLOTS), lambda i: (i, 0), memory_space=pltpu.SMEM),
            pl.BlockSpec(tab.shape, lambda i: (0, 0), pipeline_mode=pl.Buffered(1)),
            pl.BlockSpec((tb, d), lambda i: (i, 0)),
            pl.BlockSpec((tb, PEER_SLOTS), lambda i: (i, 0)),
        ],
        out_specs=pl.BlockSpec((tb, PEER_SLOTS), lambda i: (i, 0)),
        out_shape=jax.ShapeDtypeStruct((n, PEER_SLOTS), F32),
        scratch_shapes=[pltpu.VMEM((STAGE_ROWS, LANES), F32) for _ in range(PEER_GROUPS * SUBLANES)],
        compiler_params=pltpu.CompilerParams(
            dimension_semantics=("arbitrary",), vmem_limit_bytes=VMEM_LIMIT),
        name="peer_u",
    )(idx, tab, h2, gate)


PV_PAIRS = SUBLANES // 2


def _peer_v_kernel(idx_ref, w_ref, tab_ref, x1_ref, gt2_ref, e8_ref, mask_ref, o_ref, *stages):
    tb = idx_ref.shape[0]
    gt2 = gt2_ref[0]
    mask = mask_ref[...]
    e8 = e8_ref[...]

    def group(g, c):
        for k in range(PEER_GROUPS):
            eight(pl.multiple_of((g * PEER_GROUPS + k) * SUBLANES, SUBLANES), stages[k * PV_PAIRS:(k + 1) * PV_PAIRS])
        return c

    def eight(g0, stages):
        peers = []
        for p in range(PV_PAIRS):
            stage = stages[p]
            t0 = g0 + 2 * p
            for u in range(2):
                rows_t = idx_ref.at[t0 + u]
                for j in range(PEER_SLOTS):
                    stage[j * ROWS_PER_EXPERT:(j + 1) * ROWS_PER_EXPERT, u * LANES:(u + 1) * LANES] = (
                        _gather_row(tab_ref, rows_t[j]))
            w2 = w_ref[pl.ds(t0, 2), :]
            hi = w2.astype(BF16).astype(F32)
            rep = jnp.dot(jnp.concatenate([hi, w2 - hi], axis=0).astype(BF16), e8,
                          preferred_element_type=F32)
            lhs = jnp.concatenate([rep[i:i + 1, :] * mask for i in range(4)], axis=0).astype(BF16)
            out = jnp.dot(lhs, pltpu.bitcast(stage[...], BF16), preferred_element_type=F32)
            for u in range(2):
                r0 = u * SUBLANES
                peers.append(out[r0:r0 + SUBLANES, u * LANES:(u + 1) * LANES]
                             + out[2 * SUBLANES + r0:3 * SUBLANES + r0, u * LANES:(u + 1) * LANES])
        rows = pl.ds(g0, SUBLANES)
        for q in range(SUBLANES):
            cols = slice(q * LANES, (q + 1) * LANES)
            piece = jnp.concatenate([peer[q:q + 1, :] for peer in peers], axis=0)
            o_ref[rows, cols] = x1_ref[rows, cols] + gt2[:, cols] * piece

    lax.fori_loop(0, tb // (PEER_GROUPS * SUBLANES), group, 0)


def _peer_v(idx, w, tab, x1, gt2, blocks_per_batch):
    n, d = x1.shape
    tb = TB_PEER
    q = np.arange(SUBLANES)
    piece = 2 * (q % ROWS_PER_EXPERT) + q // ROWS_PER_EXPERT
    lane = np.arange(PEER_SLOTS * SUBLANES)
    mask = jnp.asarray((lane[None, :] % SUBLANES == piece[:, None]).astype(np.float32))
    e8 = jnp.asarray(np.arange(PEER_SLOTS)[:, None] == lane[None, :] // SUBLANES, BF16)
    return pl.pallas_call(
        _peer_v_kernel,
        grid=(n // tb,),
        in_specs=[
            pl.BlockSpec((tb, PEER_SLOTS), lambda i: (i, 0), memory_space=pltpu.SMEM),
            pl.BlockSpec((tb, PEER_SLOTS), lambda i: (i, 0)),
            pl.BlockSpec(tab.shape, lambda i: (0, 0), pipeline_mode=pl.Buffered(1)),
            pl.BlockSpec((tb, d), lambda i: (i, 0)),
            pl.BlockSpec((1, 1, d), lambda i: (i // blocks_per_batch, 0, 0)),
            pl.BlockSpec(e8.shape, lambda i: (0, 0)),
            pl.BlockSpec(mask.shape, lambda i: (0, 0)),
        ],
        out_specs=pl.BlockSpec((tb, d), lambda i: (i, 0)),
        out_shape=jax.ShapeDtypeStruct((n, d), F32),
        scratch_shapes=[pltpu.VMEM((STAGE_ROWS, 2 * LANES), jnp.uint32) for _ in range(PEER_GROUPS * PV_PAIRS)],
        compiler_params=pltpu.CompilerParams(
            dimension_semantics=("arbitrary",), vmem_limit_bytes=VMEM_LIMIT),
        name="peer_v",
    )(idx, w, tab, x1, gt2, e8, mask)


def _layer(x, mod, g_norm1, g_norm2, w_in, q_norm_g, k_norm_g, conv_w, conv_b, conv_ln_g, conv_ln_b,
           rel_bias, g_out_conv, g_out_attn, w_out, w_peer_q, peer_k1, peer_k2, peer_u, peer_v):
    bsz, s, d = x.shape
    n = bsz * s
    mod3 = mod.reshape(bsz, 6, d)
    row = lambda a: a.reshape(1, -1)

    c0 = 2 * CONV_CH
    c1 = c0 + 3 * ATT_W
    c2 = c1 + IDX_HEADS * IDX_DIM
    c3 = c2 + IDX_DIM
    wa = w_in[:, :c0].astype(BF16)
    wqkv = w_in[:, c0:c1].astype(BF16)
    widx = jnp.concatenate([w_in[:, c1:c2], w_in[:, c2:c3], w_in[:, c2:c3]], axis=1).astype(BF16)
    wwi = jnp.zeros((SUBLANES, d), F32).at[:IDX_HEADS].set(w_in[:, c3:c3 + IDX_HEADS].T).astype(BF16)
    head = np.arange(ATT_W) // HEAD_DIM
    e2 = jnp.asarray(head[:, None] == head[None, :], BF16)

    conv_n, q, k, vt, qi, ki, wit = _inproj(
        x, mod3, row(g_norm1), wa, wqkv, widx, wwi,
        row(jnp.tile(q_norm_g, ATT_HEADS)), row(jnp.tile(k_norm_g, ATT_HEADS)), e2,
        conv_w.reshape(CONV_WIDTH, CONV_CH), row(conv_b), row(conv_ln_g), row(conv_ln_b), row(g_out_conv))
    attn_n = _attn(q, k, vt, qi, ki, wit, rel_bias, row(g_out_attn))

    x1, h2, idx, gate = _mid(
        x, conv_n, attn_n, mod3, w_out[:CONV_CH].astype(BF16), w_out[CONV_CH:].astype(BF16), row(g_norm2),
        w_peer_q.astype(BF16), peer_k1.astype(BF16), peer_k2.astype(BF16))

    idx = idx.reshape(n, PEER_SLOTS)
    w = _peer_u(idx, _pack_table(peer_u), h2.reshape(n, d), gate.reshape(n, PEER_SLOTS))
    out = _peer_v(idx, w, _pack_table(peer_v), x1.reshape(n, d), mod3[:, 5:6, :], s // TB_PEER)
    return out.reshape(bsz, s, d)


def kernel(x, c, w_ada, b_ada, g_norm1, g_norm2, w_in, q_norm_g, k_norm_g, conv_w, conv_b, conv_ln_g,
           conv_ln_b, rel_bias, g_out_conv, g_out_attn, w_out, w_peer_q, peer_k1, peer_k2, peer_u, peer_v):
    depth = w_ada.shape[0]
    for l in range(depth):
        mod = _ada(c, w_ada[l], b_ada[l])
        x = _layer(x, mod, g_norm1[l], g_norm2[l], w_in[l], q_norm_g[l], k_norm_g[l], conv_w[l], conv_b[l],
                   conv_ln_g[l], conv_ln_b[l], rel_bias, g_out_conv[l], g_out_attn[l], w_out[l],
                   w_peer_q[l], peer_k1[l], peer_k2[l], peer_u[l], peer_v[l])
    return x
```

```python
import functools
import math

import numpy as np
import jax
import jax.numpy as jnp
from jax import lax
from jax.experimental import pallas as pl
from jax.experimental.pallas import tpu as pltpu

F32 = jnp.float32
BF16 = jnp.bfloat16
I32 = jnp.int32
HIGHEST = lax.Precision.HIGHEST

D_MODEL = 1024
CHUNK = 64
CONV_CH = 512
CONV_WIDTH = 31
ATT_HEADS = 8
HEAD_DIM = 64
ATT_W = ATT_HEADS * HEAD_DIM
IDX_HEADS = 4
IDX_DIM = 64
IDX_SCALE = (IDX_HEADS * IDX_DIM) ** -0.5
TOPK_MAX = 256
REL_BUCKETS = 32
REL_MAX_DIST = 128
PEER_HEADS = 8
N_KEYS = 128
N_EXPERTS = N_KEYS * N_KEYS
PEER_TOPK = 16
PEER_SLOTS = PEER_HEADS * PEER_TOPK
EPS = 1e-6

LANES = 128
SUBLANES = 8
PACK = 16
VMEM_LIMIT = 56 * 1024 * 1024

TM_IN = 512
CONV_ROWS = 64
HALO = 32
TQ = 256
KT = 256
TM_MID = 512
TB_PEER = 128
PEER_GROUPS = 2
PEER_GROUPS_V = 4
HALF = D_MODEL // 2
ROWS_PER_EXPERT = HALF // LANES

NT_DIMS = (((1,), (1,)), ((), ()))

_NEG_INF_BITS = int(np.array(-np.inf, np.float32).view(np.int32))
KEY_NEG_INF = _NEG_INF_BITS ^ 0x7FFFFFFF
MIN_NORMAL_BITS = 0x00800000
BAND = MIN_NORMAL_BITS >> 16
INT_MIN = -(2 ** 31)


def _nt(a, b, precision=None):
    return lax.dot_general(a, b, NT_DIMS, precision=precision, preferred_element_type=F32)


def _ada_kernel(c_ref, w_ref, b_ref, o_ref):
    a = jax.nn.silu(c_ref[...])
    o_ref[...] = jnp.dot(a, w_ref[...], precision=HIGHEST, preferred_element_type=F32) + b_ref[...]


def _ada(c, w_ada, b_ada):
    bsz, d = c.shape
    return pl.pallas_call(
        _ada_kernel,
        grid=(6,),
        in_specs=[
            pl.BlockSpec((bsz, d), lambda j: (0, 0)),
            pl.BlockSpec((d, d), lambda j: (0, j)),
            pl.BlockSpec((1, d), lambda j: (0, j)),
        ],
        out_specs=pl.BlockSpec((bsz, d), lambda j: (0, j)),
        out_shape=jax.ShapeDtypeStruct((bsz, 6 * d), F32),
        name="ada",
    )(c, w_ada, b_ada.reshape(1, 6 * d))


def _inproj_kernel(x_ref, mod_ref, g1_ref, wa_ref, wqkv_ref, widx_ref, wwi_ref, gq_ref, gk_ref, e2_ref,
                   cw_ref, cb_ref, lng_ref, lnb_ref, goc_ref,
                   conv_ref, q_ref, k_ref, vt_ref, qi_ref, ki_ref, wit_ref, ubuf):
    j = pl.program_id(1)
    tm = x_ref.shape[1]
    x = x_ref[0]
    sh1 = mod_ref[0, 0:1, :]
    sc1 = mod_ref[0, 1:2, :]
    r = lax.rsqrt(jnp.mean(x * x, axis=-1, keepdims=True) + EPS)
    h = (x * r) * g1_ref[...] * (1.0 + sc1) + sh1
    hb = h.astype(BF16)

    pa = jnp.dot(hb, wa_ref[...], preferred_element_type=F32)
    u = pa[:, :CONV_CH] * jax.nn.sigmoid(pa[:, CONV_CH:])

    @pl.when(j == 0)
    def _():
        ubuf[0:HALO, :] = jnp.zeros((HALO, CONV_CH), F32)

    ubuf[HALO:HALO + tm, :] = u
    first = HALO - (CONV_WIDTH - 1)
    for rb in range(tm // CONV_ROWS):
        base = rb * CONV_ROWS
        acc = jnp.zeros((CONV_ROWS, CONV_CH), F32) + cb_ref[...]
        for phase in range(SUBLANES):
            taps = [t for t in range(CONV_WIDTH) if (first + t) % SUBLANES == phase]
            span = (first + taps[-1]) - phase + CONV_ROWS
            slab = ubuf[base + phase:base + phase + span, :]
            for t in taps:
                off = first + t - phase
                acc = acc + cw_ref[t:t + 1, :] * slab[off:off + CONV_ROWS, :]
        mu = jnp.mean(acc, axis=-1, keepdims=True)
        xc = acc - mu
        y = xc * lax.rsqrt(jnp.mean(xc * xc, axis=-1, keepdims=True) + EPS)
        y = jax.nn.silu(y * lng_ref[...] + lnb_ref[...])
        y = y * lax.rsqrt(jnp.mean(y * y, axis=-1, keepdims=True) + EPS) * goc_ref[...]
        conv_ref[0, base:base + CONV_ROWS, :] = y.astype(BF16)
    ubuf[0:HALO, :] = ubuf[tm:tm + HALO, :]

    pq = jnp.dot(hb, wqkv_ref[...], preferred_element_type=F32)
    q = pq[:, :ATT_W]
    k = pq[:, ATT_W:2 * ATT_W]
    v = pq[:, 2 * ATT_W:]
    e2 = e2_ref[...]

    def head_mean(sq):
        hi = sq.astype(BF16)
        lo = (sq - hi.astype(F32)).astype(BF16)
        return (jnp.dot(hi, e2, preferred_element_type=F32)
                + jnp.dot(lo, e2, preferred_element_type=F32)) * (1.0 / HEAD_DIM)

    qs = head_mean(q * q)
    ks = head_mean(k * k)
    qn = q * lax.rsqrt(qs + EPS) * gq_ref[...] * (HEAD_DIM ** -0.5)
    kn = k * lax.rsqrt(ks + EPS) * gk_ref[...]
    lane = lax.broadcasted_iota(I32, (tm, LANES), 1)
    low = lane < HEAD_DIM
    for p in range(ATT_HEADS // 2):
        slab = qn[:, p * LANES:(p + 1) * LANES]
        q_ref[0, 2 * p] = jnp.where(low, slab, 0.0).T.astype(BF16)
        q_ref[0, 2 * p + 1] = jnp.where(low, 0.0, slab).T.astype(BF16)
        k_ref[0, p] = kn[:, p * LANES:(p + 1) * LANES].astype(BF16)
    vt_ref[0] = v.T.astype(BF16)

    pc = jnp.dot(hb, widx_ref[...], preferred_element_type=F32)
    for p in range(IDX_HEADS // 2):
        slab = pc[:, p * LANES:(p + 1) * LANES]
        qi_ref[0, 2 * p] = jnp.where(low, slab, 0.0).T.astype(BF16)
        qi_ref[0, 2 * p + 1] = jnp.where(low, 0.0, slab).T.astype(BF16)
    ki_ref[0] = pc[:, 2 * LANES:3 * LANES].astype(BF16)
    wit_ref[0] = _nt(wwi_ref[...], hb) * IDX_SCALE


def _inproj(x, mod3, g1, wa, wqkv, widx, wwi, gq, gk, e2, cw, cb, lng, lnb, goc):
    bsz, s, d = x.shape
    tm = TM_IN
    nt = s // tm
    full = lambda shape: pl.BlockSpec(shape, lambda b, j: (0,) * len(shape))
    out_shape = (
        jax.ShapeDtypeStruct((bsz, s, CONV_CH), BF16),
        jax.ShapeDtypeStruct((bsz, ATT_HEADS, LANES, s), BF16),
        jax.ShapeDtypeStruct((bsz, ATT_HEADS // 2, s, LANES), BF16),
        jax.ShapeDtypeStruct((bsz, ATT_W, s), BF16),
        jax.ShapeDtypeStruct((bsz, IDX_HEADS, LANES, s), BF16),
        jax.ShapeDtypeStruct((bsz, s, LANES), BF16),
        jax.ShapeDtypeStruct((bsz, SUBLANES, s), F32),
    )
    out_specs = (
        pl.BlockSpec((1, tm, CONV_CH), lambda b, j: (b, j, 0)),
        pl.BlockSpec((1, ATT_HEADS, LANES, tm), lambda b, j: (b, 0, 0, j)),
        pl.BlockSpec((1, ATT_HEADS // 2, tm, LANES), lambda b, j: (b, 0, j, 0)),
        pl.BlockSpec((1, ATT_W, tm), lambda b, j: (b, 0, j)),
        pl.BlockSpec((1, IDX_HEADS, LANES, tm), lambda b, j: (b, 0, 0, j)),
        pl.BlockSpec((1, tm, LANES), lambda b, j: (b, j, 0)),
        pl.BlockSpec((1, SUBLANES, tm), lambda b, j: (b, 0, j)),
    )
    return pl.pallas_call(
        _inproj_kernel,
        grid=(bsz, nt),
        in_specs=[
            pl.BlockSpec((1, tm, d), lambda b, j: (b, j, 0)),
            pl.BlockSpec((1, 6, d), lambda b, j: (b, 0, 0)),
            full(g1.shape), full(wa.shape), full(wqkv.shape), full(widx.shape), full(wwi.shape),
            full(gq.shape), full(gk.shape), full(e2.shape),
            full(cw.shape), full(cb.shape), full(lng.shape), full(lnb.shape), full(goc.shape),
        ],
        out_specs=out_specs,
        out_shape=out_shape,
        scratch_shapes=[pltpu.VMEM((tm + HALO, CONV_CH), F32)],
        compiler_params=pltpu.CompilerParams(
            dimension_semantics=("arbitrary", "arbitrary"), vmem_limit_bytes=VMEM_LIMIT),
        name="inproj",
    )(x, mod3, g1, wa, wqkv, widx, wwi, gq, gk, e2, cw, cb, lng, lnb, goc)


def _t5_bucket_np(rel):
    half = REL_BUCKETS // 2
    max_exact = half // 2
    ret = np.where(rel > 0, half, 0)
    n = np.abs(rel)
    nf = np.maximum(n, 1).astype(np.float64)
    large = max_exact + (np.log(nf / max_exact) / math.log(REL_MAX_DIST / max_exact)
                         * (half - max_exact)).astype(np.int32)
    large = np.minimum(large, half - 1)
    return (ret + np.where(n < max_exact, n, large)).astype(np.int32)


NEAR = REL_MAX_DIST + TQ
FAR_BUCKET = REL_BUCKETS // 2 - 1


def _near_buckets():
    r = np.arange(NEAR)[:, None]
    t = np.arange(TQ)[None, :]
    return _t5_bucket_np(r - REL_MAX_DIST - t)


def _attn_kernel(q_ref, k_ref, vt_ref, qi_ref, ki_ref, wit_ref, bkt_ref, rb_ref, goa_ref, o_ref,
                 keys_s, khi_s, maskb_s, relb_s, ot_s, m_s, mo_s, l_s, lg_s, *, nsel):
    b = pl.program_id(0)
    i = pl.program_id(1)
    t0 = i * TQ
    n_tiles = i + 1
    neg_inf = F32(-jnp.inf)

    @pl.when((b == 0) & (i == 0))
    def _():
        bk = bkt_ref[...]
        for h in range(ATT_HEADS):
            far = rb_ref[FAR_BUCKET, h]
            acc = jnp.zeros((NEAR, TQ), F32)
            for bb in range(REL_BUCKETS):
                acc = jnp.where(bk == bb, rb_ref[bb, h] - far, acc)
            relb_s[h, 0:2 * KT - NEAR, :] = jnp.zeros((2 * KT - NEAR, TQ), F32)
            relb_s[h, 2 * KT - NEAR:2 * KT, :] = acc

    qpos = t0 + lax.broadcasted_iota(I32, (1, TQ), 1)
    limit = (qpos // CHUNK + 1) * CHUNK
    row_iota = lax.broadcasted_iota(I32, (KT, TQ), 0)

    def tile_start(jt):
        return pl.multiple_of(jt * KT, KT)

    wi = wit_ref[0]

    def p1(jt, c):
        ks = tile_start(jt)
        kit = ki_ref[0, pl.ds(ks, KT), :]
        acc = jnp.zeros((KT, TQ), F32)
        for h in range(IDX_HEADS):
            lgt = jnp.dot(kit, qi_ref[0, h], preferred_element_type=F32)
            acc = acc + jnp.maximum(lgt, 0.0) * wi[h:h + 1, :]
        sc = jnp.where(row_iota + ks < limit, acc, neg_inf)
        bits = pltpu.bitcast(sc, I32)
        keys_s[pl.ds(ks, KT), :] = bits ^ (lax.shift_right_arithmetic(bits, 31) & 0x7FFFFFFF)
        top = jnp.where((bits & 0x7FFFFFFF) < MIN_NORMAL_BITS, 0, bits & I32(-65536))
        khi_s[pl.ds(ks, KT), :] = pltpu.bitcast(top, F32).astype(BF16)
        return c

    lax.fori_loop(0, n_tiles, p1, 0)

    def count(pred):
        def body(jt, acc):
            ks = tile_start(jt)
            m = pred(keys_s[pl.ds(ks, KT), :], ks).astype(F32)
            return acc + jnp.sum(m.reshape(KT // SUBLANES, SUBLANES, TQ), axis=0)
        acc = lax.fori_loop(0, n_tiles, body, jnp.zeros((SUBLANES, TQ), F32))
        return jnp.sum(acc, axis=0, keepdims=True)

    def count_top(cand_hi):
        cand_hi = jnp.where((cand_hi > 0) & (cand_hi < BAND), BAND, cand_hi)
        pat = cand_hi ^ (lax.shift_right_arithmetic(cand_hi, 15) & 0x7FFF)
        cb = pltpu.bitcast(lax.shift_left(pat, 16), F32).astype(BF16)
        one = jnp.ones((KT, TQ), BF16)
        zero = jnp.zeros((KT, TQ), BF16)

        def body(jt, acc):
            ks = tile_start(jt)
            hit = jnp.where(khi_s[pl.ds(ks, KT), :] >= cb, one, zero)
            parts = [hit[r:r + PACK, :] for r in range(0, KT, PACK)]
            while len(parts) > 1:
                parts = [parts[k] + parts[k + 1] for k in range(0, len(parts), 2)]
            return acc + parts[0].astype(F32)
        acc = lax.fori_loop(0, n_tiles, body, jnp.zeros((PACK, TQ), F32))
        return jnp.sum(acc, axis=0, keepdims=True)

    def select_thr():
        c0 = count_top(jnp.zeros((1, TQ), I32))
        ok = c0 >= nsel
        t = jnp.where(ok, 0, -(2 ** 15)).astype(I32)
        cnt = jnp.where(ok, c0, F32(2 ** 30))

        def top_step(it, carry):
            t, cnt = carry
            cand = t + lax.shift_left(I32(1), I32(14) - it)
            c = count_top(cand)
            ok = c >= nsel
            return jnp.where(ok, cand, t), jnp.where(ok, c, cnt)

        t, cnt = lax.fori_loop(0, 15, top_step, (t, cnt))

        band = jnp.where(jnp.abs(2 * t + 1) < 2 * BAND, 1.0, 0.0)

        def band_counts():
            c_zero = count(lambda kt, ks: kt >= 0)
            c_tiny = count(lambda kt, ks: kt >= 1)
            c_norm = count(lambda kt, ks: kt >= MIN_NORMAL_BITS)
            return c_zero, jnp.where(c_tiny == c_norm, jnp.where(c_zero >= nsel, 1.0, 0.0), 0.0)

        c_zero, plain = lax.cond(jnp.max(band) > 0.0, band_counts,
                                 lambda: (jnp.zeros((1, TQ), F32), jnp.ones((1, TQ), F32)))
        settled = band * plain > 0.0
        hard = band * (1.0 - plain) > 0.0
        t = jnp.where(settled, 0, lax.shift_left(jnp.where(hard, -BAND, t), 16))
        cnt = jnp.where(settled, c_zero, jnp.where(hard, F32(2 ** 30), cnt))
        first_bit = jnp.where(jnp.max(band * (1.0 - plain)) > 0.0, 23, 15).astype(I32)

        def low_cond(carry):
            bit, _, cnt = carry
            return (bit >= 0) & (jnp.max(jnp.where(settled, F32(nsel), cnt)) > nsel)

        def low_step(carry):
            bit, t, cnt = carry
            cand = t + lax.shift_left(I32(1), bit)
            c = count(lambda kt, ks: kt >= cand)
            ok = c >= nsel
            return bit - 1, jnp.where(ok, cand, t), jnp.where(ok, c, cnt)

        _, t, cnt = lax.while_loop(low_cond, low_step, (first_bit, t, cnt))
        return t, cnt

    thr, c_ge = lax.cond(
        i > 0, select_thr,
        lambda: (jnp.full((1, TQ), KEY_NEG_INF + 1, I32), jnp.full((1, TQ), nsel, F32)))

    def mask_plain():
        def p3(jt, c):
            ks = tile_start(jt)
            maskb_s[pl.ds(ks, KT), :] = jnp.where(keys_s[pl.ds(ks, KT), :] >= thr, 0.0, neg_inf)
            return c
        lax.fori_loop(0, n_tiles, p3, 0)
        return I32(0)

    def mask_ties():
        quota = nsel - count(lambda kt, ks: kt > thr)
        tri = jnp.where(lax.broadcasted_iota(I32, (KT, KT), 0) >= lax.broadcasted_iota(I32, (KT, KT), 1),
                        1.0, 0.0).astype(BF16)

        def p3(jt, before):
            ks = tile_start(jt)
            kt = keys_s[pl.ds(ks, KT), :]
            eq = kt == thr
            rank = before + jnp.dot(tri, jnp.where(eq, 1.0, 0.0).astype(BF16), preferred_element_type=F32)
            tied = jnp.where(eq, jnp.where(rank <= quota, 0.0, neg_inf), neg_inf)
            maskb_s[pl.ds(ks, KT), :] = jnp.where(kt > thr, 0.0, tied)
            return rank[KT - 1:KT, :]

        lax.fori_loop(0, n_tiles, p3, jnp.zeros((1, TQ), F32))
        return I32(0)

    lax.cond(jnp.max(c_ge) > nsel, mask_ties, mask_plain)

    m_s[...] = jnp.full((ATT_HEADS, TQ), neg_inf, F32)
    l_s[...] = jnp.zeros((ATT_HEADS, TQ), F32)
    ot_s[...] = jnp.zeros((ATT_W, TQ), F32)

    def att_tile(jt, near):
        ks = tile_start(jt)
        mb = maskb_s[pl.ds(ks, KT), :]
        for h in range(ATT_HEADS):
            l = jnp.dot(k_ref[0, h // 2, pl.ds(ks, KT), :], q_ref[0, h], preferred_element_type=F32) + mb
            if near:
                off = pl.multiple_of((jt - (n_tiles - 2)) * KT, KT)
                l = l + relb_s[h, pl.ds(off, KT), :]
            lg_s[h] = l
            m_old = m_s[h:h + 1, :]
            mo_s[h:h + 1, :] = m_old
            m_s[h:h + 1, :] = jnp.maximum(m_old, jnp.max(l, axis=0, keepdims=True))
        for h in range(ATT_HEADS):
            m_new = m_s[h:h + 1, :]
            m_ref = jnp.where(m_new == neg_inf, 0.0, m_new)
            alpha = jnp.exp(mo_s[h:h + 1, :] - m_ref)
            p = jnp.exp(lg_s[h] - m_ref)
            l_s[h:h + 1, :] = alpha * l_s[h:h + 1, :] + jnp.sum(p, axis=0, keepdims=True)
            vt = vt_ref[0, h * HEAD_DIM:(h + 1) * HEAD_DIM, pl.ds(ks, KT)]
            rows = slice(h * HEAD_DIM, (h + 1) * HEAD_DIM)
            ot_s[rows, :] = ot_s[rows, :] * alpha + jnp.dot(vt, p.astype(BF16), preferred_element_type=F32)

    def far_tile(jt, c):
        att_tile(jt, False)
        return c

    def near_tile(jt, c):
        att_tile(jt, True)
        return c

    n_far = jnp.maximum(n_tiles - 2, 0)
    lax.fori_loop(0, n_far, far_tile, 0)
    lax.fori_loop(n_far, n_tiles, near_tile, 0)
    for h in range(ATT_HEADS):
        rows = slice(h * HEAD_DIM, (h + 1) * HEAD_DIM)
        ot_s[rows, :] = ot_s[rows, :] / l_s[h:h + 1, :]

    ot = ot_s[...]
    ms = jnp.mean(ot * ot, axis=0, keepdims=True)
    y = (ot * lax.rsqrt(ms + EPS)).T * goa_ref[...]
    o_ref[0] = y.astype(BF16)


def _attn(q, k, vt, qi, ki, wit, rel_bias, goa):
    bsz, _, _, s = q.shape
    nsel = min(TOPK_MAX, s // 4)
    assert nsel == TQ and s % TQ == 0, "attention kernel assumes TOPK_MAX-sized query blocks"
    bkt = jnp.asarray(_near_buckets())
    kern = functools.partial(_attn_kernel, nsel=nsel)
    return pl.pallas_call(
        kern,
        grid=(bsz, s // TQ),
        in_specs=[
            pl.BlockSpec((1, ATT_HEADS, LANES, TQ), lambda b, i: (b, 0, 0, i)),
            pl.BlockSpec((1, ATT_HEADS // 2, s, LANES), lambda b, i: (b, 0, 0, 0)),
            pl.BlockSpec((1, ATT_W, s), lambda b, i: (b, 0, 0)),
            pl.BlockSpec((1, IDX_HEADS, LANES, TQ), lambda b, i: (b, 0, 0, i)),
            pl.BlockSpec((1, s, LANES), lambda b, i: (b, 0, 0)),
            pl.BlockSpec((1, SUBLANES, TQ), lambda b, i: (b, 0, i)),
            pl.BlockSpec((NEAR, TQ), lambda b, i: (0, 0)),
            pl.BlockSpec(memory_space=pltpu.SMEM),
            pl.BlockSpec((1, ATT_W), lambda b, i: (0, 0)),
        ],
        out_specs=pl.BlockSpec((1, TQ, ATT_W), lambda b, i: (b, i, 0)),
        out_shape=jax.ShapeDtypeStruct((bsz, s, ATT_W), BF16),
        scratch_shapes=[
            pltpu.VMEM((s, TQ), I32),
            pltpu.VMEM((s, TQ), BF16),
            pltpu.VMEM((s, TQ), F32),
            pltpu.VMEM((ATT_HEADS, 2 * KT, TQ), F32),
            pltpu.VMEM((ATT_W, TQ), F32),
            pltpu.VMEM((ATT_HEADS, TQ), F32),
            pltpu.VMEM((ATT_HEADS, TQ), F32),
            pltpu.VMEM((ATT_HEADS, TQ), F32),
            pltpu.VMEM((ATT_HEADS, KT, TQ), F32),
        ],
        compiler_params=pltpu.CompilerParams(
            dimension_semantics=("arbitrary", "arbitrary"), vmem_limit_bytes=VMEM_LIMIT),
        name="attn",
    )(q, k, vt, qi, ki, wit, bkt, rel_bias, goa)


def _topk_rows(s, payload, k):
    nrows = s.shape[0]
    rows = lax.broadcasted_iota(I32, s.shape, 0).astype(F32)
    vals, pays = [], []
    for _ in range(k):
        m = jnp.max(s, axis=0, keepdims=True)
        ix = jnp.min(jnp.where(s == m, rows, float(nrows)), axis=0, keepdims=True)
        hit = rows == ix
        if payload is None:
            pays.append(ix)
        else:
            pays.append(jnp.max(jnp.where(hit, payload, -1.0), axis=0, keepdims=True))
        vals.append(m)
        s = jnp.where(hit, -jnp.inf, s)
    return jnp.concatenate(vals, axis=0), jnp.concatenate(pays, axis=0)


_PAIR_ROWS = tuple((a, PEER_TOPK // (a + 1)) for a in range(PEER_TOPK // 2))


def _mid_kernel(x_ref, cn_ref, an_ref, mod_ref, wo1_ref, wo2_ref, g2_ref, wpq_ref, k1_ref, k2_ref,
                x1_ref, h2_ref, idx_ref, gate_ref, qq_s, idt_s, gt_s):
    tm = x_ref.shape[1]
    x = x_ref[0]
    gt1 = mod_ref[0, 2:3, :]
    sh2 = mod_ref[0, 3:4, :]
    sc2 = mod_ref[0, 4:5, :]
    proj = (jnp.dot(cn_ref[0], wo1_ref[...], preferred_element_type=F32)
            + jnp.dot(an_ref[0], wo2_ref[...], preferred_element_type=F32))
    x1 = x + gt1 * proj
    x1_ref[0] = x1
    r = lax.rsqrt(jnp.mean(x1 * x1, axis=-1, keepdims=True) + EPS)
    h2 = (x1 * r) * g2_ref[...] * (1.0 + sc2) + sh2
    h2_ref[0] = h2
    qq_s[...] = jnp.dot(h2.astype(BF16), wpq_ref[...], preferred_element_type=F32).astype(BF16)

    def route_unit(hh, lt):
        rows = pl.ds(lt * LANES, LANES)
        q1 = qq_s[rows, pl.ds(pl.multiple_of(hh * 2 * N_KEYS, LANES), N_KEYS)]
        q2 = qq_s[rows, pl.ds(pl.multiple_of(hh * 2 * N_KEYS + N_KEYS, LANES), N_KEYS)]
        v1, i1 = _topk_rows(_nt(k1_ref[hh], q1), None, PEER_TOPK)
        v2, i2 = _topk_rows(_nt(k2_ref[hh], q2), None, PEER_TOPK)
        sub = lax.broadcasted_iota(I32, (SUBLANES, LANES), 0)
        cands, cidxs = [], []
        for a, nb in _PAIR_ROWS:
            nrows = max(nb, SUBLANES)
            val = v1[a:a + 1, :] + v2[0:nrows, :]
            if nb < SUBLANES:
                val = jnp.where(sub < nb, val, -jnp.inf)
            cands.append(val)
            cidxs.append(i1[a:a + 1, :] * float(N_KEYS) + i2[0:nrows, :])
        half = PEER_TOPK // 2
        cands.append(v1[half:, :] + v2[0:1, :])
        cidxs.append(i1[half:, :] * float(N_KEYS) + i2[0:1, :])
        best, experts = _topk_rows(jnp.concatenate(cands, axis=0), jnp.concatenate(cidxs, axis=0), PEER_TOPK)
        e = jnp.exp(best - best[0:1, :])
        g = e / jnp.sum(e, axis=0, keepdims=True)
        slots = pl.ds(pl.multiple_of(hh * PEER_TOPK, PEER_TOPK), PEER_TOPK)
        cols = pl.ds(lt * LANES, LANES)
        idt_s[slots, cols] = experts * float(ROWS_PER_EXPERT)
        gt_s[slots, cols] = g

    def route(hh, c):
        for lt in range(tm // LANES):
            route_unit(hh, lt)
        return c

    lax.fori_loop(0, PEER_HEADS, route, 0)
    idx_ref[0] = idt_s[...].T.astype(I32)
    gate_ref[0] = gt_s[...].T


def _mid(x, cn, an, mod3, wo1, wo2, g2, wpq, k1, k2):
    bsz, s, d = x.shape
    tm = TM_MID
    full = lambda shape: pl.BlockSpec(shape, lambda b, j: (0,) * len(shape))
    tok = lambda w: pl.BlockSpec((1, tm, w), lambda b, j: (b, j, 0))
    return pl.pallas_call(
        _mid_kernel,
        grid=(bsz, s // tm),
        in_specs=[tok(d), tok(CONV_CH), tok(ATT_W), pl.BlockSpec((1, 6, d), lambda b, j: (b, 0, 0)),
                  full(wo1.shape), full(wo2.shape), full(g2.shape), full(wpq.shape), full(k1.shape), full(k2.shape)],
        out_specs=(tok(d), tok(d), tok(PEER_SLOTS), tok(PEER_SLOTS)),
        out_shape=(
            jax.ShapeDtypeStruct((bsz, s, d), F32),
            jax.ShapeDtypeStruct((bsz, s, d), F32),
            jax.ShapeDtypeStruct((bsz, s, PEER_SLOTS), I32),
            jax.ShapeDtypeStruct((bsz, s, PEER_SLOTS), F32),
        ),
        scratch_shapes=[
            pltpu.VMEM((tm, PEER_HEADS * 2 * N_KEYS), BF16),
            pltpu.VMEM((PEER_SLOTS, tm), F32),
            pltpu.VMEM((PEER_SLOTS, tm), F32),
        ],
        compiler_params=pltpu.CompilerParams(
            dimension_semantics=("arbitrary", "arbitrary"), vmem_limit_bytes=VMEM_LIMIT),
        name="mid",
    )(x, cn, an, mod3, wo1, wo2, g2, wpq, k1, k2)


def _pack_table(t):
    tb = t.astype(BF16)
    lo = lax.bitcast_convert_type(tb[:, :HALF], jnp.uint16).astype(jnp.uint32)
    hi = lax.bitcast_convert_type(tb[:, HALF:], jnp.uint16).astype(jnp.uint32)
    return (lo | (hi << 16)).reshape(t.shape[0] * ROWS_PER_EXPERT, LANES)


def _unpack(w):
    lo = pltpu.bitcast(lax.shift_left(w, jnp.uint32(16)), F32)
    hi = pltpu.bitcast(w & jnp.uint32(0xFFFF0000), F32)
    return lo, hi


def _gather_row(tab_ref, row):
    return tab_ref[pl.ds(pl.multiple_of(row, ROWS_PER_EXPERT), ROWS_PER_EXPERT), :]


STAGE_ROWS = PEER_SLOTS * ROWS_PER_EXPERT


def _token_pieces(xg, r, first, count):
    return jnp.concatenate(
        [xg[r:r + 1, (first + q) * LANES:(first + q + 1) * LANES] for q in range(count)], axis=0)


def _peer_u_kernel(idx_ref, tab_ref, x_ref, gate_ref, w_ref, *stages):
    tb = w_ref.shape[0]
    sub = lax.broadcasted_iota(I32, (SUBLANES, PEER_SLOTS), 0)

    def group(g, c):
        for k in range(PEER_GROUPS):
            eight(pl.multiple_of((g * PEER_GROUPS + k) * SUBLANES, SUBLANES), stages[k * SUBLANES:(k + 1) * SUBLANES])
        return c

    def eight(t0, stages):
        act8 = jnp.zeros((SUBLANES, PEER_SLOTS), F32)
        xg = x_ref[pl.ds(t0, SUBLANES), :]
        xlo, xhi, rows_t = [], [], []
        for r in range(SUBLANES):
            xlo.append(jnp.concatenate([_token_pieces(xg, r, 0, ROWS_PER_EXPERT)] * 2, axis=0))
            xhi.append(jnp.concatenate([_token_pieces(xg, r, ROWS_PER_EXPERT, ROWS_PER_EXPERT)] * 2, axis=0))
            rows_t.append(idx_ref.at[t0 + r])
        for j in range(0, PEER_SLOTS, 2):
            for r in range(SUBLANES):
                pair = jnp.concatenate(
                    [_gather_row(tab_ref, rows_t[r][j]), _gather_row(tab_ref, rows_t[r][j + 1])], axis=0)
                lo, hi = _unpack(pair)
                row = j * ROWS_PER_EXPERT
                stages[r][row:row + 2 * ROWS_PER_EXPERT, :] = lo * xlo[r] + hi * xhi[r]
        for r in range(SUBLANES):
            stage = stages[r]
            cs = stage[pl.ds(0, PEER_SLOTS, stride=ROWS_PER_EXPERT), :]
            for q in range(1, ROWS_PER_EXPERT):
                cs = cs + stage[pl.ds(q, PEER_SLOTS, stride=ROWS_PER_EXPERT), :]
            act = jnp.sum(cs.T, axis=0, keepdims=True)
            act8 = jnp.where(sub == r, act, act8)
        rows = pl.ds(t0, SUBLANES)
        w_ref[rows, :] = gate_ref[rows, :] * jax.nn.gelu(act8)

    lax.fori_loop(0, tb // (PEER_GROUPS * SUBLANES), group, 0)


def _peer_u(idx, tab, h2, gate):
    n, d = h2.shape
    tb = TB_PEER
    return pl.pallas_call(
        _peer_u_kernel,
        grid=(n // tb,),
        in_specs=[
            pl.BlockSpec((tb, PEER_SLOTS), lambda i: (i, 0), memory_space=pltpu.SMEM),
            pl.BlockSpec(tab.shape, lambda i: (0, 0), pipeline_mode=pl.Buffered(1)),
            pl.BlockSpec((tb, d), lambda i: (i, 0)),
            pl.BlockSpec((tb, PEER_SLOTS), lambda i: (i, 0)),
        ],
        out_specs=pl.BlockSpec((tb, PEER_SLOTS), lambda i: (i, 0)),
        out_shape=jax.ShapeDtypeStruct((n, PEER_SLOTS), F32),
        scratch_shapes=[pltpu.VMEM((STAGE_ROWS, LANES), F32) for _ in range(PEER_GROUPS * SUBLANES)],
        compiler_params=pltpu.CompilerParams(
            dimension_semantics=("arbitrary",), vmem_limit_bytes=VMEM_LIMIT),
        name="peer_u",
    )(idx, tab, h2, gate)


PV_PAIRS = SUBLANES // 2


def _peer_v_kernel(idx_ref, w_ref, tab_ref, x1_ref, gt2_ref, e8_ref, mask_ref, o_ref, *stages):
    tb = idx_ref.shape[0]
    gt2 = gt2_ref[0]
    mask = mask_ref[...]
    e8 = e8_ref[...]

    def group(g, c):
        for k in range(PEER_GROUPS_V):
            eight(pl.multiple_of((g * PEER_GROUPS_V + k) * SUBLANES, SUBLANES), stages[k * PV_PAIRS:(k + 1) * PV_PAIRS])
        return c

    def eight(g0, stages):
        peers = []
        for p in range(PV_PAIRS):
            stage = stages[p]
            t0 = g0 + 2 * p
            for u in range(2):
                rows_t = idx_ref.at[t0 + u]
                for j in range(PEER_SLOTS):
                    stage[j * ROWS_PER_EXPERT:(j + 1) * ROWS_PER_EXPERT, u * LANES:(u + 1) * LANES] = (
                        _gather_row(tab_ref, rows_t[j]))
            w2 = w_ref[pl.ds(t0, 2), :]
            hi = w2.astype(BF16).astype(F32)
            rep = jnp.dot(jnp.concatenate([hi, w2 - hi], axis=0).astype(BF16), e8,
                          preferred_element_type=F32)
            lhs = jnp.concatenate([rep[i:i + 1, :] * mask for i in range(4)], axis=0).astype(BF16)
            out = jnp.dot(lhs, pltpu.bitcast(stage[...], BF16), preferred_element_type=F32)
            for u in range(2):
                r0 = u * SUBLANES
                peers.append(out[r0:r0 + SUBLANES, u * LANES:(u + 1) * LANES]
                             + out[2 * SUBLANES + r0:3 * SUBLANES + r0, u * LANES:(u + 1) * LANES])
        rows = pl.ds(g0, SUBLANES)
        for q in range(SUBLANES):
            cols = slice(q * LANES, (q + 1) * LANES)
            piece = jnp.concatenate([peer[q:q + 1, :] for peer in peers], axis=0)
            o_ref[rows, cols] = x1_ref[rows, cols] + gt2[:, cols] * piece

    lax.fori_loop(0, tb // (PEER_GROUPS_V * SUBLANES), group, 0)


def _peer_v(idx, w, tab, x1, gt2, blocks_per_batch):
    n, d = x1.shape
    tb = TB_PEER
    q = np.arange(SUBLANES)
    piece = 2 * (q % ROWS_PER_EXPERT) + q // ROWS_PER_EXPERT
    lane = np.arange(PEER_SLOTS * SUBLANES)
    mask = jnp.asarray((lane[None, :] % SUBLANES == piece[:, None]).astype(np.float32))
    e8 = jnp.asarray(np.arange(PEER_SLOTS)[:, None] == lane[None, :] // SUBLANES, BF16)
    return pl.pallas_call(
        _peer_v_kernel,
        grid=(n // tb,),
        in_specs=[
            pl.BlockSpec((tb, PEER_SLOTS), lambda i: (i, 0), memory_space=pltpu.SMEM),
            pl.BlockSpec((tb, PEER_SLOTS), lambda i: (i, 0)),
            pl.BlockSpec(tab.shape, lambda i: (0, 0), pipeline_mode=pl.Buffered(1)),
            pl.BlockSpec((tb, d), lambda i: (i, 0)),
            pl.BlockSpec((1, 1, d), lambda i: (i // blocks_per_batch, 0, 0)),
            pl.BlockSpec(e8.shape, lambda i: (0, 0)),
            pl.BlockSpec(mask.shape, lambda i: (0, 0)),
        ],
        out_specs=pl.BlockSpec((tb, d), lambda i: (i, 0)),
        out_shape=jax.ShapeDtypeStruct((n, d), F32),
        scratch_shapes=[pltpu.VMEM((STAGE_ROWS, 2 * LANES), jnp.uint32) for _ in range(PEER_GROUPS_V * PV_PAIRS)],
        compiler_params=pltpu.CompilerParams(
            dimension_semantics=("arbitrary",), vmem_limit_bytes=VMEM_LIMIT),
        name="peer_v",
    )(idx, w, tab, x1, gt2, e8, mask)


def _layer(x, mod, g_norm1, g_norm2, w_in, q_norm_g, k_norm_g, conv_w, conv_b, conv_ln_g, conv_ln_b,
           rel_bias, g_out_conv, g_out_attn, w_out, w_peer_q, peer_k1, peer_k2, peer_u, peer_v):
    bsz, s, d = x.shape
    n = bsz * s
    mod3 = mod.reshape(bsz, 6, d)
    row = lambda a: a.reshape(1, -1)

    c0 = 2 * CONV_CH
    c1 = c0 + 3 * ATT_W
    c2 = c1 + IDX_HEADS * IDX_DIM
    c3 = c2 + IDX_DIM
    wa = w_in[:, :c0].astype(BF16)
    wqkv = w_in[:, c0:c1].astype(BF16)
    widx = jnp.concatenate([w_in[:, c1:c2], w_in[:, c2:c3], w_in[:, c2:c3]], axis=1).astype(BF16)
    wwi = jnp.zeros((SUBLANES, d), F32).at[:IDX_HEADS].set(w_in[:, c3:c3 + IDX_HEADS].T).astype(BF16)
    head = np.arange(ATT_W) // HEAD_DIM
    e2 = jnp.asarray(head[:, None] == head[None, :], BF16)

    conv_n, q, k, vt, qi, ki, wit = _inproj(
        x, mod3, row(g_norm1), wa, wqkv, widx, wwi,
        row(jnp.tile(q_norm_g, ATT_HEADS)), row(jnp.tile(k_norm_g, ATT_HEADS)), e2,
        conv_w.reshape(CONV_WIDTH, CONV_CH), row(conv_b), row(conv_ln_g), row(conv_ln_b), row(g_out_conv))
    attn_n = _attn(q, k, vt, qi, ki, wit, rel_bias, row(g_out_attn))

    x1, h2, idx, gate = _mid(
        x, conv_n, attn_n, mod3, w_out[:CONV_CH].astype(BF16), w_out[CONV_CH:].astype(BF16), row(g_norm2),
        w_peer_q.astype(BF16), peer_k1.astype(BF16), peer_k2.astype(BF16))

    idx = idx.reshape(n, PEER_SLOTS)
    w = _peer_u(idx, _pack_table(peer_u), h2.reshape(n, d), gate.reshape(n, PEER_SLOTS))
    out = _peer_v(idx, w, _pack_table(peer_v), x1.reshape(n, d), mod3[:, 5:6, :], s // TB_PEER)
    return out.reshape(bsz, s, d)


def kernel(x, c, w_ada, b_ada, g_norm1, g_norm2, w_in, q_norm_g, k_norm_g, conv_w, conv_b, conv_ln_g,
           conv_ln_b, rel_bias, g_out_conv, g_out_attn, w_out, w_peer_q, peer_k1, peer_k2, peer_u, peer_v):
    depth = w_ada.shape[0]
    for l in range(depth):
        mod = _ada(c, w_ada[l], b_ada[l])
        x = _layer(x, mod, g_norm1[l], g_norm2[l], w_in[l], q_norm_g[l], k_norm_g[l], conv_w[l], conv_b[l],
                   conv_ln_g[l], conv_ln_b[l], rel_bias, g_out_conv[l], g_out_attn[l], w_out[l],
                   w_peer_q[l], peer_k1[l], peer_k2[l], peer_u[l], peer_v[l])
    return x
```

```python
import functools
import math

import numpy as np
import jax
import jax.numpy as jnp
from jax import lax
from jax.experimental import pallas as pl
from jax.experimental.pallas import tpu as pltpu

F32 = jnp.float32
BF16 = jnp.bfloat16
I32 = jnp.int32
HIGHEST = lax.Precision.HIGHEST

D_MODEL = 1024
CHUNK = 64
CONV_CH = 512
CONV_WIDTH = 31
ATT_HEADS = 8
HEAD_DIM = 64
ATT_W = ATT_HEADS * HEAD_DIM
IDX_HEADS = 4
IDX_DIM = 64
IDX_SCALE = (IDX_HEADS * IDX_DIM) ** -0.5
TOPK_MAX = 256
REL_BUCKETS = 32
REL_MAX_DIST = 128
PEER_HEADS = 8
N_KEYS = 128
N_EXPERTS = N_KEYS * N_KEYS
PEER_TOPK = 16
PEER_SLOTS = PEER_HEADS * PEER_TOPK
EPS = 1e-6
LOG2E = math.log2(math.e)

LANES = 128
SUBLANES = 8
PACK = 16
VMEM_LIMIT = 56 * 1024 * 1024

TM_IN = 512
CONV_ROWS = 64
HALO = 32
TQ = 256
KT = 256
TM_MID = 512
TB_PEER = 256
PEER_GROUPS = 2
PEER_GROUPS_V = 4
HALF = D_MODEL // 2
ROWS_PER_EXPERT = HALF // LANES

NT_DIMS = (((1,), (1,)), ((), ()))

_NEG_INF_BITS = int(np.array(-np.inf, np.float32).view(np.int32))
KEY_NEG_INF = _NEG_INF_BITS ^ 0x7FFFFFFF
MIN_NORMAL_BITS = 0x00800000
BAND = MIN_NORMAL_BITS >> 16
INT_MIN = -(2 ** 31)


def _nt(a, b, precision=None):
    return lax.dot_general(a, b, NT_DIMS, precision=precision, preferred_element_type=F32)


def _ada_kernel(c_ref, w_ref, b_ref, o_ref):
    a = jax.nn.silu(c_ref[...])
    o_ref[...] = jnp.dot(a, w_ref[...], precision=HIGHEST, preferred_element_type=F32) + b_ref[...]


def _ada(c, w_ada, b_ada):
    bsz, d = c.shape
    return pl.pallas_call(
        _ada_kernel,
        grid=(6,),
        in_specs=[
            pl.BlockSpec((bsz, d), lambda j: (0, 0)),
            pl.BlockSpec((d, d), lambda j: (0, j)),
            pl.BlockSpec((1, d), lambda j: (0, j)),
        ],
        out_specs=pl.BlockSpec((bsz, d), lambda j: (0, j)),
        out_shape=jax.ShapeDtypeStruct((bsz, 6 * d), F32),
        name="ada",
    )(c, w_ada, b_ada.reshape(1, 6 * d))


def _inproj_kernel(x_ref, mod_ref, g1_ref, wa_ref, wqkv_ref, widx_ref, wwi_ref, gq_ref, gk_ref, e2_ref,
                   cw_ref, cb_ref, lng_ref, lnb_ref, goc_ref,
                   conv_ref, q_ref, k_ref, vt_ref, qi_ref, ki_ref, wit_ref, ubuf):
    j = pl.program_id(1)
    tm = x_ref.shape[1]
    x = x_ref[0]
    sh1 = mod_ref[0, 0:1, :]
    sc1 = mod_ref[0, 1:2, :]
    r = lax.rsqrt(jnp.mean(x * x, axis=-1, keepdims=True) + EPS)
    h = (x * r) * g1_ref[...] * (1.0 + sc1) + sh1
    hb = h.astype(BF16)

    pa = jnp.dot(hb, wa_ref[...], preferred_element_type=F32)
    u = pa[:, :CONV_CH] * jax.nn.sigmoid(pa[:, CONV_CH:])

    @pl.when(j == 0)
    def _():
        ubuf[0:HALO, :] = jnp.zeros((HALO, CONV_CH), F32)

    ubuf[HALO:HALO + tm, :] = u
    first = HALO - (CONV_WIDTH - 1)
    for rb in range(tm // CONV_ROWS):
        base = rb * CONV_ROWS
        acc = jnp.zeros((CONV_ROWS, CONV_CH), F32) + cb_ref[...]
        for phase in range(SUBLANES):
            taps = [t for t in range(CONV_WIDTH) if (first + t) % SUBLANES == phase]
            span = (first + taps[-1]) - phase + CONV_ROWS
            slab = ubuf[base + phase:base + phase + span, :]
            for t in taps:
                off = first + t - phase
                acc = acc + cw_ref[t:t + 1, :] * slab[off:off + CONV_ROWS, :]
        mu = jnp.mean(acc, axis=-1, keepdims=True)
        xc = acc - mu
        y = xc * lax.rsqrt(jnp.mean(xc * xc, axis=-1, keepdims=True) + EPS)
        y = jax.nn.silu(y * lng_ref[...] + lnb_ref[...])
        y = y * lax.rsqrt(jnp.mean(y * y, axis=-1, keepdims=True) + EPS) * goc_ref[...]
        conv_ref[0, base:base + CONV_ROWS, :] = y.astype(BF16)
    ubuf[0:HALO, :] = ubuf[tm:tm + HALO, :]

    pq = jnp.dot(hb, wqkv_ref[...], preferred_element_type=F32)
    q = pq[:, :ATT_W]
    k = pq[:, ATT_W:2 * ATT_W]
    v = pq[:, 2 * ATT_W:]
    e2 = e2_ref[...]

    def head_mean(sq):
        hi = sq.astype(BF16)
        lo = (sq - hi.astype(F32)).astype(BF16)
        return (jnp.dot(hi, e2, preferred_element_type=F32)
                + jnp.dot(lo, e2, preferred_element_type=F32)) * (1.0 / HEAD_DIM)

    qs = head_mean(q * q)
    ks = head_mean(k * k)
    qn = q * lax.rsqrt(qs + EPS) * gq_ref[...] * (HEAD_DIM ** -0.5 * LOG2E)
    kn = k * lax.rsqrt(ks + EPS) * gk_ref[...]
    lane = lax.broadcasted_iota(I32, (tm, LANES), 1)
    low = lane < HEAD_DIM
    for p in range(ATT_HEADS // 2):
        slab = qn[:, p * LANES:(p + 1) * LANES]
        q_ref[0, 2 * p] = jnp.where(low, slab, 0.0).T.astype(BF16)
        q_ref[0, 2 * p + 1] = jnp.where(low, 0.0, slab).T.astype(BF16)
        k_ref[0, p] = kn[:, p * LANES:(p + 1) * LANES].astype(BF16)
    vt_ref[0] = v.T.astype(BF16)

    pc = jnp.dot(hb, widx_ref[...], preferred_element_type=F32)
    for p in range(IDX_HEADS // 2):
        slab = pc[:, p * LANES:(p + 1) * LANES]
        qi_ref[0, 2 * p] = jnp.where(low, slab, 0.0).T.astype(BF16)
        qi_ref[0, 2 * p + 1] = jnp.where(low, 0.0, slab).T.astype(BF16)
    ki_ref[0] = pc[:, 2 * LANES:3 * LANES].astype(BF16)
    wit_ref[0] = _nt(wwi_ref[...], hb) * IDX_SCALE


def _inproj(x, mod3, g1, wa, wqkv, widx, wwi, gq, gk, e2, cw, cb, lng, lnb, goc):
    bsz, s, d = x.shape
    tm = TM_IN
    nt = s // tm
    full = lambda shape: pl.BlockSpec(shape, lambda b, j: (0,) * len(shape))
    out_shape = (
        jax.ShapeDtypeStruct((bsz, s, CONV_CH), BF16),
        jax.ShapeDtypeStruct((bsz, ATT_HEADS, LANES, s), BF16),
        jax.ShapeDtypeStruct((bsz, ATT_HEADS // 2, s, LANES), BF16),
        jax.ShapeDtypeStruct((bsz, ATT_W, s), BF16),
        jax.ShapeDtypeStruct((bsz, IDX_HEADS, LANES, s), BF16),
        jax.ShapeDtypeStruct((bsz, s, LANES), BF16),
        jax.ShapeDtypeStruct((bsz, SUBLANES, s), F32),
    )
    out_specs = (
        pl.BlockSpec((1, tm, CONV_CH), lambda b, j: (b, j, 0)),
        pl.BlockSpec((1, ATT_HEADS, LANES, tm), lambda b, j: (b, 0, 0, j)),
        pl.BlockSpec((1, ATT_HEADS // 2, tm, LANES), lambda b, j: (b, 0, j, 0)),
        pl.BlockSpec((1, ATT_W, tm), lambda b, j: (b, 0, j)),
        pl.BlockSpec((1, IDX_HEADS, LANES, tm), lambda b, j: (b, 0, 0, j)),
        pl.BlockSpec((1, tm, LANES), lambda b, j: (b, j, 0)),
        pl.BlockSpec((1, SUBLANES, tm), lambda b, j: (b, 0, j)),
    )
    return pl.pallas_call(
        _inproj_kernel,
        grid=(bsz, nt),
        in_specs=[
            pl.BlockSpec((1, tm, d), lambda b, j: (b, j, 0)),
            pl.BlockSpec((1, 6, d), lambda b, j: (b, 0, 0)),
            full(g1.shape), full(wa.shape), full(wqkv.shape), full(widx.shape), full(wwi.shape),
            full(gq.shape), full(gk.shape), full(e2.shape),
            full(cw.shape), full(cb.shape), full(lng.shape), full(lnb.shape), full(goc.shape),
        ],
        out_specs=out_specs,
        out_shape=out_shape,
        scratch_shapes=[pltpu.VMEM((tm + HALO, CONV_CH), F32)],
        compiler_params=pltpu.CompilerParams(
            dimension_semantics=("arbitrary", "arbitrary"), vmem_limit_bytes=VMEM_LIMIT),
        name="inproj",
    )(x, mod3, g1, wa, wqkv, widx, wwi, gq, gk, e2, cw, cb, lng, lnb, goc)


def _t5_bucket_np(rel):
    half = REL_BUCKETS // 2
    max_exact = half // 2
    ret = np.where(rel > 0, half, 0)
    n = np.abs(rel)
    nf = np.maximum(n, 1).astype(np.float64)
    large = max_exact + (np.log(nf / max_exact) / math.log(REL_MAX_DIST / max_exact)
                         * (half - max_exact)).astype(np.int32)
    large = np.minimum(large, half - 1)
    return (ret + np.where(n < max_exact, n, large)).astype(np.int32)


NEAR = REL_MAX_DIST + TQ
FAR_BUCKET = REL_BUCKETS // 2 - 1


def _near_buckets():
    r = np.arange(NEAR)[:, None]
    t = np.arange(TQ)[None, :]
    return _t5_bucket_np(r - REL_MAX_DIST - t)


def _attn_kernel(q_ref, k_ref, vt_ref, qi_ref, ki_ref, wit_ref, bkt_ref, rb_ref, goa_ref, o_ref,
                 keys_s, khi_s, maskb_s, relb_s, ot_s, m_s, mo_s, l_s, lg_s, *, nsel):
    b = pl.program_id(0)
    i = pl.program_id(1)
    t0 = i * TQ
    n_tiles = i + 1
    neg_inf = F32(-jnp.inf)

    @pl.when((b == 0) & (i == 0))
    def _():
        bk = bkt_ref[...]
        for h in range(ATT_HEADS):
            far = rb_ref[FAR_BUCKET, h]
            acc = jnp.zeros((NEAR, TQ), F32)
            for bb in range(REL_BUCKETS):
                acc = jnp.where(bk == bb, (rb_ref[bb, h] - far) * LOG2E, acc)
            relb_s[h, 0:2 * KT - NEAR, :] = jnp.zeros((2 * KT - NEAR, TQ), F32)
            relb_s[h, 2 * KT - NEAR:2 * KT, :] = acc

    qpos = t0 + lax.broadcasted_iota(I32, (1, TQ), 1)
    limit = (qpos // CHUNK + 1) * CHUNK
    row_iota = lax.broadcasted_iota(I32, (KT, TQ), 0)

    def tile_start(jt):
        return pl.multiple_of(jt * KT, KT)

    wi = wit_ref[0]

    def p1(jt, c):
        ks = tile_start(jt)
        kit = ki_ref[0, pl.ds(ks, KT), :]
        acc = jnp.zeros((KT, TQ), F32)
        for h in range(IDX_HEADS):
            lgt = jnp.dot(kit, qi_ref[0, h], preferred_element_type=F32)
            acc = acc + jnp.maximum(lgt, 0.0) * wi[h:h + 1, :]
        sc = jnp.where(row_iota + ks < limit, acc, neg_inf)
        bits = pltpu.bitcast(sc, I32)
        keys_s[pl.ds(ks, KT), :] = bits ^ (lax.shift_right_arithmetic(bits, 31) & 0x7FFFFFFF)
        top = jnp.where((bits & 0x7FFFFFFF) < MIN_NORMAL_BITS, 0, bits & I32(-65536))
        khi_s[pl.ds(ks, KT), :] = pltpu.bitcast(top, F32).astype(BF16)
        return c

    lax.fori_loop(0, n_tiles, p1, 0)

    def count(pred):
        def body(jt, acc):
            ks = tile_start(jt)
            m = pred(keys_s[pl.ds(ks, KT), :], ks).astype(F32)
            return acc + jnp.sum(m.reshape(KT // SUBLANES, SUBLANES, TQ), axis=0)
        acc = lax.fori_loop(0, n_tiles, body, jnp.zeros((SUBLANES, TQ), F32))
        return jnp.sum(acc, axis=0, keepdims=True)

    def count_top(cand_hi):
        cand_hi = jnp.where((cand_hi > 0) & (cand_hi < BAND), BAND, cand_hi)
        pat = cand_hi ^ (lax.shift_right_arithmetic(cand_hi, 15) & 0x7FFF)
        cb = pltpu.bitcast(lax.shift_left(pat, 16), F32).astype(BF16)
        one = jnp.ones((KT, TQ), BF16)
        zero = jnp.zeros((KT, TQ), BF16)

        def body(jt, acc):
            ks = tile_start(jt)
            hit = jnp.where(khi_s[pl.ds(ks, KT), :] >= cb, one, zero)
            parts = [hit[r:r + PACK, :] for r in range(0, KT, PACK)]
            while len(parts) > 1:
                parts = [parts[k] + parts[k + 1] for k in range(0, len(parts), 2)]
            return acc + parts[0].astype(F32)
        acc = lax.fori_loop(0, n_tiles, body, jnp.zeros((PACK, TQ), F32))
        return jnp.sum(acc, axis=0, keepdims=True)

    def select_thr():
        c0 = count_top(jnp.zeros((1, TQ), I32))
        ok = c0 >= nsel
        t = jnp.where(ok, 0, -(2 ** 15)).astype(I32)
        cnt = jnp.where(ok, c0, F32(2 ** 30))

        def top_step(it, carry):
            t, cnt = carry
            cand = t + lax.shift_left(I32(1), I32(14) - it)
            c = count_top(cand)
            ok = c >= nsel
            return jnp.where(ok, cand, t), jnp.where(ok, c, cnt)

        t, cnt = lax.fori_loop(0, 15, top_step, (t, cnt))

        band = jnp.where(jnp.abs(2 * t + 1) < 2 * BAND, 1.0, 0.0)

        def band_counts():
            c_zero = count(lambda kt, ks: kt >= 0)
            c_tiny = count(lambda kt, ks: kt >= 1)
            c_norm = count(lambda kt, ks: kt >= MIN_NORMAL_BITS)
            return c_zero, jnp.where(c_tiny == c_norm, jnp.where(c_zero >= nsel, 1.0, 0.0), 0.0)

        c_zero, plain = lax.cond(jnp.max(band) > 0.0, band_counts,
                                 lambda: (jnp.zeros((1, TQ), F32), jnp.ones((1, TQ), F32)))
        settled = band * plain > 0.0
        hard = band * (1.0 - plain) > 0.0
        t = jnp.where(settled, 0, lax.shift_left(jnp.where(hard, -BAND, t), 16))
        cnt = jnp.where(settled, c_zero, jnp.where(hard, F32(2 ** 30), cnt))
        first_bit = jnp.where(jnp.max(band * (1.0 - plain)) > 0.0, 23, 15).astype(I32)

        def low_cond(carry):
            bit, _, cnt = carry
            return (bit >= 0) & (jnp.max(jnp.where(settled, F32(nsel), cnt)) > nsel)

        def low_step(carry):
            bit, t, cnt = carry
            cand = t + lax.shift_left(I32(1), bit)
            c = count(lambda kt, ks: kt >= cand)
            ok = c >= nsel
            return bit - 1, jnp.where(ok, cand, t), jnp.where(ok, c, cnt)

        _, t, cnt = lax.while_loop(low_cond, low_step, (first_bit, t, cnt))
        return t, cnt

    thr, c_ge = lax.cond(
        i > 0, select_thr,
        lambda: (jnp.full((1, TQ), KEY_NEG_INF + 1, I32), jnp.full((1, TQ), nsel, F32)))

    def mask_plain():
        def p3(jt, c):
            ks = tile_start(jt)
            maskb_s[pl.ds(ks, KT), :] = jnp.where(keys_s[pl.ds(ks, KT), :] >= thr, 0.0, neg_inf)
            return c
        lax.fori_loop(0, n_tiles, p3, 0)
        return I32(0)

    def mask_ties():
        quota = nsel - count(lambda kt, ks: kt > thr)
        tri = jnp.where(lax.broadcasted_iota(I32, (KT, KT), 0) >= lax.broadcasted_iota(I32, (KT, KT), 1),
                        1.0, 0.0).astype(BF16)

        def p3(jt, before):
            ks = tile_start(jt)
            kt = keys_s[pl.ds(ks, KT), :]
            eq = kt == thr
            rank = before + jnp.dot(tri, jnp.where(eq, 1.0, 0.0).astype(BF16), preferred_element_type=F32)
            tied = jnp.where(eq, jnp.where(rank <= quota, 0.0, neg_inf), neg_inf)
            maskb_s[pl.ds(ks, KT), :] = jnp.where(kt > thr, 0.0, tied)
            return rank[KT - 1:KT, :]

        lax.fori_loop(0, n_tiles, p3, jnp.zeros((1, TQ), F32))
        return I32(0)

    lax.cond(jnp.max(c_ge) > nsel, mask_ties, mask_plain)

    m_s[...] = jnp.full((ATT_HEADS, TQ), neg_inf, F32)
    l_s[...] = jnp.zeros((ATT_HEADS, TQ), F32)
    ot_s[...] = jnp.zeros((ATT_W, TQ), F32)

    def att_tile(jt, near):
        ks = tile_start(jt)
        mb = maskb_s[pl.ds(ks, KT), :]
        for h in range(ATT_HEADS):
            l = jnp.dot(k_ref[0, h // 2, pl.ds(ks, KT), :], q_ref[0, h], preferred_element_type=F32) + mb
            if near:
                off = pl.multiple_of((jt - (n_tiles - 2)) * KT, KT)
                l = l + relb_s[h, pl.ds(off, KT), :]
            lg_s[h] = l
            m_old = m_s[h:h + 1, :]
            mo_s[h:h + 1, :] = m_old
            m_s[h:h + 1, :] = jnp.maximum(m_old, jnp.max(l, axis=0, keepdims=True))
        for h in range(ATT_HEADS):
            m_new = m_s[h:h + 1, :]
            m_ref = jnp.where(m_new == neg_inf, 0.0, m_new)
            alpha = jnp.exp2(mo_s[h:h + 1, :] - m_ref)
            p = jnp.exp2(lg_s[h] - m_ref)
            l_s[h:h + 1, :] = alpha * l_s[h:h + 1, :] + jnp.sum(p, axis=0, keepdims=True)
            vt = vt_ref[0, h * HEAD_DIM:(h + 1) * HEAD_DIM, pl.ds(ks, KT)]
            rows = slice(h * HEAD_DIM, (h + 1) * HEAD_DIM)
            ot_s[rows, :] = ot_s[rows, :] * alpha + jnp.dot(vt, p.astype(BF16), preferred_element_type=F32)

    def far_tile(jt, c):
        att_tile(jt, False)
        return c

    def near_tile(jt, c):
        att_tile(jt, True)
        return c

    n_far = jnp.maximum(n_tiles - 2, 0)
    lax.fori_loop(0, n_far, far_tile, 0)
    lax.fori_loop(n_far, n_tiles, near_tile, 0)
    for h in range(ATT_HEADS):
        rows = slice(h * HEAD_DIM, (h + 1) * HEAD_DIM)
        ot_s[rows, :] = ot_s[rows, :] / l_s[h:h + 1, :]

    ot = ot_s[...]
    ms = jnp.mean(ot * ot, axis=0, keepdims=True)
    y = (ot * lax.rsqrt(ms + EPS)).T * goa_ref[...]
    o_ref[0] = y.astype(BF16)


def _attn(q, k, vt, qi, ki, wit, rel_bias, goa):
    bsz, _, _, s = q.shape
    nsel = min(TOPK_MAX, s // 4)
    assert nsel == TQ and s % TQ == 0, "attention kernel assumes TOPK_MAX-sized query blocks"
    bkt = jnp.asarray(_near_buckets())
    kern = functools.partial(_attn_kernel, nsel=nsel)
    return pl.pallas_call(
        kern,
        grid=(bsz, s // TQ),
        in_specs=[
            pl.BlockSpec((1, ATT_HEADS, LANES, TQ), lambda b, i: (b, 0, 0, i)),
            pl.BlockSpec((1, ATT_HEADS // 2, s, LANES), lambda b, i: (b, 0, 0, 0)),
            pl.BlockSpec((1, ATT_W, s), lambda b, i: (b, 0, 0)),
            pl.BlockSpec((1, IDX_HEADS, LANES, TQ), lambda b, i: (b, 0, 0, i)),
            pl.BlockSpec((1, s, LANES), lambda b, i: (b, 0, 0)),
            pl.BlockSpec((1, SUBLANES, TQ), lambda b, i: (b, 0, i)),
            pl.BlockSpec((NEAR, TQ), lambda b, i: (0, 0)),
            pl.BlockSpec(memory_space=pltpu.SMEM),
            pl.BlockSpec((1, ATT_W), lambda b, i: (0, 0)),
        ],
        out_specs=pl.BlockSpec((1, TQ, ATT_W), lambda b, i: (b, i, 0)),
        out_shape=jax.ShapeDtypeStruct((bsz, s, ATT_W), BF16),
        scratch_shapes=[
            pltpu.VMEM((s, TQ), I32),
            pltpu.VMEM((s, TQ), BF16),
            pltpu.VMEM((s, TQ), F32),
            pltpu.VMEM((ATT_HEADS, 2 * KT, TQ), F32),
            pltpu.VMEM((ATT_W, TQ), F32),
            pltpu.VMEM((ATT_HEADS, TQ), F32),
            pltpu.VMEM((ATT_HEADS, TQ), F32),
            pltpu.VMEM((ATT_HEADS, TQ), F32),
            pltpu.VMEM((ATT_HEADS, KT, TQ), F32),
        ],
        compiler_params=pltpu.CompilerParams(
            dimension_semantics=("arbitrary", "arbitrary"), vmem_limit_bytes=VMEM_LIMIT),
        name="attn",
    )(q, k, vt, qi, ki, wit, bkt, rel_bias, goa)


def _topk_rows(s, payload, k, order=None):
    rows = lax.broadcasted_iota(I32, s.shape, 0).astype(F32) if order is None else order
    vals, pays = [], []
    for _ in range(k):
        m = jnp.max(s, axis=0, keepdims=True)
        ix = jnp.min(jnp.where(s == m, rows, _NO_ROW), axis=0, keepdims=True)
        hit = rows == ix
        if payload is None:
            pays.append(ix)
        else:
            pays.append(jnp.max(jnp.where(hit, payload, -1.0), axis=0, keepdims=True))
        vals.append(m)
        s = jnp.where(hit, -jnp.inf, s)
    return jnp.concatenate(vals, axis=0), jnp.concatenate(pays, axis=0)


_NO_ROW = 1e9

_PAIR_GROUPS = (
    ((0, 0, 8, 0),), ((0, 8, 8, 0),), ((1, 0, 8, 0),),
    ((2, 0, 5, 0), (5, 0, 2, 5)), ((3, 0, 4, 0), (4, 0, 3, 4)), ((6, 0, 2, 0), (7, 0, 2, 2)),
)
assert sorted((a, b) for g in _PAIR_GROUPS for a, b0, nb, _ in g for b in range(b0, b0 + nb)) == sorted(
    (a, b) for a in range(PEER_TOPK // 2) for b in range(PEER_TOPK // (a + 1)))


def _mid_kernel(x_ref, cn_ref, an_ref, mod_ref, wo1_ref, wo2_ref, g2_ref, wpq_ref, k1_ref, k2_ref,
                x1_ref, h2_ref, idx_ref, gate_ref, qq_s, idt_s, gt_s):
    tm = x_ref.shape[1]
    x = x_ref[0]
    gt1 = mod_ref[0, 2:3, :]
    sh2 = mod_ref[0, 3:4, :]
    sc2 = mod_ref[0, 4:5, :]
    proj = (jnp.dot(cn_ref[0], wo1_ref[...], preferred_element_type=F32)
            + jnp.dot(an_ref[0], wo2_ref[...], preferred_element_type=F32))
    x1 = x + gt1 * proj
    x1_ref[0] = x1
    r = lax.rsqrt(jnp.mean(x1 * x1, axis=-1, keepdims=True) + EPS)
    h2 = (x1 * r) * g2_ref[...] * (1.0 + sc2) + sh2
    h2_ref[0] = h2
    qq_s[...] = jnp.dot(h2.astype(BF16), wpq_ref[...], preferred_element_type=F32).astype(BF16)

    def route_unit(hh, lt):
        rows = pl.ds(lt * LANES, LANES)
        q1 = qq_s[rows, pl.ds(pl.multiple_of(hh * 2 * N_KEYS, LANES), N_KEYS)]
        q2 = qq_s[rows, pl.ds(pl.multiple_of(hh * 2 * N_KEYS + N_KEYS, LANES), N_KEYS)]
        v1, i1 = _topk_rows(_nt(k1_ref[hh], q1), None, PEER_TOPK)
        v2, i2 = _topk_rows(_nt(k2_ref[hh], q2), None, PEER_TOPK)
        sub = lax.broadcasted_iota(I32, (SUBLANES, LANES), 0)
        subf = sub.astype(F32)
        cands, cidxs, flats = [], [], []
        for group in _PAIR_GROUPS:
            val = jnp.full((SUBLANES, LANES), -jnp.inf, F32)
            cid = jnp.zeros((SUBLANES, LANES), F32)
            flat = jnp.full((SUBLANES, LANES), _NO_ROW, F32)
            for a, b0, nb, off in group:
                v2s, i2s = v2[b0:b0 + SUBLANES, :], i2[b0:b0 + SUBLANES, :]
                if off:
                    v2s, i2s = pltpu.roll(v2s, off, 0), pltpu.roll(i2s, off, 0)
                inside = lambda new, old: jnp.where(sub < off + nb, jnp.where(sub >= off, new, old), old)
                val = inside(v1[a:a + 1, :] + v2s, val)
                cid = inside(i1[a:a + 1, :] * float(N_KEYS) + i2s, cid)
                flat = inside(subf + float(a * PEER_TOPK + b0 - off), flat)
            cands.append(val)
            cidxs.append(cid)
            flats.append(flat)
        half = PEER_TOPK // 2
        cands.append(v1[half:, :] + v2[0:1, :])
        cidxs.append(i1[half:, :] * float(N_KEYS) + i2[0:1, :])
        flats.append((subf + float(half)) * float(PEER_TOPK))
        best, experts = _topk_rows(jnp.concatenate(cands, axis=0), jnp.concatenate(cidxs, axis=0), PEER_TOPK,
                                   order=jnp.concatenate(flats, axis=0))
        e = jnp.exp(best - best[0:1, :])
        g = e / jnp.sum(e, axis=0, keepdims=True)
        slots = pl.ds(pl.multiple_of(hh * PEER_TOPK, PEER_TOPK), PEER_TOPK)
        cols = pl.ds(lt * LANES, LANES)
        idt_s[slots, cols] = experts * float(ROWS_PER_EXPERT)
        gt_s[slots, cols] = g

    def route(hh, c):
        for lt in range(tm // LANES):
            route_unit(hh, lt)
        return c

    lax.fori_loop(0, PEER_HEADS, route, 0)
    idx_ref[0] = idt_s[...].T.astype(I32)
    gate_ref[0] = gt_s[...].T


def _mid(x, cn, an, mod3, wo1, wo2, g2, wpq, k1, k2):
    bsz, s, d = x.shape
    tm = TM_MID
    full = lambda shape: pl.BlockSpec(shape, lambda b, j: (0,) * len(shape))
    tok = lambda w: pl.BlockSpec((1, tm, w), lambda b, j: (b, j, 0))
    return pl.pallas_call(
        _mid_kernel,
        grid=(bsz, s // tm),
        in_specs=[tok(d), tok(CONV_CH), tok(ATT_W), pl.BlockSpec((1, 6, d), lambda b, j: (b, 0, 0)),
                  full(wo1.shape), full(wo2.shape), full(g2.shape), full(wpq.shape), full(k1.shape), full(k2.shape)],
        out_specs=(tok(d), tok(d), tok(PEER_SLOTS), tok(PEER_SLOTS)),
        out_shape=(
            jax.ShapeDtypeStruct((bsz, s, d), F32),
            jax.ShapeDtypeStruct((bsz, s, d), F32),
            jax.ShapeDtypeStruct((bsz, s, PEER_SLOTS), I32),
            jax.ShapeDtypeStruct((bsz, s, PEER_SLOTS), F32),
        ),
        scratch_shapes=[
            pltpu.VMEM((tm, PEER_HEADS * 2 * N_KEYS), BF16),
            pltpu.VMEM((PEER_SLOTS, tm), F32),
            pltpu.VMEM((PEER_SLOTS, tm), F32),
        ],
        compiler_params=pltpu.CompilerParams(
            dimension_semantics=("arbitrary", "arbitrary"), vmem_limit_bytes=VMEM_LIMIT),
        name="mid",
    )(x, cn, an, mod3, wo1, wo2, g2, wpq, k1, k2)


def _pack_table(t):
    tb = t.astype(BF16)
    lo = lax.bitcast_convert_type(tb[:, :HALF], jnp.uint16).astype(jnp.uint32)
    hi = lax.bitcast_convert_type(tb[:, HALF:], jnp.uint16).astype(jnp.uint32)
    return (lo | (hi << 16)).reshape(t.shape[0] * ROWS_PER_EXPERT, LANES)


def _unpack(w):
    lo = pltpu.bitcast(lax.shift_left(w, jnp.uint32(16)), F32)
    hi = pltpu.bitcast(w & jnp.uint32(0xFFFF0000), F32)
    return lo, hi


def _gather_row(tab_ref, row):
    return tab_ref[pl.ds(pl.multiple_of(row, ROWS_PER_EXPERT), ROWS_PER_EXPERT), :]


STAGE_ROWS = PEER_SLOTS * ROWS_PER_EXPERT


def _token_pieces(xg, r, first, count):
    return jnp.concatenate(
        [xg[r:r + 1, (first + q) * LANES:(first + q + 1) * LANES] for q in range(count)], axis=0)


def _peer_u_kernel(idx_ref, tab_ref, x_ref, gate_ref, w_ref, *stages):
    tb = w_ref.shape[0]
    sub = lax.broadcasted_iota(I32, (SUBLANES, PEER_SLOTS), 0)

    def group(g, c):
        for k in range(PEER_GROUPS):
            eight(pl.multiple_of((g * PEER_GROUPS + k) * SUBLANES, SUBLANES), stages[k * SUBLANES:(k + 1) * SUBLANES])
        return c

    def eight(t0, stages):
        act8 = jnp.zeros((SUBLANES, PEER_SLOTS), F32)
        xg = x_ref[pl.ds(t0, SUBLANES), :]
        xlo, xhi, rows_t = [], [], []
        for r in range(SUBLANES):
            xlo.append(jnp.concatenate([_token_pieces(xg, r, 0, ROWS_PER_EXPERT)] * 2, axis=0))
            xhi.append(jnp.concatenate([_token_pieces(xg, r, ROWS_PER_EXPERT, ROWS_PER_EXPERT)] * 2, axis=0))
            rows_t.append(idx_ref.at[t0 + r])
        for j in range(0, PEER_SLOTS, 2):
            for r in range(SUBLANES):
                pair = jnp.concatenate(
                    [_gather_row(tab_ref, rows_t[r][j]), _gather_row(tab_ref, rows_t[r][j + 1])], axis=0)
                lo, hi = _unpack(pair)
                row = j * ROWS_PER_EXPERT
                stages[r][row:row + 2 * ROWS_PER_EXPERT, :] = lo * xlo[r] + hi * xhi[r]
        for r in range(SUBLANES):
            stage = stages[r]
            cs = stage[pl.ds(0, PEER_SLOTS, stride=ROWS_PER_EXPERT), :]
            for q in range(1, ROWS_PER_EXPERT):
                cs = cs + stage[pl.ds(q, PEER_SLOTS, stride=ROWS_PER_EXPERT), :]
            act = jnp.sum(cs.T, axis=0, keepdims=True)
            act8 = jnp.where(sub == r, act, act8)
        rows = pl.ds(t0, SUBLANES)
        w_ref[rows, :] = gate_ref[rows, :] * jax.nn.gelu(act8)

    lax.fori_loop(0, tb // (PEER_GROUPS * SUBLANES), group, 0)


def _peer_u(idx, tab, h2, gate):
    n, d = h2.shape
    tb = TB_PEER
    return pl.pallas_call(
        _peer_u_kernel,
        grid=(n // tb,),
        in_specs=[
            pl.BlockSpec((tb, PEER_SLOTS), lambda i: (i, 0), memory_space=pltpu.SMEM),
            pl.BlockSpec(tab.shape, lambda i: (0, 0), pipeline_mode=pl.Buffered(1)),
            pl.BlockSpec((tb, d), lambda i: (i, 0)),
            pl.BlockSpec((tb, PEER_SLOTS), lambda i: (i, 0)),
        ],
        out_specs=pl.BlockSpec((tb, PEER_SLOTS), lambda i: (i, 0)),
        out_shape=jax.ShapeDtypeStruct((n, PEER_SLOTS), F32),
        scratch_shapes=[pltpu.VMEM((STAGE_ROWS, LANES), F32) for _ in range(PEER_GROUPS * SUBLANES)],
        compiler_params=pltpu.CompilerParams(
            dimension_semantics=("arbitrary",), vmem_limit_bytes=VMEM_LIMIT),
        name="peer_u",
    )(idx, tab, h2, gate)


PV_PAIRS = SUBLANES // 2


def _peer_v_kernel(idx_ref, w_ref, tab_ref, x1_ref, gt2_ref, e8_ref, mask_ref, o_ref, *stages):
    tb = idx_ref.shape[0]
    gt2 = gt2_ref[0]
    mask = mask_ref[...]
    e8 = e8_ref[...]

    def group(g, c):
        for k in range(PEER_GROUPS_V):
            eight(pl.multiple_of((g * PEER_GROUPS_V + k) * SUBLANES, SUBLANES), stages[k * PV_PAIRS:(k + 1) * PV_PAIRS])
        return c

    def eight(g0, stages):
        peers = []
        for p in range(PV_PAIRS):
            stage = stages[p]
            t0 = g0 + 2 * p
            for u in range(2):
                rows_t = idx_ref.at[t0 + u]
                for j in range(PEER_SLOTS):
                    stage[j * ROWS_PER_EXPERT:(j + 1) * ROWS_PER_EXPERT, u * LANES:(u + 1) * LANES] = (
                        _gather_row(tab_ref, rows_t[j]))
            w2 = w_ref[pl.ds(t0, 2), :]
            hi = w2.astype(BF16).astype(F32)
            rep = jnp.dot(jnp.concatenate([hi, w2 - hi], axis=0).astype(BF16), e8,
                          preferred_element_type=F32)
            lhs = jnp.concatenate([rep[i:i + 1, :] * mask for i in range(4)], axis=0).astype(BF16)
            out = jnp.dot(lhs, pltpu.bitcast(stage[...], BF16), preferred_element_type=F32)
            for u in range(2):
                r0 = u * SUBLANES
                peers.append(out[r0:r0 + SUBLANES, u * LANES:(u + 1) * LANES]
                             + out[2 * SUBLANES + r0:3 * SUBLANES + r0, u * LANES:(u + 1) * LANES])
        rows = pl.ds(g0, SUBLANES)
        for q in range(SUBLANES):
            cols = slice(q * LANES, (q + 1) * LANES)
            piece = jnp.concatenate([peer[q:q + 1, :] for peer in peers], axis=0)
            o_ref[rows, cols] = x1_ref[rows, cols] + gt2[:, cols] * piece

    lax.fori_loop(0, tb // (PEER_GROUPS_V * SUBLANES), group, 0)


def _peer_v(idx, w, tab, x1, gt2, blocks_per_batch):
    n, d = x1.shape
    tb = TB_PEER
    q = np.arange(SUBLANES)
    piece = 2 * (q % ROWS_PER_EXPERT) + q // ROWS_PER_EXPERT
    lane = np.arange(PEER_SLOTS * SUBLANES)
    mask = jnp.asarray((lane[None, :] % SUBLANES == piece[:, None]).astype(np.float32))
    e8 = jnp.asarray(np.arange(PEER_SLOTS)[:, None] == lane[None, :] // SUBLANES, BF16)
    return pl.pallas_call(
        _peer_v_kernel,
        grid=(n // tb,),
        in_specs=[
            pl.BlockSpec((tb, PEER_SLOTS), lambda i: (i, 0), memory_space=pltpu.SMEM),
            pl.BlockSpec((tb, PEER_SLOTS), lambda i: (i, 0)),
            pl.BlockSpec(tab.shape, lambda i: (0, 0), pipeline_mode=pl.Buffered(1)),
            pl.BlockSpec((tb, d), lambda i: (i, 0)),
            pl.BlockSpec((1, 1, d), lambda i: (i // blocks_per_batch, 0, 0)),
            pl.BlockSpec(e8.shape, lambda i: (0, 0)),
            pl.BlockSpec(mask.shape, lambda i: (0, 0)),
        ],
        out_specs=pl.BlockSpec((tb, d), lambda i: (i, 0)),
        out_shape=jax.ShapeDtypeStruct((n, d), F32),
        scratch_shapes=[pltpu.VMEM((STAGE_ROWS, 2 * LANES), jnp.uint32) for _ in range(PEER_GROUPS_V * PV_PAIRS)],
        compiler_params=pltpu.CompilerParams(
            dimension_semantics=("arbitrary",), vmem_limit_bytes=VMEM_LIMIT),
        name="peer_v",
    )(idx, w, tab, x1, gt2, e8, mask)


def _layer(x, mod, g_norm1, g_norm2, w_in, q_norm_g, k_norm_g, conv_w, conv_b, conv_ln_g, conv_ln_b,
           rel_bias, g_out_conv, g_out_attn, w_out, w_peer_q, peer_k1, peer_k2, peer_u, peer_v):
    bsz, s, d = x.shape
    n = bsz * s
    mod3 = mod.reshape(bsz, 6, d)
    row = lambda a: a.reshape(1, -1)

    c0 = 2 * CONV_CH
    c1 = c0 + 3 * ATT_W
    c2 = c1 + IDX_HEADS * IDX_DIM
    c3 = c2 + IDX_DIM
    wa = w_in[:, :c0].astype(BF16)
    wqkv = w_in[:, c0:c1].astype(BF16)
    widx = jnp.concatenate([w_in[:, c1:c2], w_in[:, c2:c3], w_in[:, c2:c3]], axis=1).astype(BF16)
    wwi = jnp.zeros((SUBLANES, d), F32).at[:IDX_HEADS].set(w_in[:, c3:c3 + IDX_HEADS].T).astype(BF16)
    head = np.arange(ATT_W) // HEAD_DIM
    e2 = jnp.asarray(head[:, None] == head[None, :], BF16)

    conv_n, q, k, vt, qi, ki, wit = _inproj(
        x, mod3, row(g_norm1), wa, wqkv, widx, wwi,
        row(jnp.tile(q_norm_g, ATT_HEADS)), row(jnp.tile(k_norm_g, ATT_HEADS)), e2,
        conv_w.reshape(CONV_WIDTH, CONV_CH), row(conv_b), row(conv_ln_g), row(conv_ln_b), row(g_out_conv))
    attn_n = _attn(q, k, vt, qi, ki, wit, rel_bias, row(g_out_attn))

    x1, h2, idx, gate = _mid(
        x, conv_n, attn_n, mod3, w_out[:CONV_CH].astype(BF16), w_out[CONV_CH:].astype(BF16), row(g_norm2),
        w_peer_q.astype(BF16), peer_k1.astype(BF16), peer_k2.astype(BF16))

    idx = idx.reshape(n, PEER_SLOTS)
    w = _peer_u(idx, _pack_table(peer_u), h2.reshape(n, d), gate.reshape(n, PEER_SLOTS))
    out = _peer_v(idx, w, _pack_table(peer_v), x1.reshape(n, d), mod3[:, 5:6, :], s // TB_PEER)
    return out.reshape(bsz, s, d)


def kernel(x, c, w_ada, b_ada, g_norm1, g_norm2, w_in, q_norm_g, k_norm_g, conv_w, conv_b, conv_ln_g,
           conv_ln_b, rel_bias, g_out_conv, g_out_attn, w_out, w_peer_q, peer_k1, peer_k2, peer_u, peer_v):
    depth = w_ada.shape[0]
    for l in range(depth):
        mod = _ada(c, w_ada[l], b_ada[l])
        x = _layer(x, mod, g_norm1[l], g_norm2[l], w_in[l], q_norm_g[l], k_norm_g[l], conv_w[l], conv_b[l],
                   conv_ln_g[l], conv_ln_b[l], rel_bias, g_out_conv[l], g_out_attn[l], w_out[l],
                   w_peer_q[l], peer_k1[l], peer_k2[l], peer_u[l], peer_v[l])
    return x
```

```python
import functools
import math

import numpy as np
import jax
import jax.numpy as jnp
from jax import lax
from jax.experimental import pallas as pl
from jax.experimental.pallas import tpu as pltpu

F32 = jnp.float32
BF16 = jnp.bfloat16
I32 = jnp.int32
I16 = jnp.int16
HIGHEST = lax.Precision.HIGHEST

D_MODEL = 1024
CHUNK = 64
CONV_CH = 512
CONV_WIDTH = 31
ATT_HEADS = 8
HEAD_DIM = 64
ATT_W = ATT_HEADS * HEAD_DIM
IDX_HEADS = 4
IDX_DIM = 64
IDX_SCALE = (IDX_HEADS * IDX_DIM) ** -0.5
TOPK_MAX = 256
REL_BUCKETS = 32
REL_MAX_DIST = 128
PEER_HEADS = 8
N_KEYS = 128
N_EXPERTS = N_KEYS * N_KEYS
PEER_TOPK = 16
PEER_SLOTS = PEER_HEADS * PEER_TOPK
EPS = 1e-6
LOG2E = math.log2(math.e)

LANES = 128
SUBLANES = 8
PACK = 16
VMEM_LIMIT = 56 * 1024 * 1024

TM_IN = 512
CONV_ROWS = 64
HALO = 32
TQ = 256
KT = 256
TM_MID = 512
TB_PEER = 256
PEER_GROUPS = 2
PEER_GROUPS_V = 4
HALF = D_MODEL // 2
ROWS_PER_EXPERT = HALF // LANES

NT_DIMS = (((1,), (1,)), ((), ()))

_NEG_INF_BITS = int(np.array(-np.inf, np.float32).view(np.int32))
KEY_NEG_INF = _NEG_INF_BITS ^ 0x7FFFFFFF
I16_MIN = -(2 ** 15)
I16_MAX = 2 ** 15 - 1
INT_MIN = -(2 ** 31)


def _nt(a, b, precision=None):
    return lax.dot_general(a, b, NT_DIMS, precision=precision, preferred_element_type=F32)


def _ada_kernel(c_ref, w_ref, b_ref, o_ref):
    a = jax.nn.silu(c_ref[...])
    o_ref[...] = jnp.dot(a, w_ref[...], precision=HIGHEST, preferred_element_type=F32) + b_ref[...]


def _ada(c, w_ada, b_ada):
    bsz, d = c.shape
    return pl.pallas_call(
        _ada_kernel,
        grid=(6,),
        in_specs=[
            pl.BlockSpec((bsz, d), lambda j: (0, 0)),
            pl.BlockSpec((d, d), lambda j: (0, j)),
            pl.BlockSpec((1, d), lambda j: (0, j)),
        ],
        out_specs=pl.BlockSpec((bsz, d), lambda j: (0, j)),
        out_shape=jax.ShapeDtypeStruct((bsz, 6 * d), F32),
        name="ada",
    )(c, w_ada, b_ada.reshape(1, 6 * d))


def _inproj_kernel(x_ref, mod_ref, g1_ref, wa_ref, wqkv_ref, widx_ref, wwi_ref, gq_ref, gk_ref, e2_ref,
                   cw_ref, cb_ref, lng_ref, lnb_ref, goc_ref,
                   conv_ref, q_ref, k_ref, vt_ref, qi_ref, ki_ref, wit_ref, ubuf):
    j = pl.program_id(1)
    tm = x_ref.shape[1]
    x = x_ref[0]
    sh1 = mod_ref[0, 0:1, :]
    sc1 = mod_ref[0, 1:2, :]
    r = lax.rsqrt(jnp.mean(x * x, axis=-1, keepdims=True) + EPS)
    h = (x * r) * g1_ref[...] * (1.0 + sc1) + sh1
    hb = h.astype(BF16)

    pa = jnp.dot(hb, wa_ref[...], preferred_element_type=F32)
    u = pa[:, :CONV_CH] * jax.nn.sigmoid(pa[:, CONV_CH:])

    @pl.when(j == 0)
    def _():
        ubuf[0:HALO, :] = jnp.zeros((HALO, CONV_CH), F32)

    ubuf[HALO:HALO + tm, :] = u
    first = HALO - (CONV_WIDTH - 1)
    for rb in range(tm // CONV_ROWS):
        base = rb * CONV_ROWS
        acc = jnp.zeros((CONV_ROWS, CONV_CH), F32) + cb_ref[...]
        for phase in range(SUBLANES):
            taps = [t for t in range(CONV_WIDTH) if (first + t) % SUBLANES == phase]
            span = (first + taps[-1]) - phase + CONV_ROWS
            slab = ubuf[base + phase:base + phase + span, :]
            for t in taps:
                off = first + t - phase
                acc = acc + cw_ref[t:t + 1, :] * slab[off:off + CONV_ROWS, :]
        mu = jnp.mean(acc, axis=-1, keepdims=True)
        xc = acc - mu
        y = xc * lax.rsqrt(jnp.mean(xc * xc, axis=-1, keepdims=True) + EPS)
        y = jax.nn.silu(y * lng_ref[...] + lnb_ref[...])
        y = y * lax.rsqrt(jnp.mean(y * y, axis=-1, keepdims=True) + EPS) * goc_ref[...]
        conv_ref[0, base:base + CONV_ROWS, :] = y.astype(BF16)
    ubuf[0:HALO, :] = ubuf[tm:tm + HALO, :]

    pq = jnp.dot(hb, wqkv_ref[...], preferred_element_type=F32)
    q = pq[:, :ATT_W]
    k = pq[:, ATT_W:2 * ATT_W]
    v = pq[:, 2 * ATT_W:]
    e2 = e2_ref[...]

    def head_mean(sq):
        hi = sq.astype(BF16)
        lo = (sq - hi.astype(F32)).astype(BF16)
        return (jnp.dot(hi, e2, preferred_element_type=F32)
                + jnp.dot(lo, e2, preferred_element_type=F32)) * (1.0 / HEAD_DIM)

    qs = head_mean(q * q)
    ks = head_mean(k * k)
    qn = q * lax.rsqrt(qs + EPS) * gq_ref[...] * (HEAD_DIM ** -0.5 * LOG2E)
    kn = k * lax.rsqrt(ks + EPS) * gk_ref[...]
    lane = lax.broadcasted_iota(I32, (tm, LANES), 1)
    low = lane < HEAD_DIM
    for p in range(ATT_HEADS // 2):
        slab = qn[:, p * LANES:(p + 1) * LANES]
        q_ref[0, 2 * p] = jnp.where(low, slab, 0.0).T.astype(BF16)
        q_ref[0, 2 * p + 1] = jnp.where(low, 0.0, slab).T.astype(BF16)
        k_ref[0, p] = kn[:, p * LANES:(p + 1) * LANES].astype(BF16)
    vt_ref[0] = v.T.astype(BF16)

    pc = jnp.dot(hb, widx_ref[...], preferred_element_type=F32)
    for p in range(IDX_HEADS // 2):
        slab = pc[:, p * LANES:(p + 1) * LANES]
        qi_ref[0, 2 * p] = jnp.where(low, slab, 0.0).T.astype(BF16)
        qi_ref[0, 2 * p + 1] = jnp.where(low, 0.0, slab).T.astype(BF16)
    ki_ref[0] = pc[:, 2 * LANES:3 * LANES].astype(BF16)
    wit_ref[0] = _nt(wwi_ref[...], hb) * IDX_SCALE


def _inproj(x, mod3, g1, wa, wqkv, widx, wwi, gq, gk, e2, cw, cb, lng, lnb, goc):
    bsz, s, d = x.shape
    tm = TM_IN
    nt = s // tm
    full = lambda shape: pl.BlockSpec(shape, lambda b, j: (0,) * len(shape))
    out_shape = (
        jax.ShapeDtypeStruct((bsz, s, CONV_CH), BF16),
        jax.ShapeDtypeStruct((bsz, ATT_HEADS, LANES, s), BF16),
        jax.ShapeDtypeStruct((bsz, ATT_HEADS // 2, s, LANES), BF16),
        jax.ShapeDtypeStruct((bsz, ATT_W, s), BF16),
        jax.ShapeDtypeStruct((bsz, IDX_HEADS, LANES, s), BF16),
        jax.ShapeDtypeStruct((bsz, s, LANES), BF16),
        jax.ShapeDtypeStruct((bsz, SUBLANES, s), F32),
    )
    out_specs = (
        pl.BlockSpec((1, tm, CONV_CH), lambda b, j: (b, j, 0)),
        pl.BlockSpec((1, ATT_HEADS, LANES, tm), lambda b, j: (b, 0, 0, j)),
        pl.BlockSpec((1, ATT_HEADS // 2, tm, LANES), lambda b, j: (b, 0, j, 0)),
        pl.BlockSpec((1, ATT_W, tm), lambda b, j: (b, 0, j)),
        pl.BlockSpec((1, IDX_HEADS, LANES, tm), lambda b, j: (b, 0, 0, j)),
        pl.BlockSpec((1, tm, LANES), lambda b, j: (b, j, 0)),
        pl.BlockSpec((1, SUBLANES, tm), lambda b, j: (b, 0, j)),
    )
    return pl.pallas_call(
        _inproj_kernel,
        grid=(bsz, nt),
        in_specs=[
            pl.BlockSpec((1, tm, d), lambda b, j: (b, j, 0)),
            pl.BlockSpec((1, 6, d), lambda b, j: (b, 0, 0)),
            full(g1.shape), full(wa.shape), full(wqkv.shape), full(widx.shape), full(wwi.shape),
            full(gq.shape), full(gk.shape), full(e2.shape),
            full(cw.shape), full(cb.shape), full(lng.shape), full(lnb.shape), full(goc.shape),
        ],
        out_specs=out_specs,
        out_shape=out_shape,
        scratch_shapes=[pltpu.VMEM((tm + HALO, CONV_CH), F32)],
        compiler_params=pltpu.CompilerParams(
            dimension_semantics=("arbitrary", "arbitrary"), vmem_limit_bytes=VMEM_LIMIT),
        name="inproj",
    )(x, mod3, g1, wa, wqkv, widx, wwi, gq, gk, e2, cw, cb, lng, lnb, goc)


def _t5_bucket_np(rel):
    half = REL_BUCKETS // 2
    max_exact = half // 2
    ret = np.where(rel > 0, half, 0)
    n = np.abs(rel)
    nf = np.maximum(n, 1).astype(np.float64)
    large = max_exact + (np.log(nf / max_exact) / math.log(REL_MAX_DIST / max_exact)
                         * (half - max_exact)).astype(np.int32)
    large = np.minimum(large, half - 1)
    return (ret + np.where(n < max_exact, n, large)).astype(np.int32)


NEAR = REL_MAX_DIST + TQ
FAR_BUCKET = REL_BUCKETS // 2 - 1


def _near_buckets():
    r = np.arange(NEAR)[:, None]
    t = np.arange(TQ)[None, :]
    return _t5_bucket_np(r - REL_MAX_DIST - t)


def _attn_kernel(q_ref, k_ref, vt_ref, qi_ref, ki_ref, wit_ref, bkt_ref, rb_ref, goa_ref, o_ref,
                 keys_s, hi_s, lo_s, maskb_s, relb_s, ot_s, m_s, mo_s, l_s, lg_s, *, nsel):
    b = pl.program_id(0)
    i = pl.program_id(1)
    t0 = i * TQ
    n_tiles = i + 1
    neg_inf = F32(-jnp.inf)

    @pl.when((b == 0) & (i == 0))
    def _():
        bk = bkt_ref[...]
        for h in range(ATT_HEADS):
            far = rb_ref[FAR_BUCKET, h]
            acc = jnp.zeros((NEAR, TQ), F32)
            for bb in range(REL_BUCKETS):
                acc = jnp.where(bk == bb, (rb_ref[bb, h] - far) * LOG2E, acc)
            relb_s[h, 0:2 * KT - NEAR, :] = jnp.zeros((2 * KT - NEAR, TQ), F32)
            relb_s[h, 2 * KT - NEAR:2 * KT, :] = acc

    qpos = t0 + lax.broadcasted_iota(I32, (1, TQ), 1)
    limit = (qpos // CHUNK + 1) * CHUNK
    row_iota = lax.broadcasted_iota(I32, (KT, TQ), 0)

    def tile_start(jt):
        return pl.multiple_of(jt * KT, KT)

    wi = wit_ref[0]

    def p1(jt, c):
        ks = tile_start(jt)
        kit = ki_ref[0, pl.ds(ks, KT), :]
        acc = jnp.zeros((KT, TQ), F32)
        for h in range(IDX_HEADS):
            lgt = jnp.dot(kit, qi_ref[0, h], preferred_element_type=F32)
            acc = acc + jnp.maximum(lgt, 0.0) * wi[h:h + 1, :]
        sc = jnp.where(row_iota + ks < limit, acc, neg_inf)
        bits = pltpu.bitcast(sc, I32)
        key = bits ^ (lax.shift_right_arithmetic(bits, 31) & 0x7FFFFFFF)
        keys_s[pl.ds(ks, KT), :] = key
        hi_s[pl.ds(ks, KT), :] = lax.shift_right_arithmetic(key, 16).astype(I16)
        lo_s[pl.ds(ks, KT), :] = ((key & 0xFFFF) ^ 0x8000).astype(I16)
        return c

    lax.fori_loop(0, n_tiles, p1, 0)

    def count_ge(half_ref, cand):
        c16 = cand.astype(I16)
        one = jnp.ones((KT, TQ), I16)
        zero = jnp.zeros((KT, TQ), I16)

        def body(jt, acc):
            hit = jnp.where(half_ref[pl.ds(tile_start(jt), KT), :] >= c16, one, zero)
            parts = [hit[r:r + PACK, :] for r in range(0, KT, PACK)]
            while len(parts) > 1:
                parts = [parts[k] + parts[k + 1] for k in range(0, len(parts), 2)]
            return acc + parts[0].astype(I32)
        acc = lax.fori_loop(0, n_tiles, body, jnp.zeros((PACK, TQ), I32))
        return jnp.sum(acc.astype(F32), axis=0, keepdims=True)

    def low_key(low):
        return (low ^ 0x8000) - jnp.where(low < 0x8000, 0x10000, 0)

    def select_thr():
        c0 = count_ge(hi_s, jnp.zeros((1, TQ), I32))
        ok = c0 >= nsel
        t = jnp.where(ok, 0, I16_MIN).astype(I32)
        cnt = jnp.where(ok, c0, F32(2 ** 30))

        def top_step(it, carry):
            t, cnt = carry
            cand = t + lax.shift_left(I32(1), I32(14) - it)
            c = count_ge(hi_s, cand)
            ok = c >= nsel
            return jnp.where(ok, cand, t), jnp.where(ok, c, cnt)

        top, cnt = lax.fori_loop(0, 15, top_step, (t, cnt))

        top_max = top >= I16_MAX
        above = jnp.where(top_max, 0.0, count_ge(hi_s, jnp.where(top_max, top, top + 1)))
        top16 = top.astype(I16)
        floor16 = jnp.full((KT, TQ), I16_MIN, I16)

        def keep_low(jt, c):
            rows = pl.ds(tile_start(jt), KT)
            lo_s[rows, :] = jnp.where(hi_s[rows, :] == top16, lo_s[rows, :], floor16)
            return c

        lax.fori_loop(0, n_tiles, keep_low, 0)

        at_zero = jnp.where(top == 0, above + count_ge(lo_s, low_key(jnp.ones((1, TQ), I32))), F32(2 ** 30))
        settled = at_zero < nsel

        def low_cond(carry):
            bit, _, cnt = carry
            return (bit >= 0) & (jnp.max(jnp.where(settled, F32(nsel), cnt)) > nsel)

        def low_step(carry):
            bit, low, cnt = carry
            cand = low + lax.shift_left(I32(1), bit)
            c = above + count_ge(lo_s, low_key(cand))
            ok = c >= nsel
            return bit - 1, jnp.where(ok, cand, low), jnp.where(ok, c, cnt)

        _, low, cnt = lax.while_loop(low_cond, low_step, (I32(15), jnp.zeros((1, TQ), I32), cnt))
        low_max = low >= 0xFFFF
        c_gt = above + jnp.where(low_max, 0.0, count_ge(lo_s, low_key(jnp.where(low_max, low, low + 1))))
        return lax.shift_left(top, 16) + low, cnt, c_gt

    thr, c_ge, c_gt = lax.cond(
        i > 0, select_thr,
        lambda: (jnp.full((1, TQ), KEY_NEG_INF + 1, I32), jnp.full((1, TQ), nsel, F32), jnp.zeros((1, TQ), F32)))

    def mask_plain():
        def p3(jt, c):
            ks = tile_start(jt)
            maskb_s[pl.ds(ks, KT), :] = jnp.where(keys_s[pl.ds(ks, KT), :] >= thr, 0.0, neg_inf)
            return c
        lax.fori_loop(0, n_tiles, p3, 0)
        return I32(0)

    def mask_ties():
        quota = nsel - c_gt
        tri =jnp.where(lax.broadcasted_iota(I32, (KT, KT), 0) >= lax.broadcasted_iota(I32, (KT, KT), 1),
                        1.0, 0.0).astype(BF16)

        def p3(jt, before):
            ks = tile_start(jt)
            kt = keys_s[pl.ds(ks, KT), :]
            eq = kt == thr
            rank = before + jnp.dot(tri, jnp.where(eq, 1.0, 0.0).astype(BF16), preferred_element_type=F32)
            tied = jnp.where(eq, jnp.where(rank <= quota, 0.0, neg_inf), neg_inf)
            maskb_s[pl.ds(ks, KT), :] = jnp.where(kt > thr, 0.0, tied)
            return rank[KT - 1:KT, :]

        lax.fori_loop(0, n_tiles, p3, jnp.zeros((1, TQ), F32))
        return I32(0)

    lax.cond(jnp.max(c_ge) > nsel, mask_ties, mask_plain)

    m_s[...] = jnp.full((ATT_HEADS, TQ), neg_inf, F32)
    l_s[...] = jnp.zeros((ATT_HEADS, TQ), F32)
    ot_s[...] = jnp.zeros((ATT_W, TQ), F32)

    def att_tile(jt, near):
        ks = tile_start(jt)
        mb = maskb_s[pl.ds(ks, KT), :]
        for h in range(ATT_HEADS):
            l = jnp.dot(k_ref[0, h // 2, pl.ds(ks, KT), :], q_ref[0, h], preferred_element_type=F32) + mb
            if near:
                off = pl.multiple_of((jt - (n_tiles - 2)) * KT, KT)
                l = l + relb_s[h, pl.ds(off, KT), :]
            lg_s[h] = l
            m_old = m_s[h:h + 1, :]
            mo_s[h:h + 1, :] = m_old
            m_s[h:h + 1, :] = jnp.maximum(m_old, jnp.max(l, axis=0, keepdims=True))
        for h in range(ATT_HEADS):
            m_new = m_s[h:h + 1, :]
            m_ref = jnp.where(m_new == neg_inf, 0.0, m_new)
            alpha = jnp.exp2(mo_s[h:h + 1, :] - m_ref)
            p = jnp.exp2(lg_s[h] - m_ref)
            l_s[h:h + 1, :] = alpha * l_s[h:h + 1, :] + jnp.sum(p, axis=0, keepdims=True)
            vt = vt_ref[0, h * HEAD_DIM:(h + 1) * HEAD_DIM, pl.ds(ks, KT)]
            rows = slice(h * HEAD_DIM, (h + 1) * HEAD_DIM)
            ot_s[rows, :] = ot_s[rows, :] * alpha + jnp.dot(vt, p.astype(BF16), preferred_element_type=F32)

    def far_tile(jt, c):
        att_tile(jt, False)
        return c

    def near_tile(jt, c):
        att_tile(jt, True)
        return c

    n_far = jnp.maximum(n_tiles - 2, 0)
    lax.fori_loop(0, n_far, far_tile, 0)
    lax.fori_loop(n_far, n_tiles, near_tile, 0)
    for h in range(ATT_HEADS):
        rows = slice(h * HEAD_DIM, (h + 1) * HEAD_DIM)
        ot_s[rows, :] = ot_s[rows, :] / l_s[h:h + 1, :]

    ot = ot_s[...]
    ms = jnp.mean(ot * ot, axis=0, keepdims=True)
    y = (ot * lax.rsqrt(ms + EPS)).T * goa_ref[...]
    o_ref[0] = y.astype(BF16)


def _attn(q, k, vt, qi, ki, wit, rel_bias, goa):
    bsz, _, _, s = q.shape
    nsel = min(TOPK_MAX, s // 4)
    assert nsel == TQ and s % TQ == 0, "attention kernel assumes TOPK_MAX-sized query blocks"
    bkt = jnp.asarray(_near_buckets())
    kern = functools.partial(_attn_kernel, nsel=nsel)
    return pl.pallas_call(
        kern,
        grid=(bsz, s // TQ),
        in_specs=[
            pl.BlockSpec((1, ATT_HEADS, LANES, TQ), lambda b, i: (b, 0, 0, i)),
            pl.BlockSpec((1, ATT_HEADS // 2, s, LANES), lambda b, i: (b, 0, 0, 0)),
            pl.BlockSpec((1, ATT_W, s), lambda b, i: (b, 0, 0)),
            pl.BlockSpec((1, IDX_HEADS, LANES, TQ), lambda b, i: (b, 0, 0, i)),
            pl.BlockSpec((1, s, LANES), lambda b, i: (b, 0, 0)),
            pl.BlockSpec((1, SUBLANES, TQ), lambda b, i: (b, 0, i)),
            pl.BlockSpec((NEAR, TQ), lambda b, i: (0, 0)),
            pl.BlockSpec(memory_space=pltpu.SMEM),
            pl.BlockSpec((1, ATT_W), lambda b, i: (0, 0)),
        ],
        out_specs=pl.BlockSpec((1, TQ, ATT_W), lambda b, i: (b, i, 0)),
        out_shape=jax.ShapeDtypeStruct((bsz, s, ATT_W), BF16),
        scratch_shapes=[
            pltpu.VMEM((s, TQ), I32),
            pltpu.VMEM((s, TQ), I16),
            pltpu.VMEM((s, TQ), I16),
            pltpu.VMEM((s, TQ), F32),
            pltpu.VMEM((ATT_HEADS, 2 * KT, TQ), F32),
            pltpu.VMEM((ATT_W, TQ), F32),
            pltpu.VMEM((ATT_HEADS, TQ), F32),
            pltpu.VMEM((ATT_HEADS, TQ), F32),
            pltpu.VMEM((ATT_HEADS, TQ), F32),
            pltpu.VMEM((ATT_HEADS, KT, TQ), F32),
        ],
        compiler_params=pltpu.CompilerParams(
            dimension_semantics=("arbitrary", "arbitrary"), vmem_limit_bytes=VMEM_LIMIT),
        name="attn",
    )(q, k, vt, qi, ki, wit, bkt, rel_bias, goa)


def _topk_rows(s, payload, k, order=None):
    rows = lax.broadcasted_iota(I32, s.shape, 0).astype(F32) if order is None else order
    vals, pays = [], []
    for _ in range(k):
        m = jnp.max(s, axis=0, keepdims=True)
        ix = jnp.min(jnp.where(s == m, rows, _NO_ROW), axis=0, keepdims=True)
        hit = rows == ix
        if payload is None:
            pays.append(ix)
        else:
            pays.append(jnp.max(jnp.where(hit, payload, -1.0), axis=0, keepdims=True))
        vals.append(m)
        s = jnp.where(hit, -jnp.inf, s)
    return jnp.concatenate(vals, axis=0), jnp.concatenate(pays, axis=0)


_NO_ROW = 1e9

_PAIR_GROUPS = (
    ((0, 0, 8, 0),), ((0, 8, 8, 0),), ((1, 0, 8, 0),),
    ((2, 0, 5, 0), (5, 0, 2, 5)), ((3, 0, 4, 0), (4, 0, 3, 4)), ((6, 0, 2, 0), (7, 0, 2, 2)),
)
assert sorted((a, b) for g in _PAIR_GROUPS for a, b0, nb, _ in g for b in range(b0, b0 + nb)) == sorted(
    (a, b) for a in range(PEER_TOPK // 2) for b in range(PEER_TOPK // (a + 1)))


def _mid_kernel(x_ref, cn_ref, an_ref, mod_ref, wo1_ref, wo2_ref, g2_ref, wpq_ref, k1_ref, k2_ref,
                x1_ref, h2_ref, idx_ref, gate_ref, qq_s, idt_s, gt_s):
    tm = x_ref.shape[1]
    x = x_ref[0]
    gt1 = mod_ref[0, 2:3, :]
    sh2 = mod_ref[0, 3:4, :]
    sc2 = mod_ref[0, 4:5, :]
    proj = (jnp.dot(cn_ref[0], wo1_ref[...], preferred_element_type=F32)
            + jnp.dot(an_ref[0], wo2_ref[...], preferred_element_type=F32))
    x1 = x + gt1 * proj
    x1_ref[0] = x1
    r = lax.rsqrt(jnp.mean(x1 * x1, axis=-1, keepdims=True) + EPS)
    h2 = (x1 * r) * g2_ref[...] * (1.0 + sc2) + sh2
    h2_ref[0] = h2
    qq_s[...] = jnp.dot(h2.astype(BF16), wpq_ref[...], preferred_element_type=F32).astype(BF16)

    def route_unit(hh, lt):
        rows = pl.ds(lt * LANES, LANES)
        q1 = qq_s[rows, pl.ds(pl.multiple_of(hh * 2 * N_KEYS, LANES), N_KEYS)]
        q2 = qq_s[rows, pl.ds(pl.multiple_of(hh * 2 * N_KEYS + N_KEYS, LANES), N_KEYS)]
        v1, i1 = _topk_rows(_nt(k1_ref[hh], q1), None, PEER_TOPK)
        v2, i2 = _topk_rows(_nt(k2_ref[hh], q2), None, PEER_TOPK)
        sub = lax.broadcasted_iota(I32, (SUBLANES, LANES), 0)
        subf = sub.astype(F32)
        cands, cidxs, flats = [], [], []
        for group in _PAIR_GROUPS:
            val = jnp.full((SUBLANES, LANES), -jnp.inf, F32)
            cid = jnp.zeros((SUBLANES, LANES), F32)
            flat = jnp.full((SUBLANES, LANES), _NO_ROW, F32)
            for a, b0, nb, off in group:
                v2s, i2s = v2[b0:b0 + SUBLANES, :], i2[b0:b0 + SUBLANES, :]
                if off:
                    v2s, i2s = pltpu.roll(v2s, off, 0), pltpu.roll(i2s, off, 0)
                inside = lambda new, old: jnp.where(sub < off + nb, jnp.where(sub >= off, new, old), old)
                val = inside(v1[a:a + 1, :] + v2s, val)
                cid = inside(i1[a:a + 1, :] * float(N_KEYS) + i2s, cid)
                flat = inside(subf + float(a * PEER_TOPK + b0 - off), flat)
            cands.append(val)
            cidxs.append(cid)
            flats.append(flat)
        half = PEER_TOPK // 2
        cands.append(v1[half:, :] + v2[0:1, :])
        cidxs.append(i1[half:, :] * float(N_KEYS) + i2[0:1, :])
        flats.append((subf + float(half)) * float(PEER_TOPK))
        best, experts = _topk_rows(jnp.concatenate(cands, axis=0), jnp.concatenate(cidxs, axis=0), PEER_TOPK,
                                   order=jnp.concatenate(flats, axis=0))
        e = jnp.exp(best - best[0:1, :])
        g = e / jnp.sum(e, axis=0, keepdims=True)
        slots = pl.ds(pl.multiple_of(hh * PEER_TOPK, PEER_TOPK), PEER_TOPK)
        cols = pl.ds(lt * LANES, LANES)
        idt_s[slots, cols] = experts * float(ROWS_PER_EXPERT)
        gt_s[slots, cols] = g

    def route(hh, c):
        for lt in range(tm // LANES):
            route_unit(hh, lt)
        return c

    lax.fori_loop(0, PEER_HEADS, route, 0)
    idx_ref[0] = idt_s[...].T.astype(I32)
    gate_ref[0] = gt_s[...].T


def _mid(x, cn, an, mod3, wo1, wo2, g2, wpq, k1, k2):
    bsz, s, d = x.shape
    tm = TM_MID
    full = lambda shape: pl.BlockSpec(shape, lambda b, j: (0,) * len(shape))
    tok = lambda w: pl.BlockSpec((1, tm, w), lambda b, j: (b, j, 0))
    return pl.pallas_call(
        _mid_kernel,
        grid=(bsz, s // tm),
        in_specs=[tok(d), tok(CONV_CH), tok(ATT_W), pl.BlockSpec((1, 6, d), lambda b, j: (b, 0, 0)),
                  full(wo1.shape), full(wo2.shape), full(g2.shape), full(wpq.shape), full(k1.shape), full(k2.shape)],
        out_specs=(tok(d), tok(d), tok(PEER_SLOTS), tok(PEER_SLOTS)),
        out_shape=(
            jax.ShapeDtypeStruct((bsz, s, d), F32),
            jax.ShapeDtypeStruct((bsz, s, d), F32),
            jax.ShapeDtypeStruct((bsz, s, PEER_SLOTS), I32),
            jax.ShapeDtypeStruct((bsz, s, PEER_SLOTS), F32),
        ),
        scratch_shapes=[
            pltpu.VMEM((tm, PEER_HEADS * 2 * N_KEYS), BF16),
            pltpu.VMEM((PEER_SLOTS, tm), F32),
            pltpu.VMEM((PEER_SLOTS, tm), F32),
        ],
        compiler_params=pltpu.CompilerParams(
            dimension_semantics=("arbitrary", "arbitrary"), vmem_limit_bytes=VMEM_LIMIT),
        name="mid",
    )(x, cn, an, mod3, wo1, wo2, g2, wpq, k1, k2)


def _pack_table(t):
    tb = t.astype(BF16)
    lo = lax.bitcast_convert_type(tb[:, :HALF], jnp.uint16).astype(jnp.uint32)
    hi = lax.bitcast_convert_type(tb[:, HALF:], jnp.uint16).astype(jnp.uint32)
    return (lo | (hi << 16)).reshape(t.shape[0] * ROWS_PER_EXPERT, LANES)


def _unpack(w):
    lo = pltpu.bitcast(lax.shift_left(w, jnp.uint32(16)), F32)
    hi = pltpu.bitcast(w & jnp.uint32(0xFFFF0000), F32)
    return lo, hi


def _gather_row(tab_ref, row):
    return tab_ref[pl.ds(pl.multiple_of(row, ROWS_PER_EXPERT), ROWS_PER_EXPERT), :]


STAGE_ROWS = PEER_SLOTS * ROWS_PER_EXPERT


def _token_pieces(xg, r, first, count):
    return jnp.concatenate(
        [xg[r:r + 1, (first + q) * LANES:(first + q + 1) * LANES] for q in range(count)], axis=0)


def _peer_u_kernel(idx_ref, tab_ref, x_ref, gate_ref, w_ref, *stages):
    tb = w_ref.shape[0]
    sub = lax.broadcasted_iota(I32, (SUBLANES, PEER_SLOTS), 0)

    def group(g, c):
        for k in range(PEER_GROUPS):
            eight(pl.multiple_of((g * PEER_GROUPS + k) * SUBLANES, SUBLANES), stages[k * SUBLANES:(k + 1) * SUBLANES])
        return c

    def eight(t0, stages):
        act8 = jnp.zeros((SUBLANES, PEER_SLOTS), F32)
        xg = x_ref[pl.ds(t0, SUBLANES), :]
        xlo, xhi, rows_t = [], [], []
        for r in range(SUBLANES):
            xlo.append(jnp.concatenate([_token_pieces(xg, r, 0, ROWS_PER_EXPERT)] * 2, axis=0))
            xhi.append(jnp.concatenate([_token_pieces(xg, r, ROWS_PER_EXPERT, ROWS_PER_EXPERT)] * 2, axis=0))
            rows_t.append(idx_ref.at[t0 + r])
        for j in range(0, PEER_SLOTS, 2):
            for r in range(SUBLANES):
                pair = jnp.concatenate(
                    [_gather_row(tab_ref, rows_t[r][j]), _gather_row(tab_ref, rows_t[r][j + 1])], axis=0)
                lo, hi = _unpack(pair)
                row = j * ROWS_PER_EXPERT
                stages[r][row:row + 2 * ROWS_PER_EXPERT, :] = lo * xlo[r] + hi * xhi[r]
        for r in range(SUBLANES):
            stage = stages[r]
            cs = stage[pl.ds(0, PEER_SLOTS, stride=ROWS_PER_EXPERT), :]
            for q in range(1, ROWS_PER_EXPERT):
                cs = cs + stage[pl.ds(q, PEER_SLOTS, stride=ROWS_PER_EXPERT), :]
            act = jnp.sum(cs.T, axis=0, keepdims=True)
            act8 = jnp.where(sub == r, act, act8)
        rows = pl.ds(t0, SUBLANES)
        w_ref[rows, :] = gate_ref[rows, :] * jax.nn.gelu(act8)

    lax.fori_loop(0, tb // (PEER_GROUPS * SUBLANES), group, 0)


def _peer_u(idx, tab, h2, gate):
    n, d = h2.shape
    tb = TB_PEER
    return pl.pallas_call(
        _peer_u_kernel,
        grid=(n // tb,),
        in_specs=[
            pl.BlockSpec((tb, PEER_SLOTS), lambda i: (i, 0), memory_space=pltpu.SMEM),
            pl.BlockSpec(tab.shape, lambda i: (0, 0), pipeline_mode=pl.Buffered(1)),
            pl.BlockSpec((tb, d), lambda i: (i, 0)),
            pl.BlockSpec((tb, PEER_SLOTS), lambda i: (i, 0)),
        ],
        out_specs=pl.BlockSpec((tb, PEER_SLOTS), lambda i: (i, 0)),
        out_shape=jax.ShapeDtypeStruct((n, PEER_SLOTS), F32),
        scratch_shapes=[pltpu.VMEM((STAGE_ROWS, LANES), F32) for _ in range(PEER_GROUPS * SUBLANES)],
        compiler_params=pltpu.CompilerParams(
            dimension_semantics=("arbitrary",), vmem_limit_bytes=VMEM_LIMIT),
        name="peer_u",
    )(idx, tab, h2, gate)


PV_PAIRS = SUBLANES // 2


def _peer_v_kernel(idx_ref, w_ref, tab_ref, x1_ref, gt2_ref, e8_ref, mask_ref, o_ref, *stages):
    tb = idx_ref.shape[0]
    gt2 = gt2_ref[0]
    mask = mask_ref[...]
    e8 = e8_ref[...]

    def group(g, c):
        for k in range(PEER_GROUPS_V):
            eight(pl.multiple_of((g * PEER_GROUPS_V + k) * SUBLANES, SUBLANES), stages[k * PV_PAIRS:(k + 1) * PV_PAIRS])
        return c

    def eight(g0, stages):
        peers = []
        for p in range(PV_PAIRS):
            stage = stages[p]
            t0 = g0 + 2 * p
            for u in range(2):
                rows_t = idx_ref.at[t0 + u]
                for j in range(PEER_SLOTS):
                    stage[j * ROWS_PER_EXPERT:(j + 1) * ROWS_PER_EXPERT, u * LANES:(u + 1) * LANES] = (
                        _gather_row(tab_ref, rows_t[j]))
            w2 = w_ref[pl.ds(t0, 2), :]
            hi = w2.astype(BF16).astype(F32)
            rep = jnp.dot(jnp.concatenate([hi, w2 - hi], axis=0).astype(BF16), e8,
                          preferred_element_type=F32)
            lhs = jnp.concatenate([rep[i:i + 1, :] * mask for i in range(4)], axis=0).astype(BF16)
            out = jnp.dot(lhs, pltpu.bitcast(stage[...], BF16), preferred_element_type=F32)
            for u in range(2):
                r0 = u * SUBLANES
                peers.append(out[r0:r0 + SUBLANES, u * LANES:(u + 1) * LANES]
                             + out[2 * SUBLANES + r0:3 * SUBLANES + r0, u * LANES:(u + 1) * LANES])
        rows = pl.ds(g0, SUBLANES)
        for q in range(SUBLANES):
            cols = slice(q * LANES, (q + 1) * LANES)
            piece = jnp.concatenate([peer[q:q + 1, :] for peer in peers], axis=0)
            o_ref[rows, cols] = x1_ref[rows, cols] + gt2[:, cols] * piece

    lax.fori_loop(0, tb // (PEER_GROUPS_V * SUBLANES), group, 0)


def _peer_v(idx, w, tab, x1, gt2, blocks_per_batch):
    n, d = x1.shape
    tb = TB_PEER
    q = np.arange(SUBLANES)
    piece = 2 * (q % ROWS_PER_EXPERT) + q // ROWS_PER_EXPERT
    lane = np.arange(PEER_SLOTS * SUBLANES)
    mask = jnp.asarray((lane[None, :] % SUBLANES == piece[:, None]).astype(np.float32))
    e8 = jnp.asarray(np.arange(PEER_SLOTS)[:, None] == lane[None, :] // SUBLANES, BF16)
    return pl.pallas_call(
        _peer_v_kernel,
        grid=(n // tb,),
        in_specs=[
            pl.BlockSpec((tb, PEER_SLOTS), lambda i: (i, 0), memory_space=pltpu.SMEM),
            pl.BlockSpec((tb, PEER_SLOTS), lambda i: (i, 0)),
            pl.BlockSpec(tab.shape, lambda i: (0, 0), pipeline_mode=pl.Buffered(1)),
            pl.BlockSpec((tb, d), lambda i: (i, 0)),
            pl.BlockSpec((1, 1, d), lambda i: (i // blocks_per_batch, 0, 0)),
            pl.BlockSpec(e8.shape, lambda i: (0, 0)),
            pl.BlockSpec(mask.shape, lambda i: (0, 0)),
        ],
        out_specs=pl.BlockSpec((tb, d), lambda i: (i, 0)),
        out_shape=jax.ShapeDtypeStruct((n, d), F32),
        scratch_shapes=[pltpu.VMEM((STAGE_ROWS, 2 * LANES), jnp.uint32) for _ in range(PEER_GROUPS_V * PV_PAIRS)],
        compiler_params=pltpu.CompilerParams(
            dimension_semantics=("arbitrary",), vmem_limit_bytes=VMEM_LIMIT),
        name="peer_v",
    )(idx, w, tab, x1, gt2, e8, mask)


def _layer(x, mod, g_norm1, g_norm2, w_in, q_norm_g, k_norm_g, conv_w, conv_b, conv_ln_g, conv_ln_b,
           rel_bias, g_out_conv, g_out_attn, w_out, w_peer_q, peer_k1, peer_k2, peer_u, peer_v):
    bsz, s, d = x.shape
    n = bsz * s
    mod3 = mod.reshape(bsz, 6, d)
    row = lambda a: a.reshape(1, -1)

    c0 = 2 * CONV_CH
    c1 = c0 + 3 * ATT_W
    c2 = c1 + IDX_HEADS * IDX_DIM
    c3 = c2 + IDX_DIM
    wa = w_in[:, :c0].astype(BF16)
    wqkv = w_in[:, c0:c1].astype(BF16)
    widx = jnp.concatenate([w_in[:, c1:c2], w_in[:, c2:c3], w_in[:, c2:c3]], axis=1).astype(BF16)
    wwi = jnp.zeros((SUBLANES, d), F32).at[:IDX_HEADS].set(w_in[:, c3:c3 + IDX_HEADS].T).astype(BF16)
    head = np.arange(ATT_W) // HEAD_DIM
    e2 = jnp.asarray(head[:, None] == head[None, :], BF16)

    conv_n, q, k, vt, qi, ki, wit = _inproj(
        x, mod3, row(g_norm1), wa, wqkv, widx, wwi,
        row(jnp.tile(q_norm_g, ATT_HEADS)), row(jnp.tile(k_norm_g, ATT_HEADS)), e2,
        conv_w.reshape(CONV_WIDTH, CONV_CH), row(conv_b), row(conv_ln_g), row(conv_ln_b), row(g_out_conv))
    attn_n = _attn(q, k, vt, qi, ki, wit, rel_bias, row(g_out_attn))

    x1, h2, idx, gate = _mid(
        x, conv_n, attn_n, mod3, w_out[:CONV_CH].astype(BF16), w_out[CONV_CH:].astype(BF16), row(g_norm2),
        w_peer_q.astype(BF16), peer_k1.astype(BF16), peer_k2.astype(BF16))

    idx = idx.reshape(n, PEER_SLOTS)
    w = _peer_u(idx, _pack_table(peer_u), h2.reshape(n, d), gate.reshape(n, PEER_SLOTS))
    out = _peer_v(idx, w, _pack_table(peer_v), x1.reshape(n, d), mod3[:, 5:6, :], s // TB_PEER)
    return out.reshape(bsz, s, d)


def kernel(x, c, w_ada, b_ada, g_norm1, g_norm2, w_in, q_norm_g, k_norm_g, conv_w, conv_b, conv_ln_g,
           conv_ln_b, rel_bias, g_out_conv, g_out_attn, w_out, w_peer_q, peer_k1, peer_k2, peer_u, peer_v):
    depth = w_ada.shape[0]
    for l in range(depth):
        mod = _ada(c, w_ada[l], b_ada[l])
        x = _layer(x, mod, g_norm1[l], g_norm2[l], w_in[l], q_norm_g[l], k_norm_g[l], conv_w[l], conv_b[l],
                   conv_ln_g[l], conv_ln_b[l], rel_bias, g_out_conv[l], g_out_attn[l], w_out[l],
                   w_peer_q[l], peer_k1[l], peer_k2[l], peer_u[l], peer_v[l])
    return x
```

```python
import functools
import math

import numpy as np
import jax
import jax.numpy as jnp
from jax import lax
from jax.experimental import pallas as pl
from jax.experimental.pallas import tpu as pltpu

F32 = jnp.float32
BF16 = jnp.bfloat16
I32 = jnp.int32
I16 = jnp.int16
HIGHEST = lax.Precision.HIGHEST

D_MODEL = 1024
CHUNK = 64
CONV_CH = 512
CONV_WIDTH = 31
ATT_HEADS = 8
HEAD_DIM = 64
ATT_W = ATT_HEADS * HEAD_DIM
IDX_HEADS = 4
IDX_DIM = 64
IDX_SCALE = (IDX_HEADS * IDX_DIM) ** -0.5
TOPK_MAX = 256
REL_BUCKETS = 32
REL_MAX_DIST = 128
PEER_HEADS = 8
N_KEYS = 128
N_EXPERTS = N_KEYS * N_KEYS
PEER_TOPK = 16
PEER_SLOTS = PEER_HEADS * PEER_TOPK
EPS = 1e-6
LOG2E = math.log2(math.e)

LANES = 128
SUBLANES = 8
PACK = 16
VMEM_LIMIT = 56 * 1024 * 1024

TM_IN = 512
CONV_ROWS = 64
HALO = 32
TQ = 256
KT = 256
TM_MID = 1024
TB_PEER = 256
PEER_GROUPS = 2
PEER_GROUPS_V = 4
HALF = D_MODEL // 2
ROWS_PER_EXPERT = HALF // LANES

NT_DIMS = (((1,), (1,)), ((), ()))

_NEG_INF_BITS = int(np.array(-np.inf, np.float32).view(np.int32))
KEY_NEG_INF = _NEG_INF_BITS ^ 0x7FFFFFFF
I16_MIN = -(2 ** 15)
I16_MAX = 2 ** 15 - 1
INT_MIN = -(2 ** 31)


def _nt(a, b, precision=None):
    return lax.dot_general(a, b, NT_DIMS, precision=precision, preferred_element_type=F32)


def _ada_kernel(c_ref, w_ref, b_ref, o_ref):
    a = jax.nn.silu(c_ref[...])
    o_ref[...] = jnp.dot(a, w_ref[...], precision=HIGHEST, preferred_element_type=F32) + b_ref[...]


def _ada(c, w_ada, b_ada):
    bsz, d = c.shape
    return pl.pallas_call(
        _ada_kernel,
        grid=(6,),
        in_specs=[
            pl.BlockSpec((bsz, d), lambda j: (0, 0)),
            pl.BlockSpec((d, d), lambda j: (0, j)),
            pl.BlockSpec((1, d), lambda j: (0, j)),
        ],
        out_specs=pl.BlockSpec((bsz, d), lambda j: (0, j)),
        out_shape=jax.ShapeDtypeStruct((bsz, 6 * d), F32),
        name="ada",
    )(c, w_ada, b_ada.reshape(1, 6 * d))


def _inproj_kernel(x_ref, mod_ref, g1_ref, wa_ref, wqkv_ref, widx_ref, wwi_ref, gq_ref, gk_ref, e2_ref,
                   cw_ref, cb_ref, lng_ref, lnb_ref, goc_ref,
                   conv_ref, q_ref, k_ref, vt_ref, qi_ref, ki_ref, wit_ref, ubuf):
    j = pl.program_id(1)
    tm = x_ref.shape[1]
    x = x_ref[0]
    sh1 = mod_ref[0, 0:1, :]
    sc1 = mod_ref[0, 1:2, :]
    r = lax.rsqrt(jnp.mean(x * x, axis=-1, keepdims=True) + EPS)
    h = (x * r) * g1_ref[...] * (1.0 + sc1) + sh1
    hb = h.astype(BF16)

    pa = jnp.dot(hb, wa_ref[...], preferred_element_type=F32)
    u = pa[:, :CONV_CH] * jax.nn.sigmoid(pa[:, CONV_CH:])

    @pl.when(j == 0)
    def _():
        ubuf[0:HALO, :] = jnp.zeros((HALO, CONV_CH), F32)

    ubuf[HALO:HALO + tm, :] = u
    first = HALO - (CONV_WIDTH - 1)
    for rb in range(tm // CONV_ROWS):
        base = rb * CONV_ROWS
        acc = jnp.zeros((CONV_ROWS, CONV_CH), F32) + cb_ref[...]
        for phase in range(SUBLANES):
            taps = [t for t in range(CONV_WIDTH) if (first + t) % SUBLANES == phase]
            span = (first + taps[-1]) - phase + CONV_ROWS
            slab = ubuf[base + phase:base + phase + span, :]
            for t in taps:
                off = first + t - phase
                acc = acc + cw_ref[t:t + 1, :] * slab[off:off + CONV_ROWS, :]
        mu = jnp.mean(acc, axis=-1, keepdims=True)
        xc = acc - mu
        y = xc * lax.rsqrt(jnp.mean(xc * xc, axis=-1, keepdims=True) + EPS)
        y = jax.nn.silu(y * lng_ref[...] + lnb_ref[...])
        y = y * lax.rsqrt(jnp.mean(y * y, axis=-1, keepdims=True) + EPS) * goc_ref[...]
        conv_ref[0, base:base + CONV_ROWS, :] = y.astype(BF16)
    ubuf[0:HALO, :] = ubuf[tm:tm + HALO, :]

    pq = jnp.dot(hb, wqkv_ref[...], preferred_element_type=F32)
    q = pq[:, :ATT_W]
    k = pq[:, ATT_W:2 * ATT_W]
    v = pq[:, 2 * ATT_W:]
    e2 = e2_ref[...]

    def head_mean(sq):
        hi = sq.astype(BF16)
        lo = (sq - hi.astype(F32)).astype(BF16)
        return (jnp.dot(hi, e2, preferred_element_type=F32)
                + jnp.dot(lo, e2, preferred_element_type=F32)) * (1.0 / HEAD_DIM)

    qs = head_mean(q * q)
    ks = head_mean(k * k)
    qn = q * lax.rsqrt(qs + EPS) * gq_ref[...] * (HEAD_DIM ** -0.5 * LOG2E)
    kn = k * lax.rsqrt(ks + EPS) * gk_ref[...]
    lane = lax.broadcasted_iota(I32, (tm, LANES), 1)
    low = lane < HEAD_DIM
    for p in range(ATT_HEADS // 2):
        slab = qn[:, p * LANES:(p + 1) * LANES]
        q_ref[0, 2 * p] = jnp.where(low, slab, 0.0).T.astype(BF16)
        q_ref[0, 2 * p + 1] = jnp.where(low, 0.0, slab).T.astype(BF16)
        k_ref[0, p] = kn[:, p * LANES:(p + 1) * LANES].astype(BF16)
    vt_ref[0] = v.T.astype(BF16)

    pc = jnp.dot(hb, widx_ref[...], preferred_element_type=F32)
    for p in range(IDX_HEADS // 2):
        slab = pc[:, p * LANES:(p + 1) * LANES]
        qi_ref[0, 2 * p] = jnp.where(low, slab, 0.0).T.astype(BF16)
        qi_ref[0, 2 * p + 1] = jnp.where(low, 0.0, slab).T.astype(BF16)
    ki_ref[0] = pc[:, 2 * LANES:3 * LANES].astype(BF16)
    wit_ref[0] = _nt(wwi_ref[...], hb) * IDX_SCALE


def _inproj(x, mod3, g1, wa, wqkv, widx, wwi, gq, gk, e2, cw, cb, lng, lnb, goc):
    bsz, s, d = x.shape
    tm = TM_IN
    nt = s // tm
    full = lambda shape: pl.BlockSpec(shape, lambda b, j: (0,) * len(shape))
    out_shape = (
        jax.ShapeDtypeStruct((bsz, s, CONV_CH), BF16),
        jax.ShapeDtypeStruct((bsz, ATT_HEADS, LANES, s), BF16),
        jax.ShapeDtypeStruct((bsz, ATT_HEADS // 2, s, LANES), BF16),
        jax.ShapeDtypeStruct((bsz, ATT_W, s), BF16),
        jax.ShapeDtypeStruct((bsz, IDX_HEADS, LANES, s), BF16),
        jax.ShapeDtypeStruct((bsz, s, LANES), BF16),
        jax.ShapeDtypeStruct((bsz, SUBLANES, s), F32),
    )
    out_specs = (
        pl.BlockSpec((1, tm, CONV_CH), lambda b, j: (b, j, 0)),
        pl.BlockSpec((1, ATT_HEADS, LANES, tm), lambda b, j: (b, 0, 0, j)),
        pl.BlockSpec((1, ATT_HEADS // 2, tm, LANES), lambda b, j: (b, 0, j, 0)),
        pl.BlockSpec((1, ATT_W, tm), lambda b, j: (b, 0, j)),
        pl.BlockSpec((1, IDX_HEADS, LANES, tm), lambda b, j: (b, 0, 0, j)),
        pl.BlockSpec((1, tm, LANES), lambda b, j: (b, j, 0)),
        pl.BlockSpec((1, SUBLANES, tm), lambda b, j: (b, 0, j)),
    )
    return pl.pallas_call(
        _inproj_kernel,
        grid=(bsz, nt),
        in_specs=[
            pl.BlockSpec((1, tm, d), lambda b, j: (b, j, 0)),
            pl.BlockSpec((1, 6, d), lambda b, j: (b, 0, 0)),
            full(g1.shape), full(wa.shape), full(wqkv.shape), full(widx.shape), full(wwi.shape),
            full(gq.shape), full(gk.shape), full(e2.shape),
            full(cw.shape), full(cb.shape), full(lng.shape), full(lnb.shape), full(goc.shape),
        ],
        out_specs=out_specs,
        out_shape=out_shape,
        scratch_shapes=[pltpu.VMEM((tm + HALO, CONV_CH), F32)],
        compiler_params=pltpu.CompilerParams(
            dimension_semantics=("arbitrary", "arbitrary"), vmem_limit_bytes=VMEM_LIMIT),
        name="inproj",
    )(x, mod3, g1, wa, wqkv, widx, wwi, gq, gk, e2, cw, cb, lng, lnb, goc)


def _t5_bucket_np(rel):
    half = REL_BUCKETS // 2
    max_exact = half // 2
    ret = np.where(rel > 0, half, 0)
    n = np.abs(rel)
    nf = np.maximum(n, 1).astype(np.float64)
    large = max_exact + (np.log(nf / max_exact) / math.log(REL_MAX_DIST / max_exact)
                         * (half - max_exact)).astype(np.int32)
    large = np.minimum(large, half - 1)
    return (ret + np.where(n < max_exact, n, large)).astype(np.int32)


NEAR = REL_MAX_DIST + TQ
FAR_BUCKET = REL_BUCKETS // 2 - 1


def _near_buckets():
    r = np.arange(NEAR)[:, None]
    t = np.arange(TQ)[None, :]
    return _t5_bucket_np(r - REL_MAX_DIST - t)


def _attn_kernel(q_ref, k_ref, vt_ref, qi_ref, ki_ref, wit_ref, bkt_ref, rb_ref, goa_ref, o_ref,
                 keys_s, hi_s, lo_s, maskb_s, relb_s, ot_s, m_s, mo_s, l_s, lg_s, *, nsel):
    b = pl.program_id(0)
    i = pl.program_id(1)
    t0 = i * TQ
    n_tiles = i + 1
    neg_inf = F32(-jnp.inf)

    @pl.when((b == 0) & (i == 0))
    def _():
        bk = bkt_ref[...]
        for h in range(ATT_HEADS):
            far = rb_ref[FAR_BUCKET, h]
            acc = jnp.zeros((NEAR, TQ), F32)
            for bb in range(REL_BUCKETS):
                acc = jnp.where(bk == bb, (rb_ref[bb, h] - far) * LOG2E, acc)
            relb_s[h, 0:2 * KT - NEAR, :] = jnp.zeros((2 * KT - NEAR, TQ), F32)
            relb_s[h, 2 * KT - NEAR:2 * KT, :] = acc

    qpos = t0 + lax.broadcasted_iota(I32, (1, TQ), 1)
    limit = (qpos // CHUNK + 1) * CHUNK
    row_iota = lax.broadcasted_iota(I32, (KT, TQ), 0)

    def tile_start(jt):
        return pl.multiple_of(jt * KT, KT)

    wi = wit_ref[0]

    def p1(jt, c):
        ks = tile_start(jt)
        kit = ki_ref[0, pl.ds(ks, KT), :]
        acc = jnp.zeros((KT, TQ), F32)
        for h in range(IDX_HEADS):
            lgt = jnp.dot(kit, qi_ref[0, h], preferred_element_type=F32)
            acc = acc + jnp.maximum(lgt, 0.0) * wi[h:h + 1, :]
        sc = jnp.where(row_iota + ks < limit, acc, neg_inf)
        bits = pltpu.bitcast(sc, I32)
        key = bits ^ (lax.shift_right_arithmetic(bits, 31) & 0x7FFFFFFF)
        keys_s[pl.ds(ks, KT), :] = key
        hi_s[pl.ds(ks, KT), :] = lax.shift_right_arithmetic(key, 16).astype(I16)
        lo_s[pl.ds(ks, KT), :] = ((key & 0xFFFF) ^ 0x8000).astype(I16)
        return c

    lax.fori_loop(0, n_tiles, p1, 0)

    def count_ge(half_ref, cand):
        c16 = cand.astype(I16)
        one = jnp.ones((KT, TQ), I16)
        zero = jnp.zeros((KT, TQ), I16)

        def body(jt, acc):
            hit = jnp.where(half_ref[pl.ds(tile_start(jt), KT), :] >= c16, one, zero)
            parts = [hit[r:r + PACK, :] for r in range(0, KT, PACK)]
            while len(parts) > 1:
                parts = [parts[k] + parts[k + 1] for k in range(0, len(parts), 2)]
            return acc + parts[0].astype(I32)
        acc = lax.fori_loop(0, n_tiles, body, jnp.zeros((PACK, TQ), I32))
        return jnp.sum(acc.astype(F32), axis=0, keepdims=True)

    def low_key(low):
        return (low ^ 0x8000) - jnp.where(low < 0x8000, 0x10000, 0)

    def select_thr():
        c0 = count_ge(hi_s, jnp.zeros((1, TQ), I32))
        ok = c0 >= nsel
        t = jnp.where(ok, 0, I16_MIN).astype(I32)
        cnt = jnp.where(ok, c0, F32(2 ** 30))

        def top_step(it, carry):
            t, cnt = carry
            cand = t + lax.shift_left(I32(1), I32(14) - it)
            c = count_ge(hi_s, cand)
            ok = c >= nsel
            return jnp.where(ok, cand, t), jnp.where(ok, c, cnt)

        top, cnt = lax.fori_loop(0, 15, top_step, (t, cnt))

        top_max = top >= I16_MAX
        above = jnp.where(top_max, 0.0, count_ge(hi_s, jnp.where(top_max, top, top + 1)))
        top16 = top.astype(I16)
        floor16 = jnp.full((KT, TQ), I16_MIN, I16)

        def keep_low(jt, c):
            rows = pl.ds(tile_start(jt), KT)
            lo_s[rows, :] = jnp.where(hi_s[rows, :] == top16, lo_s[rows, :], floor16)
            return c

        lax.fori_loop(0, n_tiles, keep_low, 0)

        at_zero = jnp.where(top == 0, above + count_ge(lo_s, low_key(jnp.ones((1, TQ), I32))), F32(2 ** 30))
        settled = at_zero < nsel

        def low_cond(carry):
            bit, _, cnt = carry
            return (bit >= 0) & (jnp.max(jnp.where(settled, F32(nsel), cnt)) > nsel)

        def low_step(carry):
            bit, low, cnt = carry
            cand = low + lax.shift_left(I32(1), bit)
            c = above + count_ge(lo_s, low_key(cand))
            ok = c >= nsel
            return bit - 1, jnp.where(ok, cand, low), jnp.where(ok, c, cnt)

        _, low, cnt = lax.while_loop(low_cond, low_step, (I32(15), jnp.zeros((1, TQ), I32), cnt))
        low_max = low >= 0xFFFF
        c_gt = above + jnp.where(low_max, 0.0, count_ge(lo_s, low_key(jnp.where(low_max, low, low + 1))))
        return lax.shift_left(top, 16) + low, cnt, c_gt

    thr, c_ge, c_gt = lax.cond(
        i > 0, select_thr,
        lambda: (jnp.full((1, TQ), KEY_NEG_INF + 1, I32), jnp.full((1, TQ), nsel, F32), jnp.zeros((1, TQ), F32)))

    def mask_plain():
        def p3(jt, c):
            ks = tile_start(jt)
            maskb_s[pl.ds(ks, KT), :] = jnp.where(keys_s[pl.ds(ks, KT), :] >= thr, 0.0, neg_inf)
            return c
        lax.fori_loop(0, n_tiles, p3, 0)
        return I32(0)

    def mask_ties():
        quota = nsel - c_gt
        tri =jnp.where(lax.broadcasted_iota(I32, (KT, KT), 0) >= lax.broadcasted_iota(I32, (KT, KT), 1),
                        1.0, 0.0).astype(BF16)

        def p3(jt, before):
            ks = tile_start(jt)
            kt = keys_s[pl.ds(ks, KT), :]
            eq = kt == thr
            rank = before + jnp.dot(tri, jnp.where(eq, 1.0, 0.0).astype(BF16), preferred_element_type=F32)
            tied = jnp.where(eq, jnp.where(rank <= quota, 0.0, neg_inf), neg_inf)
            maskb_s[pl.ds(ks, KT), :] = jnp.where(kt > thr, 0.0, tied)
            return rank[KT - 1:KT, :]

        lax.fori_loop(0, n_tiles, p3, jnp.zeros((1, TQ), F32))
        return I32(0)

    lax.cond(jnp.max(c_ge) > nsel, mask_ties, mask_plain)

    m_s[...] = jnp.full((ATT_HEADS, TQ), neg_inf, F32)
    l_s[...] = jnp.zeros((ATT_HEADS, TQ), F32)
    ot_s[...] = jnp.zeros((ATT_W, TQ), F32)

    def att_tile(jt, near):
        ks = tile_start(jt)
        mb = maskb_s[pl.ds(ks, KT), :]
        for h in range(ATT_HEADS):
            l = jnp.dot(k_ref[0, h // 2, pl.ds(ks, KT), :], q_ref[0, h], preferred_element_type=F32) + mb
            if near:
                off = pl.multiple_of((jt - (n_tiles - 2)) * KT, KT)
                l = l + relb_s[h, pl.ds(off, KT), :]
            lg_s[h] = l
            m_old = m_s[h:h + 1, :]
            mo_s[h:h + 1, :] = m_old
            m_s[h:h + 1, :] = jnp.maximum(m_old, jnp.max(l, axis=0, keepdims=True))
        for h in range(ATT_HEADS):
            m_new = m_s[h:h + 1, :]
            m_ref = jnp.where(m_new == neg_inf, 0.0, m_new)
            alpha = jnp.exp2(mo_s[h:h + 1, :] - m_ref)
            p = jnp.exp2(lg_s[h] - m_ref)
            l_s[h:h + 1, :] = alpha * l_s[h:h + 1, :] + jnp.sum(p, axis=0, keepdims=True)
            vt = vt_ref[0, h * HEAD_DIM:(h + 1) * HEAD_DIM, pl.ds(ks, KT)]
            rows = slice(h * HEAD_DIM, (h + 1) * HEAD_DIM)
            ot_s[rows, :] = ot_s[rows, :] * alpha + jnp.dot(vt, p.astype(BF16), preferred_element_type=F32)

    def far_tile(jt, c):
        att_tile(jt, False)
        return c

    def near_tile(jt, c):
        att_tile(jt, True)
        return c

    n_far = jnp.maximum(n_tiles - 2, 0)
    lax.fori_loop(0, n_far, far_tile, 0)
    lax.fori_loop(n_far, n_tiles, near_tile, 0)
    for h in range(ATT_HEADS):
        rows = slice(h * HEAD_DIM, (h + 1) * HEAD_DIM)
        ot_s[rows, :] = ot_s[rows, :] / l_s[h:h + 1, :]

    ot = ot_s[...]
    ms = jnp.mean(ot * ot, axis=0, keepdims=True)
    y = (ot * lax.rsqrt(ms + EPS)).T * goa_ref[...]
    o_ref[0] = y.astype(BF16)


def _attn(q, k, vt, qi, ki, wit, rel_bias, goa):
    bsz, _, _, s = q.shape
    nsel = min(TOPK_MAX, s // 4)
    assert nsel == TQ and s % TQ == 0, "attention kernel assumes TOPK_MAX-sized query blocks"
    bkt = jnp.asarray(_near_buckets())
    kern = functools.partial(_attn_kernel, nsel=nsel)
    return pl.pallas_call(
        kern,
        grid=(bsz, s // TQ),
        in_specs=[
            pl.BlockSpec((1, ATT_HEADS, LANES, TQ), lambda b, i: (b, 0, 0, i)),
            pl.BlockSpec((1, ATT_HEADS // 2, s, LANES), lambda b, i: (b, 0, 0, 0)),
            pl.BlockSpec((1, ATT_W, s), lambda b, i: (b, 0, 0)),
            pl.BlockSpec((1, IDX_HEADS, LANES, TQ), lambda b, i: (b, 0, 0, i)),
            pl.BlockSpec((1, s, LANES), lambda b, i: (b, 0, 0)),
            pl.BlockSpec((1, SUBLANES, TQ), lambda b, i: (b, 0, i)),
            pl.BlockSpec((NEAR, TQ), lambda b, i: (0, 0)),
            pl.BlockSpec(memory_space=pltpu.SMEM),
            pl.BlockSpec((1, ATT_W), lambda b, i: (0, 0)),
        ],
        out_specs=pl.BlockSpec((1, TQ, ATT_W), lambda b, i: (b, i, 0)),
        out_shape=jax.ShapeDtypeStruct((bsz, s, ATT_W), BF16),
        scratch_shapes=[
            pltpu.VMEM((s, TQ), I32),
            pltpu.VMEM((s, TQ), I16),
            pltpu.VMEM((s, TQ), I16),
            pltpu.VMEM((s, TQ), F32),
            pltpu.VMEM((ATT_HEADS, 2 * KT, TQ), F32),
            pltpu.VMEM((ATT_W, TQ), F32),
            pltpu.VMEM((ATT_HEADS, TQ), F32),
            pltpu.VMEM((ATT_HEADS, TQ), F32),
            pltpu.VMEM((ATT_HEADS, TQ), F32),
            pltpu.VMEM((ATT_HEADS, KT, TQ), F32),
        ],
        compiler_params=pltpu.CompilerParams(
            dimension_semantics=("arbitrary", "arbitrary"), vmem_limit_bytes=VMEM_LIMIT),
        name="attn",
    )(q, k, vt, qi, ki, wit, bkt, rel_bias, goa)


def _topk_rows(s, payload, k, order=None):
    rows = lax.broadcasted_iota(I32, s.shape, 0).astype(F32) if order is None else order
    vals, pays = [], []
    for _ in range(k):
        m = jnp.max(s, axis=0, keepdims=True)
        ix = jnp.min(jnp.where(s == m, rows, _NO_ROW), axis=0, keepdims=True)
        hit = rows == ix
        if payload is None:
            pays.append(ix)
        else:
            pays.append(jnp.max(jnp.where(hit, payload, -1.0), axis=0, keepdims=True))
        vals.append(m)
        s = jnp.where(hit, -jnp.inf, s)
    return jnp.concatenate(vals, axis=0), jnp.concatenate(pays, axis=0)


_NO_ROW = 1e9

_PAIR_GROUPS = (
    ((0, 0, 8, 0),), ((0, 8, 8, 0),), ((1, 0, 8, 0),),
    ((2, 0, 5, 0), (5, 0, 2, 5)), ((3, 0, 4, 0), (4, 0, 3, 4)), ((6, 0, 2, 0), (7, 0, 2, 2)),
)
assert sorted((a, b) for g in _PAIR_GROUPS for a, b0, nb, _ in g for b in range(b0, b0 + nb)) == sorted(
    (a, b) for a in range(PEER_TOPK // 2) for b in range(PEER_TOPK // (a + 1)))


def _mid_kernel(x_ref, cn_ref, an_ref, mod_ref, wo1_ref, wo2_ref, g2_ref, wpq_ref, k1_ref, k2_ref,
                x1_ref, h2_ref, idx_ref, gate_ref, qq_s, idt_s, gt_s):
    tm = x_ref.shape[1]
    x = x_ref[0]
    gt1 = mod_ref[0, 2:3, :]
    sh2 = mod_ref[0, 3:4, :]
    sc2 = mod_ref[0, 4:5, :]
    proj = (jnp.dot(cn_ref[0], wo1_ref[...], preferred_element_type=F32)
            + jnp.dot(an_ref[0], wo2_ref[...], preferred_element_type=F32))
    x1 = x + gt1 * proj
    x1_ref[0] = x1
    r = lax.rsqrt(jnp.mean(x1 * x1, axis=-1, keepdims=True) + EPS)
    h2 = (x1 * r) * g2_ref[...] * (1.0 + sc2) + sh2
    h2_ref[0] = h2
    qq_s[...] = jnp.dot(h2.astype(BF16), wpq_ref[...], preferred_element_type=F32).astype(BF16)

    def route_unit(hh, lt):
        rows = pl.ds(lt * LANES, LANES)
        q1 = qq_s[rows, pl.ds(pl.multiple_of(hh * 2 * N_KEYS, LANES), N_KEYS)]
        q2 = qq_s[rows, pl.ds(pl.multiple_of(hh * 2 * N_KEYS + N_KEYS, LANES), N_KEYS)]
        v1, i1 = _topk_rows(_nt(k1_ref[hh], q1), None, PEER_TOPK)
        v2, i2 = _topk_rows(_nt(k2_ref[hh], q2), None, PEER_TOPK)
        sub = lax.broadcasted_iota(I32, (SUBLANES, LANES), 0)
        subf = sub.astype(F32)
        cands, cidxs, flats = [], [], []
        for group in _PAIR_GROUPS:
            val = jnp.full((SUBLANES, LANES), -jnp.inf, F32)
            cid = jnp.zeros((SUBLANES, LANES), F32)
            flat = jnp.full((SUBLANES, LANES), _NO_ROW, F32)
            for a, b0, nb, off in group:
                v2s, i2s = v2[b0:b0 + SUBLANES, :], i2[b0:b0 + SUBLANES, :]
                if off:
                    v2s, i2s = pltpu.roll(v2s, off, 0), pltpu.roll(i2s, off, 0)
                inside = lambda new, old: jnp.where(sub < off + nb, jnp.where(sub >= off, new, old), old)
                val = inside(v1[a:a + 1, :] + v2s, val)
                cid = inside(i1[a:a + 1, :] * float(N_KEYS) + i2s, cid)
                flat = inside(subf + float(a * PEER_TOPK + b0 - off), flat)
            cands.append(val)
            cidxs.append(cid)
            flats.append(flat)
        half = PEER_TOPK // 2
        cands.append(v1[half:, :] + v2[0:1, :])
        cidxs.append(i1[half:, :] * float(N_KEYS) + i2[0:1, :])
        flats.append((subf + float(half)) * float(PEER_TOPK))
        best, experts = _topk_rows(jnp.concatenate(cands, axis=0), jnp.concatenate(cidxs, axis=0), PEER_TOPK,
                                   order=jnp.concatenate(flats, axis=0))
        e = jnp.exp(best - best[0:1, :])
        g = e / jnp.sum(e, axis=0, keepdims=True)
        slots = pl.ds(pl.multiple_of(hh * PEER_TOPK, PEER_TOPK), PEER_TOPK)
        cols = pl.ds(lt * LANES, LANES)
        idt_s[slots, cols] = experts * float(ROWS_PER_EXPERT)
        gt_s[slots, cols] = g

    def route(hh, c):
        for lt in range(tm // LANES):
            route_unit(hh, lt)
        return c

    lax.fori_loop(0, PEER_HEADS, route, 0)
    idx_ref[0] = idt_s[...].T.astype(I32)
    gate_ref[0] = gt_s[...].T


def _mid(x, cn, an, mod3, wo1, wo2, g2, wpq, k1, k2):
    bsz, s, d = x.shape
    tm = TM_MID
    full = lambda shape: pl.BlockSpec(shape, lambda b, j: (0,) * len(shape))
    tok = lambda w: pl.BlockSpec((1, tm, w), lambda b, j: (b, j, 0))
    return pl.pallas_call(
        _mid_kernel,
        grid=(bsz, s // tm),
        in_specs=[tok(d), tok(CONV_CH), tok(ATT_W), pl.BlockSpec((1, 6, d), lambda b, j: (b, 0, 0)),
                  full(wo1.shape), full(wo2.shape), full(g2.shape), full(wpq.shape), full(k1.shape), full(k2.shape)],
        out_specs=(tok(d), tok(d), tok(PEER_SLOTS), tok(PEER_SLOTS)),
        out_shape=(
            jax.ShapeDtypeStruct((bsz, s, d), F32),
            jax.ShapeDtypeStruct((bsz, s, d), F32),
            jax.ShapeDtypeStruct((bsz, s, PEER_SLOTS), I32),
            jax.ShapeDtypeStruct((bsz, s, PEER_SLOTS), F32),
        ),
        scratch_shapes=[
            pltpu.VMEM((tm, PEER_HEADS * 2 * N_KEYS), BF16),
            pltpu.VMEM((PEER_SLOTS, tm), F32),
            pltpu.VMEM((PEER_SLOTS, tm), F32),
        ],
        compiler_params=pltpu.CompilerParams(
            dimension_semantics=("arbitrary", "arbitrary"), vmem_limit_bytes=VMEM_LIMIT),
        name="mid",
    )(x, cn, an, mod3, wo1, wo2, g2, wpq, k1, k2)


def _pack_table(t):
    tb = t.astype(BF16)
    lo = lax.bitcast_convert_type(tb[:, :HALF], jnp.uint16).astype(jnp.uint32)
    hi = lax.bitcast_convert_type(tb[:, HALF:], jnp.uint16).astype(jnp.uint32)
    return (lo | (hi << 16)).reshape(t.shape[0] * ROWS_PER_EXPERT, LANES)


def _unpack(w):
    lo = pltpu.bitcast(lax.shift_left(w, jnp.uint32(16)), F32)
    hi = pltpu.bitcast(w & jnp.uint32(0xFFFF0000), F32)
    return lo, hi


def _gather_row(tab_ref, row):
    return tab_ref[pl.ds(pl.multiple_of(row, ROWS_PER_EXPERT), ROWS_PER_EXPERT), :]


STAGE_ROWS = PEER_SLOTS * ROWS_PER_EXPERT


def _token_pieces(xg, r, first, count):
    return jnp.concatenate(
        [xg[r:r + 1, (first + q) * LANES:(first + q + 1) * LANES] for q in range(count)], axis=0)


def _peer_u_kernel(idx_ref, tab_ref, x_ref, gate_ref, w_ref, *stages):
    tb = w_ref.shape[0]
    sub = lax.broadcasted_iota(I32, (SUBLANES, PEER_SLOTS), 0)

    def group(g, c):
        for k in range(PEER_GROUPS):
            eight(pl.multiple_of((g * PEER_GROUPS + k) * SUBLANES, SUBLANES), stages[k * SUBLANES:(k + 1) * SUBLANES])
        return c

    def eight(t0, stages):
        act8 = jnp.zeros((SUBLANES, PEER_SLOTS), F32)
        xg = x_ref[pl.ds(t0, SUBLANES), :]
        xlo, xhi, rows_t = [], [], []
        for r in range(SUBLANES):
            xlo.append(jnp.concatenate([_token_pieces(xg, r, 0, ROWS_PER_EXPERT)] * 2, axis=0))
            xhi.append(jnp.concatenate([_token_pieces(xg, r, ROWS_PER_EXPERT, ROWS_PER_EXPERT)] * 2, axis=0))
            rows_t.append(idx_ref.at[t0 + r])
        for j in range(0, PEER_SLOTS, 2):
            for r in range(SUBLANES):
                pair = jnp.concatenate(
                    [_gather_row(tab_ref, rows_t[r][j]), _gather_row(tab_ref, rows_t[r][j + 1])], axis=0)
                lo, hi = _unpack(pair)
                row = j * ROWS_PER_EXPERT
                stages[r][row:row + 2 * ROWS_PER_EXPERT, :] = lo * xlo[r] + hi * xhi[r]
        for r in range(SUBLANES):
            stage = stages[r]
            cs = stage[pl.ds(0, PEER_SLOTS, stride=ROWS_PER_EXPERT), :]
            for q in range(1, ROWS_PER_EXPERT):
                cs = cs + stage[pl.ds(q, PEER_SLOTS, stride=ROWS_PER_EXPERT), :]
            act = jnp.sum(cs.T, axis=0, keepdims=True)
            act8 = jnp.where(sub == r, act, act8)
        rows = pl.ds(t0, SUBLANES)
        w_ref[rows, :] = gate_ref[rows, :] * jax.nn.gelu(act8)

    lax.fori_loop(0, tb // (PEER_GROUPS * SUBLANES), group, 0)


def _peer_u(idx, tab, h2, gate):
    n, d = h2.shape
    tb = TB_PEER
    return pl.pallas_call(
        _peer_u_kernel,
        grid=(n // tb,),
        in_specs=[
            pl.BlockSpec((tb, PEER_SLOTS), lambda i: (i, 0), memory_space=pltpu.SMEM),
            pl.BlockSpec(tab.shape, lambda i: (0, 0), pipeline_mode=pl.Buffered(1)),
            pl.BlockSpec((tb, d), lambda i: (i, 0)),
            pl.BlockSpec((tb, PEER_SLOTS), lambda i: (i, 0)),
        ],
        out_specs=pl.BlockSpec((tb, PEER_SLOTS), lambda i: (i, 0)),
        out_shape=jax.ShapeDtypeStruct((n, PEER_SLOTS), F32),
        scratch_shapes=[pltpu.VMEM((STAGE_ROWS, LANES), F32) for _ in range(PEER_GROUPS * SUBLANES)],
        compiler_params=pltpu.CompilerParams(
            dimension_semantics=("arbitrary",), vmem_limit_bytes=VMEM_LIMIT),
        name="peer_u",
    )(idx, tab, h2, gate)


PV_PAIRS = SUBLANES // 2


def _peer_v_kernel(idx_ref, w_ref, tab_ref, x1_ref, gt2_ref, e8_ref, mask_ref, o_ref, *stages):
    tb = idx_ref.shape[0]
    gt2 = gt2_ref[0]
    mask = mask_ref[...]
    e8 = e8_ref[...]

    def group(g, c):
        for k in range(PEER_GROUPS_V):
            eight(pl.multiple_of((g * PEER_GROUPS_V + k) * SUBLANES, SUBLANES), stages[k * PV_PAIRS:(k + 1) * PV_PAIRS])
        return c

    def eight(g0, stages):
        peers = []
        for p in range(PV_PAIRS):
            stage = stages[p]
            t0 = g0 + 2 * p
            for u in range(2):
                rows_t = idx_ref.at[t0 + u]
                for j in range(PEER_SLOTS):
                    stage[j * ROWS_PER_EXPERT:(j + 1) * ROWS_PER_EXPERT, u * LANES:(u + 1) * LANES] = (
                        _gather_row(tab_ref, rows_t[j]))
            w2 = w_ref[pl.ds(t0, 2), :]
            hi = w2.astype(BF16).astype(F32)
            rep = jnp.dot(jnp.concatenate([hi, w2 - hi], axis=0).astype(BF16), e8,
                          preferred_element_type=F32)
            lhs = jnp.concatenate([rep[i:i + 1, :] * mask for i in range(4)], axis=0).astype(BF16)
            out = jnp.dot(lhs, pltpu.bitcast(stage[...], BF16), preferred_element_type=F32)
            for u in range(2):
                r0 = u * SUBLANES
                peers.append(out[r0:r0 + SUBLANES, u * LANES:(u + 1) * LANES]
                             + out[2 * SUBLANES + r0:3 * SUBLANES + r0, u * LANES:(u + 1) * LANES])
        rows = pl.ds(g0, SUBLANES)
        for q in range(SUBLANES):
            cols = slice(q * LANES, (q + 1) * LANES)
            piece = jnp.concatenate([peer[q:q + 1, :] for peer in peers], axis=0)
            o_ref[rows, cols] = x1_ref[rows, cols] + gt2[:, cols] * piece

    lax.fori_loop(0, tb // (PEER_GROUPS_V * SUBLANES), group, 0)


def _peer_v(idx, w, tab, x1, gt2, blocks_per_batch):
    n, d = x1.shape
    tb = TB_PEER
    q = np.arange(SUBLANES)
    piece = 2 * (q % ROWS_PER_EXPERT) + q // ROWS_PER_EXPERT
    lane = np.arange(PEER_SLOTS * SUBLANES)
    mask = jnp.asarray((lane[None, :] % SUBLANES == piece[:, None]).astype(np.float32))
    e8 = jnp.asarray(np.arange(PEER_SLOTS)[:, None] == lane[None, :] // SUBLANES, BF16)
    return pl.pallas_call(
        _peer_v_kernel,
        grid=(n // tb,),
        in_specs=[
            pl.BlockSpec((tb, PEER_SLOTS), lambda i: (i, 0), memory_space=pltpu.SMEM),
            pl.BlockSpec((tb, PEER_SLOTS), lambda i: (i, 0)),
            pl.BlockSpec(tab.shape, lambda i: (0, 0), pipeline_mode=pl.Buffered(1)),
            pl.BlockSpec((tb, d), lambda i: (i, 0)),
            pl.BlockSpec((1, 1, d), lambda i: (i // blocks_per_batch, 0, 0)),
            pl.BlockSpec(e8.shape, lambda i: (0, 0)),
            pl.BlockSpec(mask.shape, lambda i: (0, 0)),
        ],
        out_specs=pl.BlockSpec((tb, d), lambda i: (i, 0)),
        out_shape=jax.ShapeDtypeStruct((n, d), F32),
        scratch_shapes=[pltpu.VMEM((STAGE_ROWS, 2 * LANES), jnp.uint32) for _ in range(PEER_GROUPS_V * PV_PAIRS)],
        compiler_params=pltpu.CompilerParams(
            dimension_semantics=("arbitrary",), vmem_limit_bytes=VMEM_LIMIT),
        name="peer_v",
    )(idx, w, tab, x1, gt2, e8, mask)


def _layer(x, mod, g_norm1, g_norm2, w_in, q_norm_g, k_norm_g, conv_w, conv_b, conv_ln_g, conv_ln_b,
           rel_bias, g_out_conv, g_out_attn, w_out, w_peer_q, peer_k1, peer_k2, peer_u, peer_v):
    bsz, s, d = x.shape
    n = bsz * s
    mod3 = mod.reshape(bsz, 6, d)
    row = lambda a: a.reshape(1, -1)

    c0 = 2 * CONV_CH
    c1 = c0 + 3 * ATT_W
    c2 = c1 + IDX_HEADS * IDX_DIM
    c3 = c2 + IDX_DIM
    wa = w_in[:, :c0].astype(BF16)
    wqkv = w_in[:, c0:c1].astype(BF16)
    widx = jnp.concatenate([w_in[:, c1:c2], w_in[:, c2:c3], w_in[:, c2:c3]], axis=1).astype(BF16)
    wwi = jnp.zeros((SUBLANES, d), F32).at[:IDX_HEADS].set(w_in[:, c3:c3 + IDX_HEADS].T).astype(BF16)
    head = np.arange(ATT_W) // HEAD_DIM
    e2 = jnp.asarray(head[:, None] == head[None, :], BF16)

    conv_n, q, k, vt, qi, ki, wit = _inproj(
        x, mod3, row(g_norm1), wa, wqkv, widx, wwi,
        row(jnp.tile(q_norm_g, ATT_HEADS)), row(jnp.tile(k_norm_g, ATT_HEADS)), e2,
        conv_w.reshape(CONV_WIDTH, CONV_CH), row(conv_b), row(conv_ln_g), row(conv_ln_b), row(g_out_conv))
    attn_n = _attn(q, k, vt, qi, ki, wit, rel_bias, row(g_out_attn))

    x1, h2, idx, gate = _mid(
        x, conv_n, attn_n, mod3, w_out[:CONV_CH].astype(BF16), w_out[CONV_CH:].astype(BF16), row(g_norm2),
        w_peer_q.astype(BF16), peer_k1.astype(BF16), peer_k2.astype(BF16))

    idx = idx.reshape(n, PEER_SLOTS)
    w = _peer_u(idx, _pack_table(peer_u), h2.reshape(n, d), gate.reshape(n, PEER_SLOTS))
    out = _peer_v(idx, w, _pack_table(peer_v), x1.reshape(n, d), mod3[:, 5:6, :], s // TB_PEER)
    return out.reshape(bsz, s, d)


def kernel(x, c, w_ada, b_ada, g_norm1, g_norm2, w_in, q_norm_g, k_norm_g, conv_w, conv_b, conv_ln_g,
           conv_ln_b, rel_bias, g_out_conv, g_out_attn, w_out, w_peer_q, peer_k1, peer_k2, peer_u, peer_v):
    depth = w_ada.shape[0]
    for l in range(depth):
        mod = _ada(c, w_ada[l], b_ada[l])
        x = _layer(x, mod, g_norm1[l], g_norm2[l], w_in[l], q_norm_g[l], k_norm_g[l], conv_w[l], conv_b[l],
                   conv_ln_g[l], conv_ln_b[l], rel_bias, g_out_conv[l], g_out_attn[l], w_out[l],
                   w_peer_q[l], peer_k1[l], peer_k2[l], peer_u[l], peer_v[l])
    return x
```

```python
import functools
import math

import numpy as np
import jax
import jax.numpy as jnp
from jax import lax
from jax.experimental import pallas as pl
from jax.experimental.pallas import tpu as pltpu

F32 = jnp.float32
BF16 = jnp.bfloat16
I32 = jnp.int32
I16 = jnp.int16
HIGHEST = lax.Precision.HIGHEST

D_MODEL = 1024
CHUNK = 64
CONV_CH = 512
CONV_WIDTH = 31
ATT_HEADS = 8
HEAD_DIM = 64
ATT_W = ATT_HEADS * HEAD_DIM
IDX_HEADS = 4
IDX_DIM = 64
IDX_SCALE = (IDX_HEADS * IDX_DIM) ** -0.5
TOPK_MAX = 256
REL_BUCKETS = 32
REL_MAX_DIST = 128
PEER_HEADS = 8
N_KEYS = 128
N_EXPERTS = N_KEYS * N_KEYS
PEER_TOPK = 16
PEER_SLOTS = PEER_HEADS * PEER_TOPK
EPS = 1e-6
LOG2E = math.log2(math.e)

LANES = 128
SUBLANES = 8
PACK = 16
VMEM_LIMIT = 56 * 1024 * 1024

TM_IN = 512
CONV_ROWS = 64
HALO = 32
TQ = 256
KT = 256
TM_MID = 1024
TB_PEER = 256
PEER_GROUPS = 2
PEER_GROUPS_V = 4
HALF = D_MODEL // 2
ROWS_PER_EXPERT = HALF // LANES

NT_DIMS = (((1,), (1,)), ((), ()))

_NEG_INF_BITS = int(np.array(-np.inf, np.float32).view(np.int32))
KEY_NEG_INF = _NEG_INF_BITS ^ 0x7FFFFFFF
I16_MIN = -(2 ** 15)
I16_MAX = 2 ** 15 - 1


def _nt(a, b, precision=None):
    return lax.dot_general(a, b, NT_DIMS, precision=precision, preferred_element_type=F32)


def _ada_kernel(c_ref, w_ref, b_ref, o_ref):
    a = jax.nn.silu(c_ref[...])
    o_ref[...] = jnp.dot(a, w_ref[...], precision=HIGHEST, preferred_element_type=F32) + b_ref[...]


def _ada(c, w_ada, b_ada):
    bsz, d = c.shape
    return pl.pallas_call(
        _ada_kernel,
        grid=(6,),
        in_specs=[
            pl.BlockSpec((bsz, d), lambda j: (0, 0)),
            pl.BlockSpec((d, d), lambda j: (0, j)),
            pl.BlockSpec((1, d), lambda j: (0, j)),
        ],
        out_specs=pl.BlockSpec((bsz, d), lambda j: (0, j)),
        out_shape=jax.ShapeDtypeStruct((bsz, 6 * d), F32),
        name="ada",
    )(c, w_ada, b_ada.reshape(1, 6 * d))


def _inproj_kernel(x_ref, mod_ref, g1_ref, wa_ref, wqkv_ref, widx_ref, wwi_ref, gq_ref, gk_ref, e2_ref,
                   cw_ref, cb_ref, lng_ref, lnb_ref, goc_ref,
                   conv_ref, q_ref, k_ref, vt_ref, qi_ref, ki_ref, wit_ref, ubuf):
    j = pl.program_id(1)
    tm = x_ref.shape[1]
    x = x_ref[0]
    sh1 = mod_ref[0, 0:1, :]
    sc1 = mod_ref[0, 1:2, :]
    r = lax.rsqrt(jnp.mean(x * x, axis=-1, keepdims=True) + EPS)
    h = (x * r) * g1_ref[...] * (1.0 + sc1) + sh1
    hb = h.astype(BF16)

    pa = jnp.dot(hb, wa_ref[...], preferred_element_type=F32)
    u = pa[:, :CONV_CH] * jax.nn.sigmoid(pa[:, CONV_CH:])

    @pl.when(j == 0)
    def _():
        ubuf[0:HALO, :] = jnp.zeros((HALO, CONV_CH), F32)

    ubuf[HALO:HALO + tm, :] = u
    first = HALO - (CONV_WIDTH - 1)
    for rb in range(tm // CONV_ROWS):
        base = rb * CONV_ROWS
        acc = jnp.zeros((CONV_ROWS, CONV_CH), F32) + cb_ref[...]
        for phase in range(SUBLANES):
            taps = [t for t in range(CONV_WIDTH) if (first + t) % SUBLANES == phase]
            span = (first + taps[-1]) - phase + CONV_ROWS
            slab = ubuf[base + phase:base + phase + span, :]
            for t in taps:
                off = first + t - phase
                acc = acc + cw_ref[t:t + 1, :] * slab[off:off + CONV_ROWS, :]
        mu = jnp.mean(acc, axis=-1, keepdims=True)
        xc = acc - mu
        y = xc * lax.rsqrt(jnp.mean(xc * xc, axis=-1, keepdims=True) + EPS)
        y = jax.nn.silu(y * lng_ref[...] + lnb_ref[...])
        y = y * lax.rsqrt(jnp.mean(y * y, axis=-1, keepdims=True) + EPS) * goc_ref[...]
        conv_ref[0, base:base + CONV_ROWS, :] = y.astype(BF16)
    ubuf[0:HALO, :] = ubuf[tm:tm + HALO, :]

    pq = jnp.dot(hb, wqkv_ref[...], preferred_element_type=F32)
    q = pq[:, :ATT_W]
    k = pq[:, ATT_W:2 * ATT_W]
    v = pq[:, 2 * ATT_W:]
    e2 = e2_ref[...]

    def head_mean(sq):
        hi = sq.astype(BF16)
        lo = (sq - hi.astype(F32)).astype(BF16)
        return (jnp.dot(hi, e2, preferred_element_type=F32)
                + jnp.dot(lo, e2, preferred_element_type=F32)) * (1.0 / HEAD_DIM)

    qs = head_mean(q * q)
    ks = head_mean(k * k)
    qn = q * lax.rsqrt(qs + EPS) * gq_ref[...] * (HEAD_DIM ** -0.5 * LOG2E)
    kn = k * lax.rsqrt(ks + EPS) * gk_ref[...]
    lane = lax.broadcasted_iota(I32, (tm, LANES), 1)
    low = lane < HEAD_DIM
    for p in range(ATT_HEADS // 2):
        slab = qn[:, p * LANES:(p + 1) * LANES]
        q_ref[0, 2 * p] = jnp.where(low, slab, 0.0).T.astype(BF16)
        q_ref[0, 2 * p + 1] = jnp.where(low, 0.0, slab).T.astype(BF16)
        k_ref[0, p] = kn[:, p * LANES:(p + 1) * LANES].astype(BF16)
    vt_ref[0] = v.T.astype(BF16)

    pc = jnp.dot(hb, widx_ref[...], preferred_element_type=F32)
    for p in range(IDX_HEADS // 2):
        slab = pc[:, p * LANES:(p + 1) * LANES]
        qi_ref[0, 2 * p] = jnp.where(low, slab, 0.0).T.astype(BF16)
        qi_ref[0, 2 * p + 1] = jnp.where(low, 0.0, slab).T.astype(BF16)
    ki_ref[0] = pc[:, 2 * LANES:3 * LANES].astype(BF16)
    wit_ref[0] = _nt(wwi_ref[...], hb) * IDX_SCALE


def _inproj(x, mod3, g1, wa, wqkv, widx, wwi, gq, gk, e2, cw, cb, lng, lnb, goc):
    bsz, s, d = x.shape
    tm = TM_IN
    nt = s // tm
    full = lambda shape: pl.BlockSpec(shape, lambda b, j: (0,) * len(shape))
    out_shape = (
        jax.ShapeDtypeStruct((bsz, s, CONV_CH), BF16),
        jax.ShapeDtypeStruct((bsz, ATT_HEADS, LANES, s), BF16),
        jax.ShapeDtypeStruct((bsz, ATT_HEADS // 2, s, LANES), BF16),
        jax.ShapeDtypeStruct((bsz, ATT_W, s), BF16),
        jax.ShapeDtypeStruct((bsz, IDX_HEADS, LANES, s), BF16),
        jax.ShapeDtypeStruct((bsz, s, LANES), BF16),
        jax.ShapeDtypeStruct((bsz, SUBLANES, s), F32),
    )
    out_specs = (
        pl.BlockSpec((1, tm, CONV_CH), lambda b, j: (b, j, 0)),
        pl.BlockSpec((1, ATT_HEADS, LANES, tm), lambda b, j: (b, 0, 0, j)),
        pl.BlockSpec((1, ATT_HEADS // 2, tm, LANES), lambda b, j: (b, 0, j, 0)),
        pl.BlockSpec((1, ATT_W, tm), lambda b, j: (b, 0, j)),
        pl.BlockSpec((1, IDX_HEADS, LANES, tm), lambda b, j: (b, 0, 0, j)),
        pl.BlockSpec((1, tm, LANES), lambda b, j: (b, j, 0)),
        pl.BlockSpec((1, SUBLANES, tm), lambda b, j: (b, 0, j)),
    )
    return pl.pallas_call(
        _inproj_kernel,
        grid=(bsz, nt),
        in_specs=[
            pl.BlockSpec((1, tm, d), lambda b, j: (b, j, 0)),
            pl.BlockSpec((1, 6, d), lambda b, j: (b, 0, 0)),
            full(g1.shape), full(wa.shape), full(wqkv.shape), full(widx.shape), full(wwi.shape),
            full(gq.shape), full(gk.shape), full(e2.shape),
            full(cw.shape), full(cb.shape), full(lng.shape), full(lnb.shape), full(goc.shape),
        ],
        out_specs=out_specs,
        out_shape=out_shape,
        scratch_shapes=[pltpu.VMEM((tm + HALO, CONV_CH), F32)],
        compiler_params=pltpu.CompilerParams(
            dimension_semantics=("arbitrary", "arbitrary"), vmem_limit_bytes=VMEM_LIMIT),
        name="inproj",
    )(x, mod3, g1, wa, wqkv, widx, wwi, gq, gk, e2, cw, cb, lng, lnb, goc)


def _t5_bucket_np(rel):
    half = REL_BUCKETS // 2
    max_exact = half // 2
    ret = np.where(rel > 0, half, 0)
    n = np.abs(rel)
    nf = np.maximum(n, 1).astype(np.float64)
    large = max_exact + (np.log(nf / max_exact) / math.log(REL_MAX_DIST / max_exact)
                         * (half - max_exact)).astype(np.int32)
    large = np.minimum(large, half - 1)
    return (ret + np.where(n < max_exact, n, large)).astype(np.int32)


NEAR = REL_MAX_DIST + TQ
FAR_BUCKET = REL_BUCKETS // 2 - 1


def _near_buckets():
    r = np.arange(NEAR)[:, None]
    t = np.arange(TQ)[None, :]
    return _t5_bucket_np(r - REL_MAX_DIST - t)


def _attn_kernel(q_ref, k_ref, vt_ref, qi_ref, ki_ref, wit_ref, bkt_ref, rb_ref, goa_ref, o_ref,
                 keys_s, hi_s, lo_s, maskb_s, relb_s, ot_s, m_s, mo_s, mn_s, l_s, lg_s, *, nsel):
    b = pl.program_id(0)
    i = pl.program_id(1)
    t0 = i * TQ
    n_tiles = i + 1
    neg_inf = F32(-jnp.inf)

    @pl.when((b == 0) & (i == 0))
    def _():
        bk = bkt_ref[...]
        for h in range(ATT_HEADS):
            far = rb_ref[FAR_BUCKET, h]
            acc = jnp.zeros((NEAR, TQ), F32)
            for bb in range(REL_BUCKETS):
                acc = jnp.where(bk == bb, (rb_ref[bb, h] - far) * LOG2E, acc)
            relb_s[h, 0:2 * KT - NEAR, :] = jnp.zeros((2 * KT - NEAR, TQ), F32)
            relb_s[h, 2 * KT - NEAR:2 * KT, :] = acc

    qpos = t0 + lax.broadcasted_iota(I32, (1, TQ), 1)
    limit = (qpos // CHUNK + 1) * CHUNK
    row_iota = lax.broadcasted_iota(I32, (KT, TQ), 0)

    def tile_start(jt):
        return pl.multiple_of(jt * KT, KT)

    wi = wit_ref[0]

    def p1(jt, c):
        ks = tile_start(jt)
        kit = ki_ref[0, pl.ds(ks, KT), :]
        acc = jnp.zeros((KT, TQ), F32)
        for h in range(IDX_HEADS):
            lgt = jnp.dot(kit, qi_ref[0, h], preferred_element_type=F32)
            acc = acc + jnp.maximum(lgt, 0.0) * wi[h:h + 1, :]
        sc = jnp.where(row_iota + ks < limit, acc, neg_inf)
        bits = pltpu.bitcast(sc, I32)
        key = bits ^ (lax.shift_right_arithmetic(bits, 31) & 0x7FFFFFFF)
        keys_s[pl.ds(ks, KT), :] = key
        hi_s[pl.ds(ks, KT), :] = lax.shift_right_arithmetic(key, 16).astype(I16)
        lo_s[pl.ds(ks, KT), :] = ((key & 0xFFFF) ^ 0x8000).astype(I16)
        return c

    lax.fori_loop(0, n_tiles, p1, 0)

    def count_ge(half_ref, cand):
        c16 = cand.astype(I16)
        one = jnp.ones((KT, TQ), I16)
        zero = jnp.zeros((KT, TQ), I16)

        def body(jt, acc):
            hit = jnp.where(half_ref[pl.ds(tile_start(jt), KT), :] >= c16, one, zero)
            parts = [hit[r:r + PACK, :] for r in range(0, KT, PACK)]
            while len(parts) > 1:
                parts = [parts[k] + parts[k + 1] for k in range(0, len(parts), 2)]
            return acc + parts[0].astype(I32)
        acc = lax.fori_loop(0, n_tiles, body, jnp.zeros((PACK, TQ), I32))
        return jnp.sum(acc.astype(F32), axis=0, keepdims=True)

    def low_key(low):
        return (low ^ 0x8000) - jnp.where(low < 0x8000, 0x10000, 0)

    def select_thr():
        c0 = count_ge(hi_s, jnp.zeros((1, TQ), I32))
        ok = c0 >= nsel
        t = jnp.where(ok, 0, I16_MIN).astype(I32)
        cnt = jnp.where(ok, c0, F32(2 ** 30))

        def top_step(it, carry):
            t, cnt = carry
            cand = t + lax.shift_left(I32(1), I32(14) - it)
            c = count_ge(hi_s, cand)
            ok = c >= nsel
            return jnp.where(ok, cand, t), jnp.where(ok, c, cnt)

        top, cnt = lax.fori_loop(0, 15, top_step, (t, cnt))

        top_max = top >= I16_MAX
        above = jnp.where(top_max, 0.0, count_ge(hi_s, jnp.where(top_max, top, top + 1)))
        top16 = top.astype(I16)
        floor16 = jnp.full((KT, TQ), I16_MIN, I16)

        def keep_low(jt, c):
            rows = pl.ds(tile_start(jt), KT)
            lo_s[rows, :] = jnp.where(hi_s[rows, :] == top16, lo_s[rows, :], floor16)
            return c

        lax.fori_loop(0, n_tiles, keep_low, 0)

        at_zero = jnp.where(top == 0, above + count_ge(lo_s, low_key(jnp.ones((1, TQ), I32))), F32(2 ** 30))
        settled = at_zero < nsel

        def low_cond(carry):
            bit, _, cnt = carry
            return (bit >= 0) & (jnp.max(jnp.where(settled, F32(nsel), cnt)) > nsel)

        def low_step(carry):
            bit, low, cnt = carry
            cand = low + lax.shift_left(I32(1), bit)
            c = above + count_ge(lo_s, low_key(cand))
            ok = c >= nsel
            return bit - 1, jnp.where(ok, cand, low), jnp.where(ok, c, cnt)

        _, low, cnt = lax.while_loop(low_cond, low_step, (I32(15), jnp.zeros((1, TQ), I32), cnt))
        low_max = low >= 0xFFFF
        c_gt = above + jnp.where(low_max, 0.0, count_ge(lo_s, low_key(jnp.where(low_max, low, low + 1))))
        return lax.shift_left(top, 16) + low, cnt, c_gt

    thr, c_ge, c_gt = lax.cond(
        i > 0, select_thr,
        lambda: (jnp.full((1, TQ), KEY_NEG_INF + 1, I32), jnp.full((1, TQ), nsel, F32), jnp.zeros((1, TQ), F32)))

    def mask_plain():
        def p3(jt, c):
            ks = tile_start(jt)
            maskb_s[pl.ds(ks, KT), :] = jnp.where(keys_s[pl.ds(ks, KT), :] >= thr, 0.0, neg_inf)
            return c
        lax.fori_loop(0, n_tiles, p3, 0)
        return I32(0)

    def mask_ties():
        quota = nsel - c_gt
        tri =jnp.where(lax.broadcasted_iota(I32, (KT, KT), 0) >= lax.broadcasted_iota(I32, (KT, KT), 1),
                        1.0, 0.0).astype(BF16)

        def p3(jt, before):
            ks = tile_start(jt)
            kt = keys_s[pl.ds(ks, KT), :]
            eq = kt == thr
            rank = before + jnp.dot(tri, jnp.where(eq, 1.0, 0.0).astype(BF16), preferred_element_type=F32)
            tied = jnp.where(eq, jnp.where(rank <= quota, 0.0, neg_inf), neg_inf)
            maskb_s[pl.ds(ks, KT), :] = jnp.where(kt > thr, 0.0, tied)
            return rank[KT - 1:KT, :]

        lax.fori_loop(0, n_tiles, p3, jnp.zeros((1, TQ), F32))
        return I32(0)

    lax.cond(jnp.max(c_ge) > nsel, mask_ties, mask_plain)

    m_s[...] = jnp.full((ATT_HEADS, TQ), neg_inf, F32)
    l_s[...] = jnp.zeros((ATT_HEADS, TQ), F32)
    ot_s[...] = jnp.zeros((ATT_W, TQ), F32)

    def att_tiles(jts, near):
        starts = [tile_start(jt) for jt in jts]
        for n, (jt, ks) in enumerate(zip(jts, starts)):
            mb = maskb_s[pl.ds(ks, KT), :]
            for h in range(ATT_HEADS):
                l = jnp.dot(k_ref[0, h // 2, pl.ds(ks, KT), :], q_ref[0, h], preferred_element_type=F32) + mb
                if near:
                    off = pl.multiple_of((jt - (n_tiles - 2)) * KT, KT)
                    l = l + relb_s[h, pl.ds(off, KT), :]
                slot = n * ATT_HEADS + h
                lg_s[slot] = l
                m_old = m_s[h:h + 1, :]
                m_new = jnp.maximum(m_old, jnp.max(l, axis=0, keepdims=True))
                mo_s[slot:slot + 1, :] = m_old
                mn_s[slot:slot + 1, :] = m_new
                m_s[h:h + 1, :] = m_new
        for n, ks in enumerate(starts):
            for h in range(ATT_HEADS):
                slot = n * ATT_HEADS + h
                m_new = mn_s[slot:slot + 1, :]
                m_ref = jnp.where(m_new == neg_inf, 0.0, m_new)
                alpha = jnp.exp2(mo_s[slot:slot + 1, :] - m_ref)
                p = jnp.exp2(lg_s[slot] - m_ref)
                l_s[h:h + 1, :] = alpha * l_s[h:h + 1, :] + jnp.sum(p, axis=0, keepdims=True)
                vt = vt_ref[0, h * HEAD_DIM:(h + 1) * HEAD_DIM, pl.ds(ks, KT)]
                rows = slice(h * HEAD_DIM, (h + 1) * HEAD_DIM)
                ot_s[rows, :] = ot_s[rows, :] * alpha + jnp.dot(vt, p.astype(BF16), preferred_element_type=F32)

    n_far = jnp.maximum(n_tiles - 2, 0)

    def far_pair(jp, c):
        att_tiles([2 * jp, 2 * jp + 1], False)
        return c

    lax.fori_loop(0, n_far // 2, far_pair, 0)

    @pl.when(n_far % 2 == 1)
    def _():
        att_tiles([n_far - 1], False)

    @pl.when(i > 0)
    def _():
        att_tiles([n_tiles - 2, n_tiles - 1], True)

    @pl.when(i == 0)
    def _():
        att_tiles([n_tiles - 1], True)

    for h in range(ATT_HEADS):
        rows = slice(h * HEAD_DIM, (h + 1) * HEAD_DIM)
        ot_s[rows, :] = ot_s[rows, :] / l_s[h:h + 1, :]

    ot = ot_s[...]
    ms = jnp.mean(ot * ot, axis=0, keepdims=True)
    y = (ot * lax.rsqrt(ms + EPS)).T * goa_ref[...]
    o_ref[0] = y.astype(BF16)


def _attn(q, k, vt, qi, ki, wit, rel_bias, goa):
    bsz, _, _, s = q.shape
    nsel = min(TOPK_MAX, s // 4)
    assert nsel == TQ and s % TQ == 0, "attention kernel assumes TOPK_MAX-sized query blocks"
    bkt = jnp.asarray(_near_buckets())
    kern = functools.partial(_attn_kernel, nsel=nsel)
    return pl.pallas_call(
        kern,
        grid=(bsz, s // TQ),
        in_specs=[
            pl.BlockSpec((1, ATT_HEADS, LANES, TQ), lambda b, i: (b, 0, 0, i)),
            pl.BlockSpec((1, ATT_HEADS // 2, s, LANES), lambda b, i: (b, 0, 0, 0)),
            pl.BlockSpec((1, ATT_W, s), lambda b, i: (b, 0, 0)),
            pl.BlockSpec((1, IDX_HEADS, LANES, TQ), lambda b, i: (b, 0, 0, i)),
            pl.BlockSpec((1, s, LANES), lambda b, i: (b, 0, 0)),
            pl.BlockSpec((1, SUBLANES, TQ), lambda b, i: (b, 0, i)),
            pl.BlockSpec((NEAR, TQ), lambda b, i: (0, 0)),
            pl.BlockSpec(memory_space=pltpu.SMEM),
            pl.BlockSpec((1, ATT_W), lambda b, i: (0, 0)),
        ],
        out_specs=pl.BlockSpec((1, TQ, ATT_W), lambda b, i: (b, i, 0)),
        out_shape=jax.ShapeDtypeStruct((bsz, s, ATT_W), BF16),
        scratch_shapes=[
            pltpu.VMEM((s, TQ), I32),
            pltpu.VMEM((s, TQ), I16),
            pltpu.VMEM((s, TQ), I16),
            pltpu.VMEM((s, TQ), F32),
            pltpu.VMEM((ATT_HEADS, 2 * KT, TQ), F32),
            pltpu.VMEM((ATT_W, TQ), F32),
            pltpu.VMEM((ATT_HEADS, TQ), F32),
            pltpu.VMEM((2 * ATT_HEADS, TQ), F32),
            pltpu.VMEM((2 * ATT_HEADS, TQ), F32),
            pltpu.VMEM((ATT_HEADS, TQ), F32),
            pltpu.VMEM((2 * ATT_HEADS, KT, TQ), F32),
        ],
        compiler_params=pltpu.CompilerParams(
            dimension_semantics=("arbitrary", "arbitrary"), vmem_limit_bytes=VMEM_LIMIT),
        name="attn",
    )(q, k, vt, qi, ki, wit, bkt, rel_bias, goa)


def _topk_rows(s, payload, k, order=None):
    rows = lax.broadcasted_iota(I32, s.shape, 0).astype(F32) if order is None else order
    vals, pays = [], []
    for _ in range(k):
        m = jnp.max(s, axis=0, keepdims=True)
        ix = jnp.min(jnp.where(s == m, rows, _NO_ROW), axis=0, keepdims=True)
        hit = rows == ix
        if payload is None:
            pays.append(ix)
        else:
            pays.append(jnp.max(jnp.where(hit, payload, -1.0), axis=0, keepdims=True))
        vals.append(m)
        s = jnp.where(hit, -jnp.inf, s)
    return jnp.concatenate(vals, axis=0), jnp.concatenate(pays, axis=0)


_NO_ROW = 1e9

_PAIR_GROUPS = (
    ((0, 0, 8, 0),), ((0, 8, 8, 0),), ((1, 0, 8, 0),),
    ((2, 0, 5, 0), (5, 0, 2, 5)), ((3, 0, 4, 0), (4, 0, 3, 4)), ((6, 0, 2, 0), (7, 0, 2, 2)),
)
assert sorted((a, b) for g in _PAIR_GROUPS for a, b0, nb, _ in g for b in range(b0, b0 + nb)) == sorted(
    (a, b) for a in range(PEER_TOPK // 2) for b in range(PEER_TOPK // (a + 1)))


def _mid_kernel(x_ref, cn_ref, an_ref, mod_ref, wo1_ref, wo2_ref, g2_ref, wpq_ref, k1_ref, k2_ref,
                x1_ref, h2_ref, idx_ref, gate_ref, qq_s, idt_s, gt_s):
    tm = x_ref.shape[1]
    x = x_ref[0]
    gt1 = mod_ref[0, 2:3, :]
    sh2 = mod_ref[0, 3:4, :]
    sc2 = mod_ref[0, 4:5, :]
    proj = (jnp.dot(cn_ref[0], wo1_ref[...], preferred_element_type=F32)
            + jnp.dot(an_ref[0], wo2_ref[...], preferred_element_type=F32))
    x1 = x + gt1 * proj
    x1_ref[0] = x1
    r = lax.rsqrt(jnp.mean(x1 * x1, axis=-1, keepdims=True) + EPS)
    h2 = (x1 * r) * g2_ref[...] * (1.0 + sc2) + sh2
    h2_ref[0] = h2
    qq_s[...] = jnp.dot(h2.astype(BF16), wpq_ref[...], preferred_element_type=F32).astype(BF16)

    def route_unit(hh, lt):
        rows = pl.ds(lt * LANES, LANES)
        q1 = qq_s[rows, pl.ds(pl.multiple_of(hh * 2 * N_KEYS, LANES), N_KEYS)]
        q2 = qq_s[rows, pl.ds(pl.multiple_of(hh * 2 * N_KEYS + N_KEYS, LANES), N_KEYS)]
        v1, i1 = _topk_rows(_nt(k1_ref[hh], q1), None, PEER_TOPK)
        v2, i2 = _topk_rows(_nt(k2_ref[hh], q2), None, PEER_TOPK)
        sub = lax.broadcasted_iota(I32, (SUBLANES, LANES), 0)
        subf = sub.astype(F32)
        cands, cidxs, flats = [], [], []
        for group in _PAIR_GROUPS:
            val = jnp.full((SUBLANES, LANES), -jnp.inf, F32)
            cid = jnp.zeros((SUBLANES, LANES), F32)
            flat = jnp.full((SUBLANES, LANES), _NO_ROW, F32)
            for a, b0, nb, off in group:
                v2s, i2s = v2[b0:b0 + SUBLANES, :], i2[b0:b0 + SUBLANES, :]
                if off:
                    v2s, i2s = pltpu.roll(v2s, off, 0), pltpu.roll(i2s, off, 0)
                inside = lambda new, old: jnp.where(sub < off + nb, jnp.where(sub >= off, new, old), old)
                val = inside(v1[a:a + 1, :] + v2s, val)
                cid = inside(i1[a:a + 1, :] * float(N_KEYS) + i2s, cid)
                flat = inside(subf + float(a * PEER_TOPK + b0 - off), flat)
            cands.append(val)
            cidxs.append(cid)
            flats.append(flat)
        half = PEER_TOPK // 2
        cands.append(v1[half:, :] + v2[0:1, :])
        cidxs.append(i1[half:, :] * float(N_KEYS) + i2[0:1, :])
        flats.append((subf + float(half)) * float(PEER_TOPK))
        best, experts = _topk_rows(jnp.concatenate(cands, axis=0), jnp.concatenate(cidxs, axis=0), PEER_TOPK,
                                   order=jnp.concatenate(flats, axis=0))
        e = jnp.exp(best - best[0:1, :])
        g = e / jnp.sum(e, axis=0, keepdims=True)
        slots = pl.ds(pl.multiple_of(hh * PEER_TOPK, PEER_TOPK), PEER_TOPK)
        cols = pl.ds(lt * LANES, LANES)
        idt_s[slots, cols] = experts * float(ROWS_PER_EXPERT)
        gt_s[slots, cols] = g

    def route(hh, c):
        for lt in range(tm // LANES):
            route_unit(hh, lt)
        return c

    lax.fori_loop(0, PEER_HEADS, route, 0)
    idx_ref[0] = idt_s[...].T.astype(I32)
    gate_ref[0] = gt_s[...].T


def _mid(x, cn, an, mod3, wo1, wo2, g2, wpq, k1, k2):
    bsz, s, d = x.shape
    tm = TM_MID
    full = lambda shape: pl.BlockSpec(shape, lambda b, j: (0,) * len(shape))
    tok = lambda w: pl.BlockSpec((1, tm, w), lambda b, j: (b, j, 0))
    return pl.pallas_call(
        _mid_kernel,
        grid=(bsz, s // tm),
        in_specs=[tok(d), tok(CONV_CH), tok(ATT_W), pl.BlockSpec((1, 6, d), lambda b, j: (b, 0, 0)),
                  full(wo1.shape), full(wo2.shape), full(g2.shape), full(wpq.shape), full(k1.shape), full(k2.shape)],
        out_specs=(tok(d), tok(d), tok(PEER_SLOTS), tok(PEER_SLOTS)),
        out_shape=(
            jax.ShapeDtypeStruct((bsz, s, d), F32),
            jax.ShapeDtypeStruct((bsz, s, d), F32),
            jax.ShapeDtypeStruct((bsz, s, PEER_SLOTS), I32),
            jax.ShapeDtypeStruct((bsz, s, PEER_SLOTS), F32),
        ),
        scratch_shapes=[
            pltpu.VMEM((tm, PEER_HEADS * 2 * N_KEYS), BF16),
            pltpu.VMEM((PEER_SLOTS, tm), F32),
            pltpu.VMEM((PEER_SLOTS, tm), F32),
        ],
        compiler_params=pltpu.CompilerParams(
            dimension_semantics=("arbitrary", "arbitrary"), vmem_limit_bytes=VMEM_LIMIT),
        name="mid",
    )(x, cn, an, mod3, wo1, wo2, g2, wpq, k1, k2)


def _pack_table(t):
    tb = t.astype(BF16)
    lo = lax.bitcast_convert_type(tb[:, :HALF], jnp.uint16).astype(jnp.uint32)
    hi = lax.bitcast_convert_type(tb[:, HALF:], jnp.uint16).astype(jnp.uint32)
    return (lo | (hi << 16)).reshape(t.shape[0] * ROWS_PER_EXPERT, LANES)


def _unpack(w):
    lo = pltpu.bitcast(lax.shift_left(w, jnp.uint32(16)), F32)
    hi = pltpu.bitcast(w & jnp.uint32(0xFFFF0000), F32)
    return lo, hi


def _gather_row(tab_ref, row):
    return tab_ref[pl.ds(pl.multiple_of(row, ROWS_PER_EXPERT), ROWS_PER_EXPERT), :]


STAGE_ROWS = PEER_SLOTS * ROWS_PER_EXPERT


def _token_pieces(xg, r, first, count):
    return jnp.concatenate(
        [xg[r:r + 1, (first + q) * LANES:(first + q + 1) * LANES] for q in range(count)], axis=0)


def _peer_u_kernel(idx_ref, tab_ref, x_ref, gate_ref, w_ref, *stages):
    tb = w_ref.shape[0]
    sub = lax.broadcasted_iota(I32, (SUBLANES, PEER_SLOTS), 0)

    def group(g, c):
        for k in range(PEER_GROUPS):
            eight(pl.multiple_of((g * PEER_GROUPS + k) * SUBLANES, SUBLANES), stages[k * SUBLANES:(k + 1) * SUBLANES])
        return c

    def eight(t0, stages):
        act8 = jnp.zeros((SUBLANES, PEER_SLOTS), F32)
        xg = x_ref[pl.ds(t0, SUBLANES), :]
        xlo, xhi, rows_t = [], [], []
        for r in range(SUBLANES):
            xlo.append(jnp.concatenate([_token_pieces(xg, r, 0, ROWS_PER_EXPERT)] * 2, axis=0))
            xhi.append(jnp.concatenate([_token_pieces(xg, r, ROWS_PER_EXPERT, ROWS_PER_EXPERT)] * 2, axis=0))
            rows_t.append(idx_ref.at[t0 + r])
        for j in range(0, PEER_SLOTS, 2):
            for r in range(SUBLANES):
                pair = jnp.concatenate(
                    [_gather_row(tab_ref, rows_t[r][j]), _gather_row(tab_ref, rows_t[r][j + 1])], axis=0)
                lo, hi = _unpack(pair)
                row = j * ROWS_PER_EXPERT
                stages[r][row:row + 2 * ROWS_PER_EXPERT, :] = lo * xlo[r] + hi * xhi[r]
        for r in range(SUBLANES):
            stage = stages[r]
            cs = stage[pl.ds(0, PEER_SLOTS, stride=ROWS_PER_EXPERT), :]
            for q in range(1, ROWS_PER_EXPERT):
                cs = cs + stage[pl.ds(q, PEER_SLOTS, stride=ROWS_PER_EXPERT), :]
            act = jnp.sum(cs.T, axis=0, keepdims=True)
            act8 = jnp.where(sub == r, act, act8)
        rows = pl.ds(t0, SUBLANES)
        w_ref[rows, :] = gate_ref[rows, :] * jax.nn.gelu(act8)

    lax.fori_loop(0, tb // (PEER_GROUPS * SUBLANES), group, 0)


def _peer_u(idx, tab, h2, gate):
    n, d = h2.shape
    tb = TB_PEER
    return pl.pallas_call(
        _peer_u_kernel,
        grid=(n // tb,),
        in_specs=[
            pl.BlockSpec((tb, PEER_SLOTS), lambda i: (i, 0), memory_space=pltpu.SMEM),
            pl.BlockSpec(tab.shape, lambda i: (0, 0), pipeline_mode=pl.Buffered(1)),
            pl.BlockSpec((tb, d), lambda i: (i, 0)),
            pl.BlockSpec((tb, PEER_SLOTS), lambda i: (i, 0)),
        ],
        out_specs=pl.BlockSpec((tb, PEER_SLOTS), lambda i: (i, 0)),
        out_shape=jax.ShapeDtypeStruct((n, PEER_SLOTS), F32),
        scratch_shapes=[pltpu.VMEM((STAGE_ROWS, LANES), F32) for _ in range(PEER_GROUPS * SUBLANES)],
        compiler_params=pltpu.CompilerParams(
            dimension_semantics=("arbitrary",), vmem_limit_bytes=VMEM_LIMIT),
        name="peer_u",
    )(idx, tab, h2, gate)


PV_PAIRS = SUBLANES // 2


def _peer_v_kernel(idx_ref, w_ref, tab_ref, x1_ref, gt2_ref, e8_ref, mask_ref, o_ref, *stages):
    tb = idx_ref.shape[0]
    gt2 = gt2_ref[0]
    mask = mask_ref[...]
    e8 = e8_ref[...]

    def group(g, c):
        for k in range(PEER_GROUPS_V):
            eight(pl.multiple_of((g * PEER_GROUPS_V + k) * SUBLANES, SUBLANES), stages[k * PV_PAIRS:(k + 1) * PV_PAIRS])
        return c

    def eight(g0, stages):
        peers = []
        for p in range(PV_PAIRS):
            stage = stages[p]
            t0 = g0 + 2 * p
            for u in range(2):
                rows_t = idx_ref.at[t0 + u]
                for j in range(PEER_SLOTS):
                    stage[j * ROWS_PER_EXPERT:(j + 1) * ROWS_PER_EXPERT, u * LANES:(u + 1) * LANES] = (
                        _gather_row(tab_ref, rows_t[j]))
            w2 = w_ref[pl.ds(t0, 2), :]
            hi = w2.astype(BF16).astype(F32)
            rep = jnp.dot(jnp.concatenate([hi, w2 - hi], axis=0).astype(BF16), e8,
                          preferred_element_type=F32)
            lhs = jnp.concatenate([rep[i:i + 1, :] * mask for i in range(4)], axis=0).astype(BF16)
            out = jnp.dot(lhs, pltpu.bitcast(stage[...], BF16), preferred_element_type=F32)
            for u in range(2):
                r0 = u * SUBLANES
                peers.append(out[r0:r0 + SUBLANES, u * LANES:(u + 1) * LANES]
                             + out[2 * SUBLANES + r0:3 * SUBLANES + r0, u * LANES:(u + 1) * LANES])
        rows = pl.ds(g0, SUBLANES)
        for q in range(SUBLANES):
            cols = slice(q * LANES, (q + 1) * LANES)
            piece = jnp.concatenate([peer[q:q + 1, :] for peer in peers], axis=0)
            o_ref[rows, cols] = x1_ref[rows, cols] + gt2[:, cols] * piece

    lax.fori_loop(0, tb // (PEER_GROUPS_V * SUBLANES), group, 0)


def _peer_v(idx, w, tab, x1, gt2, blocks_per_batch):
    n, d = x1.shape
    tb = TB_PEER
    q = np.arange(SUBLANES)
    piece = 2 * (q % ROWS_PER_EXPERT) + q // ROWS_PER_EXPERT
    lane = np.arange(PEER_SLOTS * SUBLANES)
    mask = jnp.asarray((lane[None, :] % SUBLANES == piece[:, None]).astype(np.float32))
    e8 = jnp.asarray(np.arange(PEER_SLOTS)[:, None] == lane[None, :] // SUBLANES, BF16)
    return pl.pallas_call(
        _peer_v_kernel,
        grid=(n // tb,),
        in_specs=[
            pl.BlockSpec((tb, PEER_SLOTS), lambda i: (i, 0), memory_space=pltpu.SMEM),
            pl.BlockSpec((tb, PEER_SLOTS), lambda i: (i, 0)),
            pl.BlockSpec(tab.shape, lambda i: (0, 0), pipeline_mode=pl.Buffered(1)),
            pl.BlockSpec((tb, d), lambda i: (i, 0)),
            pl.BlockSpec((1, 1, d), lambda i: (i // blocks_per_batch, 0, 0)),
            pl.BlockSpec(e8.shape, lambda i: (0, 0)),
            pl.BlockSpec(mask.shape, lambda i: (0, 0)),
        ],
        out_specs=pl.BlockSpec((tb, d), lambda i: (i, 0)),
        out_shape=jax.ShapeDtypeStruct((n, d), F32),
        scratch_shapes=[pltpu.VMEM((STAGE_ROWS, 2 * LANES), jnp.uint32) for _ in range(PEER_GROUPS_V * PV_PAIRS)],
        compiler_params=pltpu.CompilerParams(
            dimension_semantics=("arbitrary",), vmem_limit_bytes=VMEM_LIMIT),
        name="peer_v",
    )(idx, w, tab, x1, gt2, e8, mask)


def _layer(x, mod, g_norm1, g_norm2, w_in, q_norm_g, k_norm_g, conv_w, conv_b, conv_ln_g, conv_ln_b,
           rel_bias, g_out_conv, g_out_attn, w_out, w_peer_q, peer_k1, peer_k2, peer_u, peer_v):
    bsz, s, d = x.shape
    n = bsz * s
    mod3 = mod.reshape(bsz, 6, d)
    row = lambda a: a.reshape(1, -1)

    c0 = 2 * CONV_CH
    c1 = c0 + 3 * ATT_W
    c2 = c1 + IDX_HEADS * IDX_DIM
    c3 = c2 + IDX_DIM
    wa = w_in[:, :c0].astype(BF16)
    wqkv = w_in[:, c0:c1].astype(BF16)
    widx = jnp.concatenate([w_in[:, c1:c2], w_in[:, c2:c3], w_in[:, c2:c3]], axis=1).astype(BF16)
    wwi = jnp.zeros((SUBLANES, d), F32).at[:IDX_HEADS].set(w_in[:, c3:c3 + IDX_HEADS].T).astype(BF16)
    head = np.arange(ATT_W) // HEAD_DIM
    e2 = jnp.asarray(head[:, None] == head[None, :], BF16)

    conv_n, q, k, vt, qi, ki, wit = _inproj(
        x, mod3, row(g_norm1), wa, wqkv, widx, wwi,
        row(jnp.tile(q_norm_g, ATT_HEADS)), row(jnp.tile(k_norm_g, ATT_HEADS)), e2,
        conv_w.reshape(CONV_WIDTH, CONV_CH), row(conv_b), row(conv_ln_g), row(conv_ln_b), row(g_out_conv))
    attn_n = _attn(q, k, vt, qi, ki, wit, rel_bias, row(g_out_attn))

    x1, h2, idx, gate = _mid(
        x, conv_n, attn_n, mod3, w_out[:CONV_CH].astype(BF16), w_out[CONV_CH:].astype(BF16), row(g_norm2),
        w_peer_q.astype(BF16), peer_k1.astype(BF16), peer_k2.astype(BF16))

    idx = idx.reshape(n, PEER_SLOTS)
    w = _peer_u(idx, _pack_table(peer_u), h2.reshape(n, d), gate.reshape(n, PEER_SLOTS))
    out = _peer_v(idx, w, _pack_table(peer_v), x1.reshape(n, d), mod3[:, 5:6, :], s // TB_PEER)
    return out.reshape(bsz, s, d)


def kernel(x, c, w_ada, b_ada, g_norm1, g_norm2, w_in, q_norm_g, k_norm_g, conv_w, conv_b, conv_ln_g,
           conv_ln_b, rel_bias, g_out_conv, g_out_attn, w_out, w_peer_q, peer_k1, peer_k2, peer_u, peer_v):
    depth = w_ada.shape[0]
    for l in range(depth):
        mod = _ada(c, w_ada[l], b_ada[l])
        x = _layer(x, mod, g_norm1[l], g_norm2[l], w_in[l], q_norm_g[l], k_norm_g[l], conv_w[l], conv_b[l],
                   conv_ln_g[l], conv_ln_b[l], rel_bias, g_out_conv[l], g_out_attn[l], w_out[l],
                   w_peer_q[l], peer_k1[l], peer_k2[l], peer_u[l], peer_v[l])
    return x
```

```python
import functools
import math

import numpy as np
import jax
import jax.numpy as jnp
from jax import lax
from jax.experimental import pallas as pl
from jax.experimental.pallas import tpu as pltpu

F32 = jnp.float32
BF16 = jnp.bfloat16
I32 = jnp.int32
I16 = jnp.int16
HIGHEST = lax.Precision.HIGHEST

D_MODEL = 1024
CHUNK = 64
CONV_CH = 512
CONV_WIDTH = 31
ATT_HEADS = 8
HEAD_DIM = 64
ATT_W = ATT_HEADS * HEAD_DIM
IDX_HEADS = 4
IDX_DIM = 64
IDX_SCALE = (IDX_HEADS * IDX_DIM) ** -0.5
TOPK_MAX = 256
REL_BUCKETS = 32
REL_MAX_DIST = 128
PEER_HEADS = 8
N_KEYS = 128
N_EXPERTS = N_KEYS * N_KEYS
PEER_TOPK = 16
PEER_SLOTS = PEER_HEADS * PEER_TOPK
EPS = 1e-6
LOG2E = math.log2(math.e)

LANES = 128
SUBLANES = 8
PACK = 16
VMEM_LIMIT = 56 * 1024 * 1024

TM_IN = 512
CONV_ROWS = 64
HALO = 32
TQ = 256
KT = 256
TM_MID = 1024
TB_PEER = 256
PEER_GROUPS = 2
PEER_GROUPS_V = 4
HALF = D_MODEL // 2
ROWS_PER_EXPERT = HALF // LANES

NT_DIMS = (((1,), (1,)), ((), ()))

_NEG_INF_BITS = int(np.array(-np.inf, np.float32).view(np.int32))
KEY_NEG_INF = _NEG_INF_BITS ^ 0x7FFFFFFF
I16_MIN = -(2 ** 15)
I16_MAX = 2 ** 15 - 1


def _nt(a, b, precision=None):
    return lax.dot_general(a, b, NT_DIMS, precision=precision, preferred_element_type=F32)


def _ada_kernel(c_ref, w_ref, b_ref, o_ref):
    a = jax.nn.silu(c_ref[...])
    o_ref[...] = jnp.dot(a, w_ref[...], precision=HIGHEST, preferred_element_type=F32) + b_ref[...]


def _ada(c, w_ada, b_ada):
    bsz, d = c.shape
    return pl.pallas_call(
        _ada_kernel,
        grid=(6,),
        in_specs=[
            pl.BlockSpec((bsz, d), lambda j: (0, 0)),
            pl.BlockSpec((d, d), lambda j: (0, j)),
            pl.BlockSpec((1, d), lambda j: (0, j)),
        ],
        out_specs=pl.BlockSpec((bsz, d), lambda j: (0, j)),
        out_shape=jax.ShapeDtypeStruct((bsz, 6 * d), F32),
        name="ada",
    )(c, w_ada, b_ada.reshape(1, 6 * d))


def _inproj_kernel(x_ref, mod_ref, g1_ref, wa_ref, wqkv_ref, widx_ref, wwi_ref, gq_ref, gk_ref, e2_ref,
                   cw_ref, cb_ref, lng_ref, lnb_ref, goc_ref,
                   conv_ref, q_ref, k_ref, vt_ref, qi_ref, ki_ref, wit_ref, ubuf):
    j = pl.program_id(1)
    tm = x_ref.shape[1]
    x = x_ref[0]
    sh1 = mod_ref[0, 0:1, :]
    sc1 = mod_ref[0, 1:2, :]
    r = lax.rsqrt(jnp.mean(x * x, axis=-1, keepdims=True) + EPS)
    h = (x * r) * g1_ref[...] * (1.0 + sc1) + sh1
    hb = h.astype(BF16)

    pa = jnp.dot(hb, wa_ref[...], preferred_element_type=F32)
    u = pa[:, :CONV_CH] * jax.nn.sigmoid(pa[:, CONV_CH:])

    @pl.when(j == 0)
    def _():
        ubuf[0:HALO, :] = jnp.zeros((HALO, CONV_CH), F32)

    ubuf[HALO:HALO + tm, :] = u
    first = HALO - (CONV_WIDTH - 1)
    for rb in range(tm // CONV_ROWS):
        base = rb * CONV_ROWS
        acc = jnp.zeros((CONV_ROWS, CONV_CH), F32) + cb_ref[...]
        for phase in range(SUBLANES):
            taps = [t for t in range(CONV_WIDTH) if (first + t) % SUBLANES == phase]
            span = (first + taps[-1]) - phase + CONV_ROWS
            slab = ubuf[base + phase:base + phase + span, :]
            for t in taps:
                off = first + t - phase
                acc = acc + cw_ref[t:t + 1, :] * slab[off:off + CONV_ROWS, :]
        mu = jnp.mean(acc, axis=-1, keepdims=True)
        xc = acc - mu
        y = xc * lax.rsqrt(jnp.mean(xc * xc, axis=-1, keepdims=True) + EPS)
        y = jax.nn.silu(y * lng_ref[...] + lnb_ref[...])
        y = y * lax.rsqrt(jnp.mean(y * y, axis=-1, keepdims=True) + EPS) * goc_ref[...]
        conv_ref[0, base:base + CONV_ROWS, :] = y.astype(BF16)
    ubuf[0:HALO, :] = ubuf[tm:tm + HALO, :]

    pq = jnp.dot(hb, wqkv_ref[...], preferred_element_type=F32)
    q = pq[:, :ATT_W]
    k = pq[:, ATT_W:2 * ATT_W]
    v = pq[:, 2 * ATT_W:]
    e2 = e2_ref[...]

    def head_mean(sq):
        hi = sq.astype(BF16)
        lo = (sq - hi.astype(F32)).astype(BF16)
        return (jnp.dot(hi, e2, preferred_element_type=F32)
                + jnp.dot(lo, e2, preferred_element_type=F32)) * (1.0 / HEAD_DIM)

    qs = head_mean(q * q)
    ks = head_mean(k * k)
    qn = q * lax.rsqrt(qs + EPS) * gq_ref[...] * (HEAD_DIM ** -0.5 * LOG2E)
    kn = k * lax.rsqrt(ks + EPS) * gk_ref[...]
    lane = lax.broadcasted_iota(I32, (tm, LANES), 1)
    low = lane < HEAD_DIM
    for p in range(ATT_HEADS // 2):
        slab = qn[:, p * LANES:(p + 1) * LANES]
        q_ref[0, 2 * p] = jnp.where(low, slab, 0.0).T.astype(BF16)
        q_ref[0, 2 * p + 1] = jnp.where(low, 0.0, slab).T.astype(BF16)
        k_ref[0, p] = kn[:, p * LANES:(p + 1) * LANES].astype(BF16)
    vt_ref[0] = v.T.astype(BF16)

    pc = jnp.dot(hb, widx_ref[...], preferred_element_type=F32)
    for p in range(IDX_HEADS // 2):
        slab = pc[:, p * LANES:(p + 1) * LANES]
        qi_ref[0, 2 * p] = jnp.where(low, slab, 0.0).T.astype(BF16)
        qi_ref[0, 2 * p + 1] = jnp.where(low, 0.0, slab).T.astype(BF16)
    ki_ref[0] = pc[:, 2 * LANES:3 * LANES].astype(BF16)
    wit_ref[0] = _nt(wwi_ref[...], hb) * IDX_SCALE


def _inproj(x, mod3, g1, wa, wqkv, widx, wwi, gq, gk, e2, cw, cb, lng, lnb, goc):
    bsz, s, d = x.shape
    tm = TM_IN
    nt = s // tm
    full = lambda shape: pl.BlockSpec(shape, lambda b, j: (0,) * len(shape))
    out_shape = (
        jax.ShapeDtypeStruct((bsz, s, CONV_CH), BF16),
        jax.ShapeDtypeStruct((bsz, ATT_HEADS, LANES, s), BF16),
        jax.ShapeDtypeStruct((bsz, ATT_HEADS // 2, s, LANES), BF16),
        jax.ShapeDtypeStruct((bsz, ATT_W, s), BF16),
        jax.ShapeDtypeStruct((bsz, IDX_HEADS, LANES, s), BF16),
        jax.ShapeDtypeStruct((bsz, s, LANES), BF16),
        jax.ShapeDtypeStruct((bsz, SUBLANES, s), F32),
    )
    out_specs = (
        pl.BlockSpec((1, tm, CONV_CH), lambda b, j: (b, j, 0)),
        pl.BlockSpec((1, ATT_HEADS, LANES, tm), lambda b, j: (b, 0, 0, j)),
        pl.BlockSpec((1, ATT_HEADS // 2, tm, LANES), lambda b, j: (b, 0, j, 0)),
        pl.BlockSpec((1, ATT_W, tm), lambda b, j: (b, 0, j)),
        pl.BlockSpec((1, IDX_HEADS, LANES, tm), lambda b, j: (b, 0, 0, j)),
        pl.BlockSpec((1, tm, LANES), lambda b, j: (b, j, 0)),
        pl.BlockSpec((1, SUBLANES, tm), lambda b, j: (b, 0, j)),
    )
    return pl.pallas_call(
        _inproj_kernel,
        grid=(bsz, nt),
        in_specs=[
            pl.BlockSpec((1, tm, d), lambda b, j: (b, j, 0)),
            pl.BlockSpec((1, 6, d), lambda b, j: (b, 0, 0)),
            full(g1.shape), full(wa.shape), full(wqkv.shape), full(widx.shape), full(wwi.shape),
            full(gq.shape), full(gk.shape), full(e2.shape),
            full(cw.shape), full(cb.shape), full(lng.shape), full(lnb.shape), full(goc.shape),
        ],
        out_specs=out_specs,
        out_shape=out_shape,
        scratch_shapes=[pltpu.VMEM((tm + HALO, CONV_CH), F32)],
        compiler_params=pltpu.CompilerParams(
            dimension_semantics=("arbitrary", "arbitrary"), vmem_limit_bytes=VMEM_LIMIT),
        name="inproj",
    )(x, mod3, g1, wa, wqkv, widx, wwi, gq, gk, e2, cw, cb, lng, lnb, goc)


def _t5_bucket_np(rel):
    half = REL_BUCKETS // 2
    max_exact = half // 2
    ret = np.where(rel > 0, half, 0)
    n = np.abs(rel)
    nf = np.maximum(n, 1).astype(np.float64)
    large = max_exact + (np.log(nf / max_exact) / math.log(REL_MAX_DIST / max_exact)
                         * (half - max_exact)).astype(np.int32)
    large = np.minimum(large, half - 1)
    return (ret + np.where(n < max_exact, n, large)).astype(np.int32)


NEAR = REL_MAX_DIST + TQ
FAR_BUCKET = REL_BUCKETS // 2 - 1


def _near_buckets():
    r = np.arange(NEAR)[:, None]
    t = np.arange(TQ)[None, :]
    return _t5_bucket_np(r - REL_MAX_DIST - t)


def _attn_kernel(q_ref, k_ref, vt_ref, qi_ref, ki_ref, wit_ref, bkt_ref, rb_ref, goa_ref, o_ref,
                 keys_s, hi_s, lo_s, maskb_s, relb_s, ot_s, m_s, mo_s, mn_s, l_s, lg_s, *, nsel):
    b = pl.program_id(0)
    i = pl.program_id(1)
    t0 = i * TQ
    n_tiles = i + 1
    neg_inf = F32(-jnp.inf)

    @pl.when((b == 0) & (i == 0))
    def _():
        bk = bkt_ref[...]
        for h in range(ATT_HEADS):
            far = rb_ref[FAR_BUCKET, h]
            acc = jnp.zeros((NEAR, TQ), F32)
            for bb in range(REL_BUCKETS):
                acc = jnp.where(bk == bb, (rb_ref[bb, h] - far) * LOG2E, acc)
            relb_s[h, 0:2 * KT - NEAR, :] = jnp.zeros((2 * KT - NEAR, TQ), F32)
            relb_s[h, 2 * KT - NEAR:2 * KT, :] = acc

    qpos = t0 + lax.broadcasted_iota(I32, (1, TQ), 1)
    limit = (qpos // CHUNK + 1) * CHUNK
    row_iota = lax.broadcasted_iota(I32, (KT, TQ), 0)

    def tile_start(jt):
        return pl.multiple_of(jt * KT, KT)

    wi = wit_ref[0]

    def score_tile(jt):
        ks = tile_start(jt)
        kit = ki_ref[0, pl.ds(ks, KT), :]
        acc = jnp.zeros((KT, TQ), F32)
        for h in range(IDX_HEADS):
            lgt = jnp.dot(kit, qi_ref[0, h], preferred_element_type=F32)
            acc = acc + jnp.maximum(lgt, 0.0) * wi[h:h + 1, :]
        sc = jnp.where(row_iota + ks < limit, acc, neg_inf)
        bits = pltpu.bitcast(sc, I32)
        key = bits ^ (lax.shift_right_arithmetic(bits, 31) & 0x7FFFFFFF)
        keys_s[pl.ds(ks, KT), :] = key
        hi_s[pl.ds(ks, KT), :] = lax.shift_right_arithmetic(key, 16).astype(I16)
        lo_s[pl.ds(ks, KT), :] = ((key & 0xFFFF) ^ 0x8000).astype(I16)

    def score_pair(jp, c):
        score_tile(2 * jp)
        score_tile(2 * jp + 1)
        return c

    lax.fori_loop(0, n_tiles // 2, score_pair, 0)

    @pl.when(n_tiles % 2 == 1)
    def _():
        score_tile(n_tiles - 1)

    def count_ge(half_ref, cand):
        c16 = cand.astype(I16)
        one = jnp.ones((KT, TQ), I16)
        zero = jnp.zeros((KT, TQ), I16)

        def body(jt, acc):
            hit = jnp.where(half_ref[pl.ds(tile_start(jt), KT), :] >= c16, one, zero)
            parts = [hit[r:r + PACK, :] for r in range(0, KT, PACK)]
            while len(parts) > 1:
                parts = [parts[k] + parts[k + 1] for k in range(0, len(parts), 2)]
            return acc + parts[0].astype(I32)
        acc = lax.fori_loop(0, n_tiles, body, jnp.zeros((PACK, TQ), I32))
        return jnp.sum(acc.astype(F32), axis=0, keepdims=True)

    def low_key(low):
        return (low ^ 0x8000) - jnp.where(low < 0x8000, 0x10000, 0)

    def select_thr():
        c0 = count_ge(hi_s, jnp.zeros((1, TQ), I32))
        ok = c0 >= nsel
        t = jnp.where(ok, 0, I16_MIN).astype(I32)
        cnt = jnp.where(ok, c0, F32(2 ** 30))

        def top_step(it, carry):
            t, cnt = carry
            cand = t + lax.shift_left(I32(1), I32(14) - it)
            c = count_ge(hi_s, cand)
            ok = c >= nsel
            return jnp.where(ok, cand, t), jnp.where(ok, c, cnt)

        top, cnt = lax.fori_loop(0, 15, top_step, (t, cnt))

        top_max = top >= I16_MAX
        above = jnp.where(top_max, 0.0, count_ge(hi_s, jnp.where(top_max, top, top + 1)))
        top16 = top.astype(I16)
        floor16 = jnp.full((KT, TQ), I16_MIN, I16)

        def keep_low(jt, c):
            rows = pl.ds(tile_start(jt), KT)
            lo_s[rows, :] = jnp.where(hi_s[rows, :] == top16, lo_s[rows, :], floor16)
            return c

        lax.fori_loop(0, n_tiles, keep_low, 0)

        at_zero = jnp.where(top == 0, above + count_ge(lo_s, low_key(jnp.ones((1, TQ), I32))), F32(2 ** 30))
        settled = at_zero < nsel

        def low_cond(carry):
            bit, _, cnt = carry
            return (bit >= 0) & (jnp.max(jnp.where(settled, F32(nsel), cnt)) > nsel)

        def low_step(carry):
            bit, low, cnt = carry
            cand = low + lax.shift_left(I32(1), bit)
            c = above + count_ge(lo_s, low_key(cand))
            ok = c >= nsel
            return bit - 1, jnp.where(ok, cand, low), jnp.where(ok, c, cnt)

        _, low, cnt = lax.while_loop(low_cond, low_step, (I32(15), jnp.zeros((1, TQ), I32), cnt))
        low_max = low >= 0xFFFF
        c_gt = above + jnp.where(low_max, 0.0, count_ge(lo_s, low_key(jnp.where(low_max, low, low + 1))))
        return lax.shift_left(top, 16) + low, cnt, c_gt

    thr, c_ge, c_gt = lax.cond(
        i > 0, select_thr,
        lambda: (jnp.full((1, TQ), KEY_NEG_INF + 1, I32), jnp.full((1, TQ), nsel, F32), jnp.zeros((1, TQ), F32)))

    def mask_plain():
        def p3(jt, c):
            ks = tile_start(jt)
            maskb_s[pl.ds(ks, KT), :] = jnp.where(keys_s[pl.ds(ks, KT), :] >= thr, 0.0, neg_inf)
            return c
        lax.fori_loop(0, n_tiles, p3, 0)
        return I32(0)

    def mask_ties():
        quota = nsel - c_gt
        tri = jnp.where(lax.broadcasted_iota(I32, (KT, KT), 0) >= lax.broadcasted_iota(I32, (KT, KT), 1),
                        1.0, 0.0).astype(BF16)

        def local_ranks(jt):
            ks = tile_start(jt)
            kt = keys_s[pl.ds(ks, KT), :]
            eq = kt == thr
            return ks, kt, eq, jnp.dot(tri, jnp.where(eq, 1.0, 0.0).astype(BF16), preferred_element_type=F32)

        def write_mask(tile, before):
            ks, kt, eq, local = tile
            rank = before + local
            tied = jnp.where(eq, jnp.where(rank <= quota, 0.0, neg_inf), neg_inf)
            maskb_s[pl.ds(ks, KT), :] = jnp.where(kt > thr, 0.0, tied)
            return rank[KT - 1:KT, :]

        def pair(jp, before):
            first, second = local_ranks(2 * jp), local_ranks(2 * jp + 1)
            return write_mask(second, write_mask(first, before))

        before = lax.fori_loop(0, n_tiles // 2, pair, jnp.zeros((1, TQ), F32))

        @pl.when(n_tiles % 2 == 1)
        def _():
            write_mask(local_ranks(n_tiles - 1), before)

        return I32(0)

    lax.cond(jnp.max(c_ge) > nsel, mask_ties, mask_plain)

    m_s[...] = jnp.full((ATT_HEADS, TQ), neg_inf, F32)
    l_s[...] = jnp.zeros((ATT_HEADS, TQ), F32)
    ot_s[...] = jnp.zeros((ATT_W, TQ), F32)

    def att_tiles(jts, near):
        starts = [tile_start(jt) for jt in jts]
        for n, (jt, ks) in enumerate(zip(jts, starts)):
            mb = maskb_s[pl.ds(ks, KT), :]
            for h in range(ATT_HEADS):
                l = jnp.dot(k_ref[0, h // 2, pl.ds(ks, KT), :], q_ref[0, h], preferred_element_type=F32) + mb
                if near:
                    off = pl.multiple_of((jt - (n_tiles - 2)) * KT, KT)
                    l = l + relb_s[h, pl.ds(off, KT), :]
                slot = n * ATT_HEADS + h
                lg_s[slot] = l
                m_old = m_s[h:h + 1, :]
                m_new = jnp.maximum(m_old, jnp.max(l, axis=0, keepdims=True))
                mo_s[slot:slot + 1, :] = m_old
                mn_s[slot:slot + 1, :] = m_new
                m_s[h:h + 1, :] = m_new
        for n, ks in enumerate(starts):
            for h in range(ATT_HEADS):
                slot = n * ATT_HEADS + h
                m_new = mn_s[slot:slot + 1, :]
                m_ref = jnp.where(m_new == neg_inf, 0.0, m_new)
                alpha = jnp.exp2(mo_s[slot:slot + 1, :] - m_ref)
                p = jnp.exp2(lg_s[slot] - m_ref)
                l_s[h:h + 1, :] = alpha * l_s[h:h + 1, :] + jnp.sum(p, axis=0, keepdims=True)
                vt = vt_ref[0, h * HEAD_DIM:(h + 1) * HEAD_DIM, pl.ds(ks, KT)]
                rows = slice(h * HEAD_DIM, (h + 1) * HEAD_DIM)
                ot_s[rows, :] = ot_s[rows, :] * alpha + jnp.dot(vt, p.astype(BF16), preferred_element_type=F32)

    n_far = jnp.maximum(n_tiles - 2, 0)

    def far_pair(jp, c):
        att_tiles([2 * jp, 2 * jp + 1], False)
        return c

    lax.fori_loop(0, n_far // 2, far_pair, 0)

    @pl.when(n_far % 2 == 1)
    def _():
        att_tiles([n_far - 1], False)

    @pl.when(i > 0)
    def _():
        att_tiles([n_tiles - 2, n_tiles - 1], True)

    @pl.when(i == 0)
    def _():
        att_tiles([n_tiles - 1], True)

    for h in range(ATT_HEADS):
        rows = slice(h * HEAD_DIM, (h + 1) * HEAD_DIM)
        ot_s[rows, :] = ot_s[rows, :] / l_s[h:h + 1, :]

    ot = ot_s[...]
    ms = jnp.mean(ot * ot, axis=0, keepdims=True)
    y = (ot * lax.rsqrt(ms + EPS)).T * goa_ref[...]
    o_ref[0] = y.astype(BF16)


def _attn(q, k, vt, qi, ki, wit, rel_bias, goa):
    bsz, _, _, s = q.shape
    nsel = min(TOPK_MAX, s // 4)
    assert nsel == TQ and s % TQ == 0, "attention kernel assumes TOPK_MAX-sized query blocks"
    bkt = jnp.asarray(_near_buckets())
    kern = functools.partial(_attn_kernel, nsel=nsel)
    return pl.pallas_call(
        kern,
        grid=(bsz, s // TQ),
        in_specs=[
            pl.BlockSpec((1, ATT_HEADS, LANES, TQ), lambda b, i: (b, 0, 0, i)),
            pl.BlockSpec((1, ATT_HEADS // 2, s, LANES), lambda b, i: (b, 0, 0, 0)),
            pl.BlockSpec((1, ATT_W, s), lambda b, i: (b, 0, 0)),
            pl.BlockSpec((1, IDX_HEADS, LANES, TQ), lambda b, i: (b, 0, 0, i)),
            pl.BlockSpec((1, s, LANES), lambda b, i: (b, 0, 0)),
            pl.BlockSpec((1, SUBLANES, TQ), lambda b, i: (b, 0, i)),
            pl.BlockSpec((NEAR, TQ), lambda b, i: (0, 0)),
            pl.BlockSpec(memory_space=pltpu.SMEM),
            pl.BlockSpec((1, ATT_W), lambda b, i: (0, 0)),
        ],
        out_specs=pl.BlockSpec((1, TQ, ATT_W), lambda b, i: (b, i, 0)),
        out_shape=jax.ShapeDtypeStruct((bsz, s, ATT_W), BF16),
        scratch_shapes=[
            pltpu.VMEM((s, TQ), I32),
            pltpu.VMEM((s, TQ), I16),
            pltpu.VMEM((s, TQ), I16),
            pltpu.VMEM((s, TQ), F32),
            pltpu.VMEM((ATT_HEADS, 2 * KT, TQ), F32),
            pltpu.VMEM((ATT_W, TQ), F32),
            pltpu.VMEM((ATT_HEADS, TQ), F32),
            pltpu.VMEM((2 * ATT_HEADS, TQ), F32),
            pltpu.VMEM((2 * ATT_HEADS, TQ), F32),
            pltpu.VMEM((ATT_HEADS, TQ), F32),
            pltpu.VMEM((2 * ATT_HEADS, KT, TQ), F32),
        ],
        compiler_params=pltpu.CompilerParams(
            dimension_semantics=("arbitrary", "arbitrary"), vmem_limit_bytes=VMEM_LIMIT),
        name="attn",
    )(q, k, vt, qi, ki, wit, bkt, rel_bias, goa)


def _topk_rows(s, payload, k, order=None):
    rows = lax.broadcasted_iota(I32, s.shape, 0).astype(F32) if order is None else order
    vals, pays = [], []
    for _ in range(k):
        m = jnp.max(s, axis=0, keepdims=True)
        ix = jnp.min(jnp.where(s == m, rows, _NO_ROW), axis=0, keepdims=True)
        hit = rows == ix
        if payload is None:
            pays.append(ix)
        else:
            pays.append(jnp.max(jnp.where(hit, payload, -1.0), axis=0, keepdims=True))
        vals.append(m)
        s = jnp.where(hit, -jnp.inf, s)
    return jnp.concatenate(vals, axis=0), jnp.concatenate(pays, axis=0)


_NO_ROW = 1e9

_PAIR_GROUPS = (
    ((0, 0, 8, 0),), ((0, 8, 8, 0),), ((1, 0, 8, 0),),
    ((2, 0, 5, 0), (5, 0, 2, 5)), ((3, 0, 4, 0), (4, 0, 3, 4)), ((6, 0, 2, 0), (7, 0, 2, 2)),
)
assert sorted((a, b) for g in _PAIR_GROUPS for a, b0, nb, _ in g for b in range(b0, b0 + nb)) == sorted(
    (a, b) for a in range(PEER_TOPK // 2) for b in range(PEER_TOPK // (a + 1)))


def _mid_kernel(x_ref, cn_ref, an_ref, mod_ref, wo1_ref, wo2_ref, g2_ref, wpq_ref, k1_ref, k2_ref,
                x1_ref, h2_ref, idx_ref, gate_ref, qq_s, idt_s, gt_s):
    tm = x_ref.shape[1]
    x = x_ref[0]
    gt1 = mod_ref[0, 2:3, :]
    sh2 = mod_ref[0, 3:4, :]
    sc2 = mod_ref[0, 4:5, :]
    proj = (jnp.dot(cn_ref[0], wo1_ref[...], preferred_element_type=F32)
            + jnp.dot(an_ref[0], wo2_ref[...], preferred_element_type=F32))
    x1 = x + gt1 * proj
    x1_ref[0] = x1
    r = lax.rsqrt(jnp.mean(x1 * x1, axis=-1, keepdims=True) + EPS)
    h2 = (x1 * r) * g2_ref[...] * (1.0 + sc2) + sh2
    h2_ref[0] = h2
    qq_s[...] = jnp.dot(h2.astype(BF16), wpq_ref[...], preferred_element_type=F32).astype(BF16)

    def route_unit(hh, lt):
        rows = pl.ds(lt * LANES, LANES)
        q1 = qq_s[rows, pl.ds(pl.multiple_of(hh * 2 * N_KEYS, LANES), N_KEYS)]
        q2 = qq_s[rows, pl.ds(pl.multiple_of(hh * 2 * N_KEYS + N_KEYS, LANES), N_KEYS)]
        v1, i1 = _topk_rows(_nt(k1_ref[hh], q1), None, PEER_TOPK)
        v2, i2 = _topk_rows(_nt(k2_ref[hh], q2), None, PEER_TOPK)
        sub = lax.broadcasted_iota(I32, (SUBLANES, LANES), 0)
        subf = sub.astype(F32)
        cands, cidxs, flats = [], [], []
        for group in _PAIR_GROUPS:
            val = jnp.full((SUBLANES, LANES), -jnp.inf, F32)
            cid = jnp.zeros((SUBLANES, LANES), F32)
            flat = jnp.full((SUBLANES, LANES), _NO_ROW, F32)
            for a, b0, nb, off in group:
                v2s, i2s = v2[b0:b0 + SUBLANES, :], i2[b0:b0 + SUBLANES, :]
                if off:
                    v2s, i2s = pltpu.roll(v2s, off, 0), pltpu.roll(i2s, off, 0)
                inside = lambda new, old: jnp.where(sub < off + nb, jnp.where(sub >= off, new, old), old)
                val = inside(v1[a:a + 1, :] + v2s, val)
                cid = inside(i1[a:a + 1, :] * float(N_KEYS) + i2s, cid)
                flat = inside(subf + float(a * PEER_TOPK + b0 - off), flat)
            cands.append(val)
            cidxs.append(cid)
            flats.append(flat)
        half = PEER_TOPK // 2
        cands.append(v1[half:, :] + v2[0:1, :])
        cidxs.append(i1[half:, :] * float(N_KEYS) + i2[0:1, :])
        flats.append((subf + float(half)) * float(PEER_TOPK))
        best, experts = _topk_rows(jnp.concatenate(cands, axis=0), jnp.concatenate(cidxs, axis=0), PEER_TOPK,
                                   order=jnp.concatenate(flats, axis=0))
        e = jnp.exp(best - best[0:1, :])
        g = e / jnp.sum(e, axis=0, keepdims=True)
        slots = pl.ds(pl.multiple_of(hh * PEER_TOPK, PEER_TOPK), PEER_TOPK)
        cols = pl.ds(lt * LANES, LANES)
        idt_s[slots, cols] = experts * float(ROWS_PER_EXPERT)
        gt_s[slots, cols] = g

    def route(hh, c):
        for lt in range(tm // LANES):
            route_unit(hh, lt)
        return c

    lax.fori_loop(0, PEER_HEADS, route, 0)
    idx_ref[0] = idt_s[...].T.astype(I32)
    gate_ref[0] = gt_s[...].T


def _mid(x, cn, an, mod3, wo1, wo2, g2, wpq, k1, k2):
    bsz, s, d = x.shape
    tm = TM_MID
    full = lambda shape: pl.BlockSpec(shape, lambda b, j: (0,) * len(shape))
    tok = lambda w: pl.BlockSpec((1, tm, w), lambda b, j: (b, j, 0))
    return pl.pallas_call(
        _mid_kernel,
        grid=(bsz, s // tm),
        in_specs=[tok(d), tok(CONV_CH), tok(ATT_W), pl.BlockSpec((1, 6, d), lambda b, j: (b, 0, 0)),
                  full(wo1.shape), full(wo2.shape), full(g2.shape), full(wpq.shape), full(k1.shape), full(k2.shape)],
        out_specs=(tok(d), tok(d), tok(PEER_SLOTS), tok(PEER_SLOTS)),
        out_shape=(
            jax.ShapeDtypeStruct((bsz, s, d), F32),
            jax.ShapeDtypeStruct((bsz, s, d), F32),
            jax.ShapeDtypeStruct((bsz, s, PEER_SLOTS), I32),
            jax.ShapeDtypeStruct((bsz, s, PEER_SLOTS), F32),
        ),
        scratch_shapes=[
            pltpu.VMEM((tm, PEER_HEADS * 2 * N_KEYS), BF16),
            pltpu.VMEM((PEER_SLOTS, tm), F32),
            pltpu.VMEM((PEER_SLOTS, tm), F32),
        ],
        compiler_params=pltpu.CompilerParams(
            dimension_semantics=("arbitrary", "arbitrary"), vmem_limit_bytes=VMEM_LIMIT),
        name="mid",
    )(x, cn, an, mod3, wo1, wo2, g2, wpq, k1, k2)


def _pack_table(t):
    tb = t.astype(BF16)
    lo = lax.bitcast_convert_type(tb[:, :HALF], jnp.uint16).astype(jnp.uint32)
    hi = lax.bitcast_convert_type(tb[:, HALF:], jnp.uint16).astype(jnp.uint32)
    return (lo | (hi << 16)).reshape(t.shape[0] * ROWS_PER_EXPERT, LANES)


def _unpack(w):
    lo = pltpu.bitcast(lax.shift_left(w, jnp.uint32(16)), F32)
    hi = pltpu.bitcast(w & jnp.uint32(0xFFFF0000), F32)
    return lo, hi


def _gather_row(tab_ref, row):
    return tab_ref[pl.ds(pl.multiple_of(row, ROWS_PER_EXPERT), ROWS_PER_EXPERT), :]


STAGE_ROWS = PEER_SLOTS * ROWS_PER_EXPERT


def _token_pieces(xg, r, first, count):
    return jnp.concatenate(
        [xg[r:r + 1, (first + q) * LANES:(first + q + 1) * LANES] for q in range(count)], axis=0)


def _peer_u_kernel(idx_ref, tab_ref, x_ref, gate_ref, w_ref, *stages):
    tb = w_ref.shape[0]
    sub = lax.broadcasted_iota(I32, (SUBLANES, PEER_SLOTS), 0)

    def group(g, c):
        for k in range(PEER_GROUPS):
            eight(pl.multiple_of((g * PEER_GROUPS + k) * SUBLANES, SUBLANES), stages[k * SUBLANES:(k + 1) * SUBLANES])
        return c

    def eight(t0, stages):
        act8 = jnp.zeros((SUBLANES, PEER_SLOTS), F32)
        xg = x_ref[pl.ds(t0, SUBLANES), :]
        xlo, xhi, rows_t = [], [], []
        for r in range(SUBLANES):
            xlo.append(jnp.concatenate([_token_pieces(xg, r, 0, ROWS_PER_EXPERT)] * 2, axis=0))
            xhi.append(jnp.concatenate([_token_pieces(xg, r, ROWS_PER_EXPERT, ROWS_PER_EXPERT)] * 2, axis=0))
            rows_t.append(idx_ref.at[t0 + r])
        for j in range(0, PEER_SLOTS, 2):
            for r in range(SUBLANES):
                pair = jnp.concatenate(
                    [_gather_row(tab_ref, rows_t[r][j]), _gather_row(tab_ref, rows_t[r][j + 1])], axis=0)
                lo, hi = _unpack(pair)
                row = j * ROWS_PER_EXPERT
                stages[r][row:row + 2 * ROWS_PER_EXPERT, :] = lo * xlo[r] + hi * xhi[r]
        for r in range(SUBLANES):
            stage = stages[r]
            cs = stage[pl.ds(0, PEER_SLOTS, stride=ROWS_PER_EXPERT), :]
            for q in range(1, ROWS_PER_EXPERT):
                cs = cs + stage[pl.ds(q, PEER_SLOTS, stride=ROWS_PER_EXPERT), :]
            act = jnp.sum(cs.T, axis=0, keepdims=True)
            act8 = jnp.where(sub == r, act, act8)
        rows = pl.ds(t0, SUBLANES)
        w_ref[rows, :] = gate_ref[rows, :] * jax.nn.gelu(act8)

    lax.fori_loop(0, tb // (PEER_GROUPS * SUBLANES), group, 0)


def _peer_u(idx, tab, h2, gate):
    n, d = h2.shape
    tb = TB_PEER
    return pl.pallas_call(
        _peer_u_kernel,
        grid=(n // tb,),
        in_specs=[
            pl.BlockSpec((tb, PEER_SLOTS), lambda i: (i, 0), memory_space=pltpu.SMEM),
            pl.BlockSpec(tab.shape, lambda i: (0, 0), pipeline_mode=pl.Buffered(1)),
            pl.BlockSpec((tb, d), lambda i: (i, 0)),
            pl.BlockSpec((tb, PEER_SLOTS), lambda i: (i, 0)),
        ],
        out_specs=pl.BlockSpec((tb, PEER_SLOTS), lambda i: (i, 0)),
        out_shape=jax.ShapeDtypeStruct((n, PEER_SLOTS), F32),
        scratch_shapes=[pltpu.VMEM((STAGE_ROWS, LANES), F32) for _ in range(PEER_GROUPS * SUBLANES)],
        compiler_params=pltpu.CompilerParams(
            dimension_semantics=("arbitrary",), vmem_limit_bytes=VMEM_LIMIT),
        name="peer_u",
    )(idx, tab, h2, gate)


PV_PAIRS = SUBLANES // 2


def _peer_v_kernel(idx_ref, w_ref, tab_ref, x1_ref, gt2_ref, e8_ref, mask_ref, o_ref, *stages):
    tb = idx_ref.shape[0]
    gt2 = gt2_ref[0]
    mask = mask_ref[...]
    e8 = e8_ref[...]

    def group(g, c):
        for k in range(PEER_GROUPS_V):
            eight(pl.multiple_of((g * PEER_GROUPS_V + k) * SUBLANES, SUBLANES), stages[k * PV_PAIRS:(k + 1) * PV_PAIRS])
        return c

    def eight(g0, stages):
        peers = []
        for p in range(PV_PAIRS):
            stage = stages[p]
            t0 = g0 + 2 * p
            for u in range(2):
                rows_t = idx_ref.at[t0 + u]
                for j in range(PEER_SLOTS):
                    stage[j * ROWS_PER_EXPERT:(j + 1) * ROWS_PER_EXPERT, u * LANES:(u + 1) * LANES] = (
                        _gather_row(tab_ref, rows_t[j]))
            w2 = w_ref[pl.ds(t0, 2), :]
            hi = w2.astype(BF16).astype(F32)
            rep = jnp.dot(jnp.concatenate([hi, w2 - hi], axis=0).astype(BF16), e8,
                          preferred_element_type=F32)
            lhs = jnp.concatenate([rep[i:i + 1, :] * mask for i in range(4)], axis=0).astype(BF16)
            out = jnp.dot(lhs, pltpu.bitcast(stage[...], BF16), preferred_element_type=F32)
            for u in range(2):
                r0 = u * SUBLANES
                peers.append(out[r0:r0 + SUBLANES, u * LANES:(u + 1) * LANES]
                             + out[2 * SUBLANES + r0:3 * SUBLANES + r0, u * LANES:(u + 1) * LANES])
        rows = pl.ds(g0, SUBLANES)
        for q in range(SUBLANES):
            cols = slice(q * LANES, (q + 1) * LANES)
            piece = jnp.concatenate([peer[q:q + 1, :] for peer in peers], axis=0)
            o_ref[rows, cols] = x1_ref[rows, cols] + gt2[:, cols] * piece

    lax.fori_loop(0, tb // (PEER_GROUPS_V * SUBLANES), group, 0)


def _peer_v(idx, w, tab, x1, gt2, blocks_per_batch):
    n, d = x1.shape
    tb = TB_PEER
    q = np.arange(SUBLANES)
    piece = 2 * (q % ROWS_PER_EXPERT) + q // ROWS_PER_EXPERT
    lane = np.arange(PEER_SLOTS * SUBLANES)
    mask = jnp.asarray((lane[None, :] % SUBLANES == piece[:, None]).astype(np.float32))
    e8 = jnp.asarray(np.arange(PEER_SLOTS)[:, None] == lane[None, :] // SUBLANES, BF16)
    return pl.pallas_call(
        _peer_v_kernel,
        grid=(n // tb,),
        in_specs=[
            pl.BlockSpec((tb, PEER_SLOTS), lambda i: (i, 0), memory_space=pltpu.SMEM),
            pl.BlockSpec((tb, PEER_SLOTS), lambda i: (i, 0)),
            pl.BlockSpec(tab.shape, lambda i: (0, 0), pipeline_mode=pl.Buffered(1)),
            pl.BlockSpec((tb, d), lambda i: (i, 0)),
            pl.BlockSpec((1, 1, d), lambda i: (i // blocks_per_batch, 0, 0)),
            pl.BlockSpec(e8.shape, lambda i: (0, 0)),
            pl.BlockSpec(mask.shape, lambda i: (0, 0)),
        ],
        out_specs=pl.BlockSpec((tb, d), lambda i: (i, 0)),
        out_shape=jax.ShapeDtypeStruct((n, d), F32),
        scratch_shapes=[pltpu.VMEM((STAGE_ROWS, 2 * LANES), jnp.uint32) for _ in range(PEER_GROUPS_V * PV_PAIRS)],
        compiler_params=pltpu.CompilerParams(
            dimension_semantics=("arbitrary",), vmem_limit_bytes=VMEM_LIMIT),
        name="peer_v",
    )(idx, w, tab, x1, gt2, e8, mask)


def _layer(x, mod, g_norm1, g_norm2, w_in, q_norm_g, k_norm_g, conv_w, conv_b, conv_ln_g, conv_ln_b,
           rel_bias, g_out_conv, g_out_attn, w_out, w_peer_q, peer_k1, peer_k2, peer_u, peer_v):
    bsz, s, d = x.shape
    n = bsz * s
    mod3 = mod.reshape(bsz, 6, d)
    row = lambda a: a.reshape(1, -1)

    c0 = 2 * CONV_CH
    c1 = c0 + 3 * ATT_W
    c2 = c1 + IDX_HEADS * IDX_DIM
    c3 = c2 + IDX_DIM
    wa = w_in[:, :c0].astype(BF16)
    wqkv = w_in[:, c0:c1].astype(BF16)
    widx = jnp.concatenate([w_in[:, c1:c2], w_in[:, c2:c3], w_in[:, c2:c3]], axis=1).astype(BF16)
    wwi = jnp.zeros((SUBLANES, d), F32).at[:IDX_HEADS].set(w_in[:, c3:c3 + IDX_HEADS].T).astype(BF16)
    head = np.arange(ATT_W) // HEAD_DIM
    e2 = jnp.asarray(head[:, None] == head[None, :], BF16)

    conv_n, q, k, vt, qi, ki, wit = _inproj(
        x, mod3, row(g_norm1), wa, wqkv, widx, wwi,
        row(jnp.tile(q_norm_g, ATT_HEADS)), row(jnp.tile(k_norm_g, ATT_HEADS)), e2,
        conv_w.reshape(CONV_WIDTH, CONV_CH), row(conv_b), row(conv_ln_g), row(conv_ln_b), row(g_out_conv))
    attn_n = _attn(q, k, vt, qi, ki, wit, rel_bias, row(g_out_attn))

    x1, h2, idx, gate = _mid(
        x, conv_n, attn_n, mod3, w_out[:CONV_CH].astype(BF16), w_out[CONV_CH:].astype(BF16), row(g_norm2),
        w_peer_q.astype(BF16), peer_k1.astype(BF16), peer_k2.astype(BF16))

    idx = idx.reshape(n, PEER_SLOTS)
    w = _peer_u(idx, _pack_table(peer_u), h2.reshape(n, d), gate.reshape(n, PEER_SLOTS))
    out = _peer_v(idx, w, _pack_table(peer_v), x1.reshape(n, d), mod3[:, 5:6, :], s // TB_PEER)
    return out.reshape(bsz, s, d)


def kernel(x, c, w_ada, b_ada, g_norm1, g_norm2, w_in, q_norm_g, k_norm_g, conv_w, conv_b, conv_ln_g,
           conv_ln_b, rel_bias, g_out_conv, g_out_attn, w_out, w_peer_q, peer_k1, peer_k2, peer_u, peer_v):
    depth = w_ada.shape[0]
    for l in range(depth):
        mod = _ada(c, w_ada[l], b_ada[l])
        x = _layer(x, mod, g_norm1[l], g_norm2[l], w_in[l], q_norm_g[l], k_norm_g[l], conv_w[l], conv_b[l],
                   conv_ln_g[l], conv_ln_b[l], rel_bias, g_out_conv[l], g_out_attn[l], w_out[l],
                   w_peer_q[l], peer_k1[l], peer_k2[l], peer_u[l], peer_v[l])
    return x
```

```python
import functools
import math

import numpy as np
import jax
import jax.numpy as jnp
from jax import lax
from jax.experimental import pallas as pl
from jax.experimental.pallas import tpu as pltpu

F32 = jnp.float32
BF16 = jnp.bfloat16
I32 = jnp.int32
I16 = jnp.int16
HIGHEST = lax.Precision.HIGHEST

D_MODEL = 1024
CHUNK = 64
CONV_CH = 512
CONV_WIDTH = 31
ATT_HEADS = 8
HEAD_DIM = 64
ATT_W = ATT_HEADS * HEAD_DIM
IDX_HEADS = 4
IDX_DIM = 64
IDX_SCALE = (IDX_HEADS * IDX_DIM) ** -0.5
TOPK_MAX = 256
REL_BUCKETS = 32
REL_MAX_DIST = 128
PEER_HEADS = 8
N_KEYS = 128
N_EXPERTS = N_KEYS * N_KEYS
PEER_TOPK = 16
PEER_SLOTS = PEER_HEADS * PEER_TOPK
EPS = 1e-6
LOG2E = math.log2(math.e)

LANES = 128
SUBLANES = 8
PACK = 16
VMEM_LIMIT = 56 * 1024 * 1024

TM_IN = 512
CONV_ROWS = 64
HALO = 32
TQ = 256
KT = 256
TM_MID = 1024
TB_PEER = 256
PEER_GROUPS = 2
PEER_GROUPS_V = 4
HALF = D_MODEL // 2
ROWS_PER_EXPERT = HALF // LANES

NT_DIMS = (((1,), (1,)), ((), ()))

_NEG_INF_BITS = int(np.array(-np.inf, np.float32).view(np.int32))
KEY_NEG_INF = _NEG_INF_BITS ^ 0x7FFFFFFF
I16_MIN = -(2 ** 15)
I16_MAX = 2 ** 15 - 1


def _nt(a, b, precision=None):
    return lax.dot_general(a, b, NT_DIMS, precision=precision, preferred_element_type=F32)


def _ada_kernel(c_ref, w_ref, b_ref, o_ref):
    a = jax.nn.silu(c_ref[...])
    o_ref[...] = jnp.dot(a, w_ref[...], precision=HIGHEST, preferred_element_type=F32) + b_ref[...]


def _ada(c, w_ada, b_ada):
    bsz, d = c.shape
    return pl.pallas_call(
        _ada_kernel,
        grid=(6,),
        in_specs=[
            pl.BlockSpec((bsz, d), lambda j: (0, 0)),
            pl.BlockSpec((d, d), lambda j: (0, j)),
            pl.BlockSpec((1, d), lambda j: (0, j)),
        ],
        out_specs=pl.BlockSpec((bsz, d), lambda j: (0, j)),
        out_shape=jax.ShapeDtypeStruct((bsz, 6 * d), F32),
        name="ada",
    )(c, w_ada, b_ada.reshape(1, 6 * d))


def _inproj_kernel(x_ref, mod_ref, g1_ref, wa_ref, wqkv_ref, widx_ref, wwi_ref, gq_ref, gk_ref, e2_ref,
                   cw_ref, cb_ref, lng_ref, lnb_ref, goc_ref,
                   conv_ref, q_ref, k_ref, vt_ref, qi_ref, ki_ref, wit_ref, ubuf):
    j = pl.program_id(1)
    tm = x_ref.shape[1]
    x = x_ref[0]
    sh1 = mod_ref[0, 0:1, :]
    sc1 = mod_ref[0, 1:2, :]
    r = lax.rsqrt(jnp.mean(x * x, axis=-1, keepdims=True) + EPS)
    h = (x * r) * g1_ref[...] * (1.0 + sc1) + sh1
    hb = h.astype(BF16)

    pa = jnp.dot(hb, wa_ref[...], preferred_element_type=F32)
    u = pa[:, :CONV_CH] * jax.nn.sigmoid(pa[:, CONV_CH:])

    @pl.when(j == 0)
    def _():
        ubuf[0:HALO, :] = jnp.zeros((HALO, CONV_CH), F32)

    ubuf[HALO:HALO + tm, :] = u
    first = HALO - (CONV_WIDTH - 1)
    for rb in range(tm // CONV_ROWS):
        base = rb * CONV_ROWS
        acc = jnp.zeros((CONV_ROWS, CONV_CH), F32) + cb_ref[...]
        for phase in range(SUBLANES):
            taps = [t for t in range(CONV_WIDTH) if (first + t) % SUBLANES == phase]
            span = (first + taps[-1]) - phase + CONV_ROWS
            slab = ubuf[base + phase:base + phase + span, :]
            for t in taps:
                off = first + t - phase
                acc = acc + cw_ref[t:t + 1, :] * slab[off:off + CONV_ROWS, :]
        mu = jnp.mean(acc, axis=-1, keepdims=True)
        xc = acc - mu
        y = xc * lax.rsqrt(jnp.mean(xc * xc, axis=-1, keepdims=True) + EPS)
        y = jax.nn.silu(y * lng_ref[...] + lnb_ref[...])
        y = y * lax.rsqrt(jnp.mean(y * y, axis=-1, keepdims=True) + EPS) * goc_ref[...]
        conv_ref[0, base:base + CONV_ROWS, :] = y.astype(BF16)
    ubuf[0:HALO, :] = ubuf[tm:tm + HALO, :]

    pq = jnp.dot(hb, wqkv_ref[...], preferred_element_type=F32)
    q = pq[:, :ATT_W]
    k = pq[:, ATT_W:2 * ATT_W]
    v = pq[:, 2 * ATT_W:]
    e2 = e2_ref[...]

    def head_mean(sq):
        hi = sq.astype(BF16)
        lo = (sq - hi.astype(F32)).astype(BF16)
        return (jnp.dot(hi, e2, preferred_element_type=F32)
                + jnp.dot(lo, e2, preferred_element_type=F32)) * (1.0 / HEAD_DIM)

    qs = head_mean(q * q)
    ks = head_mean(k * k)
    qn = q * lax.rsqrt(qs + EPS) * gq_ref[...] * (HEAD_DIM ** -0.5 * LOG2E)
    kn = k * lax.rsqrt(ks + EPS) * gk_ref[...]
    lane = lax.broadcasted_iota(I32, (tm, LANES), 1)
    low = lane < HEAD_DIM
    for p in range(ATT_HEADS // 2):
        slab = qn[:, p * LANES:(p + 1) * LANES]
        q_ref[0, 2 * p] = jnp.where(low, slab, 0.0).T.astype(BF16)
        q_ref[0, 2 * p + 1] = jnp.where(low, 0.0, slab).T.astype(BF16)
        k_ref[0, p] = kn[:, p * LANES:(p + 1) * LANES].astype(BF16)
    vt_ref[0] = v.T.astype(BF16)

    pc = jnp.dot(hb, widx_ref[...], preferred_element_type=F32)
    for p in range(IDX_HEADS // 2):
        slab = pc[:, p * LANES:(p + 1) * LANES]
        qi_ref[0, 2 * p] = jnp.where(low, slab, 0.0).T.astype(BF16)
        qi_ref[0, 2 * p + 1] = jnp.where(low, 0.0, slab).T.astype(BF16)
    ki_ref[0] = pc[:, 2 * LANES:3 * LANES].astype(BF16)
    wit_ref[0] = _nt(wwi_ref[...], hb) * IDX_SCALE


def _inproj(x, mod3, g1, wa, wqkv, widx, wwi, gq, gk, e2, cw, cb, lng, lnb, goc):
    bsz, s, d = x.shape
    tm = TM_IN
    nt = s // tm
    full = lambda shape: pl.BlockSpec(shape, lambda b, j: (0,) * len(shape))
    out_shape = (
        jax.ShapeDtypeStruct((bsz, s, CONV_CH), BF16),
        jax.ShapeDtypeStruct((bsz, ATT_HEADS, LANES, s), BF16),
        jax.ShapeDtypeStruct((bsz, ATT_HEADS // 2, s, LANES), BF16),
        jax.ShapeDtypeStruct((bsz, ATT_W, s), BF16),
        jax.ShapeDtypeStruct((bsz, IDX_HEADS, LANES, s), BF16),
        jax.ShapeDtypeStruct((bsz, s, LANES), BF16),
        jax.ShapeDtypeStruct((bsz, SUBLANES, s), F32),
    )
    out_specs = (
        pl.BlockSpec((1, tm, CONV_CH), lambda b, j: (b, j, 0)),
        pl.BlockSpec((1, ATT_HEADS, LANES, tm), lambda b, j: (b, 0, 0, j)),
        pl.BlockSpec((1, ATT_HEADS // 2, tm, LANES), lambda b, j: (b, 0, j, 0)),
        pl.BlockSpec((1, ATT_W, tm), lambda b, j: (b, 0, j)),
        pl.BlockSpec((1, IDX_HEADS, LANES, tm), lambda b, j: (b, 0, 0, j)),
        pl.BlockSpec((1, tm, LANES), lambda b, j: (b, j, 0)),
        pl.BlockSpec((1, SUBLANES, tm), lambda b, j: (b, 0, j)),
    )
    return pl.pallas_call(
        _inproj_kernel,
        grid=(bsz, nt),
        in_specs=[
            pl.BlockSpec((1, tm, d), lambda b, j: (b, j, 0)),
            pl.BlockSpec((1, 6, d), lambda b, j: (b, 0, 0)),
            full(g1.shape), full(wa.shape), full(wqkv.shape), full(widx.shape), full(wwi.shape),
            full(gq.shape), full(gk.shape), full(e2.shape),
            full(cw.shape), full(cb.shape), full(lng.shape), full(lnb.shape), full(goc.shape),
        ],
        out_specs=out_specs,
        out_shape=out_shape,
        scratch_shapes=[pltpu.VMEM((tm + HALO, CONV_CH), F32)],
        compiler_params=pltpu.CompilerParams(
            dimension_semantics=("arbitrary", "arbitrary"), vmem_limit_bytes=VMEM_LIMIT),
        name="inproj",
    )(x, mod3, g1, wa, wqkv, widx, wwi, gq, gk, e2, cw, cb, lng, lnb, goc)


def _t5_bucket_np(rel):
    half = REL_BUCKETS // 2
    max_exact = half // 2
    ret = np.where(rel > 0, half, 0)
    n = np.abs(rel)
    nf = np.maximum(n, 1).astype(np.float64)
    large = max_exact + (np.log(nf / max_exact) / math.log(REL_MAX_DIST / max_exact)
                         * (half - max_exact)).astype(np.int32)
    large = np.minimum(large, half - 1)
    return (ret + np.where(n < max_exact, n, large)).astype(np.int32)


NEAR = REL_MAX_DIST + TQ
FAR_BUCKET = REL_BUCKETS // 2 - 1


def _near_buckets():
    r = np.arange(NEAR)[:, None]
    t = np.arange(TQ)[None, :]
    return _t5_bucket_np(r - REL_MAX_DIST - t)


def _attn_kernel(q_ref, k_ref, vt_ref, qi_ref, ki_ref, wit_ref, bkt_ref, rb_ref, goa_ref, o_ref,
                 keys_s, hi_s, lo_s, maskb_s, relb_s, ot_s, m_s, mo_s, mn_s, l_s, lg_s, *, nsel):
    b = pl.program_id(0)
    i = pl.program_id(1)
    t0 = i * TQ
    n_tiles = i + 1
    neg_inf = F32(-jnp.inf)

    @pl.when((b == 0) & (i == 0))
    def _():
        bk = bkt_ref[...]
        for h in range(ATT_HEADS):
            far = rb_ref[FAR_BUCKET, h]
            acc = jnp.zeros((NEAR, TQ), F32)
            for bb in range(REL_BUCKETS):
                acc = jnp.where(bk == bb, (rb_ref[bb, h] - far) * LOG2E, acc)
            relb_s[h, 0:2 * KT - NEAR, :] = jnp.zeros((2 * KT - NEAR, TQ), F32)
            relb_s[h, 2 * KT - NEAR:2 * KT, :] = acc

    qpos = t0 + lax.broadcasted_iota(I32, (1, TQ), 1)
    limit = (qpos // CHUNK + 1) * CHUNK
    row_iota = lax.broadcasted_iota(I32, (KT, TQ), 0)

    def tile_start(jt):
        return pl.multiple_of(jt * KT, KT)

    wi = wit_ref[0]

    def score_tile(jt):
        ks = tile_start(jt)
        kit = ki_ref[0, pl.ds(ks, KT), :]
        acc = jnp.zeros((KT, TQ), F32)
        for h in range(IDX_HEADS):
            lgt = jnp.dot(kit, qi_ref[0, h], preferred_element_type=F32)
            acc = acc + jnp.maximum(lgt, 0.0) * wi[h:h + 1, :]
        sc = jnp.where(row_iota + ks < limit, acc, neg_inf)
        bits = pltpu.bitcast(sc, I32)
        key = bits ^ (lax.shift_right_arithmetic(bits, 31) & 0x7FFFFFFF)
        keys_s[pl.ds(ks, KT), :] = key
        hi_s[pl.ds(ks, KT), :] = lax.shift_right_arithmetic(key, 16).astype(I16)
        lo_s[pl.ds(ks, KT), :] = ((key & 0xFFFF) ^ 0x8000).astype(I16)

    def score_pair(jp, c):
        score_tile(2 * jp)
        score_tile(2 * jp + 1)
        return c

    lax.fori_loop(0, n_tiles // 2, score_pair, 0)

    @pl.when(n_tiles % 2 == 1)
    def _():
        score_tile(n_tiles - 1)

    def count_ge(half_ref, cand):
        c16 = cand.astype(I16)
        one = jnp.ones((KT, TQ), I16)
        zero = jnp.zeros((KT, TQ), I16)

        def tile_count(tiles):
            parts = []
            for jt in tiles:
                hit = jnp.where(half_ref[pl.ds(tile_start(jt), KT), :] >= c16, one, zero)
                parts += [hit[r:r + PACK, :] for r in range(0, KT, PACK)]
            while len(parts) > 1:
                parts = [parts[k] + parts[k + 1] for k in range(0, len(parts), 2)]
            return parts[0].astype(I32)

        acc = lax.fori_loop(0, n_tiles // 2, lambda jp, acc: acc + tile_count([2 * jp, 2 * jp + 1]),
                            jnp.zeros((PACK, TQ), I32))
        acc = lax.cond(n_tiles % 2 == 1, lambda a: a + tile_count([n_tiles - 1]), lambda a: a, acc)
        return jnp.sum(acc.astype(F32), axis=0, keepdims=True)

    def low_key(low):
        return (low ^ 0x8000) - jnp.where(low < 0x8000, 0x10000, 0)

    def select_thr():
        c0 = count_ge(hi_s, jnp.zeros((1, TQ), I32))
        ok = c0 >= nsel
        t = jnp.where(ok, 0, I16_MIN).astype(I32)
        cnt = jnp.where(ok, c0, F32(2 ** 30))

        def top_step(it, carry):
            t, cnt = carry
            cand = t + lax.shift_left(I32(1), I32(14) - it)
            c = count_ge(hi_s, cand)
            ok = c >= nsel
            return jnp.where(ok, cand, t), jnp.where(ok, c, cnt)

        top, cnt = lax.fori_loop(0, 15, top_step, (t, cnt))

        top_max = top >= I16_MAX
        above = jnp.where(top_max, 0.0, count_ge(hi_s, jnp.where(top_max, top, top + 1)))
        top16 = top.astype(I16)
        floor16 = jnp.full((KT, TQ), I16_MIN, I16)

        def keep_low(jt, c):
            rows = pl.ds(tile_start(jt), KT)
            lo_s[rows, :] = jnp.where(hi_s[rows, :] == top16, lo_s[rows, :], floor16)
            return c

        lax.fori_loop(0, n_tiles, keep_low, 0)

        at_zero = jnp.where(top == 0, above + count_ge(lo_s, low_key(jnp.ones((1, TQ), I32))), F32(2 ** 30))
        settled = at_zero < nsel

        def low_cond(carry):
            bit, _, cnt = carry
            return (bit >= 0) & (jnp.max(jnp.where(settled, F32(nsel), cnt)) > nsel)

        def low_step(carry):
            bit, low, cnt = carry
            cand = low + lax.shift_left(I32(1), bit)
            c = above + count_ge(lo_s, low_key(cand))
            ok = c >= nsel
            return bit - 1, jnp.where(ok, cand, low), jnp.where(ok, c, cnt)

        _, low, cnt = lax.while_loop(low_cond, low_step, (I32(15), jnp.zeros((1, TQ), I32), cnt))
        low_max = low >= 0xFFFF
        c_gt = above + jnp.where(low_max, 0.0, count_ge(lo_s, low_key(jnp.where(low_max, low, low + 1))))
        return lax.shift_left(top, 16) + low, cnt, c_gt

    thr, c_ge, c_gt = lax.cond(
        i > 0, select_thr,
        lambda: (jnp.full((1, TQ), KEY_NEG_INF + 1, I32), jnp.full((1, TQ), nsel, F32), jnp.zeros((1, TQ), F32)))

    def mask_plain():
        def p3(jt, c):
            ks = tile_start(jt)
            maskb_s[pl.ds(ks, KT), :] = jnp.where(keys_s[pl.ds(ks, KT), :] >= thr, 0.0, neg_inf)
            return c
        lax.fori_loop(0, n_tiles, p3, 0)
        return I32(0)

    def mask_ties():
        quota = nsel - c_gt
        tri = jnp.where(lax.broadcasted_iota(I32, (KT, KT), 0) >= lax.broadcasted_iota(I32, (KT, KT), 1),
                        1.0, 0.0).astype(BF16)

        def local_ranks(jt):
            ks = tile_start(jt)
            kt = keys_s[pl.ds(ks, KT), :]
            eq = kt == thr
            return ks, kt, eq, jnp.dot(tri, jnp.where(eq, 1.0, 0.0).astype(BF16), preferred_element_type=F32)

        def write_mask(tile, before):
            ks, kt, eq, local = tile
            rank = before + local
            tied = jnp.where(eq, jnp.where(rank <= quota, 0.0, neg_inf), neg_inf)
            maskb_s[pl.ds(ks, KT), :] = jnp.where(kt > thr, 0.0, tied)
            return rank[KT - 1:KT, :]

        def pair(jp, before):
            first, second = local_ranks(2 * jp), local_ranks(2 * jp + 1)
            return write_mask(second, write_mask(first, before))

        before = lax.fori_loop(0, n_tiles // 2, pair, jnp.zeros((1, TQ), F32))

        @pl.when(n_tiles % 2 == 1)
        def _():
            write_mask(local_ranks(n_tiles - 1), before)

        return I32(0)

    lax.cond(jnp.max(c_ge) > nsel, mask_ties, mask_plain)

    m_s[...] = jnp.full((ATT_HEADS, TQ), neg_inf, F32)
    l_s[...] = jnp.zeros((ATT_HEADS, TQ), F32)
    ot_s[...] = jnp.zeros((ATT_W, TQ), F32)

    def att_tiles(jts, near):
        starts = [tile_start(jt) for jt in jts]
        for n, (jt, ks) in enumerate(zip(jts, starts)):
            mb = maskb_s[pl.ds(ks, KT), :]
            for h in range(ATT_HEADS):
                l = jnp.dot(k_ref[0, h // 2, pl.ds(ks, KT), :], q_ref[0, h], preferred_element_type=F32) + mb
                if near:
                    off = pl.multiple_of((jt - (n_tiles - 2)) * KT, KT)
                    l = l + relb_s[h, pl.ds(off, KT), :]
                slot = n * ATT_HEADS + h
                lg_s[slot] = l
                m_old = m_s[h:h + 1, :]
                m_new = jnp.maximum(m_old, jnp.max(l, axis=0, keepdims=True))
                mo_s[slot:slot + 1, :] = m_old
                mn_s[slot:slot + 1, :] = m_new
                m_s[h:h + 1, :] = m_new
        for n, ks in enumerate(starts):
            for h in range(ATT_HEADS):
                slot = n * ATT_HEADS + h
                m_new = mn_s[slot:slot + 1, :]
                m_ref = jnp.where(m_new == neg_inf, 0.0, m_new)
                alpha = jnp.exp2(mo_s[slot:slot + 1, :] - m_ref)
                p = jnp.exp2(lg_s[slot] - m_ref)
                l_s[h:h + 1, :] = alpha * l_s[h:h + 1, :] + jnp.sum(p, axis=0, keepdims=True)
                vt = vt_ref[0, h * HEAD_DIM:(h + 1) * HEAD_DIM, pl.ds(ks, KT)]
                rows = slice(h * HEAD_DIM, (h + 1) * HEAD_DIM)
                ot_s[rows, :] = ot_s[rows, :] * alpha + jnp.dot(vt, p.astype(BF16), preferred_element_type=F32)

    n_far = jnp.maximum(n_tiles - 2, 0)

    def far_pair(jp, c):
        att_tiles([2 * jp, 2 * jp + 1], False)
        return c

    lax.fori_loop(0, n_far // 2, far_pair, 0)

    @pl.when(n_far % 2 == 1)
    def _():
        att_tiles([n_far - 1], False)

    @pl.when(i > 0)
    def _():
        att_tiles([n_tiles - 2, n_tiles - 1], True)

    @pl.when(i == 0)
    def _():
        att_tiles([n_tiles - 1], True)

    for h in range(ATT_HEADS):
        rows = slice(h * HEAD_DIM, (h + 1) * HEAD_DIM)
        ot_s[rows, :] = ot_s[rows, :] / l_s[h:h + 1, :]

    ot = ot_s[...]
    ms = jnp.mean(ot * ot, axis=0, keepdims=True)
    y = (ot * lax.rsqrt(ms + EPS)).T * goa_ref[...]
    o_ref[0] = y.astype(BF16)


def _attn(q, k, vt, qi, ki, wit, rel_bias, goa):
    bsz, _, _, s = q.shape
    nsel = min(TOPK_MAX, s // 4)
    assert nsel == TQ and s % TQ == 0, "attention kernel assumes TOPK_MAX-sized query blocks"
    bkt = jnp.asarray(_near_buckets())
    kern = functools.partial(_attn_kernel, nsel=nsel)
    return pl.pallas_call(
        kern,
        grid=(bsz, s // TQ),
        in_specs=[
            pl.BlockSpec((1, ATT_HEADS, LANES, TQ), lambda b, i: (b, 0, 0, i)),
            pl.BlockSpec((1, ATT_HEADS // 2, s, LANES), lambda b, i: (b, 0, 0, 0)),
            pl.BlockSpec((1, ATT_W, s), lambda b, i: (b, 0, 0)),
            pl.BlockSpec((1, IDX_HEADS, LANES, TQ), lambda b, i: (b, 0, 0, i)),
            pl.BlockSpec((1, s, LANES), lambda b, i: (b, 0, 0)),
            pl.BlockSpec((1, SUBLANES, TQ), lambda b, i: (b, 0, i)),
            pl.BlockSpec((NEAR, TQ), lambda b, i: (0, 0)),
            pl.BlockSpec(memory_space=pltpu.SMEM),
            pl.BlockSpec((1, ATT_W), lambda b, i: (0, 0)),
        ],
        out_specs=pl.BlockSpec((1, TQ, ATT_W), lambda b, i: (b, i, 0)),
        out_shape=jax.ShapeDtypeStruct((bsz, s, ATT_W), BF16),
        scratch_shapes=[
            pltpu.VMEM((s, TQ), I32),
            pltpu.VMEM((s, TQ), I16),
            pltpu.VMEM((s, TQ), I16),
            pltpu.VMEM((s, TQ), F32),
            pltpu.VMEM((ATT_HEADS, 2 * KT, TQ), F32),
            pltpu.VMEM((ATT_W, TQ), F32),
            pltpu.VMEM((ATT_HEADS, TQ), F32),
            pltpu.VMEM((2 * ATT_HEADS, TQ), F32),
            pltpu.VMEM((2 * ATT_HEADS, TQ), F32),
            pltpu.VMEM((ATT_HEADS, TQ), F32),
            pltpu.VMEM((2 * ATT_HEADS, KT, TQ), F32),
        ],
        compiler_params=pltpu.CompilerParams(
            dimension_semantics=("arbitrary", "arbitrary"), vmem_limit_bytes=VMEM_LIMIT),
        name="attn",
    )(q, k, vt, qi, ki, wit, bkt, rel_bias, goa)


def _topk_rows(s, payload, k, order=None):
    rows = lax.broadcasted_iota(I32, s.shape, 0).astype(F32) if order is None else order
    vals, pays = [], []
    for _ in range(k):
        m = jnp.max(s, axis=0, keepdims=True)
        ix = jnp.min(jnp.where(s == m, rows, _NO_ROW), axis=0, keepdims=True)
        hit = rows == ix
        if payload is None:
            pays.append(ix)
        else:
            pays.append(jnp.max(jnp.where(hit, payload, -1.0), axis=0, keepdims=True))
        vals.append(m)
        s = jnp.where(hit, -jnp.inf, s)
    return jnp.concatenate(vals, axis=0), jnp.concatenate(pays, axis=0)


_NO_ROW = 1e9

_PAIR_GROUPS = (
    ((0, 0, 8, 0),), ((0, 8, 8, 0),), ((1, 0, 8, 0),),
    ((2, 0, 5, 0), (5, 0, 2, 5)), ((3, 0, 4, 0), (4, 0, 3, 4)), ((6, 0, 2, 0), (7, 0, 2, 2)),
)
assert sorted((a, b) for g in _PAIR_GROUPS for a, b0, nb, _ in g for b in range(b0, b0 + nb)) == sorted(
    (a, b) for a in range(PEER_TOPK // 2) for b in range(PEER_TOPK // (a + 1)))


def _mid_kernel(x_ref, cn_ref, an_ref, mod_ref, wo1_ref, wo2_ref, g2_ref, wpq_ref, k1_ref, k2_ref,
                x1_ref, h2_ref, idx_ref, gate_ref, qq_s, idt_s, gt_s):
    tm = x_ref.shape[1]
    x = x_ref[0]
    gt1 = mod_ref[0, 2:3, :]
    sh2 = mod_ref[0, 3:4, :]
    sc2 = mod_ref[0, 4:5, :]
    proj = (jnp.dot(cn_ref[0], wo1_ref[...], preferred_element_type=F32)
            + jnp.dot(an_ref[0], wo2_ref[...], preferred_element_type=F32))
    x1 = x + gt1 * proj
    x1_ref[0] = x1
    r = lax.rsqrt(jnp.mean(x1 * x1, axis=-1, keepdims=True) + EPS)
    h2 = (x1 * r) * g2_ref[...] * (1.0 + sc2) + sh2
    h2_ref[0] = h2
    qq_s[...] = jnp.dot(h2.astype(BF16), wpq_ref[...], preferred_element_type=F32).astype(BF16)

    def route_unit(hh, lt):
        rows = pl.ds(lt * LANES, LANES)
        q1 = qq_s[rows, pl.ds(pl.multiple_of(hh * 2 * N_KEYS, LANES), N_KEYS)]
        q2 = qq_s[rows, pl.ds(pl.multiple_of(hh * 2 * N_KEYS + N_KEYS, LANES), N_KEYS)]
        v1, i1 = _topk_rows(_nt(k1_ref[hh], q1), None, PEER_TOPK)
        v2, i2 = _topk_rows(_nt(k2_ref[hh], q2), None, PEER_TOPK)
        sub = lax.broadcasted_iota(I32, (SUBLANES, LANES), 0)
        subf = sub.astype(F32)
        cands, cidxs, flats = [], [], []
        for group in _PAIR_GROUPS:
            val = jnp.full((SUBLANES, LANES), -jnp.inf, F32)
            cid = jnp.zeros((SUBLANES, LANES), F32)
            flat = jnp.full((SUBLANES, LANES), _NO_ROW, F32)
            for a, b0, nb, off in group:
                v2s, i2s = v2[b0:b0 + SUBLANES, :], i2[b0:b0 + SUBLANES, :]
                if off:
                    v2s, i2s = pltpu.roll(v2s, off, 0), pltpu.roll(i2s, off, 0)
                inside = lambda new, old: jnp.where(sub < off + nb, jnp.where(sub >= off, new, old), old)
                val = inside(v1[a:a + 1, :] + v2s, val)
                cid = inside(i1[a:a + 1, :] * float(N_KEYS) + i2s, cid)
                flat = inside(subf + float(a * PEER_TOPK + b0 - off), flat)
            cands.append(val)
            cidxs.append(cid)
            flats.append(flat)
        half = PEER_TOPK // 2
        cands.append(v1[half:, :] + v2[0:1, :])
        cidxs.append(i1[half:, :] * float(N_KEYS) + i2[0:1, :])
        flats.append((subf + float(half)) * float(PEER_TOPK))
        best, experts = _topk_rows(jnp.concatenate(cands, axis=0), jnp.concatenate(cidxs, axis=0), PEER_TOPK,
                                   order=jnp.concatenate(flats, axis=0))
        e = jnp.exp(best - best[0:1, :])
        g = e / jnp.sum(e, axis=0, keepdims=True)
        slots = pl.ds(pl.multiple_of(hh * PEER_TOPK, PEER_TOPK), PEER_TOPK)
        cols = pl.ds(lt * LANES, LANES)
        idt_s[slots, cols] = experts * float(ROWS_PER_EXPERT)
        gt_s[slots, cols] = g

    def route(hh, c):
        for lt in range(tm // LANES):
            route_unit(hh, lt)
        return c

    lax.fori_loop(0, PEER_HEADS, route, 0)
    idx_ref[0] = idt_s[...].T.astype(I32)
    gate_ref[0] = gt_s[...].T


def _mid(x, cn, an, mod3, wo1, wo2, g2, wpq, k1, k2):
    bsz, s, d = x.shape
    tm = TM_MID
    full = lambda shape: pl.BlockSpec(shape, lambda b, j: (0,) * len(shape))
    tok = lambda w: pl.BlockSpec((1, tm, w), lambda b, j: (b, j, 0))
    return pl.pallas_call(
        _mid_kernel,
        grid=(bsz, s // tm),
        in_specs=[tok(d), tok(CONV_CH), tok(ATT_W), pl.BlockSpec((1, 6, d), lambda b, j: (b, 0, 0)),
                  full(wo1.shape), full(wo2.shape), full(g2.shape), full(wpq.shape), full(k1.shape), full(k2.shape)],
        out_specs=(tok(d), tok(d), tok(PEER_SLOTS), tok(PEER_SLOTS)),
        out_shape=(
            jax.ShapeDtypeStruct((bsz, s, d), F32),
            jax.ShapeDtypeStruct((bsz, s, d), F32),
            jax.ShapeDtypeStruct((bsz, s, PEER_SLOTS), I32),
            jax.ShapeDtypeStruct((bsz, s, PEER_SLOTS), F32),
        ),
        scratch_shapes=[
            pltpu.VMEM((tm, PEER_HEADS * 2 * N_KEYS), BF16),
            pltpu.VMEM((PEER_SLOTS, tm), F32),
            pltpu.VMEM((PEER_SLOTS, tm), F32),
        ],
        compiler_params=pltpu.CompilerParams(
            dimension_semantics=("arbitrary", "arbitrary"), vmem_limit_bytes=VMEM_LIMIT),
        name="mid",
    )(x, cn, an, mod3, wo1, wo2, g2, wpq, k1, k2)


def _pack_table(t):
    tb = t.astype(BF16)
    lo = lax.bitcast_convert_type(tb[:, :HALF], jnp.uint16).astype(jnp.uint32)
    hi = lax.bitcast_convert_type(tb[:, HALF:], jnp.uint16).astype(jnp.uint32)
    return (lo | (hi << 16)).reshape(t.shape[0] * ROWS_PER_EXPERT, LANES)


def _unpack(w):
    lo = pltpu.bitcast(lax.shift_left(w, jnp.uint32(16)), F32)
    hi = pltpu.bitcast(w & jnp.uint32(0xFFFF0000), F32)
    return lo, hi


def _gather_row(tab_ref, row):
    return tab_ref[pl.ds(pl.multiple_of(row, ROWS_PER_EXPERT), ROWS_PER_EXPERT), :]


STAGE_ROWS = PEER_SLOTS * ROWS_PER_EXPERT


def _token_pieces(xg, r, first, count):
    return jnp.concatenate(
        [xg[r:r + 1, (first + q) * LANES:(first + q + 1) * LANES] for q in range(count)], axis=0)


def _peer_u_kernel(idx_ref, tab_ref, x_ref, gate_ref, w_ref, *stages):
    tb = w_ref.shape[0]
    sub = lax.broadcasted_iota(I32, (SUBLANES, PEER_SLOTS), 0)

    def group(g, c):
        for k in range(PEER_GROUPS):
            eight(pl.multiple_of((g * PEER_GROUPS + k) * SUBLANES, SUBLANES), stages[k * SUBLANES:(k + 1) * SUBLANES])
        return c

    def eight(t0, stages):
        act8 = jnp.zeros((SUBLANES, PEER_SLOTS), F32)
        xg = x_ref[pl.ds(t0, SUBLANES), :]
        xlo, xhi, rows_t = [], [], []
        for r in range(SUBLANES):
            xlo.append(jnp.concatenate([_token_pieces(xg, r, 0, ROWS_PER_EXPERT)] * 2, axis=0))
            xhi.append(jnp.concatenate([_token_pieces(xg, r, ROWS_PER_EXPERT, ROWS_PER_EXPERT)] * 2, axis=0))
            rows_t.append(idx_ref.at[t0 + r])
        for j in range(0, PEER_SLOTS, 2):
            for r in range(SUBLANES):
                pair = jnp.concatenate(
                    [_gather_row(tab_ref, rows_t[r][j]), _gather_row(tab_ref, rows_t[r][j + 1])], axis=0)
                lo, hi = _unpack(pair)
                row = j * ROWS_PER_EXPERT
                stages[r][row:row + 2 * ROWS_PER_EXPERT, :] = lo * xlo[r] + hi * xhi[r]
        for r in range(SUBLANES):
            stage = stages[r]
            cs = stage[pl.ds(0, PEER_SLOTS, stride=ROWS_PER_EXPERT), :]
            for q in range(1, ROWS_PER_EXPERT):
                cs = cs + stage[pl.ds(q, PEER_SLOTS, stride=ROWS_PER_EXPERT), :]
            act = jnp.sum(cs.T, axis=0, keepdims=True)
            act8 = jnp.where(sub == r, act, act8)
        rows = pl.ds(t0, SUBLANES)
        w_ref[rows, :] = gate_ref[rows, :] * jax.nn.gelu(act8)

    lax.fori_loop(0, tb // (PEER_GROUPS * SUBLANES), group, 0)


def _peer_u(idx, tab, h2, gate):
    n, d = h2.shape
    tb = TB_PEER
    return pl.pallas_call(
        _peer_u_kernel,
        grid=(n // tb,),
        in_specs=[
            pl.BlockSpec((tb, PEER_SLOTS), lambda i: (i, 0), memory_space=pltpu.SMEM),
            pl.BlockSpec(tab.shape, lambda i: (0, 0), pipeline_mode=pl.Buffered(1)),
            pl.BlockSpec((tb, d), lambda i: (i, 0)),
            pl.BlockSpec((tb, PEER_SLOTS), lambda i: (i, 0)),
        ],
        out_specs=pl.BlockSpec((tb, PEER_SLOTS), lambda i: (i, 0)),
        out_shape=jax.ShapeDtypeStruct((n, PEER_SLOTS), F32),
        scratch_shapes=[pltpu.VMEM((STAGE_ROWS, LANES), F32) for _ in range(PEER_GROUPS * SUBLANES)],
        compiler_params=pltpu.CompilerParams(
            dimension_semantics=("arbitrary",), vmem_limit_bytes=VMEM_LIMIT),
        name="peer_u",
    )(idx, tab, h2, gate)


PV_PAIRS = SUBLANES // 2


def _peer_v_kernel(idx_ref, w_ref, tab_ref, x1_ref, gt2_ref, e8_ref, mask_ref, o_ref, *stages):
    tb = idx_ref.shape[0]
    gt2 = gt2_ref[0]
    mask = mask_ref[...]
    e8 = e8_ref[...]

    def group(g, c):
        for k in range(PEER_GROUPS_V):
            eight(pl.multiple_of((g * PEER_GROUPS_V + k) * SUBLANES, SUBLANES), stages[k * PV_PAIRS:(k + 1) * PV_PAIRS])
        return c

    def eight(g0, stages):
        peers = []
        for p in range(PV_PAIRS):
            stage = stages[p]
            t0 = g0 + 2 * p
            for u in range(2):
                rows_t = idx_ref.at[t0 + u]
                for j in range(PEER_SLOTS):
                    stage[j * ROWS_PER_EXPERT:(j + 1) * ROWS_PER_EXPERT, u * LANES:(u + 1) * LANES] = (
                        _gather_row(tab_ref, rows_t[j]))
            w2 = w_ref[pl.ds(t0, 2), :]
            hi = w2.astype(BF16).astype(F32)
            rep = jnp.dot(jnp.concatenate([hi, w2 - hi], axis=0).astype(BF16), e8,
                          preferred_element_type=F32)
            lhs = jnp.concatenate([rep[i:i + 1, :] * mask for i in range(4)], axis=0).astype(BF16)
            out = jnp.dot(lhs, pltpu.bitcast(stage[...], BF16), preferred_element_type=F32)
            for u in range(2):
                r0 = u * SUBLANES
                peers.append(out[r0:r0 + SUBLANES, u * LANES:(u + 1) * LANES]
                             + out[2 * SUBLANES + r0:3 * SUBLANES + r0, u * LANES:(u + 1) * LANES])
        rows = pl.ds(g0, SUBLANES)
        for q in range(SUBLANES):
            cols = slice(q * LANES, (q + 1) * LANES)
            piece = jnp.concatenate([peer[q:q + 1, :] for peer in peers], axis=0)
            o_ref[rows, cols] = x1_ref[rows, cols] + gt2[:, cols] * piece

    lax.fori_loop(0, tb // (PEER_GROUPS_V * SUBLANES), group, 0)


def _peer_v(idx, w, tab, x1, gt2, blocks_per_batch):
    n, d = x1.shape
    tb = TB_PEER
    q = np.arange(SUBLANES)
    piece = 2 * (q % ROWS_PER_EXPERT) + q // ROWS_PER_EXPERT
    lane = np.arange(PEER_SLOTS * SUBLANES)
    mask = jnp.asarray((lane[None, :] % SUBLANES == piece[:, None]).astype(np.float32))
    e8 = jnp.asarray(np.arange(PEER_SLOTS)[:, None] == lane[None, :] // SUBLANES, BF16)
    return pl.pallas_call(
        _peer_v_kernel,
        grid=(n // tb,),
        in_specs=[
            pl.BlockSpec((tb, PEER_SLOTS), lambda i: (i, 0), memory_space=pltpu.SMEM),
            pl.BlockSpec((tb, PEER_SLOTS), lambda i: (i, 0)),
            pl.BlockSpec(tab.shape, lambda i: (0, 0), pipeline_mode=pl.Buffered(1)),
            pl.BlockSpec((tb, d), lambda i: (i, 0)),
            pl.BlockSpec((1, 1, d), lambda i: (i // blocks_per_batch, 0, 0)),
            pl.BlockSpec(e8.shape, lambda i: (0, 0)),
            pl.BlockSpec(mask.shape, lambda i: (0, 0)),
        ],
        out_specs=pl.BlockSpec((tb, d), lambda i: (i, 0)),
        out_shape=jax.ShapeDtypeStruct((n, d), F32),
        scratch_shapes=[pltpu.VMEM((STAGE_ROWS, 2 * LANES), jnp.uint32) for _ in range(PEER_GROUPS_V * PV_PAIRS)],
        compiler_params=pltpu.CompilerParams(
            dimension_semantics=("arbitrary",), vmem_limit_bytes=VMEM_LIMIT),
        name="peer_v",
    )(idx, w, tab, x1, gt2, e8, mask)


def _layer(x, mod, g_norm1, g_norm2, w_in, q_norm_g, k_norm_g, conv_w, conv_b, conv_ln_g, conv_ln_b,
           rel_bias, g_out_conv, g_out_attn, w_out, w_peer_q, peer_k1, peer_k2, peer_u, peer_v):
    bsz, s, d = x.shape
    n = bsz * s
    mod3 = mod.reshape(bsz, 6, d)
    row = lambda a: a.reshape(1, -1)

    c0 = 2 * CONV_CH
    c1 = c0 + 3 * ATT_W
    c2 = c1 + IDX_HEADS * IDX_DIM
    c3 = c2 + IDX_DIM
    wa = w_in[:, :c0].astype(BF16)
    wqkv = w_in[:, c0:c1].astype(BF16)
    widx = jnp.concatenate([w_in[:, c1:c2], w_in[:, c2:c3], w_in[:, c2:c3]], axis=1).astype(BF16)
    wwi = jnp.zeros((SUBLANES, d), F32).at[:IDX_HEADS].set(w_in[:, c3:c3 + IDX_HEADS].T).astype(BF16)
    head = np.arange(ATT_W) // HEAD_DIM
    e2 = jnp.asarray(head[:, None] == head[None, :], BF16)

    conv_n, q, k, vt, qi, ki, wit = _inproj(
        x, mod3, row(g_norm1), wa, wqkv, widx, wwi,
        row(jnp.tile(q_norm_g, ATT_HEADS)), row(jnp.tile(k_norm_g, ATT_HEADS)), e2,
        conv_w.reshape(CONV_WIDTH, CONV_CH), row(conv_b), row(conv_ln_g), row(conv_ln_b), row(g_out_conv))
    attn_n = _attn(q, k, vt, qi, ki, wit, rel_bias, row(g_out_attn))

    x1, h2, idx, gate = _mid(
        x, conv_n, attn_n, mod3, w_out[:CONV_CH].astype(BF16), w_out[CONV_CH:].astype(BF16), row(g_norm2),
        w_peer_q.astype(BF16), peer_k1.astype(BF16), peer_k2.astype(BF16))

    idx = idx.reshape(n, PEER_SLOTS)
    w = _peer_u(idx, _pack_table(peer_u), h2.reshape(n, d), gate.reshape(n, PEER_SLOTS))
    out = _peer_v(idx, w, _pack_table(peer_v), x1.reshape(n, d), mod3[:, 5:6, :], s // TB_PEER)
    return out.reshape(bsz, s, d)


def kernel(x, c, w_ada, b_ada, g_norm1, g_norm2, w_in, q_norm_g, k_norm_g, conv_w, conv_b, conv_ln_g,
           conv_ln_b, rel_bias, g_out_conv, g_out_attn, w_out, w_peer_q, peer_k1, peer_k2, peer_u, peer_v):
    depth = w_ada.shape[0]
    for l in range(depth):
        mod = _ada(c, w_ada[l], b_ada[l])
        x = _layer(x, mod, g_norm1[l], g_norm2[l], w_in[l], q_norm_g[l], k_norm_g[l], conv_w[l], conv_b[l],
                   conv_ln_g[l], conv_ln_b[l], rel_bias, g_out_conv[l], g_out_attn[l], w_out[l],
                   w_peer_q[l], peer_k1[l], peer_k2[l], peer_u[l], peer_v[l])
    return x
```

```python
import functools
import math

import numpy as np
import jax
import jax.numpy as jnp
from jax import lax
from jax.experimental import pallas as pl
from jax.experimental.pallas import tpu as pltpu

F32 = jnp.float32
BF16 = jnp.bfloat16
I32 = jnp.int32
I16 = jnp.int16
HIGHEST = lax.Precision.HIGHEST

D_MODEL = 1024
CHUNK = 64
CONV_CH = 512
CONV_WIDTH = 31
ATT_HEADS = 8
HEAD_DIM = 64
ATT_W = ATT_HEADS * HEAD_DIM
IDX_HEADS = 4
IDX_DIM = 64
IDX_SCALE = (IDX_HEADS * IDX_DIM) ** -0.5
TOPK_MAX = 256
REL_BUCKETS = 32
REL_MAX_DIST = 128
PEER_HEADS = 8
N_KEYS = 128
N_EXPERTS = N_KEYS * N_KEYS
PEER_TOPK = 16
PEER_SLOTS = PEER_HEADS * PEER_TOPK
EPS = 1e-6
LOG2E = math.log2(math.e)

LANES = 128
SUBLANES = 8
PACK = 16
VMEM_LIMIT = 56 * 1024 * 1024

TM_IN = 512
CONV_ROWS = 64
HALO = 32
TQ = 256
KT = 256
TM_MID = 1024
TB_PEER = 256
PEER_GROUPS = 2
PEER_GROUPS_V = 4
HALF = D_MODEL // 2
ROWS_PER_EXPERT = HALF // LANES

NT_DIMS = (((1,), (1,)), ((), ()))

_NEG_INF_BITS = int(np.array(-np.inf, np.float32).view(np.int32))
KEY_NEG_INF = _NEG_INF_BITS ^ 0x7FFFFFFF
I16_MIN = -(2 ** 15)
I16_MAX = 2 ** 15 - 1


def _nt(a, b, precision=None):
    return lax.dot_general(a, b, NT_DIMS, precision=precision, preferred_element_type=F32)


def _ada_kernel(c_ref, w_ref, b_ref, o_ref):
    a = jax.nn.silu(c_ref[...])
    o_ref[...] = jnp.dot(a, w_ref[...], precision=HIGHEST, preferred_element_type=F32) + b_ref[...]


def _ada(c, w_ada, b_ada):
    bsz, d = c.shape
    return pl.pallas_call(
        _ada_kernel,
        grid=(6,),
        in_specs=[
            pl.BlockSpec((bsz, d), lambda j: (0, 0)),
            pl.BlockSpec((d, d), lambda j: (0, j)),
            pl.BlockSpec((1, d), lambda j: (0, j)),
        ],
        out_specs=pl.BlockSpec((bsz, d), lambda j: (0, j)),
        out_shape=jax.ShapeDtypeStruct((bsz, 6 * d), F32),
        name="ada",
    )(c, w_ada, b_ada.reshape(1, 6 * d))


def _inproj_kernel(x_ref, mod_ref, g1_ref, wa_ref, wqkv_ref, widx_ref, wwi_ref, gq_ref, gk_ref, e2_ref,
                   cw_ref, cb_ref, lng_ref, lnb_ref, goc_ref,
                   conv_ref, q_ref, k_ref, vt_ref, qi_ref, ki_ref, wit_ref, ubuf):
    j = pl.program_id(1)
    tm = x_ref.shape[1]
    x = x_ref[0]
    sh1 = mod_ref[0, 0:1, :]
    sc1 = mod_ref[0, 1:2, :]
    r = lax.rsqrt(jnp.mean(x * x, axis=-1, keepdims=True) + EPS)
    h = (x * r) * g1_ref[...] * (1.0 + sc1) + sh1
    hb = h.astype(BF16)

    pa = jnp.dot(hb, wa_ref[...], preferred_element_type=F32)
    u = pa[:, :CONV_CH] * jax.nn.sigmoid(pa[:, CONV_CH:])

    @pl.when(j == 0)
    def _():
        ubuf[0:HALO, :] = jnp.zeros((HALO, CONV_CH), F32)

    ubuf[HALO:HALO + tm, :] = u
    first = HALO - (CONV_WIDTH - 1)
    for rb in range(tm // CONV_ROWS):
        base = rb * CONV_ROWS
        acc = jnp.zeros((CONV_ROWS, CONV_CH), F32) + cb_ref[...]
        for phase in range(SUBLANES):
            taps = [t for t in range(CONV_WIDTH) if (first + t) % SUBLANES == phase]
            span = (first + taps[-1]) - phase + CONV_ROWS
            slab = ubuf[base + phase:base + phase + span, :]
            for t in taps:
                off = first + t - phase
                acc = acc + cw_ref[t:t + 1, :] * slab[off:off + CONV_ROWS, :]
        mu = jnp.mean(acc, axis=-1, keepdims=True)
        xc = acc - mu
        y = xc * lax.rsqrt(jnp.mean(xc * xc, axis=-1, keepdims=True) + EPS)
        y = jax.nn.silu(y * lng_ref[...] + lnb_ref[...])
        y = y * lax.rsqrt(jnp.mean(y * y, axis=-1, keepdims=True) + EPS) * goc_ref[...]
        conv_ref[0, base:base + CONV_ROWS, :] = y.astype(BF16)
    ubuf[0:HALO, :] = ubuf[tm:tm + HALO, :]

    pq = jnp.dot(hb, wqkv_ref[...], preferred_element_type=F32)
    q = pq[:, :ATT_W]
    k = pq[:, ATT_W:2 * ATT_W]
    v = pq[:, 2 * ATT_W:]
    e2 = e2_ref[...]

    def head_mean(sq):
        hi = sq.astype(BF16)
        lo = (sq - hi.astype(F32)).astype(BF16)
        return (jnp.dot(hi, e2, preferred_element_type=F32)
                + jnp.dot(lo, e2, preferred_element_type=F32)) * (1.0 / HEAD_DIM)

    qs = head_mean(q * q)
    ks = head_mean(k * k)
    qn = q * lax.rsqrt(qs + EPS) * gq_ref[...] * (HEAD_DIM ** -0.5 * LOG2E)
    kn = k * lax.rsqrt(ks + EPS) * gk_ref[...]
    lane = lax.broadcasted_iota(I32, (tm, LANES), 1)
    low = lane < HEAD_DIM
    for p in range(ATT_HEADS // 2):
        slab = qn[:, p * LANES:(p + 1) * LANES]
        q_ref[0, 2 * p] = jnp.where(low, slab, 0.0).T.astype(BF16)
        q_ref[0, 2 * p + 1] = jnp.where(low, 0.0, slab).T.astype(BF16)
        k_ref[0, p] = kn[:, p * LANES:(p + 1) * LANES].astype(BF16)
    vt_ref[0] = v.T.astype(BF16)

    pc = jnp.dot(hb, widx_ref[...], preferred_element_type=F32)
    for p in range(IDX_HEADS // 2):
        slab = pc[:, p * LANES:(p + 1) * LANES]
        qi_ref[0, 2 * p] = jnp.where(low, slab, 0.0).T.astype(BF16)
        qi_ref[0, 2 * p + 1] = jnp.where(low, 0.0, slab).T.astype(BF16)
    ki_ref[0] = pc[:, 2 * LANES:3 * LANES].astype(BF16)
    wit_ref[0] = _nt(wwi_ref[...], hb) * IDX_SCALE


def _inproj(x, mod3, g1, wa, wqkv, widx, wwi, gq, gk, e2, cw, cb, lng, lnb, goc):
    bsz, s, d = x.shape
    tm = TM_IN
    nt = s // tm
    full = lambda shape: pl.BlockSpec(shape, lambda b, j: (0,) * len(shape))
    out_shape = (
        jax.ShapeDtypeStruct((bsz, s, CONV_CH), BF16),
        jax.ShapeDtypeStruct((bsz, ATT_HEADS, LANES, s), BF16),
        jax.ShapeDtypeStruct((bsz, ATT_HEADS // 2, s, LANES), BF16),
        jax.ShapeDtypeStruct((bsz, ATT_W, s), BF16),
        jax.ShapeDtypeStruct((bsz, IDX_HEADS, LANES, s), BF16),
        jax.ShapeDtypeStruct((bsz, s, LANES), BF16),
        jax.ShapeDtypeStruct((bsz, SUBLANES, s), F32),
    )
    out_specs = (
        pl.BlockSpec((1, tm, CONV_CH), lambda b, j: (b, j, 0)),
        pl.BlockSpec((1, ATT_HEADS, LANES, tm), lambda b, j: (b, 0, 0, j)),
        pl.BlockSpec((1, ATT_HEADS // 2, tm, LANES), lambda b, j: (b, 0, j, 0)),
        pl.BlockSpec((1, ATT_W, tm), lambda b, j: (b, 0, j)),
        pl.BlockSpec((1, IDX_HEADS, LANES, tm), lambda b, j: (b, 0, 0, j)),
        pl.BlockSpec((1, tm, LANES), lambda b, j: (b, j, 0)),
        pl.BlockSpec((1, SUBLANES, tm), lambda b, j: (b, 0, j)),
    )
    return pl.pallas_call(
        _inproj_kernel,
        grid=(bsz, nt),
        in_specs=[
            pl.BlockSpec((1, tm, d), lambda b, j: (b, j, 0)),
            pl.BlockSpec((1, 6, d), lambda b, j: (b, 0, 0)),
            full(g1.shape), full(wa.shape), full(wqkv.shape), full(widx.shape), full(wwi.shape),
            full(gq.shape), full(gk.shape), full(e2.shape),
            full(cw.shape), full(cb.shape), full(lng.shape), full(lnb.shape), full(goc.shape),
        ],
        out_specs=out_specs,
        out_shape=out_shape,
        scratch_shapes=[pltpu.VMEM((tm + HALO, CONV_CH), F32)],
        compiler_params=pltpu.CompilerParams(
            dimension_semantics=("arbitrary", "arbitrary"), vmem_limit_bytes=VMEM_LIMIT),
        name="inproj",
    )(x, mod3, g1, wa, wqkv, widx, wwi, gq, gk, e2, cw, cb, lng, lnb, goc)


def _t5_bucket_np(rel):
    half = REL_BUCKETS // 2
    max_exact = half // 2
    ret = np.where(rel > 0, half, 0)
    n = np.abs(rel)
    nf = np.maximum(n, 1).astype(np.float64)
    large = max_exact + (np.log(nf / max_exact) / math.log(REL_MAX_DIST / max_exact)
                         * (half - max_exact)).astype(np.int32)
    large = np.minimum(large, half - 1)
    return (ret + np.where(n < max_exact, n, large)).astype(np.int32)


NEAR = REL_MAX_DIST + TQ
FAR_BUCKET = REL_BUCKETS // 2 - 1


def _near_buckets():
    r = np.arange(NEAR)[:, None]
    t = np.arange(TQ)[None, :]
    return _t5_bucket_np(r - REL_MAX_DIST - t)


def _attn_kernel(q_ref, k_ref, vt_ref, qi_ref, ki_ref, wit_ref, bkt_ref, rb_ref, goa_ref, o_ref,
                 keys_s, hi_s, lo_s, maskb_s, relb_s, ot_s, m_s, mo_s, mn_s, l_s, lg_s, *, nsel):
    b = pl.program_id(0)
    i = pl.program_id(1)
    t0 = i * TQ
    n_tiles = i + 1
    neg_inf = F32(-jnp.inf)

    @pl.when((b == 0) & (i == 0))
    def _():
        bk = bkt_ref[...]
        for h in range(ATT_HEADS):
            far = rb_ref[FAR_BUCKET, h]
            acc = jnp.zeros((NEAR, TQ), F32)
            for bb in range(REL_BUCKETS):
                acc = jnp.where(bk == bb, (rb_ref[bb, h] - far) * LOG2E, acc)
            relb_s[h, 0:2 * KT - NEAR, :] = jnp.zeros((2 * KT - NEAR, TQ), F32)
            relb_s[h, 2 * KT - NEAR:2 * KT, :] = acc

    qpos = t0 + lax.broadcasted_iota(I32, (1, TQ), 1)
    limit = (qpos // CHUNK + 1) * CHUNK
    row_iota = lax.broadcasted_iota(I32, (KT, TQ), 0)

    def tile_start(jt):
        return pl.multiple_of(jt * KT, KT)

    wi = wit_ref[0]

    def score_tile(jt):
        ks = tile_start(jt)
        kit = ki_ref[0, pl.ds(ks, KT), :]
        acc = jnp.zeros((KT, TQ), F32)
        for h in range(IDX_HEADS):
            lgt = jnp.dot(kit, qi_ref[0, h], preferred_element_type=F32)
            acc = acc + jnp.maximum(lgt, 0.0) * wi[h:h + 1, :]
        sc = jnp.where(row_iota + ks < limit, acc, neg_inf)
        bits = pltpu.bitcast(sc, I32)
        key = bits ^ (lax.shift_right_arithmetic(bits, 31) & 0x7FFFFFFF)
        keys_s[pl.ds(ks, KT), :] = key
        hi_s[pl.ds(ks, KT), :] = lax.shift_right_arithmetic(key, 16).astype(I16)
        lo_s[pl.ds(ks, KT), :] = ((key & 0xFFFF) ^ 0x8000).astype(I16)

    def score_pair(jp, c):
        score_tile(2 * jp)
        score_tile(2 * jp + 1)
        return c

    lax.fori_loop(0, n_tiles // 2, score_pair, 0)

    @pl.when(n_tiles % 2 == 1)
    def _():
        score_tile(n_tiles - 1)

    def count_ge(half_ref, cand):
        c16 = cand.astype(I16)
        one = jnp.ones((KT, TQ), I16)
        zero = jnp.zeros((KT, TQ), I16)

        def tile_count(tiles):
            parts = []
            for jt in tiles:
                hit = jnp.where(half_ref[pl.ds(tile_start(jt), KT), :] >= c16, one, zero)
                parts += [hit[r:r + PACK, :] for r in range(0, KT, PACK)]
            while len(parts) > 1:
                parts = [parts[k] + parts[k + 1] for k in range(0, len(parts), 2)]
            return parts[0].astype(I32)

        acc = lax.fori_loop(0, n_tiles // 2, lambda jp, acc: acc + tile_count([2 * jp, 2 * jp + 1]),
                            jnp.zeros((PACK, TQ), I32))
        acc = lax.cond(n_tiles % 2 == 1, lambda a: a + tile_count([n_tiles - 1]), lambda a: a, acc)
        return jnp.sum(acc.astype(F32), axis=0, keepdims=True)

    def low_key(low):
        return (low ^ 0x8000) - jnp.where(low < 0x8000, 0x10000, 0)

    def select_thr():
        c0 = count_ge(hi_s, jnp.zeros((1, TQ), I32))
        ok = c0 >= nsel
        t = jnp.where(ok, 0, I16_MIN).astype(I32)
        cnt = jnp.where(ok, c0, F32(2 ** 30))

        def top_step(it, carry):
            t, cnt = carry
            cand = t + lax.shift_left(I32(1), I32(14) - it)
            c = count_ge(hi_s, cand)
            ok = c >= nsel
            return jnp.where(ok, cand, t), jnp.where(ok, c, cnt)

        top, cnt = lax.fori_loop(0, 15, top_step, (t, cnt))

        top_max = top >= I16_MAX
        above = jnp.where(top_max, 0.0, count_ge(hi_s, jnp.where(top_max, top, top + 1)))
        top16 = top.astype(I16)
        floor16 = jnp.full((KT, TQ), I16_MIN, I16)

        def keep_low(jt, c):
            rows = pl.ds(tile_start(jt), KT)
            lo_s[rows, :] = jnp.where(hi_s[rows, :] == top16, lo_s[rows, :], floor16)
            return c

        lax.fori_loop(0, n_tiles, keep_low, 0)

        at_zero = jnp.where(top == 0, above + count_ge(lo_s, low_key(jnp.ones((1, TQ), I32))), F32(2 ** 30))
        settled = at_zero < nsel

        def low_cond(carry):
            bit, _, cnt = carry
            return (bit >= 0) & (jnp.max(jnp.where(settled, F32(nsel), cnt)) > nsel)

        def low_bit(bit, low, cnt):
            cand = low + lax.shift_left(I32(1), bit)
            c = above + count_ge(lo_s, low_key(cand))
            ok = c >= nsel
            return jnp.where(ok, cand, low), jnp.where(ok, c, cnt)

        def low_step(carry):
            bit, low, cnt = carry
            low, cnt = low_bit(bit, low, cnt)
            low, cnt = low_bit(bit - 1, low, cnt)
            return bit - 2, low, cnt

        _, low, cnt = lax.while_loop(low_cond, low_step, (I32(15), jnp.zeros((1, TQ), I32), cnt))
        low_max = low >= 0xFFFF
        c_gt = above + jnp.where(low_max, 0.0, count_ge(lo_s, low_key(jnp.where(low_max, low, low + 1))))
        return lax.shift_left(top, 16) + low, cnt, c_gt

    thr, c_ge, c_gt = lax.cond(
        i > 0, select_thr,
        lambda: (jnp.full((1, TQ), KEY_NEG_INF + 1, I32), jnp.full((1, TQ), nsel, F32), jnp.zeros((1, TQ), F32)))

    def mask_plain():
        def p3(jt, c):
            ks = tile_start(jt)
            maskb_s[pl.ds(ks, KT), :] = jnp.where(keys_s[pl.ds(ks, KT), :] >= thr, 0.0, neg_inf)
            return c
        lax.fori_loop(0, n_tiles, p3, 0)
        return I32(0)

    def mask_ties():
        quota = nsel - c_gt
        tri = jnp.where(lax.broadcasted_iota(I32, (KT, KT), 0) >= lax.broadcasted_iota(I32, (KT, KT), 1),
                        1.0, 0.0).astype(BF16)

        def local_ranks(jt):
            ks = tile_start(jt)
            kt = keys_s[pl.ds(ks, KT), :]
            eq = kt == thr
            return ks, kt, eq, jnp.dot(tri, jnp.where(eq, 1.0, 0.0).astype(BF16), preferred_element_type=F32)

        def write_mask(tile, before):
            ks, kt, eq, local = tile
            rank = before + local
            tied = jnp.where(eq, jnp.where(rank <= quota, 0.0, neg_inf), neg_inf)
            maskb_s[pl.ds(ks, KT), :] = jnp.where(kt > thr, 0.0, tied)
            return rank[KT - 1:KT, :]

        def pair(jp, before):
            first, second = local_ranks(2 * jp), local_ranks(2 * jp + 1)
            return write_mask(second, write_mask(first, before))

        before = lax.fori_loop(0, n_tiles // 2, pair, jnp.zeros((1, TQ), F32))

        @pl.when(n_tiles % 2 == 1)
        def _():
            write_mask(local_ranks(n_tiles - 1), before)

        return I32(0)

    lax.cond(jnp.max(c_ge) > nsel, mask_ties, mask_plain)

    m_s[...] = jnp.full((ATT_HEADS, TQ), neg_inf, F32)
    l_s[...] = jnp.zeros((ATT_HEADS, TQ), F32)
    ot_s[...] = jnp.zeros((ATT_W, TQ), F32)

    def att_tiles(jts, near):
        starts = [tile_start(jt) for jt in jts]
        for n, (jt, ks) in enumerate(zip(jts, starts)):
            mb = maskb_s[pl.ds(ks, KT), :]
            for h in range(ATT_HEADS):
                l = jnp.dot(k_ref[0, h // 2, pl.ds(ks, KT), :], q_ref[0, h], preferred_element_type=F32) + mb
                if near:
                    off = pl.multiple_of((jt - (n_tiles - 2)) * KT, KT)
                    l = l + relb_s[h, pl.ds(off, KT), :]
                slot = n * ATT_HEADS + h
                lg_s[slot] = l
                m_old = m_s[h:h + 1, :]
                m_new = jnp.maximum(m_old, jnp.max(l, axis=0, keepdims=True))
                mo_s[slot:slot + 1, :] = m_old
                mn_s[slot:slot + 1, :] = m_new
                m_s[h:h + 1, :] = m_new
        for n, ks in enumerate(starts):
            for h in range(ATT_HEADS):
                slot = n * ATT_HEADS + h
                m_new = mn_s[slot:slot + 1, :]
                m_ref = jnp.where(m_new == neg_inf, 0.0, m_new)
                alpha = jnp.exp2(mo_s[slot:slot + 1, :] - m_ref)
                p = jnp.exp2(lg_s[slot] - m_ref)
                l_s[h:h + 1, :] = alpha * l_s[h:h + 1, :] + jnp.sum(p, axis=0, keepdims=True)
                vt = vt_ref[0, h * HEAD_DIM:(h + 1) * HEAD_DIM, pl.ds(ks, KT)]
                rows = slice(h * HEAD_DIM, (h + 1) * HEAD_DIM)
                ot_s[rows, :] = ot_s[rows, :] * alpha + jnp.dot(vt, p.astype(BF16), preferred_element_type=F32)

    n_far = jnp.maximum(n_tiles - 2, 0)

    def far_pair(jp, c):
        att_tiles([2 * jp, 2 * jp + 1], False)
        return c

    lax.fori_loop(0, n_far // 2, far_pair, 0)

    @pl.when(n_far % 2 == 1)
    def _():
        att_tiles([n_far - 1], False)

    @pl.when(i > 0)
    def _():
        att_tiles([n_tiles - 2, n_tiles - 1], True)

    @pl.when(i == 0)
    def _():
        att_tiles([n_tiles - 1], True)

    for h in range(ATT_HEADS):
        rows = slice(h * HEAD_DIM, (h + 1) * HEAD_DIM)
        ot_s[rows, :] = ot_s[rows, :] / l_s[h:h + 1, :]

    ot = ot_s[...]
    ms = jnp.mean(ot * ot, axis=0, keepdims=True)
    y = (ot * lax.rsqrt(ms + EPS)).T * goa_ref[...]
    o_ref[0] = y.astype(BF16)


def _attn(q, k, vt, qi, ki, wit, rel_bias, goa):
    bsz, _, _, s = q.shape
    nsel = min(TOPK_MAX, s // 4)
    assert nsel == TQ and s % TQ == 0, "attention kernel assumes TOPK_MAX-sized query blocks"
    bkt = jnp.asarray(_near_buckets())
    kern = functools.partial(_attn_kernel, nsel=nsel)
    return pl.pallas_call(
        kern,
        grid=(bsz, s // TQ),
        in_specs=[
            pl.BlockSpec((1, ATT_HEADS, LANES, TQ), lambda b, i: (b, 0, 0, i)),
            pl.BlockSpec((1, ATT_HEADS // 2, s, LANES), lambda b, i: (b, 0, 0, 0)),
            pl.BlockSpec((1, ATT_W, s), lambda b, i: (b, 0, 0)),
            pl.BlockSpec((1, IDX_HEADS, LANES, TQ), lambda b, i: (b, 0, 0, i)),
            pl.BlockSpec((1, s, LANES), lambda b, i: (b, 0, 0)),
            pl.BlockSpec((1, SUBLANES, TQ), lambda b, i: (b, 0, i)),
            pl.BlockSpec((NEAR, TQ), lambda b, i: (0, 0)),
            pl.BlockSpec(memory_space=pltpu.SMEM),
            pl.BlockSpec((1, ATT_W), lambda b, i: (0, 0)),
        ],
        out_specs=pl.BlockSpec((1, TQ, ATT_W), lambda b, i: (b, i, 0)),
        out_shape=jax.ShapeDtypeStruct((bsz, s, ATT_W), BF16),
        scratch_shapes=[
            pltpu.VMEM((s, TQ), I32),
            pltpu.VMEM((s, TQ), I16),
            pltpu.VMEM((s, TQ), I16),
            pltpu.VMEM((s, TQ), F32),
            pltpu.VMEM((ATT_HEADS, 2 * KT, TQ), F32),
            pltpu.VMEM((ATT_W, TQ), F32),
            pltpu.VMEM((ATT_HEADS, TQ), F32),
            pltpu.VMEM((2 * ATT_HEADS, TQ), F32),
            pltpu.VMEM((2 * ATT_HEADS, TQ), F32),
            pltpu.VMEM((ATT_HEADS, TQ), F32),
            pltpu.VMEM((2 * ATT_HEADS, KT, TQ), F32),
        ],
        compiler_params=pltpu.CompilerParams(
            dimension_semantics=("arbitrary", "arbitrary"), vmem_limit_bytes=VMEM_LIMIT),
        name="attn",
    )(q, k, vt, qi, ki, wit, bkt, rel_bias, goa)


def _topk_rows(s, payload, k, order=None):
    rows = lax.broadcasted_iota(I32, s.shape, 0).astype(F32) if order is None else order
    vals, pays = [], []
    for _ in range(k):
        m = jnp.max(s, axis=0, keepdims=True)
        ix = jnp.min(jnp.where(s == m, rows, _NO_ROW), axis=0, keepdims=True)
        hit = rows == ix
        if payload is None:
            pays.append(ix)
        else:
            pays.append(jnp.max(jnp.where(hit, payload, -1.0), axis=0, keepdims=True))
        vals.append(m)
        s = jnp.where(hit, -jnp.inf, s)
    return jnp.concatenate(vals, axis=0), jnp.concatenate(pays, axis=0)


_NO_ROW = 1e9

_PAIR_GROUPS = (
    ((0, 0, 8, 0),), ((0, 8, 8, 0),), ((1, 0, 8, 0),),
    ((2, 0, 5, 0), (5, 0, 2, 5)), ((3, 0, 4, 0), (4, 0, 3, 4)), ((6, 0, 2, 0), (7, 0, 2, 2)),
)
assert sorted((a, b) for g in _PAIR_GROUPS for a, b0, nb, _ in g for b in range(b0, b0 + nb)) == sorted(
    (a, b) for a in range(PEER_TOPK // 2) for b in range(PEER_TOPK // (a + 1)))


def _mid_kernel(x_ref, cn_ref, an_ref, mod_ref, wo1_ref, wo2_ref, g2_ref, wpq_ref, k1_ref, k2_ref,
                x1_ref, h2_ref, idx_ref, gate_ref, qq_s, idt_s, gt_s):
    tm = x_ref.shape[1]
    x = x_ref[0]
    gt1 = mod_ref[0, 2:3, :]
    sh2 = mod_ref[0, 3:4, :]
    sc2 = mod_ref[0, 4:5, :]
    proj = (jnp.dot(cn_ref[0], wo1_ref[...], preferred_element_type=F32)
            + jnp.dot(an_ref[0], wo2_ref[...], preferred_element_type=F32))
    x1 = x + gt1 * proj
    x1_ref[0] = x1
    r = lax.rsqrt(jnp.mean(x1 * x1, axis=-1, keepdims=True) + EPS)
    h2 = (x1 * r) * g2_ref[...] * (1.0 + sc2) + sh2
    h2_ref[0] = h2
    qq_s[...] = jnp.dot(h2.astype(BF16), wpq_ref[...], preferred_element_type=F32).astype(BF16)

    def route_unit(hh, lt):
        rows = pl.ds(lt * LANES, LANES)
        q1 = qq_s[rows, pl.ds(pl.multiple_of(hh * 2 * N_KEYS, LANES), N_KEYS)]
        q2 = qq_s[rows, pl.ds(pl.multiple_of(hh * 2 * N_KEYS + N_KEYS, LANES), N_KEYS)]
        v1, i1 = _topk_rows(_nt(k1_ref[hh], q1), None, PEER_TOPK)
        v2, i2 = _topk_rows(_nt(k2_ref[hh], q2), None, PEER_TOPK)
        sub = lax.broadcasted_iota(I32, (SUBLANES, LANES), 0)
        subf = sub.astype(F32)
        cands, cidxs, flats = [], [], []
        for group in _PAIR_GROUPS:
            val = jnp.full((SUBLANES, LANES), -jnp.inf, F32)
            cid = jnp.zeros((SUBLANES, LANES), F32)
            flat = jnp.full((SUBLANES, LANES), _NO_ROW, F32)
            for a, b0, nb, off in group:
                v2s, i2s = v2[b0:b0 + SUBLANES, :], i2[b0:b0 + SUBLANES, :]
                if off:
                    v2s, i2s = pltpu.roll(v2s, off, 0), pltpu.roll(i2s, off, 0)
                inside = lambda new, old: jnp.where(sub < off + nb, jnp.where(sub >= off, new, old), old)
                val = inside(v1[a:a + 1, :] + v2s, val)
                cid = inside(i1[a:a + 1, :] * float(N_KEYS) + i2s, cid)
                flat = inside(subf + float(a * PEER_TOPK + b0 - off), flat)
            cands.append(val)
            cidxs.append(cid)
            flats.append(flat)
        half = PEER_TOPK // 2
        cands.append(v1[half:, :] + v2[0:1, :])
        cidxs.append(i1[half:, :] * float(N_KEYS) + i2[0:1, :])
        flats.append((subf + float(half)) * float(PEER_TOPK))
        best, experts = _topk_rows(jnp.concatenate(cands, axis=0), jnp.concatenate(cidxs, axis=0), PEER_TOPK,
                                   order=jnp.concatenate(flats, axis=0))
        e = jnp.exp(best - best[0:1, :])
        g = e / jnp.sum(e, axis=0, keepdims=True)
        slots = pl.ds(pl.multiple_of(hh * PEER_TOPK, PEER_TOPK), PEER_TOPK)
        cols = pl.ds(lt * LANES, LANES)
        idt_s[slots, cols] = experts * float(ROWS_PER_EXPERT)
        gt_s[slots, cols] = g

    def route(hh, c):
        for lt in range(tm // LANES):
            route_unit(hh, lt)
        return c

    lax.fori_loop(0, PEER_HEADS, route, 0)
    idx_ref[0] = idt_s[...].T.astype(I32)
    gate_ref[0] = gt_s[...].T


def _mid(x, cn, an, mod3, wo1, wo2, g2, wpq, k1, k2):
    bsz, s, d = x.shape
    tm = TM_MID
    full = lambda shape: pl.BlockSpec(shape, lambda b, j: (0,) * len(shape))
    tok = lambda w: pl.BlockSpec((1, tm, w), lambda b, j: (b, j, 0))
    return pl.pallas_call(
        _mid_kernel,
        grid=(bsz, s // tm),
        in_specs=[tok(d), tok(CONV_CH), tok(ATT_W), pl.BlockSpec((1, 6, d), lambda b, j: (b, 0, 0)),
                  full(wo1.shape), full(wo2.shape), full(g2.shape), full(wpq.shape), full(k1.shape), full(k2.shape)],
        out_specs=(tok(d), tok(d), tok(PEER_SLOTS), tok(PEER_SLOTS)),
        out_shape=(
            jax.ShapeDtypeStruct((bsz, s, d), F32),
            jax.ShapeDtypeStruct((bsz, s, d), F32),
            jax.ShapeDtypeStruct((bsz, s, PEER_SLOTS), I32),
            jax.ShapeDtypeStruct((bsz, s, PEER_SLOTS), F32),
        ),
        scratch_shapes=[
            pltpu.VMEM((tm, PEER_HEADS * 2 * N_KEYS), BF16),
            pltpu.VMEM((PEER_SLOTS, tm), F32),
            pltpu.VMEM((PEER_SLOTS, tm), F32),
        ],
        compiler_params=pltpu.CompilerParams(
            dimension_semantics=("arbitrary", "arbitrary"), vmem_limit_bytes=VMEM_LIMIT),
        name="mid",
    )(x, cn, an, mod3, wo1, wo2, g2, wpq, k1, k2)


def _pack_table(t):
    tb = t.astype(BF16)
    lo = lax.bitcast_convert_type(tb[:, :HALF], jnp.uint16).astype(jnp.uint32)
    hi = lax.bitcast_convert_type(tb[:, HALF:], jnp.uint16).astype(jnp.uint32)
    return (lo | (hi << 16)).reshape(t.shape[0] * ROWS_PER_EXPERT, LANES)


def _unpack(w):
    lo = pltpu.bitcast(lax.shift_left(w, jnp.uint32(16)), F32)
    hi = pltpu.bitcast(w & jnp.uint32(0xFFFF0000), F32)
    return lo, hi


def _gather_row(tab_ref, row):
    return tab_ref[pl.ds(pl.multiple_of(row, ROWS_PER_EXPERT), ROWS_PER_EXPERT), :]


STAGE_ROWS = PEER_SLOTS * ROWS_PER_EXPERT


def _token_pieces(xg, r, first, count):
    return jnp.concatenate(
        [xg[r:r + 1, (first + q) * LANES:(first + q + 1) * LANES] for q in range(count)], axis=0)


def _peer_u_kernel(idx_ref, tab_ref, x_ref, gate_ref, w_ref, *stages):
    tb = w_ref.shape[0]
    sub = lax.broadcasted_iota(I32, (SUBLANES, PEER_SLOTS), 0)

    def group(g, c):
        for k in range(PEER_GROUPS):
            eight(pl.multiple_of((g * PEER_GROUPS + k) * SUBLANES, SUBLANES), stages[k * SUBLANES:(k + 1) * SUBLANES])
        return c

    def eight(t0, stages):
        act8 = jnp.zeros((SUBLANES, PEER_SLOTS), F32)
        xg = x_ref[pl.ds(t0, SUBLANES), :]
        xlo, xhi, rows_t = [], [], []
        for r in range(SUBLANES):
            xlo.append(jnp.concatenate([_token_pieces(xg, r, 0, ROWS_PER_EXPERT)] * 2, axis=0))
            xhi.append(jnp.concatenate([_token_pieces(xg, r, ROWS_PER_EXPERT, ROWS_PER_EXPERT)] * 2, axis=0))
            rows_t.append(idx_ref.at[t0 + r])
        for j in range(0, PEER_SLOTS, 2):
            for r in range(SUBLANES):
                pair = jnp.concatenate(
                    [_gather_row(tab_ref, rows_t[r][j]), _gather_row(tab_ref, rows_t[r][j + 1])], axis=0)
                lo, hi = _unpack(pair)
                row = j * ROWS_PER_EXPERT
                stages[r][row:row + 2 * ROWS_PER_EXPERT, :] = lo * xlo[r] + hi * xhi[r]
        for r in range(SUBLANES):
            stage = stages[r]
            cs = stage[pl.ds(0, PEER_SLOTS, stride=ROWS_PER_EXPERT), :]
            for q in range(1, ROWS_PER_EXPERT):
                cs = cs + stage[pl.ds(q, PEER_SLOTS, stride=ROWS_PER_EXPERT), :]
            act = jnp.sum(cs.T, axis=0, keepdims=True)
            act8 = jnp.where(sub == r, act, act8)
        rows = pl.ds(t0, SUBLANES)
        w_ref[rows, :] = gate_ref[rows, :] * jax.nn.gelu(act8)

    lax.fori_loop(0, tb // (PEER_GROUPS * SUBLANES), group, 0)


def _peer_u(idx, tab, h2, gate):
    n, d = h2.shape
    tb = TB_PEER
    return pl.pallas_call(
        _peer_u_kernel,
        grid=(n // tb,),
        in_specs=[
            pl.BlockSpec((tb, PEER_SLOTS), lambda i: (i, 0), memory_space=pltpu.SMEM),
            pl.BlockSpec(tab.shape, lambda i: (0, 0), pipeline_mode=pl.Buffered(1)),
            pl.BlockSpec((tb, d), lambda i: (i, 0)),
            pl.BlockSpec((tb, PEER_SLOTS), lambda i: (i, 0)),
        ],
        out_specs=pl.BlockSpec((tb, PEER_SLOTS), lambda i: (i, 0)),
        out_shape=jax.ShapeDtypeStruct((n, PEER_SLOTS), F32),
        scratch_shapes=[pltpu.VMEM((STAGE_ROWS, LANES), F32) for _ in range(PEER_GROUPS * SUBLANES)],
        compiler_params=pltpu.CompilerParams(
            dimension_semantics=("arbitrary",), vmem_limit_bytes=VMEM_LIMIT),
        name="peer_u",
    )(idx, tab, h2, gate)


PV_PAIRS = SUBLANES // 2


def _peer_v_kernel(idx_ref, w_ref, tab_ref, x1_ref, gt2_ref, e8_ref, mask_ref, o_ref, *stages):
    tb = idx_ref.shape[0]
    gt2 = gt2_ref[0]
    mask = mask_ref[...]
    e8 = e8_ref[...]

    def group(g, c):
        for k in range(PEER_GROUPS_V):
            eight(pl.multiple_of((g * PEER_GROUPS_V + k) * SUBLANES, SUBLANES), stages[k * PV_PAIRS:(k + 1) * PV_PAIRS])
        return c

    def eight(g0, stages):
        peers = []
        for p in range(PV_PAIRS):
            stage = stages[p]
            t0 = g0 + 2 * p
            for u in range(2):
                rows_t = idx_ref.at[t0 + u]
                for j in range(PEER_SLOTS):
                    stage[j * ROWS_PER_EXPERT:(j + 1) * ROWS_PER_EXPERT, u * LANES:(u + 1) * LANES] = (
                        _gather_row(tab_ref, rows_t[j]))
            w2 = w_ref[pl.ds(t0, 2), :]
            hi = w2.astype(BF16).astype(F32)
            rep = jnp.dot(jnp.concatenate([hi, w2 - hi], axis=0).astype(BF16), e8,
                          preferred_element_type=F32)
            lhs = jnp.concatenate([rep[i:i + 1, :] * mask for i in range(4)], axis=0).astype(BF16)
            out = jnp.dot(lhs, pltpu.bitcast(stage[...], BF16), preferred_element_type=F32)
            for u in range(2):
                r0 = u * SUBLANES
                peers.append(out[r0:r0 + SUBLANES, u * LANES:(u + 1) * LANES]
                             + out[2 * SUBLANES + r0:3 * SUBLANES + r0, u * LANES:(u + 1) * LANES])
        rows = pl.ds(g0, SUBLANES)
        for q in range(SUBLANES):
            cols = slice(q * LANES, (q + 1) * LANES)
            piece = jnp.concatenate([peer[q:q + 1, :] for peer in peers], axis=0)
            o_ref[rows, cols] = x1_ref[rows, cols] + gt2[:, cols] * piece

    lax.fori_loop(0, tb // (PEER_GROUPS_V * SUBLANES), group, 0)


def _peer_v(idx, w, tab, x1, gt2, blocks_per_batch):
    n, d = x1.shape
    tb = TB_PEER
    q = np.arange(SUBLANES)
    piece = 2 * (q % ROWS_PER_EXPERT) + q // ROWS_PER_EXPERT
    lane = np.arange(PEER_SLOTS * SUBLANES)
    mask = jnp.asarray((lane[None, :] % SUBLANES == piece[:, None]).astype(np.float32))
    e8 = jnp.asarray(np.arange(PEER_SLOTS)[:, None] == lane[None, :] // SUBLANES, BF16)
    return pl.pallas_call(
        _peer_v_kernel,
        grid=(n // tb,),
        in_specs=[
            pl.BlockSpec((tb, PEER_SLOTS), lambda i: (i, 0), memory_space=pltpu.SMEM),
            pl.BlockSpec((tb, PEER_SLOTS), lambda i: (i, 0)),
            pl.BlockSpec(tab.shape, lambda i: (0, 0), pipeline_mode=pl.Buffered(1)),
            pl.BlockSpec((tb, d), lambda i: (i, 0)),
            pl.BlockSpec((1, 1, d), lambda i: (i // blocks_per_batch, 0, 0)),
            pl.BlockSpec(e8.shape, lambda i: (0, 0)),
            pl.BlockSpec(mask.shape, lambda i: (0, 0)),
        ],
        out_specs=pl.BlockSpec((tb, d), lambda i: (i, 0)),
        out_shape=jax.ShapeDtypeStruct((n, d), F32),
        scratch_shapes=[pltpu.VMEM((STAGE_ROWS, 2 * LANES), jnp.uint32) for _ in range(PEER_GROUPS_V * PV_PAIRS)],
        compiler_params=pltpu.CompilerParams(
            dimension_semantics=("arbitrary",), vmem_limit_bytes=VMEM_LIMIT),
        name="peer_v",
    )(idx, w, tab, x1, gt2, e8, mask)


def _layer(x, mod, g_norm1, g_norm2, w_in, q_norm_g, k_norm_g, conv_w, conv_b, conv_ln_g, conv_ln_b,
           rel_bias, g_out_conv, g_out_attn, w_out, w_peer_q, peer_k1, peer_k2, peer_u, peer_v):
    bsz, s, d = x.shape
    n = bsz * s
    mod3 = mod.reshape(bsz, 6, d)
    row = lambda a: a.reshape(1, -1)

    c0 = 2 * CONV_CH
    c1 = c0 + 3 * ATT_W
    c2 = c1 + IDX_HEADS * IDX_DIM
    c3 = c2 + IDX_DIM
    wa = w_in[:, :c0].astype(BF16)
    wqkv = w_in[:, c0:c1].astype(BF16)
    widx = jnp.concatenate([w_in[:, c1:c2], w_in[:, c2:c3], w_in[:, c2:c3]], axis=1).astype(BF16)
    wwi = jnp.zeros((SUBLANES, d), F32).at[:IDX_HEADS].set(w_in[:, c3:c3 + IDX_HEADS].T).astype(BF16)
    head = np.arange(ATT_W) // HEAD_DIM
    e2 = jnp.asarray(head[:, None] == head[None, :], BF16)

    conv_n, q, k, vt, qi, ki, wit = _inproj(
        x, mod3, row(g_norm1), wa, wqkv, widx, wwi,
        row(jnp.tile(q_norm_g, ATT_HEADS)), row(jnp.tile(k_norm_g, ATT_HEADS)), e2,
        conv_w.reshape(CONV_WIDTH, CONV_CH), row(conv_b), row(conv_ln_g), row(conv_ln_b), row(g_out_conv))
    attn_n = _attn(q, k, vt, qi, ki, wit, rel_bias, row(g_out_attn))

    x1, h2, idx, gate = _mid(
        x, conv_n, attn_n, mod3, w_out[:CONV_CH].astype(BF16), w_out[CONV_CH:].astype(BF16), row(g_norm2),
        w_peer_q.astype(BF16), peer_k1.astype(BF16), peer_k2.astype(BF16))

    idx = idx.reshape(n, PEER_SLOTS)
    w = _peer_u(idx, _pack_table(peer_u), h2.reshape(n, d), gate.reshape(n, PEER_SLOTS))
    out = _peer_v(idx, w, _pack_table(peer_v), x1.reshape(n, d), mod3[:, 5:6, :], s // TB_PEER)
    return out.reshape(bsz, s, d)


def kernel(x, c, w_ada, b_ada, g_norm1, g_norm2, w_in, q_norm_g, k_norm_g, conv_w, conv_b, conv_ln_g,
           conv_ln_b, rel_bias, g_out_conv, g_out_attn, w_out, w_peer_q, peer_k1, peer_k2, peer_u, peer_v):
    depth = w_ada.shape[0]
    for l in range(depth):
        mod = _ada(c, w_ada[l], b_ada[l])
        x = _layer(x, mod, g_norm1[l], g_norm2[l], w_in[l], q_norm_g[l], k_norm_g[l], conv_w[l], conv_b[l],
                   conv_ln_g[l], conv_ln_b[l], rel_bias, g_out_conv[l], g_out_attn[l], w_out[l],
                   w_peer_q[l], peer_k1[l], peer_k2[l], peer_u[l], peer_v[l])
    return x
```

```python
import functools
import math

import numpy as np
import jax
import jax.numpy as jnp
from jax import lax
from jax.experimental import pallas as pl
from jax.experimental.pallas import tpu as pltpu

F32 = jnp.float32
BF16 = jnp.bfloat16
I32 = jnp.int32
I16 = jnp.int16
HIGHEST = lax.Precision.HIGHEST

D_MODEL = 1024
CHUNK = 64
CONV_CH = 512
CONV_WIDTH = 31
ATT_HEADS = 8
HEAD_DIM = 64
ATT_W = ATT_HEADS * HEAD_DIM
IDX_HEADS = 4
IDX_DIM = 64
IDX_SCALE = (IDX_HEADS * IDX_DIM) ** -0.5
TOPK_MAX = 256
REL_BUCKETS = 32
REL_MAX_DIST = 128
PEER_HEADS = 8
N_KEYS = 128
N_EXPERTS = N_KEYS * N_KEYS
PEER_TOPK = 16
PEER_SLOTS = PEER_HEADS * PEER_TOPK
EPS = 1e-6
LOG2E = math.log2(math.e)

LANES = 128
SUBLANES = 8
PACK = 16
VMEM_LIMIT = 56 * 1024 * 1024

TM_IN = 512
CONV_ROWS = 64
HALO = 32
TQ = 256
KT = 256
LOW_BITS_PER_TEST = 4
TM_MID = 1024
TB_PEER = 256
PEER_GROUPS = 2
PEER_GROUPS_V = 4
HALF = D_MODEL // 2
ROWS_PER_EXPERT = HALF // LANES

NT_DIMS = (((1,), (1,)), ((), ()))

_NEG_INF_BITS = int(np.array(-np.inf, np.float32).view(np.int32))
KEY_NEG_INF = _NEG_INF_BITS ^ 0x7FFFFFFF
I16_MIN = -(2 ** 15)
I16_MAX = 2 ** 15 - 1


def _nt(a, b, precision=None):
    return lax.dot_general(a, b, NT_DIMS, precision=precision, preferred_element_type=F32)


def _ada_kernel(c_ref, w_ref, b_ref, o_ref):
    a = jax.nn.silu(c_ref[...])
    o_ref[...] = jnp.dot(a, w_ref[...], precision=HIGHEST, preferred_element_type=F32) + b_ref[...]


def _ada(c, w_ada, b_ada):
    bsz, d = c.shape
    return pl.pallas_call(
        _ada_kernel,
        grid=(6,),
        in_specs=[
            pl.BlockSpec((bsz, d), lambda j: (0, 0)),
            pl.BlockSpec((d, d), lambda j: (0, j)),
            pl.BlockSpec((1, d), lambda j: (0, j)),
        ],
        out_specs=pl.BlockSpec((bsz, d), lambda j: (0, j)),
        out_shape=jax.ShapeDtypeStruct((bsz, 6 * d), F32),
        name="ada",
    )(c, w_ada, b_ada.reshape(1, 6 * d))


def _inproj_kernel(x_ref, mod_ref, g1_ref, wa_ref, wqkv_ref, widx_ref, wwi_ref, gq_ref, gk_ref, e2_ref,
                   cw_ref, cb_ref, lng_ref, lnb_ref, goc_ref,
                   conv_ref, q_ref, k_ref, vt_ref, qi_ref, ki_ref, wit_ref, ubuf):
    j = pl.program_id(1)
    tm = x_ref.shape[1]
    x = x_ref[0]
    sh1 = mod_ref[0, 0:1, :]
    sc1 = mod_ref[0, 1:2, :]
    r = lax.rsqrt(jnp.mean(x * x, axis=-1, keepdims=True) + EPS)
    h = (x * r) * g1_ref[...] * (1.0 + sc1) + sh1
    hb = h.astype(BF16)

    pa = jnp.dot(hb, wa_ref[...], preferred_element_type=F32)
    u = pa[:, :CONV_CH] * jax.nn.sigmoid(pa[:, CONV_CH:])

    @pl.when(j == 0)
    def _():
        ubuf[0:HALO, :] = jnp.zeros((HALO, CONV_CH), F32)

    ubuf[HALO:HALO + tm, :] = u
    first = HALO - (CONV_WIDTH - 1)
    for rb in range(tm // CONV_ROWS):
        base = rb * CONV_ROWS
        acc = jnp.zeros((CONV_ROWS, CONV_CH), F32) + cb_ref[...]
        for phase in range(SUBLANES):
            taps = [t for t in range(CONV_WIDTH) if (first + t) % SUBLANES == phase]
            span = (first + taps[-1]) - phase + CONV_ROWS
            slab = ubuf[base + phase:base + phase + span, :]
            for t in taps:
                off = first + t - phase
                acc = acc + cw_ref[t:t + 1, :] * slab[off:off + CONV_ROWS, :]
        mu = jnp.mean(acc, axis=-1, keepdims=True)
        xc = acc - mu
        y = xc * lax.rsqrt(jnp.mean(xc * xc, axis=-1, keepdims=True) + EPS)
        y = jax.nn.silu(y * lng_ref[...] + lnb_ref[...])
        y = y * lax.rsqrt(jnp.mean(y * y, axis=-1, keepdims=True) + EPS) * goc_ref[...]
        conv_ref[0, base:base + CONV_ROWS, :] = y.astype(BF16)
    ubuf[0:HALO, :] = ubuf[tm:tm + HALO, :]

    pq = jnp.dot(hb, wqkv_ref[...], preferred_element_type=F32)
    q = pq[:, :ATT_W]
    k = pq[:, ATT_W:2 * ATT_W]
    v = pq[:, 2 * ATT_W:]
    e2 = e2_ref[...]

    def head_mean(sq):
        hi = sq.astype(BF16)
        lo = (sq - hi.astype(F32)).astype(BF16)
        return (jnp.dot(hi, e2, preferred_element_type=F32)
                + jnp.dot(lo, e2, preferred_element_type=F32)) * (1.0 / HEAD_DIM)

    qs = head_mean(q * q)
    ks = head_mean(k * k)
    qn = q * lax.rsqrt(qs + EPS) * gq_ref[...] * (HEAD_DIM ** -0.5 * LOG2E)
    kn = k * lax.rsqrt(ks + EPS) * gk_ref[...]
    lane = lax.broadcasted_iota(I32, (tm, LANES), 1)
    low = lane < HEAD_DIM
    for p in range(ATT_HEADS // 2):
        slab = qn[:, p * LANES:(p + 1) * LANES]
        q_ref[0, 2 * p] = jnp.where(low, slab, 0.0).T.astype(BF16)
        q_ref[0, 2 * p + 1] = jnp.where(low, 0.0, slab).T.astype(BF16)
        k_ref[0, p] = kn[:, p * LANES:(p + 1) * LANES].astype(BF16)
    vt_ref[0] = v.T.astype(BF16)

    pc = jnp.dot(hb, widx_ref[...], preferred_element_type=F32)
    for p in range(IDX_HEADS // 2):
        slab = pc[:, p * LANES:(p + 1) * LANES]
        qi_ref[0, 2 * p] = jnp.where(low, slab, 0.0).T.astype(BF16)
        qi_ref[0, 2 * p + 1] = jnp.where(low, 0.0, slab).T.astype(BF16)
    ki_ref[0] = pc[:, 2 * LANES:3 * LANES].astype(BF16)
    wit_ref[0] = _nt(wwi_ref[...], hb) * IDX_SCALE


def _inproj(x, mod3, g1, wa, wqkv, widx, wwi, gq, gk, e2, cw, cb, lng, lnb, goc):
    bsz, s, d = x.shape
    tm = TM_IN
    nt = s // tm
    full = lambda shape: pl.BlockSpec(shape, lambda b, j: (0,) * len(shape))
    out_shape = (
        jax.ShapeDtypeStruct((bsz, s, CONV_CH), BF16),
        jax.ShapeDtypeStruct((bsz, ATT_HEADS, LANES, s), BF16),
        jax.ShapeDtypeStruct((bsz, ATT_HEADS // 2, s, LANES), BF16),
        jax.ShapeDtypeStruct((bsz, ATT_W, s), BF16),
        jax.ShapeDtypeStruct((bsz, IDX_HEADS, LANES, s), BF16),
        jax.ShapeDtypeStruct((bsz, s, LANES), BF16),
        jax.ShapeDtypeStruct((bsz, SUBLANES, s), F32),
    )
    out_specs = (
        pl.BlockSpec((1, tm, CONV_CH), lambda b, j: (b, j, 0)),
        pl.BlockSpec((1, ATT_HEADS, LANES, tm), lambda b, j: (b, 0, 0, j)),
        pl.BlockSpec((1, ATT_HEADS // 2, tm, LANES), lambda b, j: (b, 0, j, 0)),
        pl.BlockSpec((1, ATT_W, tm), lambda b, j: (b, 0, j)),
        pl.BlockSpec((1, IDX_HEADS, LANES, tm), lambda b, j: (b, 0, 0, j)),
        pl.BlockSpec((1, tm, LANES), lambda b, j: (b, j, 0)),
        pl.BlockSpec((1, SUBLANES, tm), lambda b, j: (b, 0, j)),
    )
    return pl.pallas_call(
        _inproj_kernel,
        grid=(bsz, nt),
        in_specs=[
            pl.BlockSpec((1, tm, d), lambda b, j: (b, j, 0)),
            pl.BlockSpec((1, 6, d), lambda b, j: (b, 0, 0)),
            full(g1.shape), full(wa.shape), full(wqkv.shape), full(widx.shape), full(wwi.shape),
            full(gq.shape), full(gk.shape), full(e2.shape),
            full(cw.shape), full(cb.shape), full(lng.shape), full(lnb.shape), full(goc.shape),
        ],
        out_specs=out_specs,
        out_shape=out_shape,
        scratch_shapes=[pltpu.VMEM((tm + HALO, CONV_CH), F32)],
        compiler_params=pltpu.CompilerParams(
            dimension_semantics=("arbitrary", "arbitrary"), vmem_limit_bytes=VMEM_LIMIT),
        name="inproj",
    )(x, mod3, g1, wa, wqkv, widx, wwi, gq, gk, e2, cw, cb, lng, lnb, goc)


def _t5_bucket_np(rel):
    half = REL_BUCKETS // 2
    max_exact = half // 2
    ret = np.where(rel > 0, half, 0)
    n = np.abs(rel)
    nf = np.maximum(n, 1).astype(np.float64)
    large = max_exact + (np.log(nf / max_exact) / math.log(REL_MAX_DIST / max_exact)
                         * (half - max_exact)).astype(np.int32)
    large = np.minimum(large, half - 1)
    return (ret + np.where(n < max_exact, n, large)).astype(np.int32)


NEAR = REL_MAX_DIST + TQ
FAR_BUCKET = REL_BUCKETS // 2 - 1


def _near_buckets():
    r = np.arange(NEAR)[:, None]
    t = np.arange(TQ)[None, :]
    return _t5_bucket_np(r - REL_MAX_DIST - t)


def _attn_kernel(q_ref, k_ref, vt_ref, qi_ref, ki_ref, wit_ref, bkt_ref, rb_ref, goa_ref, o_ref,
                 keys_s, hi_s, lo_s, maskb_s, relb_s, ot_s, m_s, mo_s, mn_s, l_s, lg_s, *, nsel):
    b = pl.program_id(0)
    i = pl.program_id(1)
    t0 = i * TQ
    n_tiles = i + 1
    neg_inf = F32(-jnp.inf)

    @pl.when((b == 0) & (i == 0))
    def _():
        bk = bkt_ref[...]
        for h in range(ATT_HEADS):
            far = rb_ref[FAR_BUCKET, h]
            acc = jnp.zeros((NEAR, TQ), F32)
            for bb in range(REL_BUCKETS):
                acc = jnp.where(bk == bb, (rb_ref[bb, h] - far) * LOG2E, acc)
            relb_s[h, 0:2 * KT - NEAR, :] = jnp.zeros((2 * KT - NEAR, TQ), F32)
            relb_s[h, 2 * KT - NEAR:2 * KT, :] = acc

    qpos = t0 + lax.broadcasted_iota(I32, (1, TQ), 1)
    limit = (qpos // CHUNK + 1) * CHUNK
    row_iota = lax.broadcasted_iota(I32, (KT, TQ), 0)

    def tile_start(jt):
        return pl.multiple_of(jt * KT, KT)

    wi = wit_ref[0]

    def score_tile(jt):
        ks = tile_start(jt)
        kit = ki_ref[0, pl.ds(ks, KT), :]
        acc = jnp.zeros((KT, TQ), F32)
        for h in range(IDX_HEADS):
            lgt = jnp.dot(kit, qi_ref[0, h], preferred_element_type=F32)
            acc = acc + jnp.maximum(lgt, 0.0) * wi[h:h + 1, :]
        sc = jnp.where(row_iota + ks < limit, acc, neg_inf)
        bits = pltpu.bitcast(sc, I32)
        key = bits ^ (lax.shift_right_arithmetic(bits, 31) & 0x7FFFFFFF)
        keys_s[pl.ds(ks, KT), :] = key
        hi_s[pl.ds(ks, KT), :] = lax.shift_right_arithmetic(key, 16).astype(I16)
        lo_s[pl.ds(ks, KT), :] = ((key & 0xFFFF) ^ 0x8000).astype(I16)

    def score_pair(jp, c):
        score_tile(2 * jp)
        score_tile(2 * jp + 1)
        return c

    lax.fori_loop(0, n_tiles // 2, score_pair, 0)

    @pl.when(n_tiles % 2 == 1)
    def _():
        score_tile(n_tiles - 1)

    def count_ge(half_ref, cand):
        c16 = cand.astype(I16)
        one = jnp.ones((KT, TQ), I16)
        zero = jnp.zeros((KT, TQ), I16)

        def tile_count(tiles):
            parts = []
            for jt in tiles:
                hit = jnp.where(half_ref[pl.ds(tile_start(jt), KT), :] >= c16, one, zero)
                parts += [hit[r:r + PACK, :] for r in range(0, KT, PACK)]
            while len(parts) > 1:
                parts = [parts[k] + parts[k + 1] for k in range(0, len(parts), 2)]
            return parts[0].astype(I32)

        acc = lax.fori_loop(0, n_tiles // 2, lambda jp, acc: acc + tile_count([2 * jp, 2 * jp + 1]),
                            jnp.zeros((PACK, TQ), I32))
        acc = lax.cond(n_tiles % 2 == 1, lambda a: a + tile_count([n_tiles - 1]), lambda a: a, acc)
        return jnp.sum(acc.astype(F32), axis=0, keepdims=True)

    def low_key(low):
        return (low ^ 0x8000) - jnp.where(low < 0x8000, 0x10000, 0)

    def select_thr():
        c0 = count_ge(hi_s, jnp.zeros((1, TQ), I32))
        ok = c0 >= nsel
        t = jnp.where(ok, 0, I16_MIN).astype(I32)
        cnt = jnp.where(ok, c0, F32(2 ** 30))

        def top_step(it, carry):
            t, cnt = carry
            cand = t + lax.shift_left(I32(1), I32(14) - it)
            c = count_ge(hi_s, cand)
            ok = c >= nsel
            return jnp.where(ok, cand, t), jnp.where(ok, c, cnt)

        top, cnt = lax.fori_loop(0, 15, top_step, (t, cnt))

        top_max = top >= I16_MAX
        above = jnp.where(top_max, 0.0, count_ge(hi_s, jnp.where(top_max, top, top + 1)))
        top16 = top.astype(I16)
        floor16 = jnp.full((KT, TQ), I16_MIN, I16)

        def keep_low(jt, c):
            rows = pl.ds(tile_start(jt), KT)
            lo_s[rows, :] = jnp.where(hi_s[rows, :] == top16, lo_s[rows, :], floor16)
            return c

        lax.fori_loop(0, n_tiles, keep_low, 0)

        at_zero = jnp.where(top == 0, above + count_ge(lo_s, low_key(jnp.ones((1, TQ), I32))), F32(2 ** 30))
        settled = at_zero < nsel

        def low_cond(carry):
            bit, _, cnt = carry
            return (bit >= 0) & (jnp.max(jnp.where(settled, F32(nsel), cnt)) > nsel)

        def low_bit(bit, low, cnt):
            cand = low + lax.shift_left(I32(1), bit)
            c = above + count_ge(lo_s, low_key(cand))
            ok = c >= nsel
            return jnp.where(ok, cand, low), jnp.where(ok, c, cnt)

        def low_step(carry):
            bit, low, cnt = carry
            for k in range(LOW_BITS_PER_TEST):
                low, cnt = low_bit(bit - k, low, cnt)
            return bit - LOW_BITS_PER_TEST, low, cnt

        _, low, cnt = lax.while_loop(low_cond, low_step, (I32(15), jnp.zeros((1, TQ), I32), cnt))
        low_max = low >= 0xFFFF
        c_gt = above + jnp.where(low_max, 0.0, count_ge(lo_s, low_key(jnp.where(low_max, low, low + 1))))
        return lax.shift_left(top, 16) + low, cnt, c_gt

    thr, c_ge, c_gt = lax.cond(
        i > 0, select_thr,
        lambda: (jnp.full((1, TQ), KEY_NEG_INF + 1, I32), jnp.full((1, TQ), nsel, F32), jnp.zeros((1, TQ), F32)))

    def mask_plain():
        def p3(jt, c):
            ks = tile_start(jt)
            maskb_s[pl.ds(ks, KT), :] = jnp.where(keys_s[pl.ds(ks, KT), :] >= thr, 0.0, neg_inf)
            return c
        lax.fori_loop(0, n_tiles, p3, 0)
        return I32(0)

    def mask_ties():
        quota = nsel - c_gt
        tri = jnp.where(lax.broadcasted_iota(I32, (KT, KT), 0) >= lax.broadcasted_iota(I32, (KT, KT), 1),
                        1.0, 0.0).astype(BF16)

        def local_ranks(jt):
            ks = tile_start(jt)
            kt = keys_s[pl.ds(ks, KT), :]
            eq = kt == thr
            return ks, kt, eq, jnp.dot(tri, jnp.where(eq, 1.0, 0.0).astype(BF16), preferred_element_type=F32)

        def write_mask(tile, before):
            ks, kt, eq, local = tile
            rank = before + local
            tied = jnp.where(eq, jnp.where(rank <= quota, 0.0, neg_inf), neg_inf)
            maskb_s[pl.ds(ks, KT), :] = jnp.where(kt > thr, 0.0, tied)
            return rank[KT - 1:KT, :]

        def pair(jp, before):
            first, second = local_ranks(2 * jp), local_ranks(2 * jp + 1)
            return write_mask(second, write_mask(first, before))

        before = lax.fori_loop(0, n_tiles // 2, pair, jnp.zeros((1, TQ), F32))

        @pl.when(n_tiles % 2 == 1)
        def _():
            write_mask(local_ranks(n_tiles - 1), before)

        return I32(0)

    lax.cond(jnp.max(c_ge) > nsel, mask_ties, mask_plain)

    m_s[...] = jnp.full((ATT_HEADS, TQ), neg_inf, F32)
    l_s[...] = jnp.zeros((ATT_HEADS, TQ), F32)
    ot_s[...] = jnp.zeros((ATT_W, TQ), F32)

    def att_tiles(jts, near):
        starts = [tile_start(jt) for jt in jts]
        for n, (jt, ks) in enumerate(zip(jts, starts)):
            mb = maskb_s[pl.ds(ks, KT), :]
            for h in range(ATT_HEADS):
                l = jnp.dot(k_ref[0, h // 2, pl.ds(ks, KT), :], q_ref[0, h], preferred_element_type=F32) + mb
                if near:
                    off = pl.multiple_of((jt - (n_tiles - 2)) * KT, KT)
                    l = l + relb_s[h, pl.ds(off, KT), :]
                slot = n * ATT_HEADS + h
                lg_s[slot] = l
                m_old = m_s[h:h + 1, :]
                m_new = jnp.maximum(m_old, jnp.max(l, axis=0, keepdims=True))
                mo_s[slot:slot + 1, :] = m_old
                mn_s[slot:slot + 1, :] = m_new
                m_s[h:h + 1, :] = m_new
        for n, ks in enumerate(starts):
            for h in range(ATT_HEADS):
                slot = n * ATT_HEADS + h
                m_new = mn_s[slot:slot + 1, :]
                m_ref = jnp.where(m_new == neg_inf, 0.0, m_new)
                alpha = jnp.exp2(mo_s[slot:slot + 1, :] - m_ref)
                p = jnp.exp2(lg_s[slot] - m_ref)
                l_s[h:h + 1, :] = alpha * l_s[h:h + 1, :] + jnp.sum(p, axis=0, keepdims=True)
                vt = vt_ref[0, h * HEAD_DIM:(h + 1) * HEAD_DIM, pl.ds(ks, KT)]
                rows = slice(h * HEAD_DIM, (h + 1) * HEAD_DIM)
                ot_s[rows, :] = ot_s[rows, :] * alpha + jnp.dot(vt, p.astype(BF16), preferred_element_type=F32)

    n_far = jnp.maximum(n_tiles - 2, 0)

    def far_pair(jp, c):
        att_tiles([2 * jp, 2 * jp + 1], False)
        return c

    lax.fori_loop(0, n_far // 2, far_pair, 0)

    @pl.when(n_far % 2 == 1)
    def _():
        att_tiles([n_far - 1], False)

    @pl.when(i > 0)
    def _():
        att_tiles([n_tiles - 2, n_tiles - 1], True)

    @pl.when(i == 0)
    def _():
        att_tiles([n_tiles - 1], True)

    for h in range(ATT_HEADS):
        rows = slice(h * HEAD_DIM, (h + 1) * HEAD_DIM)
        ot_s[rows, :] = ot_s[rows, :] / l_s[h:h + 1, :]

    ot = ot_s[...]
    ms = jnp.mean(ot * ot, axis=0, keepdims=True)
    y = (ot * lax.rsqrt(ms + EPS)).T * goa_ref[...]
    o_ref[0] = y.astype(BF16)


def _attn(q, k, vt, qi, ki, wit, rel_bias, goa):
    bsz, _, _, s = q.shape
    nsel = min(TOPK_MAX, s // 4)
    assert nsel == TQ and s % TQ == 0, "attention kernel assumes TOPK_MAX-sized query blocks"
    bkt = jnp.asarray(_near_buckets())
    kern = functools.partial(_attn_kernel, nsel=nsel)
    return pl.pallas_call(
        kern,
        grid=(bsz, s // TQ),
        in_specs=[
            pl.BlockSpec((1, ATT_HEADS, LANES, TQ), lambda b, i: (b, 0, 0, i)),
            pl.BlockSpec((1, ATT_HEADS // 2, s, LANES), lambda b, i: (b, 0, 0, 0)),
            pl.BlockSpec((1, ATT_W, s), lambda b, i: (b, 0, 0)),
            pl.BlockSpec((1, IDX_HEADS, LANES, TQ), lambda b, i: (b, 0, 0, i)),
            pl.BlockSpec((1, s, LANES), lambda b, i: (b, 0, 0)),
            pl.BlockSpec((1, SUBLANES, TQ), lambda b, i: (b, 0, i)),
            pl.BlockSpec((NEAR, TQ), lambda b, i: (0, 0)),
            pl.BlockSpec(memory_space=pltpu.SMEM),
            pl.BlockSpec((1, ATT_W), lambda b, i: (0, 0)),
        ],
        out_specs=pl.BlockSpec((1, TQ, ATT_W), lambda b, i: (b, i, 0)),
        out_shape=jax.ShapeDtypeStruct((bsz, s, ATT_W), BF16),
        scratch_shapes=[
            pltpu.VMEM((s, TQ), I32),
            pltpu.VMEM((s, TQ), I16),
            pltpu.VMEM((s, TQ), I16),
            pltpu.VMEM((s, TQ), F32),
            pltpu.VMEM((ATT_HEADS, 2 * KT, TQ), F32),
            pltpu.VMEM((ATT_W, TQ), F32),
            pltpu.VMEM((ATT_HEADS, TQ), F32),
            pltpu.VMEM((2 * ATT_HEADS, TQ), F32),
            pltpu.VMEM((2 * ATT_HEADS, TQ), F32),
            pltpu.VMEM((ATT_HEADS, TQ), F32),
            pltpu.VMEM((2 * ATT_HEADS, KT, TQ), F32),
        ],
        compiler_params=pltpu.CompilerParams(
            dimension_semantics=("arbitrary", "arbitrary"), vmem_limit_bytes=VMEM_LIMIT),
        name="attn",
    )(q, k, vt, qi, ki, wit, bkt, rel_bias, goa)


def _topk_rows(s, payload, k, order=None):
    rows = lax.broadcasted_iota(I32, s.shape, 0).astype(F32) if order is None else order
    vals, pays = [], []
    for _ in range(k):
        m = jnp.max(s, axis=0, keepdims=True)
        ix = jnp.min(jnp.where(s == m, rows, _NO_ROW), axis=0, keepdims=True)
        hit = rows == ix
        if payload is None:
            pays.append(ix)
        else:
            pays.append(jnp.max(jnp.where(hit, payload, -1.0), axis=0, keepdims=True))
        vals.append(m)
        s = jnp.where(hit, -jnp.inf, s)
    return jnp.concatenate(vals, axis=0), jnp.concatenate(pays, axis=0)


_NO_ROW = 1e9

_PAIR_GROUPS = (
    ((0, 0, 8, 0),), ((0, 8, 8, 0),), ((1, 0, 8, 0),),
    ((2, 0, 5, 0), (5, 0, 2, 5)), ((3, 0, 4, 0), (4, 0, 3, 4)), ((6, 0, 2, 0), (7, 0, 2, 2)),
)
assert sorted((a, b) for g in _PAIR_GROUPS for a, b0, nb, _ in g for b in range(b0, b0 + nb)) == sorted(
    (a, b) for a in range(PEER_TOPK // 2) for b in range(PEER_TOPK // (a + 1)))


def _mid_kernel(x_ref, cn_ref, an_ref, mod_ref, wo1_ref, wo2_ref, g2_ref, wpq_ref, k1_ref, k2_ref,
                x1_ref, h2_ref, idx_ref, gate_ref, qq_s, idt_s, gt_s):
    tm = x_ref.shape[1]
    x = x_ref[0]
    gt1 = mod_ref[0, 2:3, :]
    sh2 = mod_ref[0, 3:4, :]
    sc2 = mod_ref[0, 4:5, :]
    proj = (jnp.dot(cn_ref[0], wo1_ref[...], preferred_element_type=F32)
            + jnp.dot(an_ref[0], wo2_ref[...], preferred_element_type=F32))
    x1 = x + gt1 * proj
    x1_ref[0] = x1
    r = lax.rsqrt(jnp.mean(x1 * x1, axis=-1, keepdims=True) + EPS)
    h2 = (x1 * r) * g2_ref[...] * (1.0 + sc2) + sh2
    h2_ref[0] = h2
    qq_s[...] = jnp.dot(h2.astype(BF16), wpq_ref[...], preferred_element_type=F32).astype(BF16)

    def route_unit(hh, lt):
        rows = pl.ds(lt * LANES, LANES)
        q1 = qq_s[rows, pl.ds(pl.multiple_of(hh * 2 * N_KEYS, LANES), N_KEYS)]
        q2 = qq_s[rows, pl.ds(pl.multiple_of(hh * 2 * N_KEYS + N_KEYS, LANES), N_KEYS)]
        v1, i1 = _topk_rows(_nt(k1_ref[hh], q1), None, PEER_TOPK)
        v2, i2 = _topk_rows(_nt(k2_ref[hh], q2), None, PEER_TOPK)
        sub = lax.broadcasted_iota(I32, (SUBLANES, LANES), 0)
        subf = sub.astype(F32)
        cands, cidxs, flats = [], [], []
        for group in _PAIR_GROUPS:
            val = jnp.full((SUBLANES, LANES), -jnp.inf, F32)
            cid = jnp.zeros((SUBLANES, LANES), F32)
            flat = jnp.full((SUBLANES, LANES), _NO_ROW, F32)
            for a, b0, nb, off in group:
                v2s, i2s = v2[b0:b0 + SUBLANES, :], i2[b0:b0 + SUBLANES, :]
                if off:
                    v2s, i2s = pltpu.roll(v2s, off, 0), pltpu.roll(i2s, off, 0)
                inside = lambda new, old: jnp.where(sub < off + nb, jnp.where(sub >= off, new, old), old)
                val = inside(v1[a:a + 1, :] + v2s, val)
                cid = inside(i1[a:a + 1, :] * float(N_KEYS) + i2s, cid)
                flat = inside(subf + float(a * PEER_TOPK + b0 - off), flat)
            cands.append(val)
            cidxs.append(cid)
            flats.append(flat)
        half = PEER_TOPK // 2
        cands.append(v1[half:, :] + v2[0:1, :])
        cidxs.append(i1[half:, :] * float(N_KEYS) + i2[0:1, :])
        flats.append((subf + float(half)) * float(PEER_TOPK))
        best, experts = _topk_rows(jnp.concatenate(cands, axis=0), jnp.concatenate(cidxs, axis=0), PEER_TOPK,
                                   order=jnp.concatenate(flats, axis=0))
        e = jnp.exp(best - best[0:1, :])
        g = e / jnp.sum(e, axis=0, keepdims=True)
        slots = pl.ds(pl.multiple_of(hh * PEER_TOPK, PEER_TOPK), PEER_TOPK)
        cols = pl.ds(lt * LANES, LANES)
        idt_s[slots, cols] = experts * float(ROWS_PER_EXPERT)
        gt_s[slots, cols] = g

    def route(hh, c):
        for lt in range(tm // LANES):
            route_unit(hh, lt)
        return c

    lax.fori_loop(0, PEER_HEADS, route, 0)
    idx_ref[0] = idt_s[...].T.astype(I32)
    gate_ref[0] = gt_s[...].T


def _mid(x, cn, an, mod3, wo1, wo2, g2, wpq, k1, k2):
    bsz, s, d = x.shape
    tm = TM_MID
    full = lambda shape: pl.BlockSpec(shape, lambda b, j: (0,) * len(shape))
    tok = lambda w: pl.BlockSpec((1, tm, w), lambda b, j: (b, j, 0))
    return pl.pallas_call(
        _mid_kernel,
        grid=(bsz, s // tm),
        in_specs=[tok(d), tok(CONV_CH), tok(ATT_W), pl.BlockSpec((1, 6, d), lambda b, j: (b, 0, 0)),
                  full(wo1.shape), full(wo2.shape), full(g2.shape), full(wpq.shape), full(k1.shape), full(k2.shape)],
        out_specs=(tok(d), tok(d), tok(PEER_SLOTS), tok(PEER_SLOTS)),
        out_shape=(
            jax.ShapeDtypeStruct((bsz, s, d), F32),
            jax.ShapeDtypeStruct((bsz, s, d), F32),
            jax.ShapeDtypeStruct((bsz, s, PEER_SLOTS), I32),
            jax.ShapeDtypeStruct((bsz, s, PEER_SLOTS), F32),
        ),
        scratch_shapes=[
            pltpu.VMEM((tm, PEER_HEADS * 2 * N_KEYS), BF16),
            pltpu.VMEM((PEER_SLOTS, tm), F32),
            pltpu.VMEM((PEER_SLOTS, tm), F32),
        ],
        compiler_params=pltpu.CompilerParams(
            dimension_semantics=("arbitrary", "arbitrary"), vmem_limit_bytes=VMEM_LIMIT),
        name="mid",
    )(x, cn, an, mod3, wo1, wo2, g2, wpq, k1, k2)


def _pack_table(t):
    tb = t.astype(BF16)
    lo = lax.bitcast_convert_type(tb[:, :HALF], jnp.uint16).astype(jnp.uint32)
    hi = lax.bitcast_convert_type(tb[:, HALF:], jnp.uint16).astype(jnp.uint32)
    return (lo | (hi << 16)).reshape(t.shape[0] * ROWS_PER_EXPERT, LANES)


def _unpack(w):
    lo = pltpu.bitcast(lax.shift_left(w, jnp.uint32(16)), F32)
    hi = pltpu.bitcast(w & jnp.uint32(0xFFFF0000), F32)
    return lo, hi


def _gather_row(tab_ref, row):
    return tab_ref[pl.ds(pl.multiple_of(row, ROWS_PER_EXPERT), ROWS_PER_EXPERT), :]


STAGE_ROWS = PEER_SLOTS * ROWS_PER_EXPERT


def _token_pieces(xg, r, first, count):
    return jnp.concatenate(
        [xg[r:r + 1, (first + q) * LANES:(first + q + 1) * LANES] for q in range(count)], axis=0)


def _peer_u_kernel(idx_ref, tab_ref, x_ref, gate_ref, w_ref, *stages):
    tb = w_ref.shape[0]
    sub = lax.broadcasted_iota(I32, (SUBLANES, PEER_SLOTS), 0)

    def group(g, c):
        for k in range(PEER_GROUPS):
            eight(pl.multiple_of((g * PEER_GROUPS + k) * SUBLANES, SUBLANES), stages[k * SUBLANES:(k + 1) * SUBLANES])
        return c

    def eight(t0, stages):
        act8 = jnp.zeros((SUBLANES, PEER_SLOTS), F32)
        xg = x_ref[pl.ds(t0, SUBLANES), :]
        xlo, xhi, rows_t = [], [], []
        for r in range(SUBLANES):
            xlo.append(jnp.concatenate([_token_pieces(xg, r, 0, ROWS_PER_EXPERT)] * 2, axis=0))
            xhi.append(jnp.concatenate([_token_pieces(xg, r, ROWS_PER_EXPERT, ROWS_PER_EXPERT)] * 2, axis=0))
            rows_t.append(idx_ref.at[t0 + r])
        for j in range(0, PEER_SLOTS, 2):
            for r in range(SUBLANES):
                pair = jnp.concatenate(
                    [_gather_row(tab_ref, rows_t[r][j]), _gather_row(tab_ref, rows_t[r][j + 1])], axis=0)
                lo, hi = _unpack(pair)
                row = j * ROWS_PER_EXPERT
                stages[r][row:row + 2 * ROWS_PER_EXPERT, :] = lo * xlo[r] + hi * xhi[r]
        for r in range(SUBLANES):
            stage = stages[r]
            cs = stage[pl.ds(0, PEER_SLOTS, stride=ROWS_PER_EXPERT), :]
            for q in range(1, ROWS_PER_EXPERT):
                cs = cs + stage[pl.ds(q, PEER_SLOTS, stride=ROWS_PER_EXPERT), :]
            act = jnp.sum(cs.T, axis=0, keepdims=True)
            act8 = jnp.where(sub == r, act, act8)
        rows = pl.ds(t0, SUBLANES)
        w_ref[rows, :] = gate_ref[rows, :] * jax.nn.gelu(act8)

    lax.fori_loop(0, tb // (PEER_GROUPS * SUBLANES), group, 0)


def _peer_u(idx, tab, h2, gate):
    n, d = h2.shape
    tb = TB_PEER
    return pl.pallas_call(
        _peer_u_kernel,
        grid=(n // tb,),
        in_specs=[
            pl.BlockSpec((tb, PEER_SLOTS), lambda i: (i, 0), memory_space=pltpu.SMEM),
            pl.BlockSpec(tab.shape, lambda i: (0, 0), pipeline_mode=pl.Buffered(1)),
            pl.BlockSpec((tb, d), lambda i: (i, 0)),
            pl.BlockSpec((tb, PEER_SLOTS), lambda i: (i, 0)),
        ],
        out_specs=pl.BlockSpec((tb, PEER_SLOTS), lambda i: (i, 0)),
        out_shape=jax.ShapeDtypeStruct((n, PEER_SLOTS), F32),
        scratch_shapes=[pltpu.VMEM((STAGE_ROWS, LANES), F32) for _ in range(PEER_GROUPS * SUBLANES)],
        compiler_params=pltpu.CompilerParams(
            dimension_semantics=("arbitrary",), vmem_limit_bytes=VMEM_LIMIT),
        name="peer_u",
    )(idx, tab, h2, gate)


PV_PAIRS = SUBLANES // 2


def _peer_v_kernel(idx_ref, w_ref, tab_ref, x1_ref, gt2_ref, e8_ref, mask_ref, o_ref, *stages):
    tb = idx_ref.shape[0]
    gt2 = gt2_ref[0]
    mask = mask_ref[...]
    e8 = e8_ref[...]

    def group(g, c):
        for k in range(PEER_GROUPS_V):
            eight(pl.multiple_of((g * PEER_GROUPS_V + k) * SUBLANES, SUBLANES), stages[k * PV_PAIRS:(k + 1) * PV_PAIRS])
        return c

    def eight(g0, stages):
        peers = []
        for p in range(PV_PAIRS):
            stage = stages[p]
            t0 = g0 + 2 * p
            for u in range(2):
                rows_t = idx_ref.at[t0 + u]
                for j in range(PEER_SLOTS):
                    stage[j * ROWS_PER_EXPERT:(j + 1) * ROWS_PER_EXPERT, u * LANES:(u + 1) * LANES] = (
                        _gather_row(tab_ref, rows_t[j]))
            w2 = w_ref[pl.ds(t0, 2), :]
            hi = w2.astype(BF16).astype(F32)
            rep = jnp.dot(jnp.concatenate([hi, w2 - hi], axis=0).astype(BF16), e8,
                          preferred_element_type=F32)
            lhs = jnp.concatenate([rep[i:i + 1, :] * mask for i in range(4)], axis=0).astype(BF16)
            out = jnp.dot(lhs, pltpu.bitcast(stage[...], BF16), preferred_element_type=F32)
            for u in range(2):
                r0 = u * SUBLANES
                peers.append(out[r0:r0 + SUBLANES, u * LANES:(u + 1) * LANES]
                             + out[2 * SUBLANES + r0:3 * SUBLANES + r0, u * LANES:(u + 1) * LANES])
        rows = pl.ds(g0, SUBLANES)
        for q in range(SUBLANES):
            cols = slice(q * LANES, (q + 1) * LANES)
            piece = jnp.concatenate([peer[q:q + 1, :] for peer in peers], axis=0)
            o_ref[rows, cols] = x1_ref[rows, cols] + gt2[:, cols] * piece

    lax.fori_loop(0, tb // (PEER_GROUPS_V * SUBLANES), group, 0)


def _peer_v(idx, w, tab, x1, gt2, blocks_per_batch):
    n, d = x1.shape
    tb = TB_PEER
    q = np.arange(SUBLANES)
    piece = 2 * (q % ROWS_PER_EXPERT) + q // ROWS_PER_EXPERT
    lane = np.arange(PEER_SLOTS * SUBLANES)
    mask = jnp.asarray((lane[None, :] % SUBLANES == piece[:, None]).astype(np.float32))
    e8 = jnp.asarray(np.arange(PEER_SLOTS)[:, None] == lane[None, :] // SUBLANES, BF16)
    return pl.pallas_call(
        _peer_v_kernel,
        grid=(n // tb,),
        in_specs=[
            pl.BlockSpec((tb, PEER_SLOTS), lambda i: (i, 0), memory_space=pltpu.SMEM),
            pl.BlockSpec((tb, PEER_SLOTS), lambda i: (i, 0)),
            pl.BlockSpec(tab.shape, lambda i: (0, 0), pipeline_mode=pl.Buffered(1)),
            pl.BlockSpec((tb, d), lambda i: (i, 0)),
            pl.BlockSpec((1, 1, d), lambda i: (i // blocks_per_batch, 0, 0)),
            pl.BlockSpec(e8.shape, lambda i: (0, 0)),
            pl.BlockSpec(mask.shape, lambda i: (0, 0)),
        ],
        out_specs=pl.BlockSpec((tb, d), lambda i: (i, 0)),
        out_shape=jax.ShapeDtypeStruct((n, d), F32),
        scratch_shapes=[pltpu.VMEM((STAGE_ROWS, 2 * LANES), jnp.uint32) for _ in range(PEER_GROUPS_V * PV_PAIRS)],
        compiler_params=pltpu.CompilerParams(
            dimension_semantics=("arbitrary",), vmem_limit_bytes=VMEM_LIMIT),
        name="peer_v",
    )(idx, w, tab, x1, gt2, e8, mask)


def _layer(x, mod, g_norm1, g_norm2, w_in, q_norm_g, k_norm_g, conv_w, conv_b, conv_ln_g, conv_ln_b,
           rel_bias, g_out_conv, g_out_attn, w_out, w_peer_q, peer_k1, peer_k2, peer_u, peer_v):
    bsz, s, d = x.shape
    n = bsz * s
    mod3 = mod.reshape(bsz, 6, d)
    row = lambda a: a.reshape(1, -1)

    c0 = 2 * CONV_CH
    c1 = c0 + 3 * ATT_W
    c2 = c1 + IDX_HEADS * IDX_DIM
    c3 = c2 + IDX_DIM
    wa = w_in[:, :c0].astype(BF16)
    wqkv = w_in[:, c0:c1].astype(BF16)
    widx = jnp.concatenate([w_in[:, c1:c2], w_in[:, c2:c3], w_in[:, c2:c3]], axis=1).astype(BF16)
    wwi = jnp.zeros((SUBLANES, d), F32).at[:IDX_HEADS].set(w_in[:, c3:c3 + IDX_HEADS].T).astype(BF16)
    head = np.arange(ATT_W) // HEAD_DIM
    e2 = jnp.asarray(head[:, None] == head[None, :], BF16)

    conv_n, q, k, vt, qi, ki, wit = _inproj(
        x, mod3, row(g_norm1), wa, wqkv, widx, wwi,
        row(jnp.tile(q_norm_g, ATT_HEADS)), row(jnp.tile(k_norm_g, ATT_HEADS)), e2,
        conv_w.reshape(CONV_WIDTH, CONV_CH), row(conv_b), row(conv_ln_g), row(conv_ln_b), row(g_out_conv))
    attn_n = _attn(q, k, vt, qi, ki, wit, rel_bias, row(g_out_attn))

    x1, h2, idx, gate = _mid(
        x, conv_n, attn_n, mod3, w_out[:CONV_CH].astype(BF16), w_out[CONV_CH:].astype(BF16), row(g_norm2),
        w_peer_q.astype(BF16), peer_k1.astype(BF16), peer_k2.astype(BF16))

    idx = idx.reshape(n, PEER_SLOTS)
    w = _peer_u(idx, _pack_table(peer_u), h2.reshape(n, d), gate.reshape(n, PEER_SLOTS))
    out = _peer_v(idx, w, _pack_table(peer_v), x1.reshape(n, d), mod3[:, 5:6, :], s // TB_PEER)
    return out.reshape(bsz, s, d)


def kernel(x, c, w_ada, b_ada, g_norm1, g_norm2, w_in, q_norm_g, k_norm_g, conv_w, conv_b, conv_ln_g,
           conv_ln_b, rel_bias, g_out_conv, g_out_attn, w_out, w_peer_q, peer_k1, peer_k2, peer_u, peer_v):
    depth = w_ada.shape[0]
    for l in range(depth):
        mod = _ada(c, w_ada[l], b_ada[l])
        x = _layer(x, mod, g_norm1[l], g_norm2[l], w_in[l], q_norm_g[l], k_norm_g[l], conv_w[l], conv_b[l],
                   conv_ln_g[l], conv_ln_b[l], rel_bias, g_out_conv[l], g_out_attn[l], w_out[l],
                   w_peer_q[l], peer_k1[l], peer_k2[l], peer_u[l], peer_v[l])
    return x
```

```python
import functools
import math

import numpy as np
import jax
import jax.numpy as jnp
from jax import lax
from jax.experimental import pallas as pl
from jax.experimental.pallas import tpu as pltpu

F32 = jnp.float32
BF16 = jnp.bfloat16
I32 = jnp.int32
I16 = jnp.int16
HIGHEST = lax.Precision.HIGHEST

D_MODEL = 1024
CHUNK = 64
CONV_CH = 512
CONV_WIDTH = 31
ATT_HEADS = 8
HEAD_DIM = 64
ATT_W = ATT_HEADS * HEAD_DIM
IDX_HEADS = 4
IDX_DIM = 64
IDX_SCALE = (IDX_HEADS * IDX_DIM) ** -0.5
TOPK_MAX = 256
REL_BUCKETS = 32
REL_MAX_DIST = 128
PEER_HEADS = 8
N_KEYS = 128
N_EXPERTS = N_KEYS * N_KEYS
PEER_TOPK = 16
PEER_SLOTS = PEER_HEADS * PEER_TOPK
EPS = 1e-6
LOG2E = math.log2(math.e)

LANES = 128
SUBLANES = 8
PACK = 16
VMEM_LIMIT = 56 * 1024 * 1024

TM_IN = 512
CONV_ROWS = 64
HALO = 32
TQ = 256
KT = 256
LOW_BITS_PER_TEST = 4
TM_MID = 1024
TB_PEER = 256
PEER_GROUPS = 2
PEER_GROUPS_V = 8
HALF = D_MODEL // 2
ROWS_PER_EXPERT = HALF // LANES

NT_DIMS = (((1,), (1,)), ((), ()))

_NEG_INF_BITS = int(np.array(-np.inf, np.float32).view(np.int32))
KEY_NEG_INF = _NEG_INF_BITS ^ 0x7FFFFFFF
I16_MIN = -(2 ** 15)
I16_MAX = 2 ** 15 - 1


def _nt(a, b, precision=None):
    return lax.dot_general(a, b, NT_DIMS, precision=precision, preferred_element_type=F32)


def _ada_kernel(c_ref, w_ref, b_ref, o_ref):
    a = jax.nn.silu(c_ref[...])
    o_ref[...] = jnp.dot(a, w_ref[...], precision=HIGHEST, preferred_element_type=F32) + b_ref[...]


def _ada(c, w_ada, b_ada):
    bsz, d = c.shape
    return pl.pallas_call(
        _ada_kernel,
        grid=(6,),
        in_specs=[
            pl.BlockSpec((bsz, d), lambda j: (0, 0)),
            pl.BlockSpec((d, d), lambda j: (0, j)),
            pl.BlockSpec((1, d), lambda j: (0, j)),
        ],
        out_specs=pl.BlockSpec((bsz, d), lambda j: (0, j)),
        out_shape=jax.ShapeDtypeStruct((bsz, 6 * d), F32),
        name="ada",
    )(c, w_ada, b_ada.reshape(1, 6 * d))


def _inproj_kernel(x_ref, mod_ref, g1_ref, wa_ref, wqkv_ref, widx_ref, wwi_ref, gq_ref, gk_ref, e2_ref,
                   cw_ref, cb_ref, lng_ref, lnb_ref, goc_ref,
                   conv_ref, q_ref, k_ref, vt_ref, qi_ref, ki_ref, wit_ref, ubuf):
    j = pl.program_id(1)
    tm = x_ref.shape[1]
    x = x_ref[0]
    sh1 = mod_ref[0, 0:1, :]
    sc1 = mod_ref[0, 1:2, :]
    r = lax.rsqrt(jnp.mean(x * x, axis=-1, keepdims=True) + EPS)
    h = (x * r) * g1_ref[...] * (1.0 + sc1) + sh1
    hb = h.astype(BF16)

    pa = jnp.dot(hb, wa_ref[...], preferred_element_type=F32)
    u = pa[:, :CONV_CH] * jax.nn.sigmoid(pa[:, CONV_CH:])

    @pl.when(j == 0)
    def _():
        ubuf[0:HALO, :] = jnp.zeros((HALO, CONV_CH), F32)

    ubuf[HALO:HALO + tm, :] = u
    first = HALO - (CONV_WIDTH - 1)
    for rb in range(tm // CONV_ROWS):
        base = rb * CONV_ROWS
        acc = jnp.zeros((CONV_ROWS, CONV_CH), F32) + cb_ref[...]
        for phase in range(SUBLANES):
            taps = [t for t in range(CONV_WIDTH) if (first + t) % SUBLANES == phase]
            span = (first + taps[-1]) - phase + CONV_ROWS
            slab = ubuf[base + phase:base + phase + span, :]
            for t in taps:
                off = first + t - phase
                acc = acc + cw_ref[t:t + 1, :] * slab[off:off + CONV_ROWS, :]
        mu = jnp.mean(acc, axis=-1, keepdims=True)
        xc = acc - mu
        y = xc * lax.rsqrt(jnp.mean(xc * xc, axis=-1, keepdims=True) + EPS)
        y = jax.nn.silu(y * lng_ref[...] + lnb_ref[...])
        y = y * lax.rsqrt(jnp.mean(y * y, axis=-1, keepdims=True) + EPS) * goc_ref[...]
        conv_ref[0, base:base + CONV_ROWS, :] = y.astype(BF16)
    ubuf[0:HALO, :] = ubuf[tm:tm + HALO, :]

    pq = jnp.dot(hb, wqkv_ref[...], preferred_element_type=F32)
    q = pq[:, :ATT_W]
    k = pq[:, ATT_W:2 * ATT_W]
    v = pq[:, 2 * ATT_W:]
    e2 = e2_ref[...]

    def head_mean(sq):
        hi = sq.astype(BF16)
        lo = (sq - hi.astype(F32)).astype(BF16)
        return (jnp.dot(hi, e2, preferred_element_type=F32)
                + jnp.dot(lo, e2, preferred_element_type=F32)) * (1.0 / HEAD_DIM)

    qs = head_mean(q * q)
    ks = head_mean(k * k)
    qn = q * lax.rsqrt(qs + EPS) * gq_ref[...] * (HEAD_DIM ** -0.5 * LOG2E)
    kn = k * lax.rsqrt(ks + EPS) * gk_ref[...]
    lane = lax.broadcasted_iota(I32, (tm, LANES), 1)
    low = lane < HEAD_DIM
    for p in range(ATT_HEADS // 2):
        slab = qn[:, p * LANES:(p + 1) * LANES]
        q_ref[0, 2 * p] = jnp.where(low, slab, 0.0).T.astype(BF16)
        q_ref[0, 2 * p + 1] = jnp.where(low, 0.0, slab).T.astype(BF16)
        k_ref[0, p] = kn[:, p * LANES:(p + 1) * LANES].astype(BF16)
    vt_ref[0] = v.T.astype(BF16)

    pc = jnp.dot(hb, widx_ref[...], preferred_element_type=F32)
    for p in range(IDX_HEADS // 2):
        slab = pc[:, p * LANES:(p + 1) * LANES]
        qi_ref[0, 2 * p] = jnp.where(low, slab, 0.0).T.astype(BF16)
        qi_ref[0, 2 * p + 1] = jnp.where(low, 0.0, slab).T.astype(BF16)
    ki_ref[0] = pc[:, 2 * LANES:3 * LANES].astype(BF16)
    wit_ref[0] = _nt(wwi_ref[...], hb) * IDX_SCALE


def _inproj(x, mod3, g1, wa, wqkv, widx, wwi, gq, gk, e2, cw, cb, lng, lnb, goc):
    bsz, s, d = x.shape
    tm = TM_IN
    nt = s // tm
    full = lambda shape: pl.BlockSpec(shape, lambda b, j: (0,) * len(shape))
    out_shape = (
        jax.ShapeDtypeStruct((bsz, s, CONV_CH), BF16),
        jax.ShapeDtypeStruct((bsz, ATT_HEADS, LANES, s), BF16),
        jax.ShapeDtypeStruct((bsz, ATT_HEADS // 2, s, LANES), BF16),
        jax.ShapeDtypeStruct((bsz, ATT_W, s), BF16),
        jax.ShapeDtypeStruct((bsz, IDX_HEADS, LANES, s), BF16),
        jax.ShapeDtypeStruct((bsz, s, LANES), BF16),
        jax.ShapeDtypeStruct((bsz, SUBLANES, s), F32),
    )
    out_specs = (
        pl.BlockSpec((1, tm, CONV_CH), lambda b, j: (b, j, 0)),
        pl.BlockSpec((1, ATT_HEADS, LANES, tm), lambda b, j: (b, 0, 0, j)),
        pl.BlockSpec((1, ATT_HEADS // 2, tm, LANES), lambda b, j: (b, 0, j, 0)),
        pl.BlockSpec((1, ATT_W, tm), lambda b, j: (b, 0, j)),
        pl.BlockSpec((1, IDX_HEADS, LANES, tm), lambda b, j: (b, 0, 0, j)),
        pl.BlockSpec((1, tm, LANES), lambda b, j: (b, j, 0)),
        pl.BlockSpec((1, SUBLANES, tm), lambda b, j: (b, 0, j)),
    )
    return pl.pallas_call(
        _inproj_kernel,
        grid=(bsz, nt),
        in_specs=[
            pl.BlockSpec((1, tm, d), lambda b, j: (b, j, 0)),
            pl.BlockSpec((1, 6, d), lambda b, j: (b, 0, 0)),
            full(g1.shape), full(wa.shape), full(wqkv.shape), full(widx.shape), full(wwi.shape),
            full(gq.shape), full(gk.shape), full(e2.shape),
            full(cw.shape), full(cb.shape), full(lng.shape), full(lnb.shape), full(goc.shape),
        ],
        out_specs=out_specs,
        out_shape=out_shape,
        scratch_shapes=[pltpu.VMEM((tm + HALO, CONV_CH), F32)],
        compiler_params=pltpu.CompilerParams(
            dimension_semantics=("arbitrary", "arbitrary"), vmem_limit_bytes=VMEM_LIMIT),
        name="inproj",
    )(x, mod3, g1, wa, wqkv, widx, wwi, gq, gk, e2, cw, cb, lng, lnb, goc)


def _t5_bucket_np(rel):
    half = REL_BUCKETS // 2
    max_exact = half // 2
    ret = np.where(rel > 0, half, 0)
    n = np.abs(rel)
    nf = np.maximum(n, 1).astype(np.float64)
    large = max_exact + (np.log(nf / max_exact) / math.log(REL_MAX_DIST / max_exact)
                         * (half - max_exact)).astype(np.int32)
    large = np.minimum(large, half - 1)
    return (ret + np.where(n < max_exact, n, large)).astype(np.int32)


NEAR = REL_MAX_DIST + TQ
FAR_BUCKET = REL_BUCKETS // 2 - 1


def _near_buckets():
    r = np.arange(NEAR)[:, None]
    t = np.arange(TQ)[None, :]
    return _t5_bucket_np(r - REL_MAX_DIST - t)


def _attn_kernel(q_ref, k_ref, vt_ref, qi_ref, ki_ref, wit_ref, bkt_ref, rb_ref, goa_ref, o_ref,
                 keys_s, hi_s, lo_s, maskb_s, relb_s, ot_s, m_s, mo_s, mn_s, l_s, lg_s, *, nsel):
    b = pl.program_id(0)
    i = pl.program_id(1)
    t0 = i * TQ
    n_tiles = i + 1
    neg_inf = F32(-jnp.inf)

    @pl.when((b == 0) & (i == 0))
    def _():
        bk = bkt_ref[...]
        for h in range(ATT_HEADS):
            far = rb_ref[FAR_BUCKET, h]
            acc = jnp.zeros((NEAR, TQ), F32)
            for bb in range(REL_BUCKETS):
                acc = jnp.where(bk == bb, (rb_ref[bb, h] - far) * LOG2E, acc)
            relb_s[h, 0:2 * KT - NEAR, :] = jnp.zeros((2 * KT - NEAR, TQ), F32)
            relb_s[h, 2 * KT - NEAR:2 * KT, :] = acc

    qpos = t0 + lax.broadcasted_iota(I32, (1, TQ), 1)
    limit = (qpos // CHUNK + 1) * CHUNK
    row_iota = lax.broadcasted_iota(I32, (KT, TQ), 0)

    def tile_start(jt):
        return pl.multiple_of(jt * KT, KT)

    wi = wit_ref[0]

    def score_tile(jt):
        ks = tile_start(jt)
        kit = ki_ref[0, pl.ds(ks, KT), :]
        acc = jnp.zeros((KT, TQ), F32)
        for h in range(IDX_HEADS):
            lgt = jnp.dot(kit, qi_ref[0, h], preferred_element_type=F32)
            acc = acc + jnp.maximum(lgt, 0.0) * wi[h:h + 1, :]
        sc = jnp.where(row_iota + ks < limit, acc, neg_inf)
        bits = pltpu.bitcast(sc, I32)
        key = bits ^ (lax.shift_right_arithmetic(bits, 31) & 0x7FFFFFFF)
        keys_s[pl.ds(ks, KT), :] = key
        hi_s[pl.ds(ks, KT), :] = lax.shift_right_arithmetic(key, 16).astype(I16)
        lo_s[pl.ds(ks, KT), :] = ((key & 0xFFFF) ^ 0x8000).astype(I16)

    def score_pair(jp, c):
        score_tile(2 * jp)
        score_tile(2 * jp + 1)
        return c

    lax.fori_loop(0, n_tiles // 2, score_pair, 0)

    @pl.when(n_tiles % 2 == 1)
    def _():
        score_tile(n_tiles - 1)

    def count_ge(half_ref, cand):
        c16 = cand.astype(I16)
        one = jnp.ones((KT, TQ), I16)
        zero = jnp.zeros((KT, TQ), I16)

        def tile_count(tiles):
            parts = []
            for jt in tiles:
                hit = jnp.where(half_ref[pl.ds(tile_start(jt), KT), :] >= c16, one, zero)
                parts += [hit[r:r + PACK, :] for r in range(0, KT, PACK)]
            while len(parts) > 1:
                parts = [parts[k] + parts[k + 1] for k in range(0, len(parts), 2)]
            return parts[0].astype(I32)

        acc = lax.fori_loop(0, n_tiles // 2, lambda jp, acc: acc + tile_count([2 * jp, 2 * jp + 1]),
                            jnp.zeros((PACK, TQ), I32))
        acc = lax.cond(n_tiles % 2 == 1, lambda a: a + tile_count([n_tiles - 1]), lambda a: a, acc)
        return jnp.sum(acc.astype(F32), axis=0, keepdims=True)

    def low_key(low):
        return (low ^ 0x8000) - jnp.where(low < 0x8000, 0x10000, 0)

    def select_thr():
        c0 = count_ge(hi_s, jnp.zeros((1, TQ), I32))
        ok = c0 >= nsel
        t = jnp.where(ok, 0, I16_MIN).astype(I32)
        cnt = jnp.where(ok, c0, F32(2 ** 30))

        def top_step(it, carry):
            t, cnt = carry
            cand = t + lax.shift_left(I32(1), I32(14) - it)
            c = count_ge(hi_s, cand)
            ok = c >= nsel
            return jnp.where(ok, cand, t), jnp.where(ok, c, cnt)

        top, cnt = lax.fori_loop(0, 15, top_step, (t, cnt))

        top_max = top >= I16_MAX
        above = jnp.where(top_max, 0.0, count_ge(hi_s, jnp.where(top_max, top, top + 1)))
        top16 = top.astype(I16)
        floor16 = jnp.full((KT, TQ), I16_MIN, I16)

        def keep_low(jt, c):
            rows = pl.ds(tile_start(jt), KT)
            lo_s[rows, :] = jnp.where(hi_s[rows, :] == top16, lo_s[rows, :], floor16)
            return c

        lax.fori_loop(0, n_tiles, keep_low, 0)

        at_zero = jnp.where(top == 0, above + count_ge(lo_s, low_key(jnp.ones((1, TQ), I32))), F32(2 ** 30))
        settled = at_zero < nsel

        def low_cond(carry):
            bit, _, cnt = carry
            return (bit >= 0) & (jnp.max(jnp.where(settled, F32(nsel), cnt)) > nsel)

        def low_bit(bit, low, cnt):
            cand = low + lax.shift_left(I32(1), bit)
            c = above + count_ge(lo_s, low_key(cand))
            ok = c >= nsel
            return jnp.where(ok, cand, low), jnp.where(ok, c, cnt)

        def low_step(carry):
            bit, low, cnt = carry
            for k in range(LOW_BITS_PER_TEST):
                low, cnt = low_bit(bit - k, low, cnt)
            return bit - LOW_BITS_PER_TEST, low, cnt

        _, low, cnt = lax.while_loop(low_cond, low_step, (I32(15), jnp.zeros((1, TQ), I32), cnt))
        low_max = low >= 0xFFFF
        c_gt = above + jnp.where(low_max, 0.0, count_ge(lo_s, low_key(jnp.where(low_max, low, low + 1))))
        return lax.shift_left(top, 16) + low, cnt, c_gt

    thr, c_ge, c_gt = lax.cond(
        i > 0, select_thr,
        lambda: (jnp.full((1, TQ), KEY_NEG_INF + 1, I32), jnp.full((1, TQ), nsel, F32), jnp.zeros((1, TQ), F32)))

    def mask_plain():
        def p3(jt, c):
            ks = tile_start(jt)
            maskb_s[pl.ds(ks, KT), :] = jnp.where(keys_s[pl.ds(ks, KT), :] >= thr, 0.0, neg_inf)
            return c
        lax.fori_loop(0, n_tiles, p3, 0)
        return I32(0)

    def mask_ties():
        quota = nsel - c_gt
        tri = jnp.where(lax.broadcasted_iota(I32, (KT, KT), 0) >= lax.broadcasted_iota(I32, (KT, KT), 1),
                        1.0, 0.0).astype(BF16)

        def local_ranks(jt):
            ks = tile_start(jt)
            kt = keys_s[pl.ds(ks, KT), :]
            eq = kt == thr
            return ks, kt, eq, jnp.dot(tri, jnp.where(eq, 1.0, 0.0).astype(BF16), preferred_element_type=F32)

        def write_mask(tile, before):
            ks, kt, eq, local = tile
            rank = before + local
            tied = jnp.where(eq, jnp.where(rank <= quota, 0.0, neg_inf), neg_inf)
            maskb_s[pl.ds(ks, KT), :] = jnp.where(kt > thr, 0.0, tied)
            return rank[KT - 1:KT, :]

        def pair(jp, before):
            first, second = local_ranks(2 * jp), local_ranks(2 * jp + 1)
            return write_mask(second, write_mask(first, before))

        before = lax.fori_loop(0, n_tiles // 2, pair, jnp.zeros((1, TQ), F32))

        @pl.when(n_tiles % 2 == 1)
        def _():
            write_mask(local_ranks(n_tiles - 1), before)

        return I32(0)

    lax.cond(jnp.max(c_ge) > nsel, mask_ties, mask_plain)

    m_s[...] = jnp.full((ATT_HEADS, TQ), neg_inf, F32)
    l_s[...] = jnp.zeros((ATT_HEADS, TQ), F32)
    ot_s[...] = jnp.zeros((ATT_W, TQ), F32)

    def att_tiles(jts, near):
        starts = [tile_start(jt) for jt in jts]
        for n, (jt, ks) in enumerate(zip(jts, starts)):
            mb = maskb_s[pl.ds(ks, KT), :]
            for h in range(ATT_HEADS):
                l = jnp.dot(k_ref[0, h // 2, pl.ds(ks, KT), :], q_ref[0, h], preferred_element_type=F32) + mb
                if near:
                    off = pl.multiple_of((jt - (n_tiles - 2)) * KT, KT)
                    l = l + relb_s[h, pl.ds(off, KT), :]
                slot = n * ATT_HEADS + h
                lg_s[slot] = l
                m_old = m_s[h:h + 1, :]
                m_new = jnp.maximum(m_old, jnp.max(l, axis=0, keepdims=True))
                mo_s[slot:slot + 1, :] = m_old
                mn_s[slot:slot + 1, :] = m_new
                m_s[h:h + 1, :] = m_new
        for n, ks in enumerate(starts):
            for h in range(ATT_HEADS):
                slot = n * ATT_HEADS + h
                m_new = mn_s[slot:slot + 1, :]
                m_ref = jnp.where(m_new == neg_inf, 0.0, m_new)
                alpha = jnp.exp2(mo_s[slot:slot + 1, :] - m_ref)
                p = jnp.exp2(lg_s[slot] - m_ref)
                l_s[h:h + 1, :] = alpha * l_s[h:h + 1, :] + jnp.sum(p, axis=0, keepdims=True)
                vt = vt_ref[0, h * HEAD_DIM:(h + 1) * HEAD_DIM, pl.ds(ks, KT)]
                rows = slice(h * HEAD_DIM, (h + 1) * HEAD_DIM)
                ot_s[rows, :] = ot_s[rows, :] * alpha + jnp.dot(vt, p.astype(BF16), preferred_element_type=F32)

    n_far = jnp.maximum(n_tiles - 2, 0)

    def far_pair(jp, c):
        att_tiles([2 * jp, 2 * jp + 1], False)
        return c

    lax.fori_loop(0, n_far // 2, far_pair, 0)

    @pl.when(n_far % 2 == 1)
    def _():
        att_tiles([n_far - 1], False)

    @pl.when(i > 0)
    def _():
        att_tiles([n_tiles - 2, n_tiles - 1], True)

    @pl.when(i == 0)
    def _():
        att_tiles([n_tiles - 1], True)

    for h in range(ATT_HEADS):
        rows = slice(h * HEAD_DIM, (h + 1) * HEAD_DIM)
        ot_s[rows, :] = ot_s[rows, :] / l_s[h:h + 1, :]

    ot = ot_s[...]
    ms = jnp.mean(ot * ot, axis=0, keepdims=True)
    y = (ot * lax.rsqrt(ms + EPS)).T * goa_ref[...]
    o_ref[0] = y.astype(BF16)


def _attn(q, k, vt, qi, ki, wit, rel_bias, goa):
    bsz, _, _, s = q.shape
    nsel = min(TOPK_MAX, s // 4)
    assert nsel == TQ and s % TQ == 0, "attention kernel assumes TOPK_MAX-sized query blocks"
    bkt = jnp.asarray(_near_buckets())
    kern = functools.partial(_attn_kernel, nsel=nsel)
    return pl.pallas_call(
        kern,
        grid=(bsz, s // TQ),
        in_specs=[
            pl.BlockSpec((1, ATT_HEADS, LANES, TQ), lambda b, i: (b, 0, 0, i)),
            pl.BlockSpec((1, ATT_HEADS // 2, s, LANES), lambda b, i: (b, 0, 0, 0)),
            pl.BlockSpec((1, ATT_W, s), lambda b, i: (b, 0, 0)),
            pl.BlockSpec((1, IDX_HEADS, LANES, TQ), lambda b, i: (b, 0, 0, i)),
            pl.BlockSpec((1, s, LANES), lambda b, i: (b, 0, 0)),
            pl.BlockSpec((1, SUBLANES, TQ), lambda b, i: (b, 0, i)),
            pl.BlockSpec((NEAR, TQ), lambda b, i: (0, 0)),
            pl.BlockSpec(memory_space=pltpu.SMEM),
            pl.BlockSpec((1, ATT_W), lambda b, i: (0, 0)),
        ],
        out_specs=pl.BlockSpec((1, TQ, ATT_W), lambda b, i: (b, i, 0)),
        out_shape=jax.ShapeDtypeStruct((bsz, s, ATT_W), BF16),
        scratch_shapes=[
            pltpu.VMEM((s, TQ), I32),
            pltpu.VMEM((s, TQ), I16),
            pltpu.VMEM((s, TQ), I16),
            pltpu.VMEM((s, TQ), F32),
            pltpu.VMEM((ATT_HEADS, 2 * KT, TQ), F32),
            pltpu.VMEM((ATT_W, TQ), F32),
            pltpu.VMEM((ATT_HEADS, TQ), F32),
            pltpu.VMEM((2 * ATT_HEADS, TQ), F32),
            pltpu.VMEM((2 * ATT_HEADS, TQ), F32),
            pltpu.VMEM((ATT_HEADS, TQ), F32),
            pltpu.VMEM((2 * ATT_HEADS, KT, TQ), F32),
        ],
        compiler_params=pltpu.CompilerParams(
            dimension_semantics=("arbitrary", "arbitrary"), vmem_limit_bytes=VMEM_LIMIT),
        name="attn",
    )(q, k, vt, qi, ki, wit, bkt, rel_bias, goa)


def _topk_rows(s, payload, k, order=None):
    rows = lax.broadcasted_iota(I32, s.shape, 0).astype(F32) if order is None else order
    vals, pays = [], []
    for _ in range(k):
        m = jnp.max(s, axis=0, keepdims=True)
        ix = jnp.min(jnp.where(s == m, rows, _NO_ROW), axis=0, keepdims=True)
        hit = rows == ix
        if payload is None:
            pays.append(ix)
        else:
            pays.append(jnp.max(jnp.where(hit, payload, -1.0), axis=0, keepdims=True))
        vals.append(m)
        s = jnp.where(hit, -jnp.inf, s)
    return jnp.concatenate(vals, axis=0), jnp.concatenate(pays, axis=0)


_NO_ROW = 1e9

_PAIR_GROUPS = (
    ((0, 0, 8, 0),), ((0, 8, 8, 0),), ((1, 0, 8, 0),),
    ((2, 0, 5, 0), (5, 0, 2, 5)), ((3, 0, 4, 0), (4, 0, 3, 4)), ((6, 0, 2, 0), (7, 0, 2, 2)),
)
assert sorted((a, b) for g in _PAIR_GROUPS for a, b0, nb, _ in g for b in range(b0, b0 + nb)) == sorted(
    (a, b) for a in range(PEER_TOPK // 2) for b in range(PEER_TOPK // (a + 1)))


def _mid_kernel(x_ref, cn_ref, an_ref, mod_ref, wo1_ref, wo2_ref, g2_ref, wpq_ref, k1_ref, k2_ref,
                x1_ref, h2_ref, idx_ref, gate_ref, qq_s, idt_s, gt_s):
    tm = x_ref.shape[1]
    x = x_ref[0]
    gt1 = mod_ref[0, 2:3, :]
    sh2 = mod_ref[0, 3:4, :]
    sc2 = mod_ref[0, 4:5, :]
    proj = (jnp.dot(cn_ref[0], wo1_ref[...], preferred_element_type=F32)
            + jnp.dot(an_ref[0], wo2_ref[...], preferred_element_type=F32))
    x1 = x + gt1 * proj
    x1_ref[0] = x1
    r = lax.rsqrt(jnp.mean(x1 * x1, axis=-1, keepdims=True) + EPS)
    h2 = (x1 * r) * g2_ref[...] * (1.0 + sc2) + sh2
    h2_ref[0] = h2
    qq_s[...] = jnp.dot(h2.astype(BF16), wpq_ref[...], preferred_element_type=F32).astype(BF16)

    def route_unit(hh, lt):
        rows = pl.ds(lt * LANES, LANES)
        q1 = qq_s[rows, pl.ds(pl.multiple_of(hh * 2 * N_KEYS, LANES), N_KEYS)]
        q2 = qq_s[rows, pl.ds(pl.multiple_of(hh * 2 * N_KEYS + N_KEYS, LANES), N_KEYS)]
        v1, i1 = _topk_rows(_nt(k1_ref[hh], q1), None, PEER_TOPK)
        v2, i2 = _topk_rows(_nt(k2_ref[hh], q2), None, PEER_TOPK)
        sub = lax.broadcasted_iota(I32, (SUBLANES, LANES), 0)
        subf = sub.astype(F32)
        cands, cidxs, flats = [], [], []
        for group in _PAIR_GROUPS:
            val = jnp.full((SUBLANES, LANES), -jnp.inf, F32)
            cid = jnp.zeros((SUBLANES, LANES), F32)
            flat = jnp.full((SUBLANES, LANES), _NO_ROW, F32)
            for a, b0, nb, off in group:
                v2s, i2s = v2[b0:b0 + SUBLANES, :], i2[b0:b0 + SUBLANES, :]
                if off:
                    v2s, i2s = pltpu.roll(v2s, off, 0), pltpu.roll(i2s, off, 0)
                inside = lambda new, old: jnp.where(sub < off + nb, jnp.where(sub >= off, new, old), old)
                val = inside(v1[a:a + 1, :] + v2s, val)
                cid = inside(i1[a:a + 1, :] * float(N_KEYS) + i2s, cid)
                flat = inside(subf + float(a * PEER_TOPK + b0 - off), flat)
            cands.append(val)
            cidxs.append(cid)
            flats.append(flat)
        half = PEER_TOPK // 2
        cands.append(v1[half:, :] + v2[0:1, :])
        cidxs.append(i1[half:, :] * float(N_KEYS) + i2[0:1, :])
        flats.append((subf + float(half)) * float(PEER_TOPK))
        best, experts = _topk_rows(jnp.concatenate(cands, axis=0), jnp.concatenate(cidxs, axis=0), PEER_TOPK,
                                   order=jnp.concatenate(flats, axis=0))
        e = jnp.exp(best - best[0:1, :])
        g = e / jnp.sum(e, axis=0, keepdims=True)
        slots = pl.ds(pl.multiple_of(hh * PEER_TOPK, PEER_TOPK), PEER_TOPK)
        cols = pl.ds(lt * LANES, LANES)
        idt_s[slots, cols] = experts * float(ROWS_PER_EXPERT)
        gt_s[slots, cols] = g

    def route(hh, c):
        for lt in range(tm // LANES):
            route_unit(hh, lt)
        return c

    lax.fori_loop(0, PEER_HEADS, route, 0)
    idx_ref[0] = idt_s[...].T.astype(I32)
    gate_ref[0] = gt_s[...].T


def _mid(x, cn, an, mod3, wo1, wo2, g2, wpq, k1, k2):
    bsz, s, d = x.shape
    tm = TM_MID
    full = lambda shape: pl.BlockSpec(shape, lambda b, j: (0,) * len(shape))
    tok = lambda w: pl.BlockSpec((1, tm, w), lambda b, j: (b, j, 0))
    return pl.pallas_call(
        _mid_kernel,
        grid=(bsz, s // tm),
        in_specs=[tok(d), tok(CONV_CH), tok(ATT_W), pl.BlockSpec((1, 6, d), lambda b, j: (b, 0, 0)),
                  full(wo1.shape), full(wo2.shape), full(g2.shape), full(wpq.shape), full(k1.shape), full(k2.shape)],
        out_specs=(tok(d), tok(d), tok(PEER_SLOTS), tok(PEER_SLOTS)),
        out_shape=(
            jax.ShapeDtypeStruct((bsz, s, d), F32),
            jax.ShapeDtypeStruct((bsz, s, d), F32),
            jax.ShapeDtypeStruct((bsz, s, PEER_SLOTS), I32),
            jax.ShapeDtypeStruct((bsz, s, PEER_SLOTS), F32),
        ),
        scratch_shapes=[
            pltpu.VMEM((tm, PEER_HEADS * 2 * N_KEYS), BF16),
            pltpu.VMEM((PEER_SLOTS, tm), F32),
            pltpu.VMEM((PEER_SLOTS, tm), F32),
        ],
        compiler_params=pltpu.CompilerParams(
            dimension_semantics=("arbitrary", "arbitrary"), vmem_limit_bytes=VMEM_LIMIT),
        name="mid",
    )(x, cn, an, mod3, wo1, wo2, g2, wpq, k1, k2)


def _pack_table(t):
    tb = t.astype(BF16)
    lo = lax.bitcast_convert_type(tb[:, :HALF], jnp.uint16).astype(jnp.uint32)
    hi = lax.bitcast_convert_type(tb[:, HALF:], jnp.uint16).astype(jnp.uint32)
    return (lo | (hi << 16)).reshape(t.shape[0] * ROWS_PER_EXPERT, LANES)


def _unpack(w):
    lo = pltpu.bitcast(lax.shift_left(w, jnp.uint32(16)), F32)
    hi = pltpu.bitcast(w & jnp.uint32(0xFFFF0000), F32)
    return lo, hi


def _gather_row(tab_ref, row):
    return tab_ref[pl.ds(pl.multiple_of(row, ROWS_PER_EXPERT), ROWS_PER_EXPERT), :]


STAGE_ROWS = PEER_SLOTS * ROWS_PER_EXPERT


def _token_pieces(xg, r, first, count):
    return jnp.concatenate(
        [xg[r:r + 1, (first + q) * LANES:(first + q + 1) * LANES] for q in range(count)], axis=0)


def _peer_u_kernel(idx_ref, tab_ref, x_ref, gate_ref, w_ref, *stages):
    tb = w_ref.shape[0]
    sub = lax.broadcasted_iota(I32, (SUBLANES, PEER_SLOTS), 0)

    def group(g, c):
        for k in range(PEER_GROUPS):
            eight(pl.multiple_of((g * PEER_GROUPS + k) * SUBLANES, SUBLANES), stages[k * SUBLANES:(k + 1) * SUBLANES])
        return c

    def eight(t0, stages):
        act8 = jnp.zeros((SUBLANES, PEER_SLOTS), F32)
        xg = x_ref[pl.ds(t0, SUBLANES), :]
        xlo, xhi, rows_t = [], [], []
        for r in range(SUBLANES):
            xlo.append(jnp.concatenate([_token_pieces(xg, r, 0, ROWS_PER_EXPERT)] * 2, axis=0))
            xhi.append(jnp.concatenate([_token_pieces(xg, r, ROWS_PER_EXPERT, ROWS_PER_EXPERT)] * 2, axis=0))
            rows_t.append(idx_ref.at[t0 + r])
        for j in range(0, PEER_SLOTS, 2):
            for r in range(SUBLANES):
                pair = jnp.concatenate(
                    [_gather_row(tab_ref, rows_t[r][j]), _gather_row(tab_ref, rows_t[r][j + 1])], axis=0)
                lo, hi = _unpack(pair)
                row = j * ROWS_PER_EXPERT
                stages[r][row:row + 2 * ROWS_PER_EXPERT, :] = lo * xlo[r] + hi * xhi[r]
        for r in range(SUBLANES):
            stage = stages[r]
            cs = stage[pl.ds(0, PEER_SLOTS, stride=ROWS_PER_EXPERT), :]
            for q in range(1, ROWS_PER_EXPERT):
                cs = cs + stage[pl.ds(q, PEER_SLOTS, stride=ROWS_PER_EXPERT), :]
            act = jnp.sum(cs.T, axis=0, keepdims=True)
            act8 = jnp.where(sub == r, act, act8)
        rows = pl.ds(t0, SUBLANES)
        w_ref[rows, :] = gate_ref[rows, :] * jax.nn.gelu(act8)

    lax.fori_loop(0, tb // (PEER_GROUPS * SUBLANES), group, 0)


def _peer_u(idx, tab, h2, gate):
    n, d = h2.shape
    tb = TB_PEER
    return pl.pallas_call(
        _peer_u_kernel,
        grid=(n // tb,),
        in_specs=[
            pl.BlockSpec((tb, PEER_SLOTS), lambda i: (i, 0), memory_space=pltpu.SMEM),
            pl.BlockSpec(tab.shape, lambda i: (0, 0), pipeline_mode=pl.Buffered(1)),
            pl.BlockSpec((tb, d), lambda i: (i, 0)),
            pl.BlockSpec((tb, PEER_SLOTS), lambda i: (i, 0)),
        ],
        out_specs=pl.BlockSpec((tb, PEER_SLOTS), lambda i: (i, 0)),
        out_shape=jax.ShapeDtypeStruct((n, PEER_SLOTS), F32),
        scratch_shapes=[pltpu.VMEM((STAGE_ROWS, LANES), F32) for _ in range(PEER_GROUPS * SUBLANES)],
        compiler_params=pltpu.CompilerParams(
            dimension_semantics=("arbitrary",), vmem_limit_bytes=VMEM_LIMIT),
        name="peer_u",
    )(idx, tab, h2, gate)


PV_PAIRS = SUBLANES // 2


def _peer_v_kernel(idx_ref, w_ref, tab_ref, x1_ref, gt2_ref, e8_ref, mask_ref, o_ref, *stages):
    tb = idx_ref.shape[0]
    gt2 = gt2_ref[0]
    mask = mask_ref[...]
    e8 = e8_ref[...]

    def group(g, c):
        for k in range(PEER_GROUPS_V):
            eight(pl.multiple_of((g * PEER_GROUPS_V + k) * SUBLANES, SUBLANES), stages[k * PV_PAIRS:(k + 1) * PV_PAIRS])
        return c

    def eight(g0, stages):
        peers = []
        for p in range(PV_PAIRS):
            stage = stages[p]
            t0 = g0 + 2 * p
            for u in range(2):
                rows_t = idx_ref.at[t0 + u]
                for j in range(PEER_SLOTS):
                    stage[j * ROWS_PER_EXPERT:(j + 1) * ROWS_PER_EXPERT, u * LANES:(u + 1) * LANES] = (
                        _gather_row(tab_ref, rows_t[j]))
            w2 = w_ref[pl.ds(t0, 2), :]
            hi = w2.astype(BF16).astype(F32)
            rep = jnp.dot(jnp.concatenate([hi, w2 - hi], axis=0).astype(BF16), e8,
                          preferred_element_type=F32)
            lhs = jnp.concatenate([rep[i:i + 1, :] * mask for i in range(4)], axis=0).astype(BF16)
            out = jnp.dot(lhs, pltpu.bitcast(stage[...], BF16), preferred_element_type=F32)
            for u in range(2):
                r0 = u * SUBLANES
                peers.append(out[r0:r0 + SUBLANES, u * LANES:(u + 1) * LANES]
                             + out[2 * SUBLANES + r0:3 * SUBLANES + r0, u * LANES:(u + 1) * LANES])
        rows = pl.ds(g0, SUBLANES)
        for q in range(SUBLANES):
            cols = slice(q * LANES, (q + 1) * LANES)
            piece = jnp.concatenate([peer[q:q + 1, :] for peer in peers], axis=0)
            o_ref[rows, cols] = x1_ref[rows, cols] + gt2[:, cols] * piece

    lax.fori_loop(0, tb // (PEER_GROUPS_V * SUBLANES), group, 0)


def _peer_v(idx, w, tab, x1, gt2, blocks_per_batch):
    n, d = x1.shape
    tb = TB_PEER
    q = np.arange(SUBLANES)
    piece = 2 * (q % ROWS_PER_EXPERT) + q // ROWS_PER_EXPERT
    lane = np.arange(PEER_SLOTS * SUBLANES)
    mask = jnp.asarray((lane[None, :] % SUBLANES == piece[:, None]).astype(np.float32))
    e8 = jnp.asarray(np.arange(PEER_SLOTS)[:, None] == lane[None, :] // SUBLANES, BF16)
    return pl.pallas_call(
        _peer_v_kernel,
        grid=(n // tb,),
        in_specs=[
            pl.BlockSpec((tb, PEER_SLOTS), lambda i: (i, 0), memory_space=pltpu.SMEM),
            pl.BlockSpec((tb, PEER_SLOTS), lambda i: (i, 0)),
            pl.BlockSpec(tab.shape, lambda i: (0, 0), pipeline_mode=pl.Buffered(1)),
            pl.BlockSpec((tb, d), lambda i: (i, 0)),
            pl.BlockSpec((1, 1, d), lambda i: (i // blocks_per_batch, 0, 0)),
            pl.BlockSpec(e8.shape, lambda i: (0, 0)),
            pl.BlockSpec(mask.shape, lambda i: (0, 0)),
        ],
        out_specs=pl.BlockSpec((tb, d), lambda i: (i, 0)),
        out_shape=jax.ShapeDtypeStruct((n, d), F32),
        scratch_shapes=[pltpu.VMEM((STAGE_ROWS, 2 * LANES), jnp.uint32) for _ in range(PEER_GROUPS_V * PV_PAIRS)],
        compiler_params=pltpu.CompilerParams(
            dimension_semantics=("arbitrary",), vmem_limit_bytes=VMEM_LIMIT),
        name="peer_v",
    )(idx, w, tab, x1, gt2, e8, mask)


def _layer(x, mod, g_norm1, g_norm2, w_in, q_norm_g, k_norm_g, conv_w, conv_b, conv_ln_g, conv_ln_b,
           rel_bias, g_out_conv, g_out_attn, w_out, w_peer_q, peer_k1, peer_k2, peer_u, peer_v):
    bsz, s, d = x.shape
    n = bsz * s
    mod3 = mod.reshape(bsz, 6, d)
    row = lambda a: a.reshape(1, -1)

    c0 = 2 * CONV_CH
    c1 = c0 + 3 * ATT_W
    c2 = c1 + IDX_HEADS * IDX_DIM
    c3 = c2 + IDX_DIM
    wa = w_in[:, :c0].astype(BF16)
    wqkv = w_in[:, c0:c1].astype(BF16)
    widx = jnp.concatenate([w_in[:, c1:c2], w_in[:, c2:c3], w_in[:, c2:c3]], axis=1).astype(BF16)
    wwi = jnp.zeros((SUBLANES, d), F32).at[:IDX_HEADS].set(w_in[:, c3:c3 + IDX_HEADS].T).astype(BF16)
    head = np.arange(ATT_W) // HEAD_DIM
    e2 = jnp.asarray(head[:, None] == head[None, :], BF16)

    conv_n, q, k, vt, qi, ki, wit = _inproj(
        x, mod3, row(g_norm1), wa, wqkv, widx, wwi,
        row(jnp.tile(q_norm_g, ATT_HEADS)), row(jnp.tile(k_norm_g, ATT_HEADS)), e2,
        conv_w.reshape(CONV_WIDTH, CONV_CH), row(conv_b), row(conv_ln_g), row(conv_ln_b), row(g_out_conv))
    attn_n = _attn(q, k, vt, qi, ki, wit, rel_bias, row(g_out_attn))

    x1, h2, idx, gate = _mid(
        x, conv_n, attn_n, mod3, w_out[:CONV_CH].astype(BF16), w_out[CONV_CH:].astype(BF16), row(g_norm2),
        w_peer_q.astype(BF16), peer_k1.astype(BF16), peer_k2.astype(BF16))

    idx = idx.reshape(n, PEER_SLOTS)
    w = _peer_u(idx, _pack_table(peer_u), h2.reshape(n, d), gate.reshape(n, PEER_SLOTS))
    out = _peer_v(idx, w, _pack_table(peer_v), x1.reshape(n, d), mod3[:, 5:6, :], s // TB_PEER)
    return out.reshape(bsz, s, d)


def kernel(x, c, w_ada, b_ada, g_norm1, g_norm2, w_in, q_norm_g, k_norm_g, conv_w, conv_b, conv_ln_g,
           conv_ln_b, rel_bias, g_out_conv, g_out_attn, w_out, w_peer_q, peer_k1, peer_k2, peer_u, peer_v):
    depth = w_ada.shape[0]
    for l in range(depth):
        mod = _ada(c, w_ada[l], b_ada[l])
        x = _layer(x, mod, g_norm1[l], g_norm2[l], w_in[l], q_norm_g[l], k_norm_g[l], conv_w[l], conv_b[l],
                   conv_ln_g[l], conv_ln_b[l], rel_bias, g_out_conv[l], g_out_attn[l], w_out[l],
                   w_peer_q[l], peer_k1[l], peer_k2[l], peer_u[l], peer_v[l])
    return x
```
